```python
import math
import jax, jax.numpy as jnp
from jax import lax
import numpy as np

D_MODEL = 2048
BATCH = 2
SEQ = 4096
DEPTH = 2

MIX_WIDTH = D_MODEL
GLA_HEADS = 8
GLA_DV = (MIX_WIDTH // 2) // GLA_HEADS
GLA_DK = GLA_DV // 2
GLA_GATE_RANK = 16
GLA_GATE_TAU = 16.0
GLA_CHUNK = 64
SB_HEADS = 8
SB_DH = (MIX_WIDTH // 2) // SB_HEADS
SB_BLOCK = 128
D_FF = 4 * D_MODEL
EPS = 1e-6

COL_SIZES = (
    GLA_HEADS * GLA_DK,
    GLA_HEADS * GLA_DK,
    GLA_HEADS * GLA_DV,
    GLA_HEADS * GLA_DV,
    GLA_GATE_RANK,
    SB_HEADS * SB_DH,
    SB_HEADS * SB_DH,
    SB_HEADS * SB_DH,
)
D_IN = sum(COL_SIZES)
COL_SPLITS = tuple(int(v) for v in np.cumsum(COL_SIZES)[:-1])

kernel_name = "hybrid_gla_stickbreaking_parallel_heads"


def rms_norm(x, g):
    xf = x.astype(jnp.float32)
    y = xf * lax.rsqrt(jnp.mean(xf * xf, axis=-1, keepdims=True) + EPS)
    return (y * g.astype(jnp.float32)).astype(x.dtype)


def split_heads(t, n_heads):
    b, s, _ = t.shape
    return t.reshape(b, s, n_heads, -1).transpose(0, 2, 1, 3)


def gla_chunked(q, k, v, log_a):
    b_, h_, t_, dk = q.shape
    dv = v.shape[-1]
    n = t_ // GLA_CHUNK
    f32 = jnp.float32
    q = q.astype(f32).reshape(b_, h_, n, GLA_CHUNK, dk) * (dk ** -0.5)
    k = k.astype(f32).reshape(b_, h_, n, GLA_CHUNK, dk)
    v = v.astype(f32).reshape(b_, h_, n, GLA_CHUNK, dv)
    g = log_a.astype(f32).reshape(b_, h_, n, GLA_CHUNK, dk)
    bcum = jnp.cumsum(g, axis=3)
    b_last = bcum[:, :, :, -1:, :]
    q_e = q * jnp.exp(bcum)
    k_e = k * jnp.exp(-bcum)
    causal = jnp.tril(jnp.ones((GLA_CHUNK, GLA_CHUNK), dtype=bool))
    scores = jnp.einsum('bhncd,bhnsd->bhncs', q_e, k_e)
    scores = jnp.where(causal, scores, 0.0)
    o_intra = jnp.einsum('bhncs,bhnse->bhnce', scores, v)
    k_dec = k * jnp.exp(b_last - bcum)
    u = jnp.einsum('bhncd,bhnce->bhnde', k_dec, v)
    decay = jnp.exp(b_last[:, :, :, 0, :])

    def step(state, inp):
        d, uu = inp
        return d[..., None] * state + uu, state

    s0 = jnp.zeros((b_, h_, dk, dv), f32)
    _, s_prev = lax.scan(step, s0, (jnp.moveaxis(decay, 2, 0), jnp.moveaxis(u, 2, 0)))
    s_prev = jnp.moveaxis(s_prev, 0, 2)
    o_inter = jnp.einsum('bhncd,bhnde->bhnce', q_e, s_prev)
    return (o_intra + o_inter).reshape(b_, h_, t_, dv)


def stick_breaking_attention(q, k, v):
    b_, h_, t_, d = q.shape
    n_blocks = t_ // SB_BLOCK
    qf = q.astype(jnp.float32)
    kf = k.astype(jnp.float32)
    vf = v.astype(jnp.float32)
    scale = 1.0 / math.sqrt(d)
    key_pos = jnp.arange(t_)

    def one_block(i):
        start = i * SB_BLOCK
        qb = lax.dynamic_slice_in_dim(qf, start, SB_BLOCK, axis=2)
        z = jnp.einsum('bhqd,bhkd->bhqk', qb, kf) * scale
        q_pos = start + jnp.arange(SB_BLOCK)
        mask = key_pos[None, :] < q_pos[:, None]
        log_beta = jax.nn.log_sigmoid(z)
        log_1m = jnp.where(mask, jax.nn.log_sigmoid(-z), 0.0)
        suffix = lax.cumsum(log_1m, axis=3, reverse=True) - log_1m
        a = jnp.where(mask, jnp.exp(log_beta + suffix), 0.0)
        return jnp.einsum('bhqk,bhkd->bhqd', a, vf)

    out = lax.map(one_block, jnp.arange(n_blocks))
    return jnp.moveaxis(out, 0, 2).reshape(b_, h_, t_, d)


def hybrid_layer(x, attn_norm, w_in, w_gate_up, b_gate, gla_out_norm,
                 sb_q_norm, sb_k_norm, sb_out_norm, w_o, mlp_norm, w_up, w_down):
    b_, s_, _ = x.shape
    h = rms_norm(x, attn_norm)
    proj = jnp.einsum('bsd,de->bse', h, w_in)
    g_q, g_k, g_v, g_gate, g_lr, s_q, s_k, s_v = jnp.split(proj, COL_SPLITS, axis=-1)

    gate_logits = jnp.einsum('bsr,re->bse', g_lr, w_gate_up) + b_gate
    log_a = jax.nn.log_sigmoid(gate_logits.astype(jnp.float32)) / GLA_GATE_TAU
    o_gla = gla_chunked(split_heads(g_q, GLA_HEADS), split_heads(g_k, GLA_HEADS),
                        split_heads(g_v, GLA_HEADS), split_heads(log_a, GLA_HEADS))
    o_gla = rms_norm(o_gla.transpose(0, 2, 1, 3), gla_out_norm).astype(x.dtype)
    o_gla = o_gla * jax.nn.silu(g_gate.reshape(b_, s_, GLA_HEADS, GLA_DV))
    o_gla = o_gla.reshape(b_, s_, GLA_HEADS * GLA_DV)

    sq = rms_norm(split_heads(s_q, SB_HEADS), sb_q_norm)
    sk = rms_norm(split_heads(s_k, SB_HEADS), sb_k_norm)
    o_sb = stick_breaking_attention(sq, sk, split_heads(s_v, SB_HEADS))
    o_sb = rms_norm(o_sb.transpose(0, 2, 1, 3), sb_out_norm).astype(x.dtype)
    o_sb = o_sb.reshape(b_, s_, SB_HEADS * SB_DH)

    mixed = jnp.concatenate([o_gla, o_sb], axis=-1)
    x = x + jnp.einsum('bse,ed->bsd', mixed, w_o)

    hm = rms_norm(x, mlp_norm)
    up = jnp.square(jax.nn.relu(jnp.einsum('bsd,df->bsf', hm, w_up)))
    return x + jnp.einsum('bsf,fd->bsd', up, w_down)


def setup_inputs(seed: int = 0) -> dict:
    key = jax.random.key(seed)
    ks = jax.random.split(key, 16)
    f32 = jnp.float32
    nrm = lambda k, shape, s: jax.random.normal(k, shape, f32) * s
    gain = lambda k, shape: 1.0 + 0.02 * jax.random.normal(k, shape, f32)
    return {
        "x": jax.random.normal(ks[0], (BATCH, SEQ, D_MODEL), f32),
        "attn_norm": gain(ks[1], (DEPTH, D_MODEL)),
        "w_in": nrm(ks[2], (DEPTH, D_MODEL, D_IN), D_MODEL ** -0.5),
        "w_gate_up": nrm(ks[3], (DEPTH, GLA_GATE_RANK, GLA_HEADS * GLA_DK), GLA_GATE_RANK ** -0.5),
        "b_gate": nrm(ks[4], (DEPTH, GLA_HEADS * GLA_DK), 0.1),
        "gla_out_norm": gain(ks[5], (DEPTH, GLA_DV)),
        "sb_q_norm": gain(ks[6], (DEPTH, SB_DH)),
        "sb_k_norm": gain(ks[7], (DEPTH, SB_DH)),
        "sb_out_norm": gain(ks[8], (DEPTH, SB_DH)),
        "w_o": nrm(ks[9], (DEPTH, MIX_WIDTH, D_MODEL), MIX_WIDTH ** -0.5),
        "mlp_norm": gain(ks[10], (DEPTH, D_MODEL)),
        "w_up": nrm(ks[11], (DEPTH, D_MODEL, D_FF), D_MODEL ** -0.5),
        "w_down": nrm(ks[12], (DEPTH, D_FF, D_MODEL), D_FF ** -0.5),
    }


def reference(x, attn_norm, w_in, w_gate_up, b_gate, gla_out_norm, sb_q_norm,
              sb_k_norm, sb_out_norm, w_o, mlp_norm, w_up, w_down):
    for l in range(DEPTH):
        x = hybrid_layer(x, attn_norm[l], w_in[l], w_gate_up[l], b_gate[l], gla_out_norm[l],
                         sb_q_norm[l], sb_k_norm[l], sb_out_norm[l], w_o[l],
                         mlp_norm[l], w_up[l], w_down[l])
    return x
```

```python
import functools
import math

import jax
import jax.numpy as jnp
from jax import lax
from jax.experimental import pallas as pl
from jax.experimental.pallas import tpu as pltpu

EPS = 1e-6
LANES = 128

GLA_HEADS = 8
GLA_DK = 64
GLA_DV = 128
GLA_GATE_RANK = 16
GLA_GATE_TAU = 16.0
GLA_CHUNK = 64
SB_HEADS = 8
SB_DH = 128

SB_EXIT = 104.0

VMEM_LIMIT = 48 * 1024 * 1024

F32 = jnp.float32
BF16 = jnp.bfloat16
NT_DIMS = (((1,), (1,)), ((), ()))
TN_DIMS = (((0,), (0,)), ((), ()))


def _params(*sem):
    return pltpu.CompilerParams(dimension_semantics=sem, vmem_limit_bytes=VMEM_LIMIT)


def _log_sigmoid(z):
    return jnp.minimum(z, 0.0) - jnp.log1p(jnp.exp(-jnp.abs(z)))


def _rms_scale(y):
    return lax.rsqrt(jnp.mean(y * y, axis=-1, keepdims=True) + EPS)


def _rmsnorm_kernel(x_ref, g_ref, o_ref):
    x = x_ref[...]
    o_ref[...] = (x * _rms_scale(x) * g_ref[...]).astype(o_ref.dtype)


def _rmsnorm(x, gain, rows=256):
    m, d = x.shape
    rows = min(rows, m)
    return pl.pallas_call(
        _rmsnorm_kernel,
        grid=(m // rows,),
        in_specs=[pl.BlockSpec((rows, d), lambda i: (i, 0)),
                  pl.BlockSpec((1, d), lambda i: (0, 0))],
        out_specs=pl.BlockSpec((rows, d), lambda i: (i, 0)),
        out_shape=jax.ShapeDtypeStruct((m, d), BF16),
        compiler_params=_params("parallel"),
        name="rmsnorm",
    )(x, gain.reshape(1, d))


def _mm_plain_kernel(a_ref, w_ref, o_ref):
    acc = jnp.dot(a_ref[...], w_ref[...], preferred_element_type=F32)
    o_ref[...] = acc.astype(o_ref.dtype)


def _mm_relu2_kernel(a_ref, w_ref, o_ref):
    acc = jnp.dot(a_ref[...], w_ref[...], preferred_element_type=F32)
    r = jnp.maximum(acc, 0.0)
    o_ref[...] = (r * r).astype(o_ref.dtype)


def _mm_headnorm_kernel(a_ref, w_ref, g_ref, o_ref, *, norm_tiles):
    acc = jnp.dot(a_ref[...], w_ref[...], preferred_element_type=F32)
    j = pl.program_id(1)

    @pl.when(j < norm_tiles)
    def _():
        for c in range(acc.shape[1] // LANES):
            sl = slice(c * LANES, (c + 1) * LANES)
            y = acc[:, sl]
            o_ref[:, sl] = (y * _rms_scale(y) * g_ref[:, sl]).astype(o_ref.dtype)

    @pl.when(j >= norm_tiles)
    def _():
        o_ref[...] = acc.astype(o_ref.dtype)


def _matmul(kernel_fn, a, w, extra=(), *, out_dtype, tm=1024, tn=1024, name):
    m, k = a.shape
    n = w.shape[1]
    tm = min(tm, m)
    in_specs = [pl.BlockSpec((tm, k), lambda i, j: (i, 0)),
                pl.BlockSpec((k, tn), lambda i, j: (0, j))]
    in_specs += [pl.BlockSpec((1, tn), lambda i, j: (0, j)) for _ in extra]
    return pl.pallas_call(
        kernel_fn,
        grid=(m // tm, n // tn),
        in_specs=in_specs,
        out_specs=pl.BlockSpec((tm, tn), lambda i, j: (i, j)),
        out_shape=jax.ShapeDtypeStruct((m, n), out_dtype),
        compiler_params=_params("parallel", "arbitrary"),
        name=name,
    )(a, w, *extra)


def _oproj_kernel(a1_ref, a2_ref, w1_ref, w2_ref, x_ref, o_ref):
    acc = jnp.dot(a1_ref[...], w1_ref[...], preferred_element_type=F32)
    acc += jnp.dot(a2_ref[...], w2_ref[...], preferred_element_type=F32)
    o_ref[...] = x_ref[...] + acc


def _oproj(a1, a2, w1, w2, x, *, tm=1024, tn=1024):
    m, k1 = a1.shape
    k2 = a2.shape[1]
    n = w1.shape[1]
    tm = min(tm, m)
    return pl.pallas_call(
        _oproj_kernel,
        grid=(m // tm, n // tn),
        in_specs=[pl.BlockSpec((tm, k1), lambda i, j: (i, 0)),
                  pl.BlockSpec((tm, k2), lambda i, j: (i, 0)),
                  pl.BlockSpec((k1, tn), lambda i, j: (0, j)),
                  pl.BlockSpec((k2, tn), lambda i, j: (0, j)),
                  pl.BlockSpec((tm, tn), lambda i, j: (i, j))],
        out_specs=pl.BlockSpec((tm, tn), lambda i, j: (i, j)),
        out_shape=jax.ShapeDtypeStruct((m, n), F32),
        compiler_params=_params("parallel", "arbitrary"),
        name="oproj",
    )(a1, a2, w1, w2, x)


def _down_kernel(a_ref, w_ref, x_ref, o_ref, acc_ref):
    kk = pl.program_id(2)

    @pl.when(kk == 0)
    def _():
        acc_ref[...] = jnp.zeros_like(acc_ref)

    acc_ref[...] += jnp.dot(a_ref[...], w_ref[...], preferred_element_type=F32)

    @pl.when(kk == pl.num_programs(2) - 1)
    def _():
        o_ref[...] = x_ref[...] + acc_ref[...]


def _down(a, w, x, *, tm=1024, tn=1024, tk=2048):
    m, k = a.shape
    n = w.shape[1]
    tm = min(tm, m)
    return pl.pallas_call(
        _down_kernel,
        grid=(m // tm, n // tn, k // tk),
        in_specs=[pl.BlockSpec((tm, tk), lambda i, j, kk: (i, kk)),
                  pl.BlockSpec((tk, tn), lambda i, j, kk: (kk, j)),
                  pl.BlockSpec((tm, tn), lambda i, j, kk: (i, j))],
        out_specs=pl.BlockSpec((tm, tn), lambda i, j, kk: (i, j)),
        out_shape=jax.ShapeDtypeStruct((m, n), F32),
        scratch_shapes=[pltpu.VMEM((tm, tn), F32)],
        compiler_params=_params("parallel", "arbitrary", "arbitrary"),
        name="down",
    )(a, w, x)


def _gla_kernel(q_ref, k_ref, v_ref, gate_ref, lr_ref, wg_ref, bg_ref, gn_ref,
                o_ref, state_ref, *, tb):
    @pl.when(pl.program_id(1) == 0)
    def _():
        state_ref[...] = jnp.zeros_like(state_ref)

    c = GLA_CHUNK
    row = lax.broadcasted_iota(jnp.int32, (c, c), 0)
    col = lax.broadcasted_iota(jnp.int32, (c, c), 1)
    causal = col <= row
    tri = causal.astype(F32)
    lane = lax.broadcasted_iota(jnp.int32, (c, LANES), 1)
    head_mask = (lane < GLA_DK, lane >= GLA_DK)

    def chunk(ci, carry):
        r0 = pl.multiple_of(ci * c, c)
        rows = pl.ds(r0, c)
        logits = jnp.dot(lr_ref[rows, :], wg_ref[...], precision=lax.Precision.HIGHEST,
                         preferred_element_type=F32) + bg_ref[...]
        log_a = _log_sigmoid(logits) * (1.0 / GLA_GATE_TAU)
        bcum = jnp.dot(tri, log_a, precision=lax.Precision.HIGHEST,
                       preferred_element_type=F32)
        b_last = bcum[c - 1:c, :]
        kk = k_ref[rows, :]
        q_e = q_ref[rows, :] * (GLA_DK ** -0.5) * jnp.exp(bcum)
        k_e = (kk * jnp.exp(-bcum)).astype(BF16)
        k_dec = (kk * jnp.exp(b_last - bcum)).astype(BF16)
        decay = jnp.exp(b_last)
        for p in range(GLA_HEADS // 2):
            psl = slice(p * LANES, (p + 1) * LANES)
            for hh in range(2):
                h = 2 * p + hh
                hsl = slice(h * GLA_DV, (h + 1) * GLA_DV)
                qm = jnp.where(head_mask[hh], q_e[:, psl], 0.0).astype(BF16)
                vb = v_ref[rows, hsl].astype(BF16)
                s = lax.dot_general(qm, k_e[:, psl], NT_DIMS, preferred_element_type=F32)
                s = jnp.where(causal, s, 0.0)
                o = jnp.dot(s.astype(BF16), vb, preferred_element_type=F32)
                st = state_ref[h]
                o += lax.dot_general(qm, st.astype(BF16), NT_DIMS,
                                     preferred_element_type=F32)
                u_t = lax.dot_general(vb, k_dec[:, psl], TN_DIMS,
                                      preferred_element_type=F32)
                state_ref[h] = decay[:, psl] * st + u_t
                y = o * _rms_scale(o) * gn_ref[...]
                g = gate_ref[rows, hsl]
                y = y * (g * (1.0 / (1.0 + jnp.exp(-g))))
                o_ref[rows, hsl] = y.astype(o_ref.dtype)
        return carry

    lax.fori_loop(0, tb // c, chunk, 0)


def _gla(proj, wg, bg, gn, *, batch, seq, tb=512):
    m = proj.shape[0]
    tb = min(tb, seq)
    nt = seq // tb
    hk = GLA_HEADS * GLA_DK
    hv = GLA_HEADS * GLA_DV
    row = lambda b, t: b * nt + t
    return pl.pallas_call(
        functools.partial(_gla_kernel, tb=tb),
        grid=(batch, nt),
        in_specs=[pl.BlockSpec((tb, hk), lambda b, t: (row(b, t), 0)),
                  pl.BlockSpec((tb, hk), lambda b, t: (row(b, t), 1)),
                  pl.BlockSpec((tb, hv), lambda b, t: (row(b, t), 1)),
                  pl.BlockSpec((tb, hv), lambda b, t: (row(b, t), 2)),
                  pl.BlockSpec((tb, LANES), lambda b, t: (row(b, t), (2 * hk + 2 * hv) // LANES)),
                  pl.BlockSpec((LANES, hk), lambda b, t: (0, 0)),
                  pl.BlockSpec((1, hk), lambda b, t: (0, 0)),
                  pl.BlockSpec((1, GLA_DV), lambda b, t: (0, 0))],
        out_specs=pl.BlockSpec((tb, hv), lambda b, t: (row(b, t), 0)),
        out_shape=jax.ShapeDtypeStruct((m, hv), BF16),
        scratch_shapes=[pltpu.VMEM((GLA_HEADS, GLA_DV, LANES), F32)],
        compiler_params=_params("parallel", "arbitrary"),
        name="gla",
    )(proj, proj, proj, proj, proj, wg, bg, gn)


def _sb_kernel(q_ref, k_ref, v_ref, w_ref, gn_ref, o_ref, *, tq, tk):
    i = pl.program_id(2)
    q = q_ref[...]
    wmat = w_ref[...]

    def block(start, carry, acc, mask):
        kb = k_ref[pl.ds(start, tk), :]
        vb = v_ref[pl.ds(start, tk), :]
        z = lax.dot_general(q, kb, NT_DIMS, preferred_element_type=F32)
        log_beta = _log_sigmoid(z)
        log_1m = log_beta - z
        if mask is not None:
            log_1m = jnp.where(mask, log_1m, 0.0)
        hi = log_1m.astype(BF16)
        lo = (log_1m - hi.astype(F32)).astype(BF16)
        s2 = jnp.dot(jnp.concatenate([hi, lo], axis=1), wmat, preferred_element_type=F32)
        a = jnp.exp(log_beta + s2[:, :tk] + carry)
        if mask is not None:
            a = jnp.where(mask, a, 0.0)
        acc = acc + jnp.dot(a.astype(BF16), vb, preferred_element_type=F32)
        return carry + s2[:, tk:], acc

    carry = jnp.zeros((tq, tk), F32)
    acc = jnp.zeros((tq, SB_DH), F32)
    rowi = lax.broadcasted_iota(jnp.int32, (tq, tk), 0)
    coli = lax.broadcasted_iota(jnp.int32, (tq, tk), 1)
    nd = tq // tk
    for d in range(nd - 1, -1, -1):
        start = pl.multiple_of(i * tq + d * tk, tk)
        carry, acc = block(start, carry, acc, coli + d * tk < rowi)

    def cond(st):
        kb, carry, _ = st
        return jnp.logical_and(kb >= 0, jnp.max(carry) > -SB_EXIT)

    def body(st):
        kb, carry, acc = st
        carry, acc = block(pl.multiple_of(kb * tk, tk), carry, acc, None)
        return kb - 1, carry, acc

    _, _, acc = lax.while_loop(cond, body, (i * nd - 1, carry, acc))
    o_ref[...] = (acc * _rms_scale(acc) * gn_ref[...]).astype(o_ref.dtype)


def _sb(proj, gn, *, batch, seq, tq=128, tk=128):
    m = proj.shape[0]
    tq = min(tq, seq)
    nq = seq // tq
    jj = lax.broadcasted_iota(jnp.int32, (2 * tk, 2 * tk), 0) % tk
    ss = lax.broadcasted_iota(jnp.int32, (2 * tk, 2 * tk), 1)
    wmat = jnp.logical_or(ss >= tk, jj > ss).astype(BF16)
    return pl.pallas_call(
        functools.partial(_sb_kernel, tq=tq, tk=tk),
        grid=(batch, SB_HEADS, nq),
        in_specs=[pl.BlockSpec((tq, SB_DH), lambda b, h, i: (b * nq + i, h)),
                  pl.BlockSpec((seq, SB_DH), lambda b, h, i: (b, SB_HEADS + h)),
                  pl.BlockSpec((seq, SB_DH), lambda b, h, i: (b, 2 * SB_HEADS + h)),
                  pl.BlockSpec((2 * tk, 2 * tk), lambda b, h, i: (0, 0)),
                  pl.BlockSpec((1, SB_DH), lambda b, h, i: (0, 0))],
        out_specs=pl.BlockSpec((tq, SB_DH), lambda b, h, i: (b * nq + i, h)),
        out_shape=jax.ShapeDtypeStruct((m, SB_HEADS * SB_DH), BF16),
        compiler_params=_params("parallel", "parallel", "arbitrary"),
        name="stickbreak",
    )(proj, proj, proj, wmat, gn)


def _layer(x, attn_norm, w_in, w_gate_up, b_gate, gla_out_norm, sb_q_norm, sb_k_norm,
           sb_out_norm, w_o, mlp_norm, w_up, w_down, *, batch, seq):
    hk = GLA_HEADS * GLA_DK
    hv = GLA_HEADS * GLA_DV
    hs = SB_HEADS * SB_DH
    n_gla = 2 * hk + 2 * hv
    pad = LANES - GLA_GATE_RANK
    w_gla = jnp.concatenate(
        [w_in[:, :n_gla], jnp.pad(w_in[:, n_gla:n_gla + GLA_GATE_RANK], ((0, 0), (0, pad)))],
        axis=1).astype(BF16)
    w_sb = w_in[:, n_gla + GLA_GATE_RANK:].astype(BF16)
    wg = jnp.pad(w_gate_up, ((0, pad), (0, 0)))
    sb_gain = jnp.concatenate([jnp.tile(sb_q_norm * (1.0 / math.sqrt(SB_DH)), SB_HEADS),
                               jnp.tile(sb_k_norm, SB_HEADS),
                               jnp.ones((hs,), F32)]).reshape(1, 3 * hs)

    h = _rmsnorm(x, attn_norm)
    p_gla = _matmul(_mm_plain_kernel, h, w_gla, out_dtype=F32, tn=640, name="inproj_gla")
    sb_tn = 1024
    p_sb = _matmul(functools.partial(_mm_headnorm_kernel, norm_tiles=2 * hs // sb_tn),
                   h, w_sb, extra=(sb_gain,), out_dtype=BF16, tn=sb_tn, name="inproj_sb")
    o_gla = _gla(p_gla, wg, b_gate.reshape(1, hk), gla_out_norm.reshape(1, GLA_DV),
                 batch=batch, seq=seq)
    o_sb = _sb(p_sb, sb_out_norm.reshape(1, SB_DH), batch=batch, seq=seq)
    x = _oproj(o_gla, o_sb, w_o[:hv].astype(BF16), w_o[hv:].astype(BF16), x)
    hm = _rmsnorm(x, mlp_norm)
    up = _matmul(_mm_relu2_kernel, hm, w_up.astype(BF16), out_dtype=BF16, name="mlp_up")
    return _down(up, w_down.astype(BF16), x)


def kernel(x, attn_norm, w_in, w_gate_up, b_gate, gla_out_norm, sb_q_norm, sb_k_norm,
           sb_out_norm, w_o, mlp_norm, w_up, w_down):
    batch, seq, d = x.shape
    y = x.reshape(batch * seq, d)
    for l in range(w_in.shape[0]):
        y = _layer(y, attn_norm[l], w_in[l], w_gate_up[l], b_gate[l], gla_out_norm[l],
                   sb_q_norm[l], sb_k_norm[l], sb_out_norm[l], w_o[l], mlp_norm[l],
                   w_up[l], w_down[l], batch=batch, seq=seq)
    return y.reshape(batch, seq, d)
```

```python
import functools
import math

import jax
import jax.numpy as jnp
from jax import lax
from jax.experimental import pallas as pl
from jax.experimental.pallas import tpu as pltpu

EPS = 1e-6
LANES = 128

GLA_HEADS = 8
GLA_DK = 64
GLA_DV = 128
GLA_GATE_RANK = 16
GLA_GATE_TAU = 16.0
GLA_CHUNK = 64
SB_HEADS = 8
SB_DH = 128

SB_EXIT_LOG2 = 126.0
SB_DEAD = -1e30

VMEM_LIMIT = 48 * 1024 * 1024

F32 = jnp.float32
BF16 = jnp.bfloat16
NT_DIMS = (((1,), (1,)), ((), ()))
TN_DIMS = (((0,), (0,)), ((), ()))


def _params(*sem):
    return pltpu.CompilerParams(dimension_semantics=sem, vmem_limit_bytes=VMEM_LIMIT)


def _log_sigmoid(z):
    return jnp.minimum(z, 0.0) - jnp.log(1.0 + jnp.exp(-jnp.abs(z)))


def _rms_scale(y):
    return lax.rsqrt(jnp.mean(y * y, axis=-1, keepdims=True) + EPS)


def _rmsnorm_kernel(x_ref, g_ref, o_ref):
    x = x_ref[...]
    o_ref[...] = (x * _rms_scale(x) * g_ref[...]).astype(o_ref.dtype)


def _rmsnorm(x, gain, rows=256):
    m, d = x.shape
    rows = min(rows, m)
    return pl.pallas_call(
        _rmsnorm_kernel,
        grid=(m // rows,),
        in_specs=[pl.BlockSpec((rows, d), lambda i: (i, 0)),
                  pl.BlockSpec((1, d), lambda i: (0, 0))],
        out_specs=pl.BlockSpec((rows, d), lambda i: (i, 0)),
        out_shape=jax.ShapeDtypeStruct((m, d), BF16),
        compiler_params=_params("parallel"),
        name="rmsnorm",
    )(x, gain.reshape(1, d))


def _mm_plain_kernel(a_ref, w_ref, o_ref):
    acc = jnp.dot(a_ref[...], w_ref[...], preferred_element_type=F32)
    o_ref[...] = acc.astype(o_ref.dtype)


def _mm_relu2_kernel(a_ref, w_ref, o_ref):
    acc = jnp.dot(a_ref[...], w_ref[...], preferred_element_type=F32)
    r = jnp.maximum(acc, 0.0)
    o_ref[...] = (r * r).astype(o_ref.dtype)


def _matmul(kernel_fn, a, w, *, out_dtype, tm=1024, tn=1024, name):
    m, k = a.shape
    n = w.shape[1]
    tm = min(tm, m)
    tn = min(tn, n)
    return pl.pallas_call(
        kernel_fn,
        grid=(m // tm, n // tn),
        in_specs=[pl.BlockSpec((tm, k), lambda i, j: (i, 0)),
                  pl.BlockSpec((k, tn), lambda i, j: (0, j))],
        out_specs=pl.BlockSpec((tm, tn), lambda i, j: (i, j)),
        out_shape=jax.ShapeDtypeStruct((m, n), out_dtype),
        compiler_params=_params("parallel", "arbitrary"),
        name=name,
    )(a, w)


def _oproj_kernel(a1_ref, a2_ref, w1_ref, w2_ref, x_ref, o_ref):
    acc = jnp.dot(a1_ref[...], w1_ref[...], preferred_element_type=F32)
    acc += jnp.dot(a2_ref[...], w2_ref[...], preferred_element_type=F32)
    o_ref[...] = x_ref[...] + acc


def _oproj(a1, a2, w1, w2, x, *, tm=1024, tn=1024):
    m, k1 = a1.shape
    k2 = a2.shape[1]
    n = w1.shape[1]
    tm = min(tm, m)
    return pl.pallas_call(
        _oproj_kernel,
        grid=(m // tm, n // tn),
        in_specs=[pl.BlockSpec((tm, k1), lambda i, j: (i, 0)),
                  pl.BlockSpec((tm, k2), lambda i, j: (i, 0)),
                  pl.BlockSpec((k1, tn), lambda i, j: (0, j)),
                  pl.BlockSpec((k2, tn), lambda i, j: (0, j)),
                  pl.BlockSpec((tm, tn), lambda i, j: (i, j))],
        out_specs=pl.BlockSpec((tm, tn), lambda i, j: (i, j)),
        out_shape=jax.ShapeDtypeStruct((m, n), F32),
        compiler_params=_params("parallel", "arbitrary"),
        name="oproj",
    )(a1, a2, w1, w2, x)


def _down_kernel(a_ref, w_ref, x_ref, o_ref, acc_ref):
    kk = pl.program_id(2)

    @pl.when(kk == 0)
    def _():
        acc_ref[...] = jnp.zeros_like(acc_ref)

    acc_ref[...] += jnp.dot(a_ref[...], w_ref[...], preferred_element_type=F32)

    @pl.when(kk == pl.num_programs(2) - 1)
    def _():
        o_ref[...] = x_ref[...] + acc_ref[...]


def _down(a, w, x, *, tm=1024, tn=1024, tk=2048):
    m, k = a.shape
    n = w.shape[1]
    tm = min(tm, m)
    return pl.pallas_call(
        _down_kernel,
        grid=(m // tm, n // tn, k // tk),
        in_specs=[pl.BlockSpec((tm, tk), lambda i, j, kk: (i, kk)),
                  pl.BlockSpec((tk, tn), lambda i, j, kk: (kk, j)),
                  pl.BlockSpec((tm, tn), lambda i, j, kk: (i, j))],
        out_specs=pl.BlockSpec((tm, tn), lambda i, j, kk: (i, j)),
        out_shape=jax.ShapeDtypeStruct((m, n), F32),
        scratch_shapes=[pltpu.VMEM((tm, tn), F32)],
        compiler_params=_params("parallel", "arbitrary", "arbitrary"),
        name="down",
    )(a, w, x)


def _gla_kernel(q_ref, k_ref, v_ref, gate_ref, lr_ref, wg_ref, bg_ref, gn_ref,
                o_ref, state_ref, *, tb):
    @pl.when(pl.program_id(1) == 0)
    def _():
        state_ref[...] = jnp.zeros_like(state_ref)

    c = GLA_CHUNK
    row = lax.broadcasted_iota(jnp.int32, (c, c), 0)
    col = lax.broadcasted_iota(jnp.int32, (c, c), 1)
    causal = col <= row
    tri = causal.astype(F32)
    lane = lax.broadcasted_iota(jnp.int32, (c, LANES), 1)
    head_mask = (lane < GLA_DK, lane >= GLA_DK)

    def chunk(ci, carry):
        r0 = pl.multiple_of(ci * c, c)
        rows = pl.ds(r0, c)
        logits = jnp.dot(lr_ref[rows, :], wg_ref[...], precision=lax.Precision.HIGHEST,
                         preferred_element_type=F32) + bg_ref[...]
        log_a = _log_sigmoid(logits) * (1.0 / GLA_GATE_TAU)
        bcum = jnp.dot(tri, log_a, precision=lax.Precision.HIGHEST,
                       preferred_element_type=F32)
        b_last = bcum[c - 1:c, :]
        kk = k_ref[rows, :].astype(F32)
        q_e = q_ref[rows, :].astype(F32) * (GLA_DK ** -0.5) * jnp.exp(bcum)
        k_e = (kk * jnp.exp(-bcum)).astype(BF16)
        k_dec = (kk * jnp.exp(b_last - bcum)).astype(BF16)
        decay = jnp.exp(b_last)
        for p in range(GLA_HEADS // 2):
            psl = slice(p * LANES, (p + 1) * LANES)
            for hh in range(2):
                h = 2 * p + hh
                hsl = slice(h * GLA_DV, (h + 1) * GLA_DV)
                qm = jnp.where(head_mask[hh], q_e[:, psl], 0.0).astype(BF16)
                vb = v_ref[rows, hsl]
                s = lax.dot_general(qm, k_e[:, psl], NT_DIMS, preferred_element_type=F32)
                s = jnp.where(causal, s, 0.0)
                o = jnp.dot(s.astype(BF16), vb, preferred_element_type=F32)
                st = state_ref[h]
                o += lax.dot_general(qm, st.astype(BF16), NT_DIMS,
                                     preferred_element_type=F32)
                u_t = lax.dot_general(vb, k_dec[:, psl], TN_DIMS,
                                      preferred_element_type=F32)
                state_ref[h] = decay[:, psl] * st + u_t
                y = o * _rms_scale(o) * gn_ref[...]
                g = gate_ref[rows, hsl].astype(F32)
                y = y * (g * (1.0 / (1.0 + jnp.exp(-g))))
                o_ref[rows, hsl] = y.astype(o_ref.dtype)
        return carry

    lax.fori_loop(0, tb // c, chunk, 0)


def _gla(proj, lr, wg, bg, gn, *, batch, seq, tb=512):
    m = proj.shape[0]
    tb = min(tb, seq)
    nt = seq // tb
    hk = GLA_HEADS * GLA_DK
    hv = GLA_HEADS * GLA_DV
    row = lambda b, t: b * nt + t
    return pl.pallas_call(
        functools.partial(_gla_kernel, tb=tb),
        grid=(batch, nt),
        in_specs=[pl.BlockSpec((tb, hk), lambda b, t: (row(b, t), 0)),
                  pl.BlockSpec((tb, hk), lambda b, t: (row(b, t), 1)),
                  pl.BlockSpec((tb, hv), lambda b, t: (row(b, t), 1)),
                  pl.BlockSpec((tb, hv), lambda b, t: (row(b, t), 2)),
                  pl.BlockSpec((tb, LANES), lambda b, t: (row(b, t), 0)),
                  pl.BlockSpec((LANES, hk), lambda b, t: (0, 0)),
                  pl.BlockSpec((1, hk), lambda b, t: (0, 0)),
                  pl.BlockSpec((1, GLA_DV), lambda b, t: (0, 0))],
        out_specs=pl.BlockSpec((tb, hv), lambda b, t: (row(b, t), 0)),
        out_shape=jax.ShapeDtypeStruct((m, hv), BF16),
        scratch_shapes=[pltpu.VMEM((GLA_HEADS, GLA_DV, LANES), F32)],
        compiler_params=_params("parallel", "arbitrary"),
        name="gla",
    )(proj, proj, proj, proj, lr, wg, bg, gn)


def _sb_kernel(q_ref, k_ref, v_ref, w_ref, gq_ref, gk_ref, gn_ref, o_ref, qn_ref, kn_ref,
               *, tq, tk, seq):
    wmat = w_ref[...]
    nd = tq // tk
    rowi = lax.broadcasted_iota(jnp.int32, (tk, tk), 0)
    coli = lax.broadcasted_iota(jnp.int32, (tk, tk), 1)
    below_diag = jnp.concatenate([coli < rowi] * nd, axis=0)

    def head_norm(i, _):
        rows = pl.ds(pl.multiple_of(i * tq, tq), tq)
        y = q_ref[rows, :].astype(F32)
        qn_ref[rows, :] = (y * _rms_scale(y) * gq_ref[...]).astype(BF16)
        y = k_ref[rows, :].astype(F32)
        kn_ref[rows, :] = (y * _rms_scale(y) * gk_ref[...]).astype(BF16)
        return 0

    lax.fori_loop(0, seq // tq, head_norm, 0)

    def sweep(q, q0, j, carry, acc, mask):
        group = lambda x, g: x[g * tk:(g + 1) * tk]
        first = [q0 + (g - j) * tk for g in range(nd)]
        start = [pl.multiple_of(jnp.maximum(f, 0), tk) for f in first]
        z = jnp.concatenate(
            [lax.dot_general(group(q, g), kn_ref[pl.ds(start[g], tk), :], NT_DIMS,
                             preferred_element_type=F32) for g in range(nd)], axis=0)
        log_beta = jnp.minimum(z, 0.0) - jnp.log2(1.0 + jnp.exp2(-jnp.abs(z)))
        log_1m = log_beta - z
        if mask is not None:
            log_1m = jnp.where(mask, log_1m, 0.0)
        else:
            carry = jnp.concatenate(
                [jnp.where(first[g] >= 0, group(carry, g), SB_DEAD) for g in range(nd)], axis=0)
        hi = log_1m.astype(BF16)
        lo = (log_1m - hi.astype(F32)).astype(BF16)
        s2 = jnp.dot(jnp.concatenate([hi, lo], axis=1), wmat, preferred_element_type=F32)
        a = jnp.exp2(log_beta + s2[:, :tk] + carry)
        if mask is not None:
            a = jnp.where(mask, a, 0.0)
        a = a.astype(BF16)
        pv = jnp.concatenate(
            [jnp.dot(group(a, g), v_ref[pl.ds(start[g], tk), :], preferred_element_type=F32)
             for g in range(nd)], axis=0)
        return carry + s2[:, tk:], acc + pv

    def live(c):
        return (jnp.max(c) > -SB_EXIT_LOG2).astype(jnp.int32)

    def qblock(i, _):
        q0 = pl.multiple_of(i * tq, tq)
        q = qn_ref[pl.ds(q0, tq), :]
        carry = jnp.zeros((tq, tk), F32)
        acc = jnp.zeros((tq, SB_DH), F32)
        carry, acc = sweep(q, q0, 0, carry, acc, below_diag)
        carry, acc = sweep(q, q0, 1, carry, acc, None)

        def cond(st):
            j, go, _, _ = st
            return jnp.logical_and(go > 0, j < (i + 1) * nd)

        def body(st):
            j, _, carry, acc = st
            carry, acc = sweep(q, q0, j, carry, acc, None)
            return j + 1, live(carry), carry, acc

        _, _, _, acc = lax.while_loop(cond, body, (2, live(carry), carry, acc))
        o_ref[pl.ds(q0, tq), :] = (acc * _rms_scale(acc) * gn_ref[...]).astype(o_ref.dtype)
        return 0

    lax.fori_loop(0, seq // tq, qblock, 0)


def _sb(proj, gq, gk, gn, *, col0, batch, seq, tq=512, tk=128):
    m = proj.shape[0]
    tq = min(tq, seq)
    assert tq % tk == 0 and seq % tq == 0
    jj = lax.broadcasted_iota(jnp.int32, (2 * tk, 2 * tk), 0) % tk
    ss = lax.broadcasted_iota(jnp.int32, (2 * tk, 2 * tk), 1)
    wmat = jnp.logical_or(ss >= tk, jj > ss).astype(BF16)
    vec = pl.BlockSpec((1, SB_DH), lambda b, h: (0, 0))
    return pl.pallas_call(
        functools.partial(_sb_kernel, tq=tq, tk=tk, seq=seq),
        grid=(batch, SB_HEADS),
        in_specs=[pl.BlockSpec((seq, SB_DH), lambda b, h: (b, col0 + h)),
                  pl.BlockSpec((seq, SB_DH), lambda b, h: (b, col0 + SB_HEADS + h)),
                  pl.BlockSpec((seq, SB_DH), lambda b, h: (b, col0 + 2 * SB_HEADS + h)),
                  pl.BlockSpec((2 * tk, 2 * tk), lambda b, h: (0, 0)),
                  vec, vec, vec],
        out_specs=pl.BlockSpec((seq, SB_DH), lambda b, h: (b, h)),
        out_shape=jax.ShapeDtypeStruct((m, SB_HEADS * SB_DH), BF16),
        scratch_shapes=[pltpu.VMEM((seq, SB_DH), BF16), pltpu.VMEM((seq, SB_DH), BF16)],
        compiler_params=_params("parallel", "parallel"),
        name="stickbreak",
    )(proj, proj, proj, wmat, gq, gk, gn)


def _layer(x, attn_norm, w_in, w_gate_up, b_gate, gla_out_norm, sb_q_norm, sb_k_norm,
           sb_out_norm, w_o, mlp_norm, w_up, w_down, *, batch, seq):
    hk = GLA_HEADS * GLA_DK
    hv = GLA_HEADS * GLA_DV
    n_gla = 2 * hk + 2 * hv
    pad = LANES - GLA_GATE_RANK
    w_heads = jnp.concatenate([w_in[:, :n_gla], w_in[:, n_gla + GLA_GATE_RANK:]],
                              axis=1).astype(BF16)
    w_lr = jnp.pad(w_in[:, n_gla:n_gla + GLA_GATE_RANK], ((0, 0), (0, pad))).astype(BF16)
    wg = jnp.pad(w_gate_up, ((0, pad), (0, 0)))
    gq = (sb_q_norm * (math.log2(math.e) / math.sqrt(SB_DH))).reshape(1, SB_DH)

    h = _rmsnorm(x, attn_norm)
    proj = _matmul(_mm_plain_kernel, h, w_heads, out_dtype=BF16, name="inproj")
    lr = _matmul(_mm_plain_kernel, h, w_lr, out_dtype=F32, name="inproj_lr")
    o_gla = _gla(proj, lr, wg, b_gate.reshape(1, hk), gla_out_norm.reshape(1, GLA_DV),
                 batch=batch, seq=seq)
    o_sb = _sb(proj, gq, sb_k_norm.reshape(1, SB_DH), sb_out_norm.reshape(1, SB_DH),
               col0=n_gla // LANES, batch=batch, seq=seq)
    x = _oproj(o_gla, o_sb, w_o[:hv].astype(BF16), w_o[hv:].astype(BF16), x)
    hm = _rmsnorm(x, mlp_norm)
    up = _matmul(_mm_relu2_kernel, hm, w_up.astype(BF16), out_dtype=BF16, name="mlp_up")
    return _down(up, w_down.astype(BF16), x)


def kernel(x, attn_norm, w_in, w_gate_up, b_gate, gla_out_norm, sb_q_norm, sb_k_norm,
           sb_out_norm, w_o, mlp_norm, w_up, w_down):
    batch, seq, d = x.shape
    y = x.reshape(batch * seq, d)
    for l in range(w_in.shape[0]):
        y = _layer(y, attn_norm[l], w_in[l], w_gate_up[l], b_gate[l], gla_out_norm[l],
                   sb_q_norm[l], sb_k_norm[l], sb_out_norm[l], w_o[l], mlp_norm[l],
                   w_up[l], w_down[l], batch=batch, seq=seq)
    return y.reshape(batch, seq, d)
```

```python
import functools
import math

import jax
import jax.numpy as jnp
from jax import lax
from jax.experimental import pallas as pl
from jax.experimental.pallas import tpu as pltpu

EPS = 1e-6
LANES = 128

GLA_HEADS = 8
GLA_DK = 64
GLA_DV = 128
GLA_GATE_RANK = 16
GLA_GATE_TAU = 16.0
GLA_CHUNK = 64
SB_HEADS = 8
SB_DH = 128

SB_EXIT_LOG2 = 126.0
SB_DEAD = -1e30

VMEM_LIMIT = 56 * 1024 * 1024

F32 = jnp.float32
BF16 = jnp.bfloat16
NT_DIMS = (((1,), (1,)), ((), ()))
TN_DIMS = (((0,), (0,)), ((), ()))


def _params(*sem):
    return pltpu.CompilerParams(dimension_semantics=sem, vmem_limit_bytes=VMEM_LIMIT)


def _log_sigmoid(z):
    return jnp.minimum(z, 0.0) - jnp.log(1.0 + jnp.exp(-jnp.abs(z)))


def _rms_scale(y):
    return lax.rsqrt(jnp.mean(y * y, axis=-1, keepdims=True) + EPS)


def _rmsnorm_kernel(x_ref, g_ref, o_ref):
    x = x_ref[...]
    o_ref[...] = (x * _rms_scale(x) * g_ref[...]).astype(o_ref.dtype)


def _rmsnorm(x, gain, rows=256):
    m, d = x.shape
    rows = min(rows, m)
    return pl.pallas_call(
        _rmsnorm_kernel,
        grid=(m // rows,),
        in_specs=[pl.BlockSpec((rows, d), lambda i: (i, 0)),
                  pl.BlockSpec((1, d), lambda i: (0, 0))],
        out_specs=pl.BlockSpec((rows, d), lambda i: (i, 0)),
        out_shape=jax.ShapeDtypeStruct((m, d), BF16),
        compiler_params=_params("parallel"),
        name="rmsnorm",
    )(x, gain.reshape(1, d))


def _w_spec(w, rows, cols, index_map, layer):
    if w.ndim == 2:
        return pl.BlockSpec((rows, cols), index_map)
    return pl.BlockSpec((None, rows, cols), lambda *g: (layer,) + tuple(index_map(*g)))


def _mm_plain_kernel(a_ref, w_ref, o_ref):
    acc = jnp.dot(a_ref[...], w_ref[...].astype(BF16), preferred_element_type=F32)
    o_ref[...] = acc.astype(o_ref.dtype)


def _mm_relu2_kernel(a_ref, w_ref, o_ref):
    acc = jnp.dot(a_ref[...], w_ref[...].astype(BF16), preferred_element_type=F32)
    r = jnp.maximum(acc, 0.0)
    o_ref[...] = (r * r).astype(o_ref.dtype)


def _matmul(kernel_fn, a, w, *, n=None, layer=0, out_dtype, tm=1024, tn=1024, name):
    m, k = a.shape
    n = w.shape[-1] if n is None else n
    tm = min(tm, m)
    tn = min(tn, n)
    return pl.pallas_call(
        kernel_fn,
        grid=(m // tm, n // tn),
        in_specs=[pl.BlockSpec((tm, k), lambda i, j: (i, 0)),
                  _w_spec(w, k, tn, lambda i, j: (0, j), layer)],
        out_specs=pl.BlockSpec((tm, tn), lambda i, j: (i, j)),
        out_shape=jax.ShapeDtypeStruct((m, n), out_dtype),
        compiler_params=_params("parallel", "arbitrary"),
        name=name,
    )(a, w)


def _oproj_kernel(a1_ref, a2_ref, w1_ref, w2_ref, x_ref, o_ref):
    acc = jnp.dot(a1_ref[...], w1_ref[...].astype(BF16), preferred_element_type=F32)
    acc += jnp.dot(a2_ref[...], w2_ref[...].astype(BF16), preferred_element_type=F32)
    o_ref[...] = x_ref[...] + acc


def _oproj(a1, a2, w, x, *, layer, tm=1024, tn=1024):
    m, kh = a1.shape
    n = w.shape[-1]
    tm = min(tm, m)
    return pl.pallas_call(
        _oproj_kernel,
        grid=(m // tm, n // tn),
        in_specs=[pl.BlockSpec((tm, kh), lambda i, j: (i, 0)),
                  pl.BlockSpec((tm, kh), lambda i, j: (i, 0)),
                  _w_spec(w, kh, tn, lambda i, j: (0, j), layer),
                  _w_spec(w, kh, tn, lambda i, j: (1, j), layer),
                  pl.BlockSpec((tm, tn), lambda i, j: (i, j))],
        out_specs=pl.BlockSpec((tm, tn), lambda i, j: (i, j)),
        out_shape=jax.ShapeDtypeStruct((m, n), F32),
        compiler_params=_params("parallel", "arbitrary"),
        name="oproj",
    )(a1, a2, w, w, x)


def _down_kernel(a_ref, w_ref, x_ref, o_ref, acc_ref):
    kk = pl.program_id(2)

    @pl.when(kk == 0)
    def _():
        acc_ref[...] = jnp.zeros_like(acc_ref)

    acc_ref[...] += jnp.dot(a_ref[...], w_ref[...].astype(BF16), preferred_element_type=F32)

    @pl.when(kk == pl.num_programs(2) - 1)
    def _():
        o_ref[...] = x_ref[...] + acc_ref[...]


def _down(a, w, x, *, layer, tm=1024, tn=1024, tk=2048):
    m, k = a.shape
    n = w.shape[-1]
    tm = min(tm, m)
    return pl.pallas_call(
        _down_kernel,
        grid=(m // tm, n // tn, k // tk),
        in_specs=[pl.BlockSpec((tm, tk), lambda i, j, kk: (i, kk)),
                  _w_spec(w, tk, tn, lambda i, j, kk: (kk, j), layer),
                  pl.BlockSpec((tm, tn), lambda i, j, kk: (i, j))],
        out_specs=pl.BlockSpec((tm, tn), lambda i, j, kk: (i, j)),
        out_shape=jax.ShapeDtypeStruct((m, n), F32),
        scratch_shapes=[pltpu.VMEM((tm, tn), F32)],
        compiler_params=_params("parallel", "arbitrary", "arbitrary"),
        name="down",
    )(a, w, x)


def _gla_kernel(q_ref, k_ref, v_ref, gate_ref, lr_ref, wg_ref, bg_ref, gn_ref,
                o_ref, state_ref, *, tb):
    @pl.when(pl.program_id(1) == 0)
    def _():
        state_ref[...] = jnp.zeros_like(state_ref)

    c = GLA_CHUNK
    row = lax.broadcasted_iota(jnp.int32, (c, c), 0)
    col = lax.broadcasted_iota(jnp.int32, (c, c), 1)
    causal = col <= row
    tri = causal.astype(F32)
    lane = lax.broadcasted_iota(jnp.int32, (c, LANES), 1)
    head_mask = (lane < GLA_DK, lane >= GLA_DK)

    def chunk(ci, carry):
        r0 = pl.multiple_of(ci * c, c)
        rows = pl.ds(r0, c)
        logits = jnp.dot(lr_ref[rows, :], wg_ref[...], precision=lax.Precision.HIGHEST,
                         preferred_element_type=F32) + bg_ref[...]
        log_a = _log_sigmoid(logits) * (1.0 / GLA_GATE_TAU)
        bcum = jnp.dot(tri, log_a, precision=lax.Precision.HIGHEST,
                       preferred_element_type=F32)
        b_last = bcum[c - 1:c, :]
        kk = k_ref[rows, :].astype(F32)
        q_e = q_ref[rows, :].astype(F32) * (GLA_DK ** -0.5) * jnp.exp(bcum)
        k_e = (kk * jnp.exp(-bcum)).astype(BF16)
        k_dec = (kk * jnp.exp(b_last - bcum)).astype(BF16)
        decay = jnp.exp(b_last)
        for p in range(GLA_HEADS // 2):
            psl = slice(p * LANES, (p + 1) * LANES)
            for hh in range(2):
                h = 2 * p + hh
                hsl = slice(h * GLA_DV, (h + 1) * GLA_DV)
                qm = jnp.where(head_mask[hh], q_e[:, psl], 0.0).astype(BF16)
                vb = v_ref[rows, hsl]
                s = lax.dot_general(qm, k_e[:, psl], NT_DIMS, preferred_element_type=F32)
                s = jnp.where(causal, s, 0.0)
                o = jnp.dot(s.astype(BF16), vb, preferred_element_type=F32)
                st = state_ref[h]
                o += lax.dot_general(qm, st.astype(BF16), NT_DIMS,
                                     preferred_element_type=F32)
                u_t = lax.dot_general(vb, k_dec[:, psl], TN_DIMS,
                                      preferred_element_type=F32)
                state_ref[h] = decay[:, psl] * st + u_t
                y = o * _rms_scale(o) * gn_ref[...]
                g = gate_ref[rows, hsl].astype(F32)
                y = y * (g * (1.0 / (1.0 + jnp.exp(-g))))
                o_ref[rows, hsl] = y.astype(o_ref.dtype)
        return carry

    lax.fori_loop(0, tb // c, chunk, 0)


def _gla(proj, lr, wg, bg, gn, *, batch, seq, tb=512):
    m = proj.shape[0]
    tb = min(tb, seq)
    nt = seq // tb
    hk = GLA_HEADS * GLA_DK
    hv = GLA_HEADS * GLA_DV
    row = lambda b, t: b * nt + t
    return pl.pallas_call(
        functools.partial(_gla_kernel, tb=tb),
        grid=(batch, nt),
        in_specs=[pl.BlockSpec((tb, hk), lambda b, t: (row(b, t), 0)),
                  pl.BlockSpec((tb, hk), lambda b, t: (row(b, t), 1)),
                  pl.BlockSpec((tb, hv), lambda b, t: (row(b, t), 1)),
                  pl.BlockSpec((tb, hv), lambda b, t: (row(b, t), 2)),
                  pl.BlockSpec((tb, LANES), lambda b, t: (row(b, t), 0)),
                  pl.BlockSpec((LANES, hk), lambda b, t: (0, 0)),
                  pl.BlockSpec((1, hk), lambda b, t: (0, 0)),
                  pl.BlockSpec((1, GLA_DV), lambda b, t: (0, 0))],
        out_specs=pl.BlockSpec((tb, hv), lambda b, t: (row(b, t), 0)),
        out_shape=jax.ShapeDtypeStruct((m, hv), BF16),
        scratch_shapes=[pltpu.VMEM((GLA_HEADS, GLA_DV, LANES), F32)],
        compiler_params=_params("parallel", "arbitrary"),
        name="gla",
    )(proj, proj, proj, proj, lr, wg, bg, gn)


def _sb_kernel(q_ref, k_ref, v_ref, w_ref, gq_ref, gk_ref, gn_ref, o_ref, qn_ref, kn_ref,
               *, tq, tk, seq):
    wmat = w_ref[...]
    nd = tq // tk
    rowi = lax.broadcasted_iota(jnp.int32, (tk, tk), 0)
    coli = lax.broadcasted_iota(jnp.int32, (tk, tk), 1)
    below_diag = jnp.concatenate([coli < rowi] * nd, axis=0)

    def head_norm(i, _):
        rows = pl.ds(pl.multiple_of(i * tq, tq), tq)
        y = q_ref[rows, :].astype(F32)
        qn_ref[rows, :] = (y * _rms_scale(y) * gq_ref[...]).astype(BF16)
        y = k_ref[rows, :].astype(F32)
        kn_ref[rows, :] = (y * _rms_scale(y) * gk_ref[...]).astype(BF16)
        return 0

    lax.fori_loop(0, seq // tq, head_norm, 0)

    def sweep(q, q0, j, carry, acc, mask):
        group = lambda x, g: x[g * tk:(g + 1) * tk]
        first = [q0 + (g - j) * tk for g in range(nd)]
        start = [pl.multiple_of(jnp.maximum(f, 0), tk) for f in first]
        z = jnp.concatenate(
            [lax.dot_general(group(q, g), kn_ref[pl.ds(start[g], tk), :], NT_DIMS,
                             preferred_element_type=F32) for g in range(nd)], axis=0)
        log_beta = jnp.minimum(z, 0.0) - jnp.log2(1.0 + jnp.exp2(-jnp.abs(z)))
        log_1m = log_beta - z
        if mask is not None:
            log_1m = jnp.where(mask, log_1m, 0.0)
        else:
            carry = jnp.concatenate(
                [jnp.where(first[g] >= 0, group(carry, g), SB_DEAD) for g in range(nd)], axis=0)
        hi = log_1m.astype(BF16)
        lo = (log_1m - hi.astype(F32)).astype(BF16)
        s2 = jnp.dot(jnp.concatenate([hi, lo], axis=1), wmat, preferred_element_type=F32)
        a = jnp.exp2(log_beta + s2[:, :tk] + carry)
        if mask is not None:
            a = jnp.where(mask, a, 0.0)
        a = a.astype(BF16)
        pv = jnp.concatenate(
            [jnp.dot(group(a, g), v_ref[pl.ds(start[g], tk), :], preferred_element_type=F32)
             for g in range(nd)], axis=0)
        return carry + s2[:, tk:], acc + pv

    def live(c):
        return (jnp.max(c) > -SB_EXIT_LOG2).astype(jnp.int32)

    def qblock(i, _):
        q0 = pl.multiple_of(i * tq, tq)
        q = qn_ref[pl.ds(q0, tq), :]
        carry = jnp.zeros((tq, tk), F32)
        acc = jnp.zeros((tq, SB_DH), F32)
        carry, acc = sweep(q, q0, 0, carry, acc, below_diag)
        carry, acc = sweep(q, q0, 1, carry, acc, None)

        def cond(st):
            j, go, _, _ = st
            return jnp.logical_and(go > 0, j < (i + 1) * nd)

        def body(st):
            j, _, carry, acc = st
            carry, acc = sweep(q, q0, j, carry, acc, None)
            return j + 1, live(carry), carry, acc

        _, _, _, acc = lax.while_loop(cond, body, (2, live(carry), carry, acc))
        o_ref[pl.ds(q0, tq), :] = (acc * _rms_scale(acc) * gn_ref[...]).astype(o_ref.dtype)
        return 0

    lax.fori_loop(0, seq // tq, qblock, 0)


def _sb(proj, gq, gk, gn, *, col0, batch, seq, tq=512, tk=128):
    m = proj.shape[0]
    tq = min(tq, seq)
    assert tq % tk == 0 and seq % tq == 0
    jj = lax.broadcasted_iota(jnp.int32, (2 * tk, 2 * tk), 0) % tk
    ss = lax.broadcasted_iota(jnp.int32, (2 * tk, 2 * tk), 1)
    wmat = jnp.logical_or(ss >= tk, jj > ss).astype(BF16)
    vec = pl.BlockSpec((1, SB_DH), lambda b, h: (0, 0))
    return pl.pallas_call(
        functools.partial(_sb_kernel, tq=tq, tk=tk, seq=seq),
        grid=(batch, SB_HEADS),
        in_specs=[pl.BlockSpec((seq, SB_DH), lambda b, h: (b, col0 + h)),
                  pl.BlockSpec((seq, SB_DH), lambda b, h: (b, col0 + SB_HEADS + h)),
                  pl.BlockSpec((seq, SB_DH), lambda b, h: (b, col0 + 2 * SB_HEADS + h)),
                  pl.BlockSpec((2 * tk, 2 * tk), lambda b, h: (0, 0)),
                  vec, vec, vec],
        out_specs=pl.BlockSpec((seq, SB_DH), lambda b, h: (b, h)),
        out_shape=jax.ShapeDtypeStruct((m, SB_HEADS * SB_DH), BF16),
        scratch_shapes=[pltpu.VMEM((seq, SB_DH), BF16), pltpu.VMEM((seq, SB_DH), BF16)],
        compiler_params=_params("parallel", "parallel"),
        name="stickbreak",
    )(proj, proj, proj, wmat, gq, gk, gn)


def _layer(x, l, attn_norm, w_in, w_gate_up, b_gate, gla_out_norm, sb_q_norm, sb_k_norm,
           sb_out_norm, w_o, mlp_norm, w_up, w_down, *, batch, seq):
    hk = GLA_HEADS * GLA_DK
    hv = GLA_HEADS * GLA_DV
    n_gla = 2 * hk + 2 * hv
    n_sb = 3 * SB_HEADS * SB_DH
    pad = LANES - GLA_GATE_RANK
    w_sb = w_in[l, :, n_gla + GLA_GATE_RANK:]
    w_lr = jnp.pad(w_in[l, :, n_gla:n_gla + GLA_GATE_RANK], ((0, 0), (0, pad)))
    wg = jnp.pad(w_gate_up[l], ((0, pad), (0, 0)))
    gq = (sb_q_norm[l] * (math.log2(math.e) / math.sqrt(SB_DH))).reshape(1, SB_DH)

    h = _rmsnorm(x, attn_norm[l])
    p_gla = _matmul(_mm_plain_kernel, h, w_in, n=n_gla, layer=l, out_dtype=BF16,
                    name="inproj_gla")
    p_sb = _matmul(_mm_plain_kernel, h, w_sb, n=n_sb, out_dtype=BF16, name="inproj_sb")
    lr = _matmul(_mm_plain_kernel, h, w_lr, out_dtype=F32, name="inproj_lr")
    o_gla = _gla(p_gla, lr, wg, b_gate[l].reshape(1, hk), gla_out_norm[l].reshape(1, GLA_DV),
                 batch=batch, seq=seq)
    o_sb = _sb(p_sb, gq, sb_k_norm[l].reshape(1, SB_DH), sb_out_norm[l].reshape(1, SB_DH),
               col0=0, batch=batch, seq=seq)
    x = _oproj(o_gla, o_sb, w_o, x, layer=l)
    hm = _rmsnorm(x, mlp_norm[l])
    up = _matmul(_mm_relu2_kernel, hm, w_up, layer=l, out_dtype=BF16, name="mlp_up")
    return _down(up, w_down, x, layer=l)


def kernel(x, attn_norm, w_in, w_gate_up, b_gate, gla_out_norm, sb_q_norm, sb_k_norm,
           sb_out_norm, w_o, mlp_norm, w_up, w_down):
    batch, seq, d = x.shape
    y = x.reshape(batch * seq, d)
    for l in range(w_in.shape[0]):
        y = _layer(y, l, attn_norm, w_in, w_gate_up, b_gate, gla_out_norm, sb_q_norm,
                   sb_k_norm, sb_out_norm, w_o, mlp_norm, w_up, w_down, batch=batch, seq=seq)
    return y.reshape(batch, seq, d)
```

```python
import functools
import math

import jax
import jax.numpy as jnp
from jax import lax
from jax.experimental import pallas as pl
from jax.experimental.pallas import tpu as pltpu

EPS = 1e-6
LANES = 128

GLA_HEADS = 8
GLA_DK = 64
GLA_DV = 128
GLA_GATE_RANK = 16
GLA_GATE_TAU = 16.0
GLA_CHUNK = 64
SB_HEADS = 8
SB_DH = 128

SB_EXIT_LOG2 = 126.0
SB_DEAD = -1e30

VMEM_LIMIT = 56 * 1024 * 1024

F32 = jnp.float32
BF16 = jnp.bfloat16
NT_DIMS = (((1,), (1,)), ((), ()))
TN_DIMS = (((0,), (0,)), ((), ()))


def _params(*sem):
    return pltpu.CompilerParams(dimension_semantics=sem, vmem_limit_bytes=VMEM_LIMIT)


def _log_sigmoid(z):
    return jnp.minimum(z, 0.0) - jnp.log(1.0 + jnp.exp(-jnp.abs(z)))


def _rms_scale(y):
    return lax.rsqrt(jnp.mean(y * y, axis=-1, keepdims=True) + EPS)


def _rmsnorm_kernel(x_ref, g_ref, o_ref):
    x = x_ref[...]
    o_ref[...] = (x * _rms_scale(x) * g_ref[...]).astype(o_ref.dtype)


def _rmsnorm(x, gain, rows=256):
    m, d = x.shape
    rows = min(rows, m)
    return pl.pallas_call(
        _rmsnorm_kernel,
        grid=(m // rows,),
        in_specs=[pl.BlockSpec((rows, d), lambda i: (i, 0)),
                  pl.BlockSpec((1, d), lambda i: (0, 0))],
        out_specs=pl.BlockSpec((rows, d), lambda i: (i, 0)),
        out_shape=jax.ShapeDtypeStruct((m, d), BF16),
        compiler_params=_params("parallel"),
        name="rmsnorm",
    )(x, gain.reshape(1, d))


def _w_spec(w, rows, cols, index_map, layer):
    if w.ndim == 2:
        return pl.BlockSpec((rows, cols), index_map)
    return pl.BlockSpec((None, rows, cols), lambda *g: (layer,) + tuple(index_map(*g)))


def _mm_plain_kernel(a_ref, w_ref, o_ref):
    acc = jnp.dot(a_ref[...], w_ref[...].astype(BF16), preferred_element_type=F32)
    o_ref[...] = acc.astype(o_ref.dtype)


def _mm_relu2_kernel(a_ref, w_ref, o_ref):
    acc = jnp.dot(a_ref[...], w_ref[...].astype(BF16), preferred_element_type=F32)
    r = jnp.maximum(acc, 0.0)
    o_ref[...] = (r * r).astype(o_ref.dtype)


def _matmul(kernel_fn, a, w, *, n=None, layer=0, out_dtype, tm=1024, tn=1024, name):
    m, k = a.shape
    n = w.shape[-1] if n is None else n
    tm = min(tm, m)
    tn = min(tn, n)
    return pl.pallas_call(
        kernel_fn,
        grid=(n // tn, m // tm),
        in_specs=[pl.BlockSpec((tm, k), lambda j, i: (i, 0)),
                  _w_spec(w, k, tn, lambda j, i: (0, j), layer)],
        out_specs=pl.BlockSpec((tm, tn), lambda j, i: (i, j)),
        out_shape=jax.ShapeDtypeStruct((m, n), out_dtype),
        compiler_params=_params("parallel", "arbitrary"),
        name=name,
    )(a, w)


def _oproj_kernel(a1_ref, a2_ref, w1_ref, w2_ref, x_ref, o_ref):
    acc = jnp.dot(a1_ref[...], w1_ref[...].astype(BF16), preferred_element_type=F32)
    acc += jnp.dot(a2_ref[...], w2_ref[...].astype(BF16), preferred_element_type=F32)
    o_ref[...] = x_ref[...] + acc


def _oproj(a1, a2, w, x, *, layer, tm=1024, tn=1024):
    m, kh = a1.shape
    n = w.shape[-1]
    tm = min(tm, m)
    return pl.pallas_call(
        _oproj_kernel,
        grid=(n // tn, m // tm),
        in_specs=[pl.BlockSpec((tm, kh), lambda j, i: (i, 0)),
                  pl.BlockSpec((tm, kh), lambda j, i: (i, 0)),
                  _w_spec(w, kh, tn, lambda j, i: (0, j), layer),
                  _w_spec(w, kh, tn, lambda j, i: (1, j), layer),
                  pl.BlockSpec((tm, tn), lambda j, i: (i, j))],
        out_specs=pl.BlockSpec((tm, tn), lambda j, i: (i, j)),
        out_shape=jax.ShapeDtypeStruct((m, n), F32),
        compiler_params=_params("parallel", "arbitrary"),
        name="oproj",
    )(a1, a2, w, w, x)


def _down_kernel(a_ref, w_ref, x_ref, o_ref, acc_ref, *, tm):
    kk = pl.program_id(1)
    last = pl.num_programs(1) - 1
    rows = pl.ds(pl.multiple_of(pl.program_id(2) * tm, tm), tm)
    part = jnp.dot(a_ref[...], w_ref[...].astype(BF16), preferred_element_type=F32)

    @pl.when(kk == 0)
    def _():
        acc_ref[rows, :] = part

    @pl.when(jnp.logical_and(kk > 0, kk < last))
    def _():
        acc_ref[rows, :] += part

    @pl.when(kk == last)
    def _():
        o_ref[...] = x_ref[...] + (acc_ref[rows, :] + part)


def _down(a, w, x, *, layer, tm=1024, tn=512, tk=2048):
    m, k = a.shape
    n = w.shape[-1]
    tm = min(tm, m)
    nk = k // tk
    assert nk >= 2
    xo_map = lambda j, kk, i: (jnp.where(kk == nk - 1, i, 0), j)
    return pl.pallas_call(
        functools.partial(_down_kernel, tm=tm),
        grid=(n // tn, nk, m // tm),
        in_specs=[pl.BlockSpec((tm, tk), lambda j, kk, i: (i, kk)),
                  _w_spec(w, tk, tn, lambda j, kk, i: (kk, j), layer),
                  pl.BlockSpec((tm, tn), xo_map)],
        out_specs=pl.BlockSpec((tm, tn), xo_map),
        out_shape=jax.ShapeDtypeStruct((m, n), F32),
        scratch_shapes=[pltpu.VMEM((m, tn), F32)],
        compiler_params=_params("arbitrary", "arbitrary", "arbitrary"),
        name="down",
    )(a, w, x)


def _gla_kernel(q_ref, k_ref, v_ref, gate_ref, lr_ref, wg_ref, bg_ref, gn_ref,
                o_ref, state_ref, *, tb):
    @pl.when(pl.program_id(1) == 0)
    def _():
        state_ref[...] = jnp.zeros_like(state_ref)

    c = GLA_CHUNK
    row = lax.broadcasted_iota(jnp.int32, (c, c), 0)
    col = lax.broadcasted_iota(jnp.int32, (c, c), 1)
    causal = col <= row
    tri = causal.astype(F32)
    lane = lax.broadcasted_iota(jnp.int32, (c, LANES), 1)
    head_mask = (lane < GLA_DK, lane >= GLA_DK)

    def chunk(ci, carry):
        r0 = pl.multiple_of(ci * c, c)
        rows = pl.ds(r0, c)
        logits = jnp.dot(lr_ref[rows, :], wg_ref[...], precision=lax.Precision.HIGHEST,
                         preferred_element_type=F32) + bg_ref[...]
        log_a = _log_sigmoid(logits) * (1.0 / GLA_GATE_TAU)
        bcum = jnp.dot(tri, log_a, precision=lax.Precision.HIGHEST,
                       preferred_element_type=F32)
        b_last = bcum[c - 1:c, :]
        kk = k_ref[rows, :].astype(F32)
        q_e = q_ref[rows, :].astype(F32) * (GLA_DK ** -0.5) * jnp.exp(bcum)
        k_e = (kk * jnp.exp(-bcum)).astype(BF16)
        k_dec = (kk * jnp.exp(b_last - bcum)).astype(BF16)
        decay = jnp.exp(b_last)
        for p in range(GLA_HEADS // 2):
            psl = slice(p * LANES, (p + 1) * LANES)
            for hh in range(2):
                h = 2 * p + hh
                hsl = slice(h * GLA_DV, (h + 1) * GLA_DV)
                qm = jnp.where(head_mask[hh], q_e[:, psl], 0.0).astype(BF16)
                vb = v_ref[rows, hsl]
                s = lax.dot_general(qm, k_e[:, psl], NT_DIMS, preferred_element_type=F32)
                s = jnp.where(causal, s, 0.0)
                o = jnp.dot(s.astype(BF16), vb, preferred_element_type=F32)
                st = state_ref[h]
                o += lax.dot_general(qm, st.astype(BF16), NT_DIMS,
                                     preferred_element_type=F32)
                u_t = lax.dot_general(vb, k_dec[:, psl], TN_DIMS,
                                      preferred_element_type=F32)
                state_ref[h] = decay[:, psl] * st + u_t
                y = o * _rms_scale(o) * gn_ref[...]
                g = gate_ref[rows, hsl].astype(F32)
                y = y * (g * (1.0 / (1.0 + jnp.exp(-g))))
                o_ref[rows, hsl] = y.astype(o_ref.dtype)
        return carry

    lax.fori_loop(0, tb // c, chunk, 0)


def _gla(proj, lr, wg, bg, gn, *, batch, seq, tb=512):
    m = proj.shape[0]
    tb = min(tb, seq)
    nt = seq // tb
    hk = GLA_HEADS * GLA_DK
    hv = GLA_HEADS * GLA_DV
    row = lambda b, t: b * nt + t
    return pl.pallas_call(
        functools.partial(_gla_kernel, tb=tb),
        grid=(batch, nt),
        in_specs=[pl.BlockSpec((tb, hk), lambda b, t: (row(b, t), 0)),
                  pl.BlockSpec((tb, hk), lambda b, t: (row(b, t), 1)),
                  pl.BlockSpec((tb, hv), lambda b, t: (row(b, t), 1)),
                  pl.BlockSpec((tb, hv), lambda b, t: (row(b, t), 2)),
                  pl.BlockSpec((tb, LANES), lambda b, t: (row(b, t), 0)),
                  pl.BlockSpec((LANES, hk), lambda b, t: (0, 0)),
                  pl.BlockSpec((1, hk), lambda b, t: (0, 0)),
                  pl.BlockSpec((1, GLA_DV), lambda b, t: (0, 0))],
        out_specs=pl.BlockSpec((tb, hv), lambda b, t: (row(b, t), 0)),
        out_shape=jax.ShapeDtypeStruct((m, hv), BF16),
        scratch_shapes=[pltpu.VMEM((GLA_HEADS, GLA_DV, LANES), F32)],
        compiler_params=_params("parallel", "arbitrary"),
        name="gla",
    )(proj, proj, proj, proj, lr, wg, bg, gn)


def _sb_kernel(q_ref, k_ref, v_ref, w_ref, gq_ref, gk_ref, gn_ref, o_ref, qn_ref, kn_ref,
               *, tq, tk, seq):
    wmat = w_ref[...]
    nd = tq // tk
    rowi = lax.broadcasted_iota(jnp.int32, (tk, tk), 0)
    coli = lax.broadcasted_iota(jnp.int32, (tk, tk), 1)
    below_diag = jnp.concatenate([coli < rowi] * nd, axis=0)

    def head_norm(i, _):
        rows = pl.ds(pl.multiple_of(i * tq, tq), tq)
        y = q_ref[rows, :].astype(F32)
        qn_ref[rows, :] = (y * _rms_scale(y) * gq_ref[...]).astype(BF16)
        y = k_ref[rows, :].astype(F32)
        kn_ref[rows, :] = (y * _rms_scale(y) * gk_ref[...]).astype(BF16)
        return 0

    lax.fori_loop(0, seq // tq, head_norm, 0)

    def sweep(q, q0, j, carry, acc, mask):
        group = lambda x, g: x[g * tk:(g + 1) * tk]
        first = [q0 + (g - j) * tk for g in range(nd)]
        start = [pl.multiple_of(jnp.maximum(f, 0), tk) for f in first]
        z = jnp.concatenate(
            [lax.dot_general(group(q, g), kn_ref[pl.ds(start[g], tk), :], NT_DIMS,
                             preferred_element_type=F32) for g in range(nd)], axis=0)
        log_beta = jnp.minimum(z, 0.0) - jnp.log2(1.0 + jnp.exp2(-jnp.abs(z)))
        log_1m = log_beta - z
        if mask is not None:
            log_1m = jnp.where(mask, log_1m, 0.0)
        else:
            carry = jnp.concatenate(
                [jnp.where(first[g] >= 0, group(carry, g), SB_DEAD) for g in range(nd)], axis=0)
        hi = log_1m.astype(BF16)
        lo = (log_1m - hi.astype(F32)).astype(BF16)
        s2 = jnp.dot(jnp.concatenate([hi, lo], axis=1), wmat, preferred_element_type=F32)
        a = jnp.exp2(log_beta + s2[:, :tk] + carry)
        if mask is not None:
            a = jnp.where(mask, a, 0.0)
        a = a.astype(BF16)
        pv = jnp.concatenate(
            [jnp.dot(group(a, g), v_ref[pl.ds(start[g], tk), :], preferred_element_type=F32)
             for g in range(nd)], axis=0)
        return carry + s2[:, tk:], acc + pv

    def live(c):
        return (jnp.max(c) > -SB_EXIT_LOG2).astype(jnp.int32)

    def qblock(i, _):
        q0 = pl.multiple_of(i * tq, tq)
        q = qn_ref[pl.ds(q0, tq), :]
        carry = jnp.zeros((tq, tk), F32)
        acc = jnp.zeros((tq, SB_DH), F32)
        carry, acc = sweep(q, q0, 0, carry, acc, below_diag)
        carry, acc = sweep(q, q0, 1, carry, acc, None)

        def cond(st):
            j, go, _, _ = st
            return jnp.logical_and(go > 0, j < (i + 1) * nd)

        def body(st):
            j, _, carry, acc = st
            carry, acc = sweep(q, q0, j, carry, acc, None)
            return j + 1, live(carry), carry, acc

        _, _, _, acc = lax.while_loop(cond, body, (2, live(carry), carry, acc))
        o_ref[pl.ds(q0, tq), :] = (acc * _rms_scale(acc) * gn_ref[...]).astype(o_ref.dtype)
        return 0

    lax.fori_loop(0, seq // tq, qblock, 0)


def _sb(proj, gq, gk, gn, *, col0, batch, seq, tq=512, tk=128):
    m = proj.shape[0]
    tq = min(tq, seq)
    assert tq % tk == 0 and seq % tq == 0
    jj = lax.broadcasted_iota(jnp.int32, (2 * tk, 2 * tk), 0) % tk
    ss = lax.broadcasted_iota(jnp.int32, (2 * tk, 2 * tk), 1)
    wmat = jnp.logical_or(ss >= tk, jj > ss).astype(BF16)
    vec = pl.BlockSpec((1, SB_DH), lambda b, h: (0, 0))
    return pl.pallas_call(
        functools.partial(_sb_kernel, tq=tq, tk=tk, seq=seq),
        grid=(batch, SB_HEADS),
        in_specs=[pl.BlockSpec((seq, SB_DH), lambda b, h: (b, col0 + h)),
                  pl.BlockSpec((seq, SB_DH), lambda b, h: (b, col0 + SB_HEADS + h)),
                  pl.BlockSpec((seq, SB_DH), lambda b, h: (b, col0 + 2 * SB_HEADS + h)),
                  pl.BlockSpec((2 * tk, 2 * tk), lambda b, h: (0, 0)),
                  vec, vec, vec],
        out_specs=pl.BlockSpec((seq, SB_DH), lambda b, h: (b, h)),
        out_shape=jax.ShapeDtypeStruct((m, SB_HEADS * SB_DH), BF16),
        scratch_shapes=[pltpu.VMEM((seq, SB_DH), BF16), pltpu.VMEM((seq, SB_DH), BF16)],
        compiler_params=_params("parallel", "parallel"),
        name="stickbreak",
    )(proj, proj, proj, wmat, gq, gk, gn)


def _layer(x, l, attn_norm, w_in, w_gate_up, b_gate, gla_out_norm, sb_q_norm, sb_k_norm,
           sb_out_norm, w_o, mlp_norm, w_up, w_down, *, batch, seq):
    hk = GLA_HEADS * GLA_DK
    hv = GLA_HEADS * GLA_DV
    n_gla = 2 * hk + 2 * hv
    pad = LANES - GLA_GATE_RANK
    w_gla = w_in[l, :, :n_gla]
    w_sb = w_in[l, :, n_gla + GLA_GATE_RANK:]
    w_lr = jnp.pad(w_in[l, :, n_gla:n_gla + GLA_GATE_RANK], ((0, 0), (0, pad)))
    wg = jnp.pad(w_gate_up[l], ((0, pad), (0, 0)))
    gq = (sb_q_norm[l] * (math.log2(math.e) / math.sqrt(SB_DH))).reshape(1, SB_DH)

    h = _rmsnorm(x, attn_norm[l])
    p_gla = _matmul(_mm_plain_kernel, h, w_gla, out_dtype=BF16, name="inproj_gla")
    p_sb = _matmul(_mm_plain_kernel, h, w_sb, out_dtype=BF16, name="inproj_sb")
    lr = _matmul(_mm_plain_kernel, h, w_lr, out_dtype=F32, name="inproj_lr")
    o_gla = _gla(p_gla, lr, wg, b_gate[l].reshape(1, hk), gla_out_norm[l].reshape(1, GLA_DV),
                 batch=batch, seq=seq)
    o_sb = _sb(p_sb, gq, sb_k_norm[l].reshape(1, SB_DH), sb_out_norm[l].reshape(1, SB_DH),
               col0=0, batch=batch, seq=seq)
    x = _oproj(o_gla, o_sb, w_o, x, layer=l)
    hm = _rmsnorm(x, mlp_norm[l])
    up = _matmul(_mm_relu2_kernel, hm, w_up, layer=l, out_dtype=BF16, name="mlp_up")
    return _down(up, w_down, x, layer=l)


def kernel(x, attn_norm, w_in, w_gate_up, b_gate, gla_out_norm, sb_q_norm, sb_k_norm,
           sb_out_norm, w_o, mlp_norm, w_up, w_down):
    batch, seq, d = x.shape
    y = x.reshape(batch * seq, d)
    for l in range(w_in.shape[0]):
        y = _layer(y, l, attn_norm, w_in, w_gate_up, b_gate, gla_out_norm, sb_q_norm,
                   sb_k_norm, sb_out_norm, w_o, mlp_norm, w_up, w_down, batch=batch, seq=seq)
    return y.reshape(batch, seq, d)
```

```python
import functools
import math

import jax
import jax.numpy as jnp
from jax import lax
from jax.experimental import pallas as pl
from jax.experimental.pallas import tpu as pltpu

EPS = 1e-6
LANES = 128

GLA_HEADS = 8
GLA_DK = 64
GLA_DV = 128
GLA_GATE_RANK = 16
GLA_GATE_TAU = 16.0
GLA_CHUNK = 64
SB_HEADS = 8
SB_DH = 128

SB_EXIT_LOG2 = 126.0
SB_DEAD = -1e30

VMEM_LIMIT = 56 * 1024 * 1024

F32 = jnp.float32
BF16 = jnp.bfloat16
NT_DIMS = (((1,), (1,)), ((), ()))
TN_DIMS = (((0,), (0,)), ((), ()))


def _params(*sem):
    return pltpu.CompilerParams(dimension_semantics=sem, vmem_limit_bytes=VMEM_LIMIT)


def _log_sigmoid(z):
    return jnp.minimum(z, 0.0) - jnp.log(1.0 + jnp.exp(-jnp.abs(z)))


def _rms_scale(y):
    return lax.rsqrt(jnp.mean(y * y, axis=-1, keepdims=True) + EPS)


def _rmsnorm_kernel(x_ref, g_ref, o_ref):
    x = x_ref[...]
    o_ref[...] = (x * _rms_scale(x) * g_ref[...]).astype(o_ref.dtype)


def _rmsnorm(x, gain, rows=256):
    m, d = x.shape
    rows = min(rows, m)
    return pl.pallas_call(
        _rmsnorm_kernel,
        grid=(m // rows,),
        in_specs=[pl.BlockSpec((rows, d), lambda i: (i, 0)),
                  pl.BlockSpec((1, d), lambda i: (0, 0))],
        out_specs=pl.BlockSpec((rows, d), lambda i: (i, 0)),
        out_shape=jax.ShapeDtypeStruct((m, d), BF16),
        compiler_params=_params("parallel"),
        name="rmsnorm",
    )(x, gain.reshape(1, d))


def _w_spec(w, rows, cols, index_map, layer):
    if w.ndim == 2:
        return pl.BlockSpec((rows, cols), index_map)
    return pl.BlockSpec((None, rows, cols), lambda *g: (layer,) + tuple(index_map(*g)))


def _mm_plain_kernel(a_ref, w_ref, o_ref):
    acc = jnp.dot(a_ref[...], w_ref[...].astype(BF16), preferred_element_type=F32)
    o_ref[...] = acc.astype(o_ref.dtype)


def _mm_relu2_kernel(a_ref, w_ref, o_ref):
    acc = jnp.dot(a_ref[...], w_ref[...].astype(BF16), preferred_element_type=F32)
    r = jnp.maximum(acc, 0.0)
    o_ref[...] = (r * r).astype(o_ref.dtype)


def _matmul(kernel_fn, a, w, *, n=None, layer=0, out_dtype, tm=1024, tn=1024, name):
    m, k = a.shape
    n = w.shape[-1] if n is None else n
    tm = min(tm, m)
    tn = min(tn, n)
    return pl.pallas_call(
        kernel_fn,
        grid=(n // tn, m // tm),
        in_specs=[pl.BlockSpec((tm, k), lambda j, i: (i, 0)),
                  _w_spec(w, k, tn, lambda j, i: (0, j), layer)],
        out_specs=pl.BlockSpec((tm, tn), lambda j, i: (i, j)),
        out_shape=jax.ShapeDtypeStruct((m, n), out_dtype),
        compiler_params=_params("parallel", "arbitrary"),
        name=name,
    )(a, w)


def _oproj_kernel(a1_ref, a2_ref, w1_ref, w2_ref, x_ref, o_ref):
    acc = jnp.dot(a1_ref[...], w1_ref[...].astype(BF16), preferred_element_type=F32)
    acc += jnp.dot(a2_ref[...], w2_ref[...].astype(BF16), preferred_element_type=F32)
    o_ref[...] = x_ref[...] + acc


def _oproj(a1, a2, w, x, *, layer, tm=1024, tn=1024):
    m, kh = a1.shape
    n = w.shape[-1]
    tm = min(tm, m)
    return pl.pallas_call(
        _oproj_kernel,
        grid=(n // tn, m // tm),
        in_specs=[pl.BlockSpec((tm, kh), lambda j, i: (i, 0)),
                  pl.BlockSpec((tm, kh), lambda j, i: (i, 0)),
                  _w_spec(w, kh, tn, lambda j, i: (0, j), layer),
                  _w_spec(w, kh, tn, lambda j, i: (1, j), layer),
                  pl.BlockSpec((tm, tn), lambda j, i: (i, j))],
        out_specs=pl.BlockSpec((tm, tn), lambda j, i: (i, j)),
        out_shape=jax.ShapeDtypeStruct((m, n), F32),
        compiler_params=_params("parallel", "arbitrary"),
        name="oproj",
    )(a1, a2, w, w, x)


def _down_kernel(a_ref, w_ref, x_ref, o_ref, acc_ref, *, tm):
    kk = pl.program_id(1)
    last = pl.num_programs(1) - 1
    rows = pl.ds(pl.multiple_of(pl.program_id(2) * tm, tm), tm)

    @pl.when(kk == 0)
    def _():
        acc_ref[rows, :] = jnp.zeros((tm, acc_ref.shape[1]), F32)

    acc_ref[rows, :] += jnp.dot(a_ref[...], w_ref[...].astype(BF16),
                                preferred_element_type=F32)

    @pl.when(kk == last)
    def _():
        o_ref[...] = x_ref[...] + acc_ref[rows, :]


def _down(a, w, x, *, layer, tm=1024, tn=512, tk=2048):
    m, k = a.shape
    n = w.shape[-1]
    tm = min(tm, m)
    nk = k // tk
    assert nk >= 2
    xo_map = lambda j, kk, i: (jnp.where(kk == nk - 1, i, 0), j)
    return pl.pallas_call(
        functools.partial(_down_kernel, tm=tm),
        grid=(n // tn, nk, m // tm),
        in_specs=[pl.BlockSpec((tm, tk), lambda j, kk, i: (i, kk)),
                  _w_spec(w, tk, tn, lambda j, kk, i: (kk, j), layer),
                  pl.BlockSpec((tm, tn), xo_map)],
        out_specs=pl.BlockSpec((tm, tn), xo_map),
        out_shape=jax.ShapeDtypeStruct((m, n), F32),
        scratch_shapes=[pltpu.VMEM((m, tn), F32)],
        compiler_params=_params("arbitrary", "arbitrary", "arbitrary"),
        name="down",
    )(a, w, x)


def _gla_kernel(q_ref, k_ref, v_ref, gate_ref, lr_ref, wg_ref, bg_ref, gn_ref,
                o_ref, state_ref, loga_ref, *, tb):
    @pl.when(pl.program_id(1) == 0)
    def _():
        state_ref[...] = jnp.zeros_like(state_ref)

    c = GLA_CHUNK
    row = lax.broadcasted_iota(jnp.int32, (c, c), 0)
    col = lax.broadcasted_iota(jnp.int32, (c, c), 1)
    causal = col <= row
    tri = jnp.where(causal, 1.0, 0.0).astype(BF16)
    tri2 = jnp.concatenate([tri, tri], axis=1)
    lane = lax.broadcasted_iota(jnp.int32, (c, LANES), 1)
    head_mask = (lane < GLA_DK, lane >= GLA_DK)

    r = GLA_GATE_RANK
    lr = lr_ref[...]
    lr_hi = lr.astype(BF16)
    lr_lo = (lr - lr_hi.astype(F32)).astype(BF16)
    lr_lane = lax.broadcasted_iota(jnp.int32, lr.shape, 1)
    lr_mix = jnp.where(jnp.logical_and(lr_lane >= r, lr_lane < 2 * r), lr_lo, lr_hi)
    wg = wg_ref[...]
    wg_hi = wg.astype(BF16)
    wg_lo = (wg - wg_hi.astype(F32)).astype(BF16)
    wg_mix = jnp.where(lax.broadcasted_iota(jnp.int32, wg.shape, 0) >= 2 * r, wg_lo, wg_hi)
    logits = jnp.dot(lr_mix, wg_mix, preferred_element_type=F32) + bg_ref[...]
    loga_ref[...] = _log_sigmoid(logits) * (1.0 / GLA_GATE_TAU)

    def chunk(ci, carry):
        r0 = pl.multiple_of(ci * c, c)
        rows = pl.ds(r0, c)
        log_a = loga_ref[rows, :]
        hi = log_a.astype(BF16)
        lo = (log_a - hi.astype(F32)).astype(BF16)
        bcum = jnp.dot(tri2, jnp.concatenate([hi, lo], axis=0), preferred_element_type=F32)
        b_last = bcum[c - 1:c, :]
        kk = k_ref[rows, :].astype(F32)
        q_e = q_ref[rows, :].astype(F32) * (GLA_DK ** -0.5) * jnp.exp(bcum)
        k_e = (kk * jnp.exp(-bcum)).astype(BF16)
        k_dec = (kk * jnp.exp(b_last - bcum)).astype(BF16)
        decay = jnp.exp(b_last)
        for p in range(GLA_HEADS // 2):
            psl = slice(p * LANES, (p + 1) * LANES)
            for hh in range(2):
                h = 2 * p + hh
                hsl = slice(h * GLA_DV, (h + 1) * GLA_DV)
                qm = jnp.where(head_mask[hh], q_e[:, psl], 0.0).astype(BF16)
                vb = v_ref[rows, hsl]
                s = lax.dot_general(qm, k_e[:, psl], NT_DIMS, preferred_element_type=F32)
                s = jnp.where(causal, s, 0.0)
                o = jnp.dot(s.astype(BF16), vb, preferred_element_type=F32)
                st = state_ref[h]
                o += lax.dot_general(qm, st.astype(BF16), NT_DIMS,
                                     preferred_element_type=F32)
                u_t = lax.dot_general(vb, k_dec[:, psl], TN_DIMS,
                                      preferred_element_type=F32)
                state_ref[h] = decay[:, psl] * st + u_t
                y = o * _rms_scale(o) * gn_ref[...]
                g = gate_ref[rows, hsl].astype(F32)
                y = y * (g * (1.0 / (1.0 + jnp.exp(-g))))
                o_ref[rows, hsl] = y.astype(o_ref.dtype)
        return carry

    lax.fori_loop(0, tb // c, chunk, 0, unroll=4)


def _gla(proj, lr, wg, bg, gn, *, batch, seq, tb=512):
    m = proj.shape[0]
    tb = min(tb, seq)
    nt = seq // tb
    hk = GLA_HEADS * GLA_DK
    hv = GLA_HEADS * GLA_DV
    row = lambda b, t: b * nt + t
    return pl.pallas_call(
        functools.partial(_gla_kernel, tb=tb),
        grid=(batch, nt),
        in_specs=[pl.BlockSpec((tb, hk), lambda b, t: (row(b, t), 0)),
                  pl.BlockSpec((tb, hk), lambda b, t: (row(b, t), 1)),
                  pl.BlockSpec((tb, hv), lambda b, t: (row(b, t), 1)),
                  pl.BlockSpec((tb, hv), lambda b, t: (row(b, t), 2)),
                  pl.BlockSpec((tb, LANES), lambda b, t: (row(b, t), 0)),
                  pl.BlockSpec((LANES, hk), lambda b, t: (0, 0)),
                  pl.BlockSpec((1, hk), lambda b, t: (0, 0)),
                  pl.BlockSpec((1, GLA_DV), lambda b, t: (0, 0))],
        out_specs=pl.BlockSpec((tb, hv), lambda b, t: (row(b, t), 0)),
        out_shape=jax.ShapeDtypeStruct((m, hv), BF16),
        scratch_shapes=[pltpu.VMEM((GLA_HEADS, GLA_DV, LANES), F32),
                        pltpu.VMEM((tb, hk), F32)],
        compiler_params=_params("parallel", "arbitrary"),
        name="gla",
    )(proj, proj, proj, proj, lr, wg, bg, gn)


def _sb_kernel(q_ref, k_ref, v_ref, w_ref, gq_ref, gk_ref, gn_ref, o_ref, qn_ref, kn_ref,
               *, tq, tk, seq):
    wmat = w_ref[...]
    nd = tq // tk
    rowi = lax.broadcasted_iota(jnp.int32, (tk, tk), 0)
    coli = lax.broadcasted_iota(jnp.int32, (tk, tk), 1)
    below_diag = jnp.concatenate([coli < rowi] * nd, axis=0)

    def head_norm(i, _):
        rows = pl.ds(pl.multiple_of(i * tq, tq), tq)
        y = q_ref[rows, :].astype(F32)
        qn_ref[rows, :] = (y * _rms_scale(y) * gq_ref[...]).astype(BF16)
        y = k_ref[rows, :].astype(F32)
        kn_ref[rows, :] = (y * _rms_scale(y) * gk_ref[...]).astype(BF16)
        return 0

    lax.fori_loop(0, seq // tq, head_norm, 0)

    def sweep(q, q0, j, carry, acc, mask):
        group = lambda x, g: x[g * tk:(g + 1) * tk]
        first = [q0 + (g - j) * tk for g in range(nd)]
        start = [pl.multiple_of(jnp.maximum(f, 0), tk) for f in first]
        z = jnp.concatenate(
            [lax.dot_general(group(q, g), kn_ref[pl.ds(start[g], tk), :], NT_DIMS,
                             preferred_element_type=F32) for g in range(nd)], axis=0)
        log_beta = jnp.minimum(z, 0.0) - jnp.log2(1.0 + jnp.exp2(-jnp.abs(z)))
        log_1m = log_beta - z
        if mask is not None:
            log_1m = jnp.where(mask, log_1m, 0.0)
        else:
            carry = jnp.concatenate(
                [jnp.where(first[g] >= 0, group(carry, g), SB_DEAD) for g in range(nd)], axis=0)
        hi = log_1m.astype(BF16)
        lo = (log_1m - hi.astype(F32)).astype(BF16)
        s2 = jnp.dot(jnp.concatenate([hi, lo], axis=1), wmat, preferred_element_type=F32)
        a = jnp.exp2(log_beta + s2[:, :tk] + carry)
        if mask is not None:
            a = jnp.where(mask, a, 0.0)
        a = a.astype(BF16)
        pv = jnp.concatenate(
            [jnp.dot(group(a, g), v_ref[pl.ds(start[g], tk), :], preferred_element_type=F32)
             for g in range(nd)], axis=0)
        return carry + s2[:, tk:], acc + pv

    def live(c):
        return (jnp.max(c) > -SB_EXIT_LOG2).astype(jnp.int32)

    def qblock(i, _):
        q0 = pl.multiple_of(i * tq, tq)
        q = qn_ref[pl.ds(q0, tq), :]
        carry = jnp.zeros((tq, tk), F32)
        acc = jnp.zeros((tq, SB_DH), F32)
        carry, acc = sweep(q, q0, 0, carry, acc, below_diag)
        carry, acc = sweep(q, q0, 1, carry, acc, None)

        def cond(st):
            j, go, _, _ = st
            return jnp.logical_and(go > 0, j < (i + 1) * nd)

        def body(st):
            j, _, carry, acc = st
            carry, acc = sweep(q, q0, j, carry, acc, None)
            return j + 1, live(carry), carry, acc

        _, _, _, acc = lax.while_loop(cond, body, (2, live(carry), carry, acc))
        o_ref[pl.ds(q0, tq), :] = (acc * _rms_scale(acc) * gn_ref[...]).astype(o_ref.dtype)
        return 0

    lax.fori_loop(0, seq // tq, qblock, 0)


def _sb(proj, gq, gk, gn, *, col0, batch, seq, tq=512, tk=128):
    m = proj.shape[0]
    tq = min(tq, seq)
    assert tq % tk == 0 and seq % tq == 0
    jj = lax.broadcasted_iota(jnp.int32, (2 * tk, 2 * tk), 0) % tk
    ss = lax.broadcasted_iota(jnp.int32, (2 * tk, 2 * tk), 1)
    wmat = jnp.logical_or(ss >= tk, jj > ss).astype(BF16)
    vec = pl.BlockSpec((1, SB_DH), lambda b, h: (0, 0))
    return pl.pallas_call(
        functools.partial(_sb_kernel, tq=tq, tk=tk, seq=seq),
        grid=(batch, SB_HEADS),
        in_specs=[pl.BlockSpec((seq, SB_DH), lambda b, h: (b, col0 + h)),
                  pl.BlockSpec((seq, SB_DH), lambda b, h: (b, col0 + SB_HEADS + h)),
                  pl.BlockSpec((seq, SB_DH), lambda b, h: (b, col0 + 2 * SB_HEADS + h)),
                  pl.BlockSpec((2 * tk, 2 * tk), lambda b, h: (0, 0)),
                  vec, vec, vec],
        out_specs=pl.BlockSpec((seq, SB_DH), lambda b, h: (b, h)),
        out_shape=jax.ShapeDtypeStruct((m, SB_HEADS * SB_DH), BF16),
        scratch_shapes=[pltpu.VMEM((seq, SB_DH), BF16), pltpu.VMEM((seq, SB_DH), BF16)],
        compiler_params=_params("parallel", "parallel"),
        name="stickbreak",
    )(proj, proj, proj, wmat, gq, gk, gn)


def _layer(x, l, attn_norm, w_in, w_gate_up, b_gate, gla_out_norm, sb_q_norm, sb_k_norm,
           sb_out_norm, w_o, mlp_norm, w_up, w_down, *, batch, seq):
    hk = GLA_HEADS * GLA_DK
    hv = GLA_HEADS * GLA_DV
    n_gla = 2 * hk + 2 * hv
    pad = LANES - 3 * GLA_GATE_RANK
    w_gla = w_in[l, :, :n_gla]
    w_sb = w_in[l, :, n_gla + GLA_GATE_RANK:]
    w_lr = jnp.pad(jnp.tile(w_in[l, :, n_gla:n_gla + GLA_GATE_RANK], (1, 3)), ((0, 0), (0, pad)))
    wg = jnp.pad(jnp.tile(w_gate_up[l], (3, 1)), ((0, pad), (0, 0)))
    gq = (sb_q_norm[l] * (math.log2(math.e) / math.sqrt(SB_DH))).reshape(1, SB_DH)

    h = _rmsnorm(x, attn_norm[l])
    p_gla = _matmul(_mm_plain_kernel, h, w_gla, out_dtype=BF16, name="inproj_gla")
    p_sb = _matmul(_mm_plain_kernel, h, w_sb, out_dtype=BF16, name="inproj_sb")
    lr = _matmul(_mm_plain_kernel, h, w_lr, out_dtype=F32, name="inproj_lr")
    o_gla = _gla(p_gla, lr, wg, b_gate[l].reshape(1, hk), gla_out_norm[l].reshape(1, GLA_DV),
                 batch=batch, seq=seq)
    o_sb = _sb(p_sb, gq, sb_k_norm[l].reshape(1, SB_DH), sb_out_norm[l].reshape(1, SB_DH),
               col0=0, batch=batch, seq=seq)
    x = _oproj(o_gla, o_sb, w_o, x, layer=l)
    hm = _rmsnorm(x, mlp_norm[l])
    up = _matmul(_mm_relu2_kernel, hm, w_up, layer=l, out_dtype=BF16, name="mlp_up")
    return _down(up, w_down, x, layer=l)


def kernel(x, attn_norm, w_in, w_gate_up, b_gate, gla_out_norm, sb_q_norm, sb_k_norm,
           sb_out_norm, w_o, mlp_norm, w_up, w_down):
    batch, seq, d = x.shape
    y = x.reshape(batch * seq, d)
    for l in range(w_in.shape[0]):
        y = _layer(y, l, attn_norm, w_in, w_gate_up, b_gate, gla_out_norm, sb_q_norm,
                   sb_k_norm, sb_out_norm, w_o, mlp_norm, w_up, w_down, batch=batch, seq=seq)
    return y.reshape(batch, seq, d)
```

```python
import functools
import math

import jax
import jax.numpy as jnp
from jax import lax
from jax.experimental import pallas as pl
from jax.experimental.pallas import tpu as pltpu

EPS = 1e-6
LANES = 128

GLA_HEADS = 8
GLA_DK = 64
GLA_DV = 128
GLA_GATE_RANK = 16
GLA_GATE_TAU = 16.0
GLA_CHUNK = 64
SB_HEADS = 8
SB_DH = 128

SB_EXIT_LOG2 = 126.0
SB_DEAD = -1e30

VMEM_LIMIT = 56 * 1024 * 1024

F32 = jnp.float32
BF16 = jnp.bfloat16
NT_DIMS = (((1,), (1,)), ((), ()))
TN_DIMS = (((0,), (0,)), ((), ()))


def _params(*sem):
    return pltpu.CompilerParams(dimension_semantics=sem, vmem_limit_bytes=VMEM_LIMIT)


def _log_sigmoid(z):
    return jnp.minimum(z, 0.0) - jnp.log(1.0 + jnp.exp(-jnp.abs(z)))


def _rms_scale(y):
    return lax.rsqrt(jnp.mean(y * y, axis=-1, keepdims=True) + EPS)


def _rmsnorm_kernel(x_ref, g_ref, o_ref):
    x = x_ref[...]
    o_ref[...] = (x * _rms_scale(x) * g_ref[...]).astype(o_ref.dtype)


def _rmsnorm(x, gain, rows=256):
    m, d = x.shape
    rows = min(rows, m)
    return pl.pallas_call(
        _rmsnorm_kernel,
        grid=(m // rows,),
        in_specs=[pl.BlockSpec((rows, d), lambda i: (i, 0)),
                  pl.BlockSpec((1, d), lambda i: (0, 0))],
        out_specs=pl.BlockSpec((rows, d), lambda i: (i, 0)),
        out_shape=jax.ShapeDtypeStruct((m, d), BF16),
        compiler_params=_params("parallel"),
        name="rmsnorm",
    )(x, gain.reshape(1, d))


def _w_spec(w, rows, cols, index_map, layer):
    if w.ndim == 2:
        return pl.BlockSpec((rows, cols), index_map)
    return pl.BlockSpec((None, rows, cols), lambda *g: (layer,) + tuple(index_map(*g)))


def _mm_plain_kernel(a_ref, w_ref, o_ref):
    acc = jnp.dot(a_ref[...], w_ref[...].astype(BF16), preferred_element_type=F32)
    o_ref[...] = acc.astype(o_ref.dtype)


def _mm_relu2_kernel(a_ref, w_ref, o_ref):
    acc = jnp.dot(a_ref[...], w_ref[...].astype(BF16), preferred_element_type=F32)
    r = jnp.maximum(acc, 0.0)
    o_ref[...] = (r * r).astype(o_ref.dtype)


def _matmul(kernel_fn, a, w, *, n=None, layer=0, out_dtype, tm=1024, tn=1024, name):
    m, k = a.shape
    n = w.shape[-1] if n is None else n
    tm = min(tm, m)
    tn = min(tn, n)
    return pl.pallas_call(
        kernel_fn,
        grid=(n // tn, m // tm),
        in_specs=[pl.BlockSpec((tm, k), lambda j, i: (i, 0)),
                  _w_spec(w, k, tn, lambda j, i: (0, j), layer)],
        out_specs=pl.BlockSpec((tm, tn), lambda j, i: (i, j)),
        out_shape=jax.ShapeDtypeStruct((m, n), out_dtype),
        compiler_params=_params("parallel", "arbitrary"),
        name=name,
    )(a, w)


def _oproj_kernel(a1_ref, a2_ref, w1_ref, w2_ref, x_ref, o_ref):
    acc = jnp.dot(a1_ref[...], w1_ref[...].astype(BF16), preferred_element_type=F32)
    acc += jnp.dot(a2_ref[...], w2_ref[...].astype(BF16), preferred_element_type=F32)
    o_ref[...] = x_ref[...] + acc


def _oproj(a1, a2, w, x, *, layer, tm=1024, tn=1024):
    m, kh = a1.shape
    n = w.shape[-1]
    tm = min(tm, m)
    return pl.pallas_call(
        _oproj_kernel,
        grid=(n // tn, m // tm),
        in_specs=[pl.BlockSpec((tm, kh), lambda j, i: (i, 0)),
                  pl.BlockSpec((tm, kh), lambda j, i: (i, 0)),
                  _w_spec(w, kh, tn, lambda j, i: (0, j), layer),
                  _w_spec(w, kh, tn, lambda j, i: (1, j), layer),
                  pl.BlockSpec((tm, tn), lambda j, i: (i, j))],
        out_specs=pl.BlockSpec((tm, tn), lambda j, i: (i, j)),
        out_shape=jax.ShapeDtypeStruct((m, n), F32),
        compiler_params=_params("parallel", "arbitrary"),
        name="oproj",
    )(a1, a2, w, w, x)


def _down_kernel(a_ref, w_ref, x_ref, o_ref, acc_ref):
    kk = pl.program_id(2)

    @pl.when(kk == 0)
    def _():
        acc_ref[...] = jnp.zeros_like(acc_ref)

    acc_ref[...] += jnp.dot(a_ref[...], w_ref[...].astype(BF16), preferred_element_type=F32)

    @pl.when(kk == pl.num_programs(2) - 1)
    def _():
        o_ref[...] = x_ref[...] + acc_ref[...]


def _down(a, w, x, *, layer, tm=1024, tn=1024, tk=2048):
    m, k = a.shape
    n = w.shape[-1]
    tm = min(tm, m)
    return pl.pallas_call(
        _down_kernel,
        grid=(m // tm, n // tn, k // tk),
        in_specs=[pl.BlockSpec((tm, tk), lambda i, j, kk: (i, kk)),
                  _w_spec(w, tk, tn, lambda i, j, kk: (kk, j), layer),
                  pl.BlockSpec((tm, tn), lambda i, j, kk: (i, j))],
        out_specs=pl.BlockSpec((tm, tn), lambda i, j, kk: (i, j)),
        out_shape=jax.ShapeDtypeStruct((m, n), F32),
        scratch_shapes=[pltpu.VMEM((tm, tn), F32)],
        compiler_params=_params("parallel", "arbitrary", "arbitrary"),
        name="down",
    )(a, w, x)


def _gla_kernel(q_ref, k_ref, v_ref, gate_ref, lr_ref, wg_ref, bg_ref, gn_ref,
                o_ref, state_ref, loga_ref, *, tb):
    @pl.when(pl.program_id(1) == 0)
    def _():
        state_ref[...] = jnp.zeros_like(state_ref)

    c = GLA_CHUNK
    row = lax.broadcasted_iota(jnp.int32, (c, c), 0)
    col = lax.broadcasted_iota(jnp.int32, (c, c), 1)
    causal = col <= row
    tri = jnp.where(causal, 1.0, 0.0).astype(BF16)
    tri2 = jnp.concatenate([tri, tri], axis=1)
    lane = lax.broadcasted_iota(jnp.int32, (c, LANES), 1)
    head_mask = (lane < GLA_DK, lane >= GLA_DK)

    r = GLA_GATE_RANK
    lr = lr_ref[...]
    lr_hi = lr.astype(BF16)
    lr_lo = (lr - lr_hi.astype(F32)).astype(BF16)
    lr_lane = lax.broadcasted_iota(jnp.int32, lr.shape, 1)
    lr_mix = jnp.where(jnp.logical_and(lr_lane >= r, lr_lane < 2 * r), lr_lo, lr_hi)
    wg = wg_ref[...]
    wg_hi = wg.astype(BF16)
    wg_lo = (wg - wg_hi.astype(F32)).astype(BF16)
    wg_mix = jnp.where(lax.broadcasted_iota(jnp.int32, wg.shape, 0) >= 2 * r, wg_lo, wg_hi)
    logits = jnp.dot(lr_mix, wg_mix, preferred_element_type=F32) + bg_ref[...]
    loga_ref[...] = _log_sigmoid(logits) * (1.0 / GLA_GATE_TAU)

    def chunk(ci, carry):
        r0 = pl.multiple_of(ci * c, c)
        rows = pl.ds(r0, c)
        log_a = loga_ref[rows, :]
        hi = log_a.astype(BF16)
        lo = (log_a - hi.astype(F32)).astype(BF16)
        bcum = jnp.dot(tri2, jnp.concatenate([hi, lo], axis=0), preferred_element_type=F32)
        b_last = bcum[c - 1:c, :]
        kk = k_ref[rows, :].astype(F32)
        q_e = q_ref[rows, :].astype(F32) * (GLA_DK ** -0.5) * jnp.exp(bcum)
        k_e = (kk * jnp.exp(-bcum)).astype(BF16)
        k_dec = (kk * jnp.exp(b_last - bcum)).astype(BF16)
        decay = jnp.exp(b_last)
        for p in range(GLA_HEADS // 2):
            psl = slice(p * LANES, (p + 1) * LANES)
            for hh in range(2):
                h = 2 * p + hh
                hsl = slice(h * GLA_DV, (h + 1) * GLA_DV)
                qm = jnp.where(head_mask[hh], q_e[:, psl], 0.0).astype(BF16)
                vb = v_ref[rows, hsl]
                s = lax.dot_general(qm, k_e[:, psl], NT_DIMS, preferred_element_type=F32)
                s = jnp.where(causal, s, 0.0)
                o = jnp.dot(s.astype(BF16), vb, preferred_element_type=F32)
                st = state_ref[h]
                o += lax.dot_general(qm, st.astype(BF16), NT_DIMS,
                                     preferred_element_type=F32)
                u_t = lax.dot_general(vb, k_dec[:, psl], TN_DIMS,
                                      preferred_element_type=F32)
                state_ref[h] = decay[:, psl] * st + u_t
                y = o * _rms_scale(o) * gn_ref[...]
                g = gate_ref[rows, hsl].astype(F32)
                y = y * (g * (1.0 / (1.0 + jnp.exp(-g))))
                o_ref[rows, hsl] = y.astype(o_ref.dtype)
        return carry

    lax.fori_loop(0, tb // c, chunk, 0, unroll=4)


def _gla(proj, lr, wg, bg, gn, *, batch, seq, tb=512):
    m = proj.shape[0]
    tb = min(tb, seq)
    nt = seq // tb
    hk = GLA_HEADS * GLA_DK
    hv = GLA_HEADS * GLA_DV
    row = lambda b, t: b * nt + t
    return pl.pallas_call(
        functools.partial(_gla_kernel, tb=tb),
        grid=(batch, nt),
        in_specs=[pl.BlockSpec((tb, hk), lambda b, t: (row(b, t), 0)),
                  pl.BlockSpec((tb, hk), lambda b, t: (row(b, t), 1)),
                  pl.BlockSpec((tb, hv), lambda b, t: (row(b, t), 1)),
                  pl.BlockSpec((tb, hv), lambda b, t: (row(b, t), 2)),
                  pl.BlockSpec((tb, LANES), lambda b, t: (row(b, t), 0)),
                  pl.BlockSpec((LANES, hk), lambda b, t: (0, 0)),
                  pl.BlockSpec((1, hk), lambda b, t: (0, 0)),
                  pl.BlockSpec((1, GLA_DV), lambda b, t: (0, 0))],
        out_specs=pl.BlockSpec((tb, hv), lambda b, t: (row(b, t), 0)),
        out_shape=jax.ShapeDtypeStruct((m, hv), BF16),
        scratch_shapes=[pltpu.VMEM((GLA_HEADS, GLA_DV, LANES), F32),
                        pltpu.VMEM((tb, hk), F32)],
        compiler_params=_params("parallel", "arbitrary"),
        name="gla",
    )(proj, proj, proj, proj, lr, wg, bg, gn)


def _sb_kernel(q_ref, k_ref, v_ref, w_ref, gq_ref, gk_ref, gn_ref, o_ref, qn_ref, kn_ref,
               *, tq, tk, seq):
    wmat = w_ref[...]
    nd = tq // tk
    rowi = lax.broadcasted_iota(jnp.int32, (tk, tk), 0)
    coli = lax.broadcasted_iota(jnp.int32, (tk, tk), 1)
    below_diag = jnp.concatenate([coli < rowi] * nd, axis=0)

    def head_norm(i, _):
        rows = pl.ds(pl.multiple_of(i * tq, tq), tq)
        y = q_ref[rows, :].astype(F32)
        qn_ref[rows, :] = (y * _rms_scale(y) * gq_ref[...]).astype(BF16)
        y = k_ref[rows, :].astype(F32)
        kn_ref[rows, :] = (y * _rms_scale(y) * gk_ref[...]).astype(BF16)
        return 0

    lax.fori_loop(0, seq // tq, head_norm, 0)

    group = lambda x, g: x[g * tk:(g + 1) * tk]

    def scores(q, q0, j):
        first = [q0 + (g - j) * tk for g in range(nd)]
        start = [pl.multiple_of(jnp.maximum(f, 0), tk) for f in first]
        z = jnp.concatenate(
            [lax.dot_general(group(q, g), kn_ref[pl.ds(start[g], tk), :], NT_DIMS,
                             preferred_element_type=F32) for g in range(nd)], axis=0)
        log_beta = jnp.minimum(z, 0.0) - jnp.log2(1.0 + jnp.exp2(-jnp.abs(z)))
        return first, start, log_beta, log_beta - z

    def suffix_sums(log_1m):
        hi = log_1m.astype(BF16)
        lo = (log_1m - hi.astype(F32)).astype(BF16)
        return jnp.dot(jnp.concatenate([hi, lo], axis=1), wmat, preferred_element_type=F32)

    def retire(carry, first):
        return jnp.concatenate(
            [jnp.where(first[g] >= 0, group(carry, g), SB_DEAD) for g in range(nd)], axis=0)

    def weigh(a, start):
        a = a.astype(BF16)
        return jnp.concatenate(
            [jnp.dot(group(a, g), v_ref[pl.ds(start[g], tk), :], preferred_element_type=F32)
             for g in range(nd)], axis=0)

    def sweep_pair(q, q0, j, carry, acc, mask):
        first0, start0, log_beta0, log_1m0 = scores(q, q0, j)
        first1, start1, log_beta1, log_1m1 = scores(q, q0, j + 1)
        if mask is not None:
            log_1m0 = jnp.where(mask, log_1m0, 0.0)
        s2_0 = suffix_sums(log_1m0)
        s2_1 = suffix_sums(log_1m1)
        if mask is None:
            carry = retire(carry, first0)
        a0 = jnp.exp2(log_beta0 + s2_0[:, :tk] + carry)
        if mask is not None:
            a0 = jnp.where(mask, a0, 0.0)
        carry = retire(carry + s2_0[:, tk:], first1)
        a1 = jnp.exp2(log_beta1 + s2_1[:, :tk] + carry)
        carry = carry + s2_1[:, tk:]
        return carry, acc + weigh(a0, start0) + weigh(a1, start1)

    def live(c):
        return (jnp.max(c) > -SB_EXIT_LOG2).astype(jnp.int32)

    def qblock(i, _):
        q0 = pl.multiple_of(i * tq, tq)
        q = qn_ref[pl.ds(q0, tq), :]
        carry = jnp.zeros((tq, tk), F32)
        acc = jnp.zeros((tq, SB_DH), F32)
        carry, acc = sweep_pair(q, q0, 0, carry, acc, below_diag)

        def cond(st):
            j, go, _, _ = st
            return jnp.logical_and(go > 0, j < (i + 1) * nd)

        def body(st):
            j, _, carry, acc = st
            carry, acc = sweep_pair(q, q0, j, carry, acc, None)
            return j + 2, live(carry), carry, acc

        _, _, _, acc = lax.while_loop(cond, body, (2, live(carry), carry, acc))
        o_ref[pl.ds(q0, tq), :] = (acc * _rms_scale(acc) * gn_ref[...]).astype(o_ref.dtype)
        return 0

    lax.fori_loop(0, seq // tq, qblock, 0)


def _sb(proj, gq, gk, gn, *, col0, batch, seq, tq=1024, tk=128):
    m = proj.shape[0]
    tq = min(tq, seq)
    assert tq % tk == 0 and seq % tq == 0
    jj = lax.broadcasted_iota(jnp.int32, (2 * tk, 2 * tk), 0) % tk
    ss = lax.broadcasted_iota(jnp.int32, (2 * tk, 2 * tk), 1)
    wmat = jnp.logical_or(ss >= tk, jj > ss).astype(BF16)
    vec = pl.BlockSpec((1, SB_DH), lambda b, h: (0, 0))
    return pl.pallas_call(
        functools.partial(_sb_kernel, tq=tq, tk=tk, seq=seq),
        grid=(batch, SB_HEADS),
        in_specs=[pl.BlockSpec((seq, SB_DH), lambda b, h: (b, col0 + h)),
                  pl.BlockSpec((seq, SB_DH), lambda b, h: (b, col0 + SB_HEADS + h)),
                  pl.BlockSpec((seq, SB_DH), lambda b, h: (b, col0 + 2 * SB_HEADS + h)),
                  pl.BlockSpec((2 * tk, 2 * tk), lambda b, h: (0, 0)),
                  vec, vec, vec],
        out_specs=pl.BlockSpec((seq, SB_DH), lambda b, h: (b, h)),
        out_shape=jax.ShapeDtypeStruct((m, SB_HEADS * SB_DH), BF16),
        scratch_shapes=[pltpu.VMEM((seq, SB_DH), BF16), pltpu.VMEM((seq, SB_DH), BF16)],
        compiler_params=_params("parallel", "parallel"),
        name="stickbreak",
    )(proj, proj, proj, wmat, gq, gk, gn)


def _layer(x, l, attn_norm, w_in, w_gate_up, b_gate, gla_out_norm, sb_q_norm, sb_k_norm,
           sb_out_norm, w_o, mlp_norm, w_up, w_down, *, batch, seq):
    hk = GLA_HEADS * GLA_DK
    hv = GLA_HEADS * GLA_DV
    n_gla = 2 * hk + 2 * hv
    pad = LANES - 3 * GLA_GATE_RANK
    w_gla = w_in[l, :, :n_gla]
    w_sb = w_in[l, :, n_gla + GLA_GATE_RANK:]
    w_lr = jnp.pad(jnp.tile(w_in[l, :, n_gla:n_gla + GLA_GATE_RANK], (1, 3)), ((0, 0), (0, pad)))
    wg = jnp.pad(jnp.tile(w_gate_up[l], (3, 1)), ((0, pad), (0, 0)))
    gq = (sb_q_norm[l] * (math.log2(math.e) / math.sqrt(SB_DH))).reshape(1, SB_DH)

    h = _rmsnorm(x, attn_norm[l])
    p_gla = _matmul(_mm_plain_kernel, h, w_gla, out_dtype=BF16, name="inproj_gla")
    p_sb = _matmul(_mm_plain_kernel, h, w_sb, out_dtype=BF16, name="inproj_sb")
    lr = _matmul(_mm_plain_kernel, h, w_lr, out_dtype=F32, name="inproj_lr")
    o_gla = _gla(p_gla, lr, wg, b_gate[l].reshape(1, hk), gla_out_norm[l].reshape(1, GLA_DV),
                 batch=batch, seq=seq)
    o_sb = _sb(p_sb, gq, sb_k_norm[l].reshape(1, SB_DH), sb_out_norm[l].reshape(1, SB_DH),
               col0=0, batch=batch, seq=seq)
    x = _oproj(o_gla, o_sb, w_o, x, layer=l)
    hm = _rmsnorm(x, mlp_norm[l])
    up = _matmul(_mm_relu2_kernel, hm, w_up, layer=l, out_dtype=BF16, name="mlp_up")
    return _down(up, w_down, x, layer=l)


def kernel(x, attn_norm, w_in, w_gate_up, b_gate, gla_out_norm, sb_q_norm, sb_k_norm,
           sb_out_norm, w_o, mlp_norm, w_up, w_down):
    batch, seq, d = x.shape
    y = x.reshape(batch * seq, d)
    for l in range(w_in.shape[0]):
        y = _layer(y, l, attn_norm, w_in, w_gate_up, b_gate, gla_out_norm, sb_q_norm,
                   sb_k_norm, sb_out_norm, w_o, mlp_norm, w_up, w_down, batch=batch, seq=seq)
    return y.reshape(batch, seq, d)
```

```python
import functools
import math

import jax
import jax.numpy as jnp
from jax import lax
from jax.experimental import pallas as pl
from jax.experimental.pallas import tpu as pltpu

EPS = 1e-6
LANES = 128

GLA_HEADS = 8
GLA_DK = 64
GLA_DV = 128
GLA_GATE_RANK = 16
GLA_GATE_TAU = 16.0
GLA_CHUNK = 64
SB_HEADS = 8
SB_DH = 128

SB_EXIT_LOG2 = 126.0
SB_DEAD = -1e30

VMEM_LIMIT = 56 * 1024 * 1024

F32 = jnp.float32
BF16 = jnp.bfloat16
NT_DIMS = (((1,), (1,)), ((), ()))
TN_DIMS = (((0,), (0,)), ((), ()))


def _params(*sem):
    return pltpu.CompilerParams(dimension_semantics=sem, vmem_limit_bytes=VMEM_LIMIT)


def _log_sigmoid(z):
    return jnp.minimum(z, 0.0) - jnp.log(1.0 + jnp.exp(-jnp.abs(z)))


def _rms_scale(y):
    return lax.rsqrt(jnp.mean(y * y, axis=-1, keepdims=True) + EPS)


def _sumsq(y):
    return jnp.broadcast_to(jnp.sum(y * y, axis=-1, keepdims=True), (y.shape[0], LANES))


def _row_scale(ss_ref, d):
    return lax.rsqrt(jnp.sum(ss_ref[...], axis=0) * (1.0 / d) + EPS)


def _emit_normed(y, g_ref, o_ref, xb_ref, ss_ref):
    o_ref[...] = y
    xb_ref[...] = (y * g_ref[...]).astype(BF16)
    ss_ref[...] = _sumsq(y)


def _prenorm_kernel(x_ref, g_ref, xb_ref, ss_ref):
    x = x_ref[...]
    xb_ref[...] = (x * g_ref[...]).astype(BF16)
    ss_ref[...] = _sumsq(x)


def _prenorm(x, gain, rows=256):
    m, d = x.shape
    rows = min(rows, m)
    return pl.pallas_call(
        _prenorm_kernel,
        grid=(m // rows,),
        in_specs=[pl.BlockSpec((rows, d), lambda i: (i, 0)),
                  pl.BlockSpec((1, d), lambda i: (0, 0))],
        out_specs=[pl.BlockSpec((rows, d), lambda i: (i, 0)),
                   pl.BlockSpec((None, rows, LANES), lambda i: (0, i, 0))],
        out_shape=[jax.ShapeDtypeStruct((m, d), BF16),
                   jax.ShapeDtypeStruct((1, m, LANES), F32)],
        compiler_params=_params("parallel"),
        name="prenorm",
    )(x, gain.reshape(1, d))


def _w_spec(w, rows, cols, index_map, layer):
    if w.ndim == 2:
        return pl.BlockSpec((rows, cols), index_map)
    return pl.BlockSpec((None, rows, cols), lambda *g: (layer,) + tuple(index_map(*g)))


def _mm_kernel(a_ref, w_ref, ss_ref, o_ref, *, post):
    acc = jnp.dot(a_ref[...], w_ref[...].astype(BF16), preferred_element_type=F32)
    r = _row_scale(ss_ref, a_ref.shape[1])
    for c in range(acc.shape[1] // LANES):
        sl = slice(c * LANES, (c + 1) * LANES)
        o_ref[:, sl] = post(acc[:, sl] * r).astype(o_ref.dtype)


def _identity(y):
    return y


def _relu2(y):
    return jnp.square(jnp.maximum(y, 0.0))


def _matmul(post, a, ss, w, *, layer=0, out_dtype, tm=1024, tn=1024, name):
    m, k = a.shape
    n = w.shape[-1]
    tm = min(tm, m)
    tn = min(tn, n)
    return pl.pallas_call(
        functools.partial(_mm_kernel, post=post),
        grid=(n // tn, m // tm),
        in_specs=[pl.BlockSpec((tm, k), lambda j, i: (i, 0)),
                  _w_spec(w, k, tn, lambda j, i: (0, j), layer),
                  pl.BlockSpec((ss.shape[0], tm, LANES), lambda j, i: (0, i, 0))],
        out_specs=pl.BlockSpec((tm, tn), lambda j, i: (i, j)),
        out_shape=jax.ShapeDtypeStruct((m, n), out_dtype),
        compiler_params=_params("parallel", "arbitrary"),
        name=name,
    )(a, w, ss)


def _normed_out(m, n, tm, tn, tile_map, slot_map):
    specs = [pl.BlockSpec((tm, tn), tile_map), pl.BlockSpec((tm, tn), tile_map),
             pl.BlockSpec((None, tm, LANES), slot_map)]
    shapes = [jax.ShapeDtypeStruct((m, n), F32), jax.ShapeDtypeStruct((m, n), BF16),
              jax.ShapeDtypeStruct((n // tn, m, LANES), F32)]
    return specs, shapes


def _oproj_kernel(a1_ref, a2_ref, w1_ref, w2_ref, x_ref, g_ref, o_ref, xb_ref, ss_ref):
    acc = jnp.dot(a1_ref[...], w1_ref[...].astype(BF16), preferred_element_type=F32)
    acc += jnp.dot(a2_ref[...], w2_ref[...].astype(BF16), preferred_element_type=F32)
    _emit_normed(x_ref[...] + acc, g_ref, o_ref, xb_ref, ss_ref)


def _oproj(a1, a2, w, x, gain, *, layer, tm=1024, tn=1024):
    m, kh = a1.shape
    n = w.shape[-1]
    tm = min(tm, m)
    out_specs, out_shape = _normed_out(m, n, tm, tn, lambda j, i: (i, j), lambda j, i: (j, i, 0))
    return pl.pallas_call(
        _oproj_kernel,
        grid=(n // tn, m // tm),
        in_specs=[pl.BlockSpec((tm, kh), lambda j, i: (i, 0)),
                  pl.BlockSpec((tm, kh), lambda j, i: (i, 0)),
                  _w_spec(w, kh, tn, lambda j, i: (0, j), layer),
                  _w_spec(w, kh, tn, lambda j, i: (1, j), layer),
                  pl.BlockSpec((tm, tn), lambda j, i: (i, j)),
                  pl.BlockSpec((1, tn), lambda j, i: (0, j))],
        out_specs=out_specs,
        out_shape=out_shape,
        compiler_params=_params("parallel", "arbitrary"),
        name="oproj",
    )(a1, a2, w, w, x, gain.reshape(1, n))


def _down_kernel(a_ref, w_ref, x_ref, *rest, normed):
    acc_ref = rest[-1]
    kk = pl.program_id(2)

    @pl.when(kk == 0)
    def _():
        acc_ref[...] = jnp.zeros_like(acc_ref)

    acc_ref[...] += jnp.dot(a_ref[...], w_ref[...].astype(BF16), preferred_element_type=F32)

    @pl.when(kk == pl.num_programs(2) - 1)
    def _():
        y = x_ref[...] + acc_ref[...]
        if normed:
            _emit_normed(y, *rest[:-1])
        else:
            rest[0][...] = y


def _down(a, w, x, gain, *, layer, tm=1024, tn=1024, tk=2048):
    m, k = a.shape
    n = w.shape[-1]
    tm = min(tm, m)
    tile_map = lambda i, j, kk: (i, j)
    in_specs = [pl.BlockSpec((tm, tk), lambda i, j, kk: (i, kk)),
                _w_spec(w, tk, tn, lambda i, j, kk: (kk, j), layer),
                pl.BlockSpec((tm, tn), tile_map)]
    args = [a, w, x]
    if gain is None:
        out_specs = pl.BlockSpec((tm, tn), tile_map)
        out_shape = jax.ShapeDtypeStruct((m, n), F32)
    else:
        in_specs.append(pl.BlockSpec((1, tn), lambda i, j, kk: (0, j)))
        args.append(gain.reshape(1, n))
        out_specs, out_shape = _normed_out(m, n, tm, tn, tile_map, lambda i, j, kk: (j, i, 0))
    return pl.pallas_call(
        functools.partial(_down_kernel, normed=gain is not None),
        grid=(m // tm, n // tn, k // tk),
        in_specs=in_specs,
        out_specs=out_specs,
        out_shape=out_shape,
        scratch_shapes=[pltpu.VMEM((tm, tn), F32)],
        compiler_params=_params("parallel", "arbitrary", "arbitrary"),
        name="down",
    )(*args)


def _gla_kernel(q_ref, k_ref, v_ref, gate_ref, lr_ref, wg_ref, bg_ref, gn_ref,
                o_ref, state_ref, loga_ref, *, tb):
    @pl.when(pl.program_id(1) == 0)
    def _():
        state_ref[...] = jnp.zeros_like(state_ref)

    c = GLA_CHUNK
    row = lax.broadcasted_iota(jnp.int32, (c, c), 0)
    col = lax.broadcasted_iota(jnp.int32, (c, c), 1)
    causal = col <= row
    tri = jnp.where(causal, 1.0, 0.0).astype(BF16)
    tri2 = jnp.concatenate([tri, tri], axis=1)
    lane = lax.broadcasted_iota(jnp.int32, (c, LANES), 1)
    head_mask = (lane < GLA_DK, lane >= GLA_DK)

    r = GLA_GATE_RANK
    lr = lr_ref[...]
    lr_hi = lr.astype(BF16)
    lr_lo = (lr - lr_hi.astype(F32)).astype(BF16)
    lr_lane = lax.broadcasted_iota(jnp.int32, lr.shape, 1)
    lr_mix = jnp.where(jnp.logical_and(lr_lane >= r, lr_lane < 2 * r), lr_lo, lr_hi)
    wg = wg_ref[...]
    wg_hi = wg.astype(BF16)
    wg_lo = (wg - wg_hi.astype(F32)).astype(BF16)
    wg_mix = jnp.where(lax.broadcasted_iota(jnp.int32, wg.shape, 0) >= 2 * r, wg_lo, wg_hi)
    logits = jnp.dot(lr_mix, wg_mix, preferred_element_type=F32) + bg_ref[...]
    loga_ref[...] = _log_sigmoid(logits) * (1.0 / GLA_GATE_TAU)

    def chunk(ci, carry):
        r0 = pl.multiple_of(ci * c, c)
        rows = pl.ds(r0, c)
        log_a = loga_ref[rows, :]
        hi = log_a.astype(BF16)
        lo = (log_a - hi.astype(F32)).astype(BF16)
        bcum = jnp.dot(tri2, jnp.concatenate([hi, lo], axis=0), preferred_element_type=F32)
        b_last = bcum[c - 1:c, :]
        kk = k_ref[rows, :].astype(F32)
        q_e = q_ref[rows, :].astype(F32) * (GLA_DK ** -0.5) * jnp.exp(bcum)
        k_e = (kk * jnp.exp(-bcum)).astype(BF16)
        k_dec = (kk * jnp.exp(b_last - bcum)).astype(BF16)
        decay = jnp.exp(b_last)
        for p in range(GLA_HEADS // 2):
            psl = slice(p * LANES, (p + 1) * LANES)
            for hh in range(2):
                h = 2 * p + hh
                hsl = slice(h * GLA_DV, (h + 1) * GLA_DV)
                qm = jnp.where(head_mask[hh], q_e[:, psl], 0.0).astype(BF16)
                vb = v_ref[rows, hsl]
                s = lax.dot_general(qm, k_e[:, psl], NT_DIMS, preferred_element_type=F32)
                s = jnp.where(causal, s, 0.0)
                o = jnp.dot(s.astype(BF16), vb, preferred_element_type=F32)
                st = state_ref[h]
                o += lax.dot_general(qm, st.astype(BF16), NT_DIMS,
                                     preferred_element_type=F32)
                u_t = lax.dot_general(vb, k_dec[:, psl], TN_DIMS,
                                      preferred_element_type=F32)
                state_ref[h] = decay[:, psl] * st + u_t
                y = o * _rms_scale(o) * gn_ref[...]
                g = gate_ref[rows, hsl].astype(F32)
                y = y * (g * (1.0 / (1.0 + jnp.exp(-g))))
                o_ref[rows, hsl] = y.astype(o_ref.dtype)
        return carry

    lax.fori_loop(0, tb // c, chunk, 0, unroll=4)


def _gla(proj, lr, wg, bg, gn, *, batch, seq, tb=512):
    m = proj.shape[0]
    tb = min(tb, seq)
    nt = seq // tb
    hk = GLA_HEADS * GLA_DK
    hv = GLA_HEADS * GLA_DV
    row = lambda b, t: b * nt + t
    return pl.pallas_call(
        functools.partial(_gla_kernel, tb=tb),
        grid=(batch, nt),
        in_specs=[pl.BlockSpec((tb, hk), lambda b, t: (row(b, t), 0)),
                  pl.BlockSpec((tb, hk), lambda b, t: (row(b, t), 1)),
                  pl.BlockSpec((tb, hv), lambda b, t: (row(b, t), 1)),
                  pl.BlockSpec((tb, hv), lambda b, t: (row(b, t), 2)),
                  pl.BlockSpec((tb, LANES), lambda b, t: (row(b, t), 0)),
                  pl.BlockSpec((LANES, hk), lambda b, t: (0, 0)),
                  pl.BlockSpec((1, hk), lambda b, t: (0, 0)),
                  pl.BlockSpec((1, GLA_DV), lambda b, t: (0, 0))],
        out_specs=pl.BlockSpec((tb, hv), lambda b, t: (row(b, t), 0)),
        out_shape=jax.ShapeDtypeStruct((m, hv), BF16),
        scratch_shapes=[pltpu.VMEM((GLA_HEADS, GLA_DV, LANES), F32),
                        pltpu.VMEM((tb, hk), F32)],
        compiler_params=_params("parallel", "arbitrary"),
        name="gla",
    )(proj, proj, proj, proj, lr, wg, bg, gn)


def _sb_kernel(q_ref, k_ref, v_ref, w_ref, gq_ref, gk_ref, gn_ref, o_ref, qn_ref, kn_ref,
               *, tq, tk, seq):
    wmat = w_ref[...]
    nd = tq // tk
    rowi = lax.broadcasted_iota(jnp.int32, (tk, tk), 0)
    coli = lax.broadcasted_iota(jnp.int32, (tk, tk), 1)
    below_diag = jnp.concatenate([coli < rowi] * nd, axis=0)

    def head_norm(i, _):
        rows = pl.ds(pl.multiple_of(i * tq, tq), tq)
        y = q_ref[rows, :].astype(F32)
        qn_ref[rows, :] = (y * _rms_scale(y) * gq_ref[...]).astype(BF16)
        y = k_ref[rows, :].astype(F32)
        kn_ref[rows, :] = (y * _rms_scale(y) * gk_ref[...]).astype(BF16)
        return 0

    lax.fori_loop(0, seq // tq, head_norm, 0)

    group = lambda x, g: x[g * tk:(g + 1) * tk]

    def scores(q, q0, j):
        first = [q0 + (g - j) * tk for g in range(nd)]
        start = [pl.multiple_of(jnp.maximum(f, 0), tk) for f in first]
        z = jnp.concatenate(
            [lax.dot_general(group(q, g), kn_ref[pl.ds(start[g], tk), :], NT_DIMS,
                             preferred_element_type=F32) for g in range(nd)], axis=0)
        log_beta = jnp.minimum(z, 0.0) - jnp.log2(1.0 + jnp.exp2(-jnp.abs(z)))
        return first, start, log_beta, log_beta - z

    def suffix_sums(log_1m):
        hi = log_1m.astype(BF16)
        lo = (log_1m - hi.astype(F32)).astype(BF16)
        return jnp.dot(jnp.concatenate([hi, lo], axis=1), wmat, preferred_element_type=F32)

    def retire(carry, first):
        return jnp.concatenate(
            [jnp.where(first[g] >= 0, group(carry, g), SB_DEAD) for g in range(nd)], axis=0)

    def weigh(a, start):
        a = a.astype(BF16)
        return jnp.concatenate(
            [jnp.dot(group(a, g), v_ref[pl.ds(start[g], tk), :], preferred_element_type=F32)
             for g in range(nd)], axis=0)

    def sweep_pair(q, q0, j, carry, acc, mask):
        first0, start0, log_beta0, log_1m0 = scores(q, q0, j)
        first1, start1, log_beta1, log_1m1 = scores(q, q0, j + 1)
        if mask is not None:
            log_1m0 = jnp.where(mask, log_1m0, 0.0)
        s2_0 = suffix_sums(log_1m0)
        s2_1 = suffix_sums(log_1m1)
        if mask is None:
            carry = retire(carry, first0)
        a0 = jnp.exp2(log_beta0 + s2_0[:, :tk] + carry)
        if mask is not None:
            a0 = jnp.where(mask, a0, 0.0)
        carry = retire(carry + s2_0[:, tk:], first1)
        a1 = jnp.exp2(log_beta1 + s2_1[:, :tk] + carry)
        carry = carry + s2_1[:, tk:]
        return carry, acc + weigh(a0, start0) + weigh(a1, start1)

    def live(c):
        return (jnp.max(c) > -SB_EXIT_LOG2).astype(jnp.int32)

    def qblock(i, _):
        q0 = pl.multiple_of(i * tq, tq)
        q = qn_ref[pl.ds(q0, tq), :]
        carry = jnp.zeros((tq, tk), F32)
        acc = jnp.zeros((tq, SB_DH), F32)
        carry, acc = sweep_pair(q, q0, 0, carry, acc, below_diag)

        def cond(st):
            j, go, _, _ = st
            return jnp.logical_and(go > 0, j < (i + 1) * nd)

        def body(st):
            j, _, carry, acc = st
            carry, acc = sweep_pair(q, q0, j, carry, acc, None)
            return j + 2, live(carry), carry, acc

        _, _, _, acc = lax.while_loop(cond, body, (2, live(carry), carry, acc))
        o_ref[pl.ds(q0, tq), :] = (acc * _rms_scale(acc) * gn_ref[...]).astype(o_ref.dtype)
        return 0

    lax.fori_loop(0, seq // tq, qblock, 0)


def _sb(proj, gq, gk, gn, *, col0, batch, seq, tq=1024, tk=128):
    m = proj.shape[0]
    tq = min(tq, seq)
    assert tq % tk == 0 and seq % tq == 0
    jj = lax.broadcasted_iota(jnp.int32, (2 * tk, 2 * tk), 0) % tk
    ss = lax.broadcasted_iota(jnp.int32, (2 * tk, 2 * tk), 1)
    wmat = jnp.logical_or(ss >= tk, jj > ss).astype(BF16)
    vec = pl.BlockSpec((1, SB_DH), lambda b, h: (0, 0))
    return pl.pallas_call(
        functools.partial(_sb_kernel, tq=tq, tk=tk, seq=seq),
        grid=(batch, SB_HEADS),
        in_specs=[pl.BlockSpec((seq, SB_DH), lambda b, h: (b, col0 + h)),
                  pl.BlockSpec((seq, SB_DH), lambda b, h: (b, col0 + SB_HEADS + h)),
                  pl.BlockSpec((seq, SB_DH), lambda b, h: (b, col0 + 2 * SB_HEADS + h)),
                  pl.BlockSpec((2 * tk, 2 * tk), lambda b, h: (0, 0)),
                  vec, vec, vec],
        out_specs=pl.BlockSpec((seq, SB_DH), lambda b, h: (b, h)),
        out_shape=jax.ShapeDtypeStruct((m, SB_HEADS * SB_DH), BF16),
        scratch_shapes=[pltpu.VMEM((seq, SB_DH), BF16), pltpu.VMEM((seq, SB_DH), BF16)],
        compiler_params=_params("parallel", "parallel"),
        name="stickbreak",
    )(proj, proj, proj, wmat, gq, gk, gn)


def _layer(x, xb, ss, l, next_gain, w_in, w_gate_up, b_gate, gla_out_norm, sb_q_norm,
           sb_k_norm, sb_out_norm, w_o, mlp_norm, w_up, w_down, *, batch, seq):
    hk = GLA_HEADS * GLA_DK
    hv = GLA_HEADS * GLA_DV
    n_gla = 2 * hk + 2 * hv
    pad = LANES - 3 * GLA_GATE_RANK
    w_gla = w_in[l, :, :n_gla]
    w_sb = w_in[l, :, n_gla + GLA_GATE_RANK:]
    w_lr = jnp.pad(jnp.tile(w_in[l, :, n_gla:n_gla + GLA_GATE_RANK], (1, 3)), ((0, 0), (0, pad)))
    wg = jnp.pad(jnp.tile(w_gate_up[l], (3, 1)), ((0, pad), (0, 0)))
    gq = (sb_q_norm[l] * (math.log2(math.e) / math.sqrt(SB_DH))).reshape(1, SB_DH)

    p_gla = _matmul(_identity, xb, ss, w_gla, out_dtype=BF16, name="inproj_gla")
    p_sb = _matmul(_identity, xb, ss, w_sb, out_dtype=BF16, name="inproj_sb")
    lr = _matmul(_identity, xb, ss, w_lr, out_dtype=F32, name="inproj_lr")
    o_gla = _gla(p_gla, lr, wg, b_gate[l].reshape(1, hk), gla_out_norm[l].reshape(1, GLA_DV),
                 batch=batch, seq=seq)
    o_sb = _sb(p_sb, gq, sb_k_norm[l].reshape(1, SB_DH), sb_out_norm[l].reshape(1, SB_DH),
               col0=0, batch=batch, seq=seq)
    x, xb, ss = _oproj(o_gla, o_sb, w_o, x, mlp_norm[l], layer=l)
    up = _matmul(_relu2, xb, ss, w_up, layer=l, out_dtype=BF16, name="mlp_up")
    if next_gain is None:
        return _down(up, w_down, x, None, layer=l), None, None
    return _down(up, w_down, x, next_gain, layer=l)


def kernel(x, attn_norm, w_in, w_gate_up, b_gate, gla_out_norm, sb_q_norm, sb_k_norm,
           sb_out_norm, w_o, mlp_norm, w_up, w_down):
    batch, seq, d = x.shape
    depth = w_in.shape[0]
    y = x.reshape(batch * seq, d)
    yb, ss = _prenorm(y, attn_norm[0])
    for l in range(depth):
        next_gain = attn_norm[l + 1] if l + 1 < depth else None
        y, yb, ss = _layer(y, yb, ss, l, next_gain, w_in, w_gate_up, b_gate, gla_out_norm,
                           sb_q_norm, sb_k_norm, sb_out_norm, w_o, mlp_norm, w_up, w_down,
                           batch=batch, seq=seq)
    return y.reshape(batch, seq, d)
```

```python
import functools
import math

import jax
import jax.numpy as jnp
from jax import lax
from jax.experimental import pallas as pl
from jax.experimental.pallas import tpu as pltpu

EPS = 1e-6
LANES = 128

GLA_HEADS = 8
GLA_DK = 64
GLA_DV = 128
GLA_GATE_RANK = 16
GLA_GATE_TAU = 16.0
GLA_CHUNK = 64
SB_HEADS = 8
SB_DH = 128

SB_EXIT_LOG2 = 126.0
SB_DEAD = -1e30

VMEM_LIMIT = 56 * 1024 * 1024
DOWN_ROW_CHUNKS = 4

F32 = jnp.float32
BF16 = jnp.bfloat16
NT_DIMS = (((1,), (1,)), ((), ()))
TN_DIMS = (((0,), (0,)), ((), ()))


def _params(*sem):
    return pltpu.CompilerParams(dimension_semantics=sem, vmem_limit_bytes=VMEM_LIMIT)


def _log_sigmoid(z):
    return jnp.minimum(z, 0.0) - jnp.log(1.0 + jnp.exp(-jnp.abs(z)))


def _rms_scale(y):
    return lax.rsqrt(jnp.mean(y * y, axis=-1, keepdims=True) + EPS)


def _sumsq(y):
    return jnp.broadcast_to(jnp.sum(y * y, axis=-1, keepdims=True), (y.shape[0], LANES))


def _row_scale(ss_ref, d):
    return lax.rsqrt(jnp.sum(ss_ref[...], axis=0) * (1.0 / d) + EPS)


def _emit_normed(y, g_ref, o_ref, xb_ref, ss_ref):
    o_ref[...] = y
    xb_ref[...] = (y * g_ref[...]).astype(BF16)
    ss_ref[...] = _sumsq(y)


def _prenorm_kernel(x_ref, g_ref, xb_ref, ss_ref):
    x = x_ref[...]
    xb_ref[...] = (x * g_ref[...]).astype(BF16)
    ss_ref[...] = _sumsq(x)


def _prenorm(x, gain, rows=256):
    m, d = x.shape
    rows = min(rows, m)
    return pl.pallas_call(
        _prenorm_kernel,
        grid=(m // rows,),
        in_specs=[pl.BlockSpec((rows, d), lambda i: (i, 0)),
                  pl.BlockSpec((1, d), lambda i: (0, 0))],
        out_specs=[pl.BlockSpec((rows, d), lambda i: (i, 0)),
                   pl.BlockSpec((None, rows, LANES), lambda i: (0, i, 0))],
        out_shape=[jax.ShapeDtypeStruct((m, d), BF16),
                   jax.ShapeDtypeStruct((1, m, LANES), F32)],
        compiler_params=_params("parallel"),
        name="prenorm",
    )(x, gain.reshape(1, d))


def _w_spec(w, rows, cols, index_map, layer):
    if w.ndim == 2:
        return pl.BlockSpec((rows, cols), index_map)
    return pl.BlockSpec((None, rows, cols), lambda *g: (layer,) + tuple(index_map(*g)))


def _mm_kernel(a_ref, w_ref, ss_ref, o_ref, *, post):
    acc = jnp.dot(a_ref[...], w_ref[...].astype(BF16), preferred_element_type=F32)
    r = _row_scale(ss_ref, a_ref.shape[1])
    for c in range(acc.shape[1] // LANES):
        sl = slice(c * LANES, (c + 1) * LANES)
        o_ref[:, sl] = post(acc[:, sl] * r).astype(o_ref.dtype)


def _identity(y):
    return y


def _relu2(y):
    return jnp.square(jnp.maximum(y, 0.0))


def _matmul(post, a, ss, w, *, layer=0, out_dtype, tm=1024, tn=1024, name):
    m, k = a.shape
    n = w.shape[-1]
    tm = min(tm, m)
    tn = min(tn, n)
    return pl.pallas_call(
        functools.partial(_mm_kernel, post=post),
        grid=(n // tn, m // tm),
        in_specs=[pl.BlockSpec((tm, k), lambda j, i: (i, 0)),
                  _w_spec(w, k, tn, lambda j, i: (0, j), layer),
                  pl.BlockSpec((ss.shape[0], tm, LANES), lambda j, i: (0, i, 0))],
        out_specs=pl.BlockSpec((tm, tn), lambda j, i: (i, j)),
        out_shape=jax.ShapeDtypeStruct((m, n), out_dtype),
        compiler_params=_params("parallel", "arbitrary"),
        name=name,
    )(a, w, ss)


def _normed_out(m, n, tm, tn, tile_map, slot_map):
    specs = [pl.BlockSpec((tm, tn), tile_map), pl.BlockSpec((tm, tn), tile_map),
             pl.BlockSpec((None, tm, LANES), slot_map)]
    shapes = [jax.ShapeDtypeStruct((m, n), F32), jax.ShapeDtypeStruct((m, n), BF16),
              jax.ShapeDtypeStruct((n // tn, m, LANES), F32)]
    return specs, shapes


def _oproj_kernel(a1_ref, a2_ref, w1_ref, w2_ref, x_ref, g_ref, o_ref, xb_ref, ss_ref):
    acc = jnp.dot(a1_ref[...], w1_ref[...].astype(BF16), preferred_element_type=F32)
    acc += jnp.dot(a2_ref[...], w2_ref[...].astype(BF16), preferred_element_type=F32)
    _emit_normed(x_ref[...] + acc, g_ref, o_ref, xb_ref, ss_ref)


def _oproj(a1, a2, w, x, gain, *, layer, tm=1024, tn=1024):
    m, kh = a1.shape
    n = w.shape[-1]
    tm = min(tm, m)
    out_specs, out_shape = _normed_out(m, n, tm, tn, lambda j, i: (i, j), lambda j, i: (j, i, 0))
    return pl.pallas_call(
        _oproj_kernel,
        grid=(n // tn, m // tm),
        in_specs=[pl.BlockSpec((tm, kh), lambda j, i: (i, 0)),
                  pl.BlockSpec((tm, kh), lambda j, i: (i, 0)),
                  _w_spec(w, kh, tn, lambda j, i: (0, j), layer),
                  _w_spec(w, kh, tn, lambda j, i: (1, j), layer),
                  pl.BlockSpec((tm, tn), lambda j, i: (i, j)),
                  pl.BlockSpec((1, tn), lambda j, i: (0, j))],
        out_specs=out_specs,
        out_shape=out_shape,
        compiler_params=_params("parallel", "arbitrary"),
        name="oproj",
    )(a1, a2, w, w, x, gain.reshape(1, n))


def _down_kernel(a_ref, w_ref, x_hbm, *rest, normed):
    sem = rest[-1]
    o_ref = rest[1] if normed else rest[0]
    tm, tn = o_ref.shape
    i, j, kk = pl.program_id(0), pl.program_id(1), pl.program_id(2)

    def residual_copy():
        rows = pl.ds(pl.multiple_of(i * tm, tm), tm)
        cols = pl.ds(pl.multiple_of(j * tn, tn), tn)
        return pltpu.make_async_copy(x_hbm.at[rows, cols], o_ref, sem)

    def partial_sums():
        wb = w_ref[...].astype(BF16)
        rb = tm // DOWN_ROW_CHUNKS
        return [(slice(c * rb, (c + 1) * rb),
                 jnp.dot(a_ref[c * rb:(c + 1) * rb, :], wb, preferred_element_type=F32))
                for c in range(DOWN_ROW_CHUNKS)]

    @pl.when(kk == 0)
    def _():
        copy = residual_copy()
        copy.start()
        parts = partial_sums()
        copy.wait()
        for rows, part in parts:
            o_ref[rows, :] += part

    @pl.when(kk > 0)
    def _():
        for rows, part in partial_sums():
            o_ref[rows, :] += part

    if normed:
        @pl.when(kk == pl.num_programs(2) - 1)
        def _():
            g_ref, _, xb_ref, ss_ref = rest[:-1]
            y = o_ref[...]
            xb_ref[...] = (y * g_ref[...]).astype(BF16)
            ss_ref[...] = _sumsq(y)


def _down(a, w, x, gain, *, layer, tm=2048, tn=1024, tk=1024):
    m, k = a.shape
    n = w.shape[-1]
    tm = min(tm, m)
    tile_map = lambda i, j, kk: (i, j)
    in_specs = [pl.BlockSpec((tm, tk), lambda i, j, kk: (i, kk)),
                _w_spec(w, tk, tn, lambda i, j, kk: (kk, j), layer),
                pl.BlockSpec(memory_space=pl.ANY)]
    args = [a, w, x]
    if gain is None:
        out_specs = pl.BlockSpec((tm, tn), tile_map)
        out_shape = jax.ShapeDtypeStruct((m, n), F32)
    else:
        in_specs.append(pl.BlockSpec((1, tn), lambda i, j, kk: (0, j)))
        args.append(gain.reshape(1, n))
        out_specs, out_shape = _normed_out(m, n, tm, tn, tile_map, lambda i, j, kk: (j, i, 0))
    return pl.pallas_call(
        functools.partial(_down_kernel, normed=gain is not None),
        grid=(m // tm, n // tn, k // tk),
        in_specs=in_specs,
        out_specs=out_specs,
        out_shape=out_shape,
        scratch_shapes=[pltpu.SemaphoreType.DMA(())],
        compiler_params=_params("parallel", "arbitrary", "arbitrary"),
        name="down",
    )(*args)


def _gla_kernel(q_ref, k_ref, v_ref, gate_ref, lr_ref, wg_ref, bg_ref, gn_ref,
                o_ref, state_ref, loga_ref, *, tb):
    @pl.when(pl.program_id(1) == 0)
    def _():
        state_ref[...] = jnp.zeros_like(state_ref)

    c = GLA_CHUNK
    row = lax.broadcasted_iota(jnp.int32, (c, c), 0)
    col = lax.broadcasted_iota(jnp.int32, (c, c), 1)
    causal = col <= row
    tri = jnp.where(causal, 1.0, 0.0).astype(BF16)
    tri2 = jnp.concatenate([tri, tri], axis=1)
    lane = lax.broadcasted_iota(jnp.int32, (c, LANES), 1)
    head_mask = (lane < GLA_DK, lane >= GLA_DK)

    r = GLA_GATE_RANK
    lr = lr_ref[...]
    lr_hi = lr.astype(BF16)
    lr_lo = (lr - lr_hi.astype(F32)).astype(BF16)
    lr_lane = lax.broadcasted_iota(jnp.int32, lr.shape, 1)
    lr_mix = jnp.where(jnp.logical_and(lr_lane >= r, lr_lane < 2 * r), lr_lo, lr_hi)
    wg = wg_ref[...]
    wg_hi = wg.astype(BF16)
    wg_lo = (wg - wg_hi.astype(F32)).astype(BF16)
    wg_mix = jnp.where(lax.broadcasted_iota(jnp.int32, wg.shape, 0) >= 2 * r, wg_lo, wg_hi)
    logits = jnp.dot(lr_mix, wg_mix, preferred_element_type=F32) + bg_ref[...]
    loga_ref[...] = _log_sigmoid(logits) * (1.0 / GLA_GATE_TAU)

    def chunk(ci, carry):
        r0 = pl.multiple_of(ci * c, c)
        rows = pl.ds(r0, c)
        log_a = loga_ref[rows, :]
        hi = log_a.astype(BF16)
        lo = (log_a - hi.astype(F32)).astype(BF16)
        bcum = jnp.dot(tri2, jnp.concatenate([hi, lo], axis=0), preferred_element_type=F32)
        b_last = bcum[c - 1:c, :]
        kk = k_ref[rows, :].astype(F32)
        q_e = q_ref[rows, :].astype(F32) * (GLA_DK ** -0.5) * jnp.exp(bcum)
        k_e = (kk * jnp.exp(-bcum)).astype(BF16)
        k_dec = (kk * jnp.exp(b_last - bcum)).astype(BF16)
        decay = jnp.exp(b_last)
        for p in range(GLA_HEADS // 2):
            psl = slice(p * LANES, (p + 1) * LANES)
            for hh in range(2):
                h = 2 * p + hh
                hsl = slice(h * GLA_DV, (h + 1) * GLA_DV)
                qm = jnp.where(head_mask[hh], q_e[:, psl], 0.0).astype(BF16)
                vb = v_ref[rows, hsl]
                s = lax.dot_general(qm, k_e[:, psl], NT_DIMS, preferred_element_type=F32)
                s = jnp.where(causal, s, 0.0)
                o = jnp.dot(s.astype(BF16), vb, preferred_element_type=F32)
                st = state_ref[h]
                o += lax.dot_general(qm, st.astype(BF16), NT_DIMS,
                                     preferred_element_type=F32)
                u_t = lax.dot_general(vb, k_dec[:, psl], TN_DIMS,
                                      preferred_element_type=F32)
                state_ref[h] = decay[:, psl] * st + u_t
                y = o * _rms_scale(o) * gn_ref[...]
                g = gate_ref[rows, hsl].astype(F32)
                y = y * (g * (1.0 / (1.0 + jnp.exp(-g))))
                o_ref[rows, hsl] = y.astype(o_ref.dtype)
        return carry

    lax.fori_loop(0, tb // c, chunk, 0, unroll=4)


def _gla(proj, lr, wg, bg, gn, *, batch, seq, tb=512):
    m = proj.shape[0]
    tb = min(tb, seq)
    nt = seq // tb
    hk = GLA_HEADS * GLA_DK
    hv = GLA_HEADS * GLA_DV
    row = lambda b, t: b * nt + t
    return pl.pallas_call(
        functools.partial(_gla_kernel, tb=tb),
        grid=(batch, nt),
        in_specs=[pl.BlockSpec((tb, hk), lambda b, t: (row(b, t), 0)),
                  pl.BlockSpec((tb, hk), lambda b, t: (row(b, t), 1)),
                  pl.BlockSpec((tb, hv), lambda b, t: (row(b, t), 1)),
                  pl.BlockSpec((tb, hv), lambda b, t: (row(b, t), 2)),
                  pl.BlockSpec((tb, LANES), lambda b, t: (row(b, t), 0)),
                  pl.BlockSpec((LANES, hk), lambda b, t: (0, 0)),
                  pl.BlockSpec((1, hk), lambda b, t: (0, 0)),
                  pl.BlockSpec((1, GLA_DV), lambda b, t: (0, 0))],
        out_specs=pl.BlockSpec((tb, hv), lambda b, t: (row(b, t), 0)),
        out_shape=jax.ShapeDtypeStruct((m, hv), BF16),
        scratch_shapes=[pltpu.VMEM((GLA_HEADS, GLA_DV, LANES), F32),
                        pltpu.VMEM((tb, hk), F32)],
        compiler_params=_params("parallel", "arbitrary"),
        name="gla",
    )(proj, proj, proj, proj, lr, wg, bg, gn)


def _sb_kernel(q_ref, k_ref, v_ref, w_ref, gq_ref, gk_ref, gn_ref, o_ref, qn_ref, kn_ref,
               *, tq, tk, seq):
    wmat = w_ref[...]
    nd = tq // tk
    rowi = lax.broadcasted_iota(jnp.int32, (tk, tk), 0)
    coli = lax.broadcasted_iota(jnp.int32, (tk, tk), 1)
    below_diag = jnp.concatenate([coli < rowi] * nd, axis=0)

    def head_norm(i, _):
        rows = pl.ds(pl.multiple_of(i * tq, tq), tq)
        y = q_ref[rows, :].astype(F32)
        qn_ref[rows, :] = (y * _rms_scale(y) * gq_ref[...]).astype(BF16)
        y = k_ref[rows, :].astype(F32)
        kn_ref[rows, :] = (y * _rms_scale(y) * gk_ref[...]).astype(BF16)
        return 0

    lax.fori_loop(0, seq // tq, head_norm, 0)

    group = lambda x, g: x[g * tk:(g + 1) * tk]

    def scores(q, q0, j):
        first = [q0 + (g - j) * tk for g in range(nd)]
        start = [pl.multiple_of(jnp.maximum(f, 0), tk) for f in first]
        z = jnp.concatenate(
            [lax.dot_general(group(q, g), kn_ref[pl.ds(start[g], tk), :], NT_DIMS,
                             preferred_element_type=F32) for g in range(nd)], axis=0)
        log_beta = jnp.minimum(z, 0.0) - jnp.log2(1.0 + jnp.exp2(-jnp.abs(z)))
        return first, start, log_beta, log_beta - z

    def suffix_sums(log_1m):
        hi = log_1m.astype(BF16)
        lo = (log_1m - hi.astype(F32)).astype(BF16)
        return jnp.dot(jnp.concatenate([hi, lo], axis=1), wmat, preferred_element_type=F32)

    def retire(carry, first):
        return jnp.concatenate(
            [jnp.where(first[g] >= 0, group(carry, g), SB_DEAD) for g in range(nd)], axis=0)

    def weigh(a, start):
        a = a.astype(BF16)
        return jnp.concatenate(
            [jnp.dot(group(a, g), v_ref[pl.ds(start[g], tk), :], preferred_element_type=F32)
             for g in range(nd)], axis=0)

    def sweep_pair(q, q0, j, carry, acc, mask):
        first0, start0, log_beta0, log_1m0 = scores(q, q0, j)
        first1, start1, log_beta1, log_1m1 = scores(q, q0, j + 1)
        if mask is not None:
            log_1m0 = jnp.where(mask, log_1m0, 0.0)
        s2_0 = suffix_sums(log_1m0)
        s2_1 = suffix_sums(log_1m1)
        if mask is None:
            carry = retire(carry, first0)
        a0 = jnp.exp2(log_beta0 + s2_0[:, :tk] + carry)
        if mask is not None:
            a0 = jnp.where(mask, a0, 0.0)
        carry = retire(carry + s2_0[:, tk:], first1)
        a1 = jnp.exp2(log_beta1 + s2_1[:, :tk] + carry)
        carry = carry + s2_1[:, tk:]
        return carry, acc + weigh(a0, start0) + weigh(a1, start1)

    def live(c):
        return (jnp.max(c) > -SB_EXIT_LOG2).astype(jnp.int32)

    def qblock(i, _):
        q0 = pl.multiple_of(i * tq, tq)
        q = qn_ref[pl.ds(q0, tq), :]
        carry = jnp.zeros((tq, tk), F32)
        acc = jnp.zeros((tq, SB_DH), F32)
        carry, acc = sweep_pair(q, q0, 0, carry, acc, below_diag)

        def cond(st):
            j, go, _, _ = st
            return jnp.logical_and(go > 0, j < (i + 1) * nd)

        def body(st):
            j, _, carry, acc = st
            carry, acc = sweep_pair(q, q0, j, carry, acc, None)
            return j + 2, live(carry), carry, acc

        _, _, _, acc = lax.while_loop(cond, body, (2, live(carry), carry, acc))
        o_ref[pl.ds(q0, tq), :] = (acc * _rms_scale(acc) * gn_ref[...]).astype(o_ref.dtype)
        return 0

    lax.fori_loop(0, seq // tq, qblock, 0)


def _sb(proj, gq, gk, gn, *, col0, batch, seq, tq=1024, tk=128):
    m = proj.shape[0]
    tq = min(tq, seq)
    assert tq % tk == 0 and seq % tq == 0
    jj = lax.broadcasted_iota(jnp.int32, (2 * tk, 2 * tk), 0) % tk
    ss = lax.broadcasted_iota(jnp.int32, (2 * tk, 2 * tk), 1)
    wmat = jnp.logical_or(ss >= tk, jj > ss).astype(BF16)
    vec = pl.BlockSpec((1, SB_DH), lambda b, h: (0, 0))
    return pl.pallas_call(
        functools.partial(_sb_kernel, tq=tq, tk=tk, seq=seq),
        grid=(batch, SB_HEADS),
        in_specs=[pl.BlockSpec((seq, SB_DH), lambda b, h: (b, col0 + h)),
                  pl.BlockSpec((seq, SB_DH), lambda b, h: (b, col0 + SB_HEADS + h)),
                  pl.BlockSpec((seq, SB_DH), lambda b, h: (b, col0 + 2 * SB_HEADS + h)),
                  pl.BlockSpec((2 * tk, 2 * tk), lambda b, h: (0, 0)),
                  vec, vec, vec],
        out_specs=pl.BlockSpec((seq, SB_DH), lambda b, h: (b, h)),
        out_shape=jax.ShapeDtypeStruct((m, SB_HEADS * SB_DH), BF16),
        scratch_shapes=[pltpu.VMEM((seq, SB_DH), BF16), pltpu.VMEM((seq, SB_DH), BF16)],
        compiler_params=_params("parallel", "parallel"),
        name="stickbreak",
    )(proj, proj, proj, wmat, gq, gk, gn)


def _layer(x, xb, ss, l, next_gain, w_in, w_gate_up, b_gate, gla_out_norm, sb_q_norm,
           sb_k_norm, sb_out_norm, w_o, mlp_norm, w_up, w_down, *, batch, seq):
    hk = GLA_HEADS * GLA_DK
    hv = GLA_HEADS * GLA_DV
    n_gla = 2 * hk + 2 * hv
    pad = LANES - 3 * GLA_GATE_RANK
    w_gla = w_in[l, :, :n_gla]
    w_sb = w_in[l, :, n_gla + GLA_GATE_RANK:]
    w_lr = jnp.pad(jnp.tile(w_in[l, :, n_gla:n_gla + GLA_GATE_RANK], (1, 3)), ((0, 0), (0, pad)))
    wg = jnp.pad(jnp.tile(w_gate_up[l], (3, 1)), ((0, pad), (0, 0)))
    gq = (sb_q_norm[l] * (math.log2(math.e) / math.sqrt(SB_DH))).reshape(1, SB_DH)

    p_gla = _matmul(_identity, xb, ss, w_gla, out_dtype=BF16, name="inproj_gla")
    p_sb = _matmul(_identity, xb, ss, w_sb, out_dtype=BF16, name="inproj_sb")
    lr = _matmul(_identity, xb, ss, w_lr, out_dtype=F32, name="inproj_lr")
    o_gla = _gla(p_gla, lr, wg, b_gate[l].reshape(1, hk), gla_out_norm[l].reshape(1, GLA_DV),
                 batch=batch, seq=seq)
    o_sb = _sb(p_sb, gq, sb_k_norm[l].reshape(1, SB_DH), sb_out_norm[l].reshape(1, SB_DH),
               col0=0, batch=batch, seq=seq)
    x, xb, ss = _oproj(o_gla, o_sb, w_o, x, mlp_norm[l], layer=l)
    up = _matmul(_relu2, xb, ss, w_up, layer=l, out_dtype=BF16, name="mlp_up")
    if next_gain is None:
        return _down(up, w_down, x, None, layer=l), None, None
    return _down(up, w_down, x, next_gain, layer=l)


def kernel(x, attn_norm, w_in, w_gate_up, b_gate, gla_out_norm, sb_q_norm, sb_k_norm,
           sb_out_norm, w_o, mlp_norm, w_up, w_down):
    batch, seq, d = x.shape
    depth = w_in.shape[0]
    y = x.reshape(batch * seq, d)
    yb, ss = _prenorm(y, attn_norm[0])
    for l in range(depth):
        next_gain = attn_norm[l + 1] if l + 1 < depth else None
        y, yb, ss = _layer(y, yb, ss, l, next_gain, w_in, w_gate_up, b_gate, gla_out_norm,
                           sb_q_norm, sb_k_norm, sb_out_norm, w_o, mlp_norm, w_up, w_down,
                           batch=batch, seq=seq)
    return y.reshape(batch, seq, d)
```

```python
import functools
import math

import jax
import jax.numpy as jnp
from jax import lax
from jax.experimental import pallas as pl
from jax.experimental.pallas import tpu as pltpu

EPS = 1e-6
LANES = 128

GLA_HEADS = 8
GLA_DK = 64
GLA_DV = 128
GLA_GATE_RANK = 16
GLA_GATE_TAU = 16.0
GLA_CHUNK = 64
SB_HEADS = 8
SB_DH = 128

SB_EXIT_LOG2 = 126.0
SB_DEAD = -1e30

VMEM_LIMIT = 56 * 1024 * 1024

F32 = jnp.float32
BF16 = jnp.bfloat16
NT_DIMS = (((1,), (1,)), ((), ()))
TN_DIMS = (((0,), (0,)), ((), ()))


def _params(*sem):
    return pltpu.CompilerParams(dimension_semantics=sem, vmem_limit_bytes=VMEM_LIMIT)


def _log_sigmoid(z):
    return jnp.minimum(z, 0.0) - jnp.log(1.0 + jnp.exp(-jnp.abs(z)))


def _rms_scale(y):
    return lax.rsqrt(jnp.mean(y * y, axis=-1, keepdims=True) + EPS)


def _sumsq(y):
    return jnp.broadcast_to(jnp.sum(y * y, axis=-1, keepdims=True), (y.shape[0], LANES))


def _row_scale(ss_ref, d):
    return lax.rsqrt(jnp.sum(ss_ref[...], axis=0) * (1.0 / d) + EPS)


def _emit_normed(y, g_ref, o_ref, xb_ref, ss_ref):
    o_ref[...] = y
    xb_ref[...] = (y * g_ref[...]).astype(BF16)
    ss_ref[...] = _sumsq(y)


def _prenorm_kernel(x_ref, g_ref, xb_ref, ss_ref):
    x = x_ref[...]
    xb_ref[...] = (x * g_ref[...]).astype(BF16)
    ss_ref[...] = _sumsq(x)


def _prenorm(x, gain, rows=256):
    m, d = x.shape
    rows = min(rows, m)
    return pl.pallas_call(
        _prenorm_kernel,
        grid=(m // rows,),
        in_specs=[pl.BlockSpec((rows, d), lambda i: (i, 0)),
                  pl.BlockSpec((1, d), lambda i: (0, 0))],
        out_specs=[pl.BlockSpec((rows, d), lambda i: (i, 0)),
                   pl.BlockSpec((None, rows, LANES), lambda i: (0, i, 0))],
        out_shape=[jax.ShapeDtypeStruct((m, d), BF16),
                   jax.ShapeDtypeStruct((1, m, LANES), F32)],
        compiler_params=_params("parallel"),
        name="prenorm",
    )(x, gain.reshape(1, d))


def _w_spec(w, rows, cols, index_map, layer):
    if w.ndim == 2:
        return pl.BlockSpec((rows, cols), index_map)
    return pl.BlockSpec((None, rows, cols), lambda *g: (layer,) + tuple(index_map(*g)))


def _mm_kernel(a_ref, w_ref, ss_ref, o_ref, *, post):
    acc = jnp.dot(a_ref[...], w_ref[...].astype(BF16), preferred_element_type=F32)
    r = _row_scale(ss_ref, a_ref.shape[1])
    for c in range(acc.shape[1] // LANES):
        sl = slice(c * LANES, (c + 1) * LANES)
        o_ref[:, sl] = post(acc[:, sl] * r).astype(o_ref.dtype)


def _identity(y):
    return y


def _relu2(y):
    return jnp.square(jnp.maximum(y, 0.0))


def _matmul(post, a, ss, w, *, layer=0, out_dtype, tm=1024, tn=1024, name):
    m, k = a.shape
    n = w.shape[-1]
    tm = min(tm, m)
    tn = min(tn, n)
    return pl.pallas_call(
        functools.partial(_mm_kernel, post=post),
        grid=(n // tn, m // tm),
        in_specs=[pl.BlockSpec((tm, k), lambda j, i: (i, 0)),
                  _w_spec(w, k, tn, lambda j, i: (0, j), layer),
                  pl.BlockSpec((ss.shape[0], tm, LANES), lambda j, i: (0, i, 0))],
        out_specs=pl.BlockSpec((tm, tn), lambda j, i: (i, j)),
        out_shape=jax.ShapeDtypeStruct((m, n), out_dtype),
        compiler_params=_params("parallel", "arbitrary"),
        name=name,
    )(a, w, ss)


def _normed_out(m, n, tm, tn, tile_map, slot_map):
    specs = [pl.BlockSpec((tm, tn), tile_map), pl.BlockSpec((tm, tn), tile_map),
             pl.BlockSpec((None, tm, LANES), slot_map)]
    shapes = [jax.ShapeDtypeStruct((m, n), F32), jax.ShapeDtypeStruct((m, n), BF16),
              jax.ShapeDtypeStruct((n // tn, m, LANES), F32)]
    return specs, shapes


def _oproj_kernel(a1_ref, a2_ref, w1_ref, w2_ref, x_ref, g_ref, o_ref, xb_ref, ss_ref):
    acc = jnp.dot(a1_ref[...], w1_ref[...].astype(BF16), preferred_element_type=F32)
    acc += jnp.dot(a2_ref[...], w2_ref[...].astype(BF16), preferred_element_type=F32)
    _emit_normed(x_ref[...] + acc, g_ref, o_ref, xb_ref, ss_ref)


def _oproj(a1, a2, w, x, gain, *, layer, tm=1024, tn=1024):
    m, kh = a1.shape
    n = w.shape[-1]
    tm = min(tm, m)
    out_specs, out_shape = _normed_out(m, n, tm, tn, lambda j, i: (i, j), lambda j, i: (j, i, 0))
    return pl.pallas_call(
        _oproj_kernel,
        grid=(n // tn, m // tm),
        in_specs=[pl.BlockSpec((tm, kh), lambda j, i: (i, 0)),
                  pl.BlockSpec((tm, kh), lambda j, i: (i, 0)),
                  _w_spec(w, kh, tn, lambda j, i: (0, j), layer),
                  _w_spec(w, kh, tn, lambda j, i: (1, j), layer),
                  pl.BlockSpec((tm, tn), lambda j, i: (i, j)),
                  pl.BlockSpec((1, tn), lambda j, i: (0, j))],
        out_specs=out_specs,
        out_shape=out_shape,
        compiler_params=_params("parallel", "arbitrary"),
        name="oproj",
    )(a1, a2, w, w, x, gain.reshape(1, n))


def _down_kernel(a_ref, w_ref, x_ref, *rest, normed):
    acc_ref = rest[-1]
    kk = pl.program_id(2)

    @pl.when(kk == 0)
    def _():
        acc_ref[...] = jnp.zeros_like(acc_ref)

    acc_ref[...] += jnp.dot(a_ref[...], w_ref[...].astype(BF16), preferred_element_type=F32)

    @pl.when(kk == pl.num_programs(2) - 1)
    def _():
        y = x_ref[...] + acc_ref[...]
        if normed:
            _emit_normed(y, *rest[:-1])
        else:
            rest[0][...] = y


def _down(a, w, x, gain, *, layer, tm=1024, tn=1024, tk=2048):
    m, k = a.shape
    n = w.shape[-1]
    tm = min(tm, m)
    tile_map = lambda i, j, kk: (i, j)
    in_specs = [pl.BlockSpec((tm, tk), lambda i, j, kk: (i, kk)),
                _w_spec(w, tk, tn, lambda i, j, kk: (kk, j), layer),
                pl.BlockSpec((tm, tn), tile_map)]
    args = [a, w, x]
    if gain is None:
        out_specs = pl.BlockSpec((tm, tn), tile_map)
        out_shape = jax.ShapeDtypeStruct((m, n), F32)
    else:
        in_specs.append(pl.BlockSpec((1, tn), lambda i, j, kk: (0, j)))
        args.append(gain.reshape(1, n))
        out_specs, out_shape = _normed_out(m, n, tm, tn, tile_map, lambda i, j, kk: (j, i, 0))
    return pl.pallas_call(
        functools.partial(_down_kernel, normed=gain is not None),
        grid=(m // tm, n // tn, k // tk),
        in_specs=in_specs,
        out_specs=out_specs,
        out_shape=out_shape,
        scratch_shapes=[pltpu.VMEM((tm, tn), F32)],
        compiler_params=_params("parallel", "arbitrary", "arbitrary"),
        name="down",
    )(*args)


def _gla_kernel(q_ref, k_ref, v_ref, gate_ref, xb_ref, ss_ref, wlr_ref, wg_ref, bg_ref, gn_ref,
                o_ref, state_ref, loga_ref, *, tb):
    @pl.when(pl.program_id(1) == 0)
    def _():
        state_ref[...] = jnp.zeros_like(state_ref)

    c = GLA_CHUNK
    row = lax.broadcasted_iota(jnp.int32, (c, c), 0)
    col = lax.broadcasted_iota(jnp.int32, (c, c), 1)
    causal = col <= row
    tri = jnp.where(causal, 1.0, 0.0).astype(BF16)
    tri2 = jnp.concatenate([tri, tri], axis=1)
    lane = lax.broadcasted_iota(jnp.int32, (c, LANES), 1)
    head_mask = (lane < GLA_DK, lane >= GLA_DK)

    r = GLA_GATE_RANK
    lr = _row_scale(ss_ref, xb_ref.shape[1]) * jnp.dot(
        xb_ref[...], wlr_ref[...].astype(BF16), preferred_element_type=F32)
    lr_hi = lr.astype(BF16)
    lr_lo = (lr - lr_hi.astype(F32)).astype(BF16)
    lr_lane = lax.broadcasted_iota(jnp.int32, lr.shape, 1)
    lr_mix = jnp.where(jnp.logical_and(lr_lane >= r, lr_lane < 2 * r), lr_lo, lr_hi)
    wg = wg_ref[...]
    wg_hi = wg.astype(BF16)
    wg_lo = (wg - wg_hi.astype(F32)).astype(BF16)
    wg_mix = jnp.where(lax.broadcasted_iota(jnp.int32, wg.shape, 0) >= 2 * r, wg_lo, wg_hi)
    logits = jnp.dot(lr_mix, wg_mix, preferred_element_type=F32) + bg_ref[...]
    loga_ref[...] = _log_sigmoid(logits) * (1.0 / GLA_GATE_TAU)

    def chunk(ci, carry):
        r0 = pl.multiple_of(ci * c, c)
        rows = pl.ds(r0, c)
        log_a = loga_ref[rows, :]
        hi = log_a.astype(BF16)
        lo = (log_a - hi.astype(F32)).astype(BF16)
        bcum = jnp.dot(tri2, jnp.concatenate([hi, lo], axis=0), preferred_element_type=F32)
        b_last = bcum[c - 1:c, :]
        kk = k_ref[rows, :].astype(F32)
        q_e = q_ref[rows, :].astype(F32) * (GLA_DK ** -0.5) * jnp.exp(bcum)
        k_e = (kk * jnp.exp(-bcum)).astype(BF16)
        k_dec = (kk * jnp.exp(b_last - bcum)).astype(BF16)
        decay = jnp.exp(b_last)
        heads = range(GLA_HEADS)
        pair = lambda x, h: x[:, (h // 2) * LANES:(h // 2 + 1) * LANES]
        hsl = lambda h: slice(h * GLA_DV, (h + 1) * GLA_DV)
        qm = [jnp.where(head_mask[h % 2], pair(q_e, h), 0.0).astype(BF16) for h in heads]
        vb = [v_ref[rows, hsl(h)] for h in heads]
        s = [lax.dot_general(qm[h], pair(k_e, h), NT_DIMS, preferred_element_type=F32)
             for h in heads]
        st = [state_ref[h] for h in heads]
        o_inter = [lax.dot_general(qm[h], st[h].astype(BF16), NT_DIMS,
                                   preferred_element_type=F32) for h in heads]
        u_t = [lax.dot_general(vb[h], pair(k_dec, h), TN_DIMS, preferred_element_type=F32)
               for h in heads]
        for h in heads:
            state_ref[h] = pair(decay, h) * st[h] + u_t[h]
        o_intra = [jnp.dot(jnp.where(causal, s[h], 0.0).astype(BF16), vb[h],
                           preferred_element_type=F32) for h in heads]
        for h in heads:
            o = o_intra[h] + o_inter[h]
            y = o * _rms_scale(o) * gn_ref[...]
            g = gate_ref[rows, hsl(h)].astype(F32)
            y = y * (g * (1.0 / (1.0 + jnp.exp(-g))))
            o_ref[rows, hsl(h)] = y.astype(o_ref.dtype)
        return carry

    lax.fori_loop(0, tb // c, chunk, 0, unroll=4)


def _gla(proj, xb, ss, w_lr, wg, bg, gn, *, batch, seq, tb=512):
    m = proj.shape[0]
    d = xb.shape[1]
    tb = min(tb, seq)
    nt = seq // tb
    hk = GLA_HEADS * GLA_DK
    hv = GLA_HEADS * GLA_DV
    row = lambda b, t: b * nt + t
    return pl.pallas_call(
        functools.partial(_gla_kernel, tb=tb),
        grid=(batch, nt),
        in_specs=[pl.BlockSpec((tb, hk), lambda b, t: (row(b, t), 0)),
                  pl.BlockSpec((tb, hk), lambda b, t: (row(b, t), 1)),
                  pl.BlockSpec((tb, hv), lambda b, t: (row(b, t), 1)),
                  pl.BlockSpec((tb, hv), lambda b, t: (row(b, t), 2)),
                  pl.BlockSpec((tb, d), lambda b, t: (row(b, t), 0)),
                  pl.BlockSpec((ss.shape[0], tb, LANES), lambda b, t: (0, row(b, t), 0)),
                  pl.BlockSpec((d, LANES), lambda b, t: (0, 0)),
                  pl.BlockSpec((LANES, hk), lambda b, t: (0, 0)),
                  pl.BlockSpec((1, hk), lambda b, t: (0, 0)),
                  pl.BlockSpec((1, GLA_DV), lambda b, t: (0, 0))],
        out_specs=pl.BlockSpec((tb, hv), lambda b, t: (row(b, t), 0)),
        out_shape=jax.ShapeDtypeStruct((m, hv), BF16),
        scratch_shapes=[pltpu.VMEM((GLA_HEADS, GLA_DV, LANES), F32),
                        pltpu.VMEM((tb, hk), F32)],
        compiler_params=_params("parallel", "arbitrary"),
        name="gla",
    )(proj, proj, proj, proj, xb, ss, w_lr, wg, bg, gn)


def _sb_kernel(q_ref, k_ref, v_ref, w_ref, gq_ref, gk_ref, gn_ref, o_ref, qn_ref, kn_ref,
               *, tq, tk, seq):
    wmat = w_ref[...]
    nd = tq // tk
    rowi = lax.broadcasted_iota(jnp.int32, (tk, tk), 0)
    coli = lax.broadcasted_iota(jnp.int32, (tk, tk), 1)
    below_diag = jnp.concatenate([coli < rowi] * nd, axis=0)

    def head_norm(i, _):
        rows = pl.ds(pl.multiple_of(i * tq, tq), tq)
        y = q_ref[rows, :].astype(F32)
        qn_ref[rows, :] = (y * _rms_scale(y) * gq_ref[...]).astype(BF16)
        y = k_ref[rows, :].astype(F32)
        kn_ref[rows, :] = (y * _rms_scale(y) * gk_ref[...]).astype(BF16)
        return 0

    lax.fori_loop(0, seq // tq, head_norm, 0)

    group = lambda x, g: x[g * tk:(g + 1) * tk]

    def scores(q, q0, j):
        first = [q0 + (g - j) * tk for g in range(nd)]
        start = [pl.multiple_of(jnp.maximum(f, 0), tk) for f in first]
        z = jnp.concatenate(
            [lax.dot_general(group(q, g), kn_ref[pl.ds(start[g], tk), :], NT_DIMS,
                             preferred_element_type=F32) for g in range(nd)], axis=0)
        log_beta = jnp.minimum(z, 0.0) - jnp.log2(1.0 + jnp.exp2(-jnp.abs(z)))
        return first, start, log_beta, log_beta - z

    def suffix_sums(log_1m):
        hi = log_1m.astype(BF16)
        lo = (log_1m - hi.astype(F32)).astype(BF16)
        return jnp.dot(jnp.concatenate([hi, lo], axis=1), wmat, preferred_element_type=F32)

    def retire(carry, first):
        return jnp.concatenate(
            [jnp.where(first[g] >= 0, group(carry, g), SB_DEAD) for g in range(nd)], axis=0)

    def weigh(a, start):
        a = a.astype(BF16)
        return jnp.concatenate(
            [jnp.dot(group(a, g), v_ref[pl.ds(start[g], tk), :], preferred_element_type=F32)
             for g in range(nd)], axis=0)

    def sweep_pair(q, q0, j, carry, acc, mask):
        first0, start0, log_beta0, log_1m0 = scores(q, q0, j)
        first1, start1, log_beta1, log_1m1 = scores(q, q0, j + 1)
        if mask is not None:
            log_1m0 = jnp.where(mask, log_1m0, 0.0)
        s2_0 = suffix_sums(log_1m0)
        s2_1 = suffix_sums(log_1m1)
        if mask is None:
            carry = retire(carry, first0)
        a0 = jnp.exp2(log_beta0 + s2_0[:, :tk] + carry)
        if mask is not None:
            a0 = jnp.where(mask, a0, 0.0)
        carry = retire(carry + s2_0[:, tk:], first1)
        a1 = jnp.exp2(log_beta1 + s2_1[:, :tk] + carry)
        carry = carry + s2_1[:, tk:]
        return carry, acc + weigh(a0, start0) + weigh(a1, start1)

    def live(c):
        return (jnp.max(c) > -SB_EXIT_LOG2).astype(jnp.int32)

    def qblock(i, _):
        q0 = pl.multiple_of(i * tq, tq)
        q = qn_ref[pl.ds(q0, tq), :]
        carry = jnp.zeros((tq, tk), F32)
        acc = jnp.zeros((tq, SB_DH), F32)
        carry, acc = sweep_pair(q, q0, 0, carry, acc, below_diag)

        def cond(st):
            j, go, _, _ = st
            return jnp.logical_and(go > 0, j < (i + 1) * nd)

        def body(st):
            j, _, carry, acc = st
            carry, acc = sweep_pair(q, q0, j, carry, acc, None)
            return j + 2, live(carry), carry, acc

        _, _, _, acc = lax.while_loop(cond, body, (2, live(carry), carry, acc))
        o_ref[pl.ds(q0, tq), :] = (acc * _rms_scale(acc) * gn_ref[...]).astype(o_ref.dtype)
        return 0

    lax.fori_loop(0, seq // tq, qblock, 0)


def _sb(proj, gq, gk, gn, *, col0, batch, seq, tq=1024, tk=128):
    m = proj.shape[0]
    tq = min(tq, seq)
    assert tq % tk == 0 and seq % tq == 0
    jj = lax.broadcasted_iota(jnp.int32, (2 * tk, 2 * tk), 0) % tk
    ss = lax.broadcasted_iota(jnp.int32, (2 * tk, 2 * tk), 1)
    wmat = jnp.logical_or(ss >= tk, jj > ss).astype(BF16)
    vec = pl.BlockSpec((1, SB_DH), lambda b, h: (0, 0))
    return pl.pallas_call(
        functools.partial(_sb_kernel, tq=tq, tk=tk, seq=seq),
        grid=(batch, SB_HEADS),
        in_specs=[pl.BlockSpec((seq, SB_DH), lambda b, h: (b, col0 + h)),
                  pl.BlockSpec((seq, SB_DH), lambda b, h: (b, col0 + SB_HEADS + h)),
                  pl.BlockSpec((seq, SB_DH), lambda b, h: (b, col0 + 2 * SB_HEADS + h)),
                  pl.BlockSpec((2 * tk, 2 * tk), lambda b, h: (0, 0)),
                  vec, vec, vec],
        out_specs=pl.BlockSpec((seq, SB_DH), lambda b, h: (b, h)),
        out_shape=jax.ShapeDtypeStruct((m, SB_HEADS * SB_DH), BF16),
        scratch_shapes=[pltpu.VMEM((seq, SB_DH), BF16), pltpu.VMEM((seq, SB_DH), BF16)],
        compiler_params=_params("parallel", "parallel"),
        name="stickbreak",
    )(proj, proj, proj, wmat, gq, gk, gn)


def _layer(x, xb, ss, l, next_gain, w_in, w_gate_up, b_gate, gla_out_norm, sb_q_norm,
           sb_k_norm, sb_out_norm, w_o, mlp_norm, w_up, w_down, *, batch, seq):
    hk = GLA_HEADS * GLA_DK
    hv = GLA_HEADS * GLA_DV
    n_gla = 2 * hk + 2 * hv
    pad = LANES - 3 * GLA_GATE_RANK
    w_gla = w_in[l, :, :n_gla]
    w_sb = w_in[l, :, n_gla + GLA_GATE_RANK:]
    w_lr = jnp.pad(jnp.tile(w_in[l, :, n_gla:n_gla + GLA_GATE_RANK], (1, 3)), ((0, 0), (0, pad)))
    wg = jnp.pad(jnp.tile(w_gate_up[l], (3, 1)), ((0, pad), (0, 0)))
    gq = (sb_q_norm[l] * (math.log2(math.e) / math.sqrt(SB_DH))).reshape(1, SB_DH)

    p_gla = _matmul(_identity, xb, ss, w_gla, out_dtype=BF16, name="inproj_gla")
    p_sb = _matmul(_identity, xb, ss, w_sb, out_dtype=BF16, name="inproj_sb")
    o_gla = _gla(p_gla, xb, ss, w_lr, wg, b_gate[l].reshape(1, hk),
                 gla_out_norm[l].reshape(1, GLA_DV), batch=batch, seq=seq)
    o_sb = _sb(p_sb, gq, sb_k_norm[l].reshape(1, SB_DH), sb_out_norm[l].reshape(1, SB_DH),
               col0=0, batch=batch, seq=seq)
    x, xb, ss = _oproj(o_gla, o_sb, w_o, x, mlp_norm[l], layer=l)
    up = _matmul(_relu2, xb, ss, w_up, layer=l, out_dtype=BF16, name="mlp_up")
    if next_gain is None:
        return _down(up, w_down, x, None, layer=l), None, None
    return _down(up, w_down, x, next_gain, layer=l)


def kernel(x, attn_norm, w_in, w_gate_up, b_gate, gla_out_norm, sb_q_norm, sb_k_norm,
           sb_out_norm, w_o, mlp_norm, w_up, w_down):
    batch, seq, d = x.shape
    depth = w_in.shape[0]
    y = x.reshape(batch * seq, d)
    yb, ss = _prenorm(y, attn_norm[0])
    for l in range(depth):
        next_gain = attn_norm[l + 1] if l + 1 < depth else None
        y, yb, ss = _layer(y, yb, ss, l, next_gain, w_in, w_gate_up, b_gate, gla_out_norm,
                           sb_q_norm, sb_k_norm, sb_out_norm, w_o, mlp_norm, w_up, w_down,
                           batch=batch, seq=seq)
    return y.reshape(batch, seq, d)
```

```python
import functools
import math

import jax
import jax.numpy as jnp
from jax import lax
from jax.experimental import pallas as pl
from jax.experimental.pallas import tpu as pltpu

EPS = 1e-6
LANES = 128

GLA_HEADS = 8
GLA_DK = 64
GLA_DV = 128
GLA_GATE_RANK = 16
GLA_GATE_TAU = 16.0
GLA_CHUNK = 64
SB_HEADS = 8
SB_DH = 128

SB_EXIT_LOG2 = 126.0
SB_DEAD = -1e30
SB_PEELED_SWEEPS = 2
SB_LOOP_SWEEPS = 1

VMEM_LIMIT = 56 * 1024 * 1024

F32 = jnp.float32
BF16 = jnp.bfloat16
NT_DIMS = (((1,), (1,)), ((), ()))
TN_DIMS = (((0,), (0,)), ((), ()))


def _params(*sem):
    return pltpu.CompilerParams(dimension_semantics=sem, vmem_limit_bytes=VMEM_LIMIT)


def _log_sigmoid(z):
    return jnp.minimum(z, 0.0) - jnp.log(1.0 + jnp.exp(-jnp.abs(z)))


def _rms_scale(y):
    return lax.rsqrt(jnp.mean(y * y, axis=-1, keepdims=True) + EPS)


def _sumsq(y):
    return jnp.broadcast_to(jnp.sum(y * y, axis=-1, keepdims=True), (y.shape[0], LANES))


def _row_scale(ss_ref, d):
    return lax.rsqrt(jnp.sum(ss_ref[...], axis=0) * (1.0 / d) + EPS)


def _emit_normed(y, g_ref, o_ref, xb_ref, ss_ref):
    o_ref[...] = y
    xb_ref[...] = (y * g_ref[...]).astype(BF16)
    ss_ref[...] = _sumsq(y)


def _prenorm_kernel(x_ref, g_ref, xb_ref, ss_ref):
    x = x_ref[...]
    xb_ref[...] = (x * g_ref[...]).astype(BF16)
    ss_ref[...] = _sumsq(x)


def _prenorm(x, gain, rows=256):
    m, d = x.shape
    rows = min(rows, m)
    return pl.pallas_call(
        _prenorm_kernel,
        grid=(m // rows,),
        in_specs=[pl.BlockSpec((rows, d), lambda i: (i, 0)),
                  pl.BlockSpec((1, d), lambda i: (0, 0))],
        out_specs=[pl.BlockSpec((rows, d), lambda i: (i, 0)),
                   pl.BlockSpec((None, rows, LANES), lambda i: (0, i, 0))],
        out_shape=[jax.ShapeDtypeStruct((m, d), BF16),
                   jax.ShapeDtypeStruct((1, m, LANES), F32)],
        compiler_params=_params("parallel"),
        name="prenorm",
    )(x, gain.reshape(1, d))


def _w_spec(w, rows, cols, index_map, layer):
    if w.ndim == 2:
        return pl.BlockSpec((rows, cols), index_map)
    return pl.BlockSpec((None, rows, cols), lambda *g: (layer,) + tuple(index_map(*g)))


def _mm_kernel(a_ref, w_ref, ss_ref, o_ref, *, post):
    acc = jnp.dot(a_ref[...], w_ref[...].astype(BF16), preferred_element_type=F32)
    r = _row_scale(ss_ref, a_ref.shape[1])
    for c in range(acc.shape[1] // LANES):
        sl = slice(c * LANES, (c + 1) * LANES)
        o_ref[:, sl] = post(acc[:, sl] * r).astype(o_ref.dtype)


def _identity(y):
    return y


def _relu2(y):
    return jnp.square(jnp.maximum(y, 0.0))


def _matmul(post, a, ss, w, *, layer=0, out_dtype, tm=1024, tn=1024, name):
    m, k = a.shape
    n = w.shape[-1]
    tm = min(tm, m)
    tn = min(tn, n)
    return pl.pallas_call(
        functools.partial(_mm_kernel, post=post),
        grid=(n // tn, m // tm),
        in_specs=[pl.BlockSpec((tm, k), lambda j, i: (i, 0)),
                  _w_spec(w, k, tn, lambda j, i: (0, j), layer),
                  pl.BlockSpec((ss.shape[0], tm, LANES), lambda j, i: (0, i, 0))],
        out_specs=pl.BlockSpec((tm, tn), lambda j, i: (i, j)),
        out_shape=jax.ShapeDtypeStruct((m, n), out_dtype),
        compiler_params=_params("parallel", "arbitrary"),
        name=name,
    )(a, w, ss)


def _normed_out(m, n, tm, tn, tile_map, slot_map):
    specs = [pl.BlockSpec((tm, tn), tile_map), pl.BlockSpec((tm, tn), tile_map),
             pl.BlockSpec((None, tm, LANES), slot_map)]
    shapes = [jax.ShapeDtypeStruct((m, n), F32), jax.ShapeDtypeStruct((m, n), BF16),
              jax.ShapeDtypeStruct((n // tn, m, LANES), F32)]
    return specs, shapes


def _oproj_kernel(a1_ref, a2_ref, w1_ref, w2_ref, x_ref, g_ref, o_ref, xb_ref, ss_ref):
    acc = jnp.dot(a1_ref[...], w1_ref[...].astype(BF16), preferred_element_type=F32)
    acc += jnp.dot(a2_ref[...], w2_ref[...].astype(BF16), preferred_element_type=F32)
    _emit_normed(x_ref[...] + acc, g_ref, o_ref, xb_ref, ss_ref)


def _oproj(a1, a2, w, x, gain, *, layer, tm=1024, tn=1024):
    m, kh = a1.shape
    n = w.shape[-1]
    tm = min(tm, m)
    out_specs, out_shape = _normed_out(m, n, tm, tn, lambda j, i: (i, j), lambda j, i: (j, i, 0))
    return pl.pallas_call(
        _oproj_kernel,
        grid=(n // tn, m // tm),
        in_specs=[pl.BlockSpec((tm, kh), lambda j, i: (i, 0)),
                  pl.BlockSpec((tm, kh), lambda j, i: (i, 0)),
                  _w_spec(w, kh, tn, lambda j, i: (0, j), layer),
                  _w_spec(w, kh, tn, lambda j, i: (1, j), layer),
                  pl.BlockSpec((tm, tn), lambda j, i: (i, j)),
                  pl.BlockSpec((1, tn), lambda j, i: (0, j))],
        out_specs=out_specs,
        out_shape=out_shape,
        compiler_params=_params("parallel", "arbitrary"),
        name="oproj",
    )(a1, a2, w, w, x, gain.reshape(1, n))


def _down_kernel(a_ref, w_ref, x_ref, *rest, normed):
    acc_ref = rest[-1]
    kk = pl.program_id(2)

    @pl.when(kk == 0)
    def _():
        acc_ref[...] = jnp.zeros_like(acc_ref)

    acc_ref[...] += jnp.dot(a_ref[...], w_ref[...].astype(BF16), preferred_element_type=F32)

    @pl.when(kk == pl.num_programs(2) - 1)
    def _():
        y = x_ref[...] + acc_ref[...]
        if normed:
            _emit_normed(y, *rest[:-1])
        else:
            rest[0][...] = y


def _down(a, w, x, gain, *, layer, tm=1024, tn=1024, tk=2048):
    m, k = a.shape
    n = w.shape[-1]
    tm = min(tm, m)
    tile_map = lambda i, j, kk: (i, j)
    in_specs = [pl.BlockSpec((tm, tk), lambda i, j, kk: (i, kk)),
                _w_spec(w, tk, tn, lambda i, j, kk: (kk, j), layer),
                pl.BlockSpec((tm, tn), tile_map)]
    args = [a, w, x]
    if gain is None:
        out_specs = pl.BlockSpec((tm, tn), tile_map)
        out_shape = jax.ShapeDtypeStruct((m, n), F32)
    else:
        in_specs.append(pl.BlockSpec((1, tn), lambda i, j, kk: (0, j)))
        args.append(gain.reshape(1, n))
        out_specs, out_shape = _normed_out(m, n, tm, tn, tile_map, lambda i, j, kk: (j, i, 0))
    return pl.pallas_call(
        functools.partial(_down_kernel, normed=gain is not None),
        grid=(m // tm, n // tn, k // tk),
        in_specs=in_specs,
        out_specs=out_specs,
        out_shape=out_shape,
        scratch_shapes=[pltpu.VMEM((tm, tn), F32)],
        compiler_params=_params("parallel", "arbitrary", "arbitrary"),
        name="down",
    )(*args)


def _gla_kernel(q_ref, k_ref, v_ref, gate_ref, xb_ref, ss_ref, wlr_ref, wg_ref, bg_ref, gn_ref,
                o_ref, state_ref, loga_ref, *, tb):
    @pl.when(pl.program_id(1) == 0)
    def _():
        state_ref[...] = jnp.zeros_like(state_ref)

    c = GLA_CHUNK
    row = lax.broadcasted_iota(jnp.int32, (c, c), 0)
    col = lax.broadcasted_iota(jnp.int32, (c, c), 1)
    causal = col <= row
    tri = jnp.where(causal, 1.0, 0.0).astype(BF16)
    tri2 = jnp.concatenate([tri, tri], axis=1)
    lane = lax.broadcasted_iota(jnp.int32, (c, LANES), 1)
    head_mask = (lane < GLA_DK, lane >= GLA_DK)

    r = GLA_GATE_RANK
    lr = _row_scale(ss_ref, xb_ref.shape[1]) * jnp.dot(
        xb_ref[...], wlr_ref[...].astype(BF16), preferred_element_type=F32)
    lr_hi = lr.astype(BF16)
    lr_lo = (lr - lr_hi.astype(F32)).astype(BF16)
    lr_lane = lax.broadcasted_iota(jnp.int32, lr.shape, 1)
    lr_mix = jnp.where(jnp.logical_and(lr_lane >= r, lr_lane < 2 * r), lr_lo, lr_hi)
    wg = wg_ref[...]
    wg_hi = wg.astype(BF16)
    wg_lo = (wg - wg_hi.astype(F32)).astype(BF16)
    wg_mix = jnp.where(lax.broadcasted_iota(jnp.int32, wg.shape, 0) >= 2 * r, wg_lo, wg_hi)
    logits = jnp.dot(lr_mix, wg_mix, preferred_element_type=F32) + bg_ref[...]
    loga_ref[...] = _log_sigmoid(logits) * (1.0 / GLA_GATE_TAU)

    def chunk(ci, carry):
        r0 = pl.multiple_of(ci * c, c)
        rows = pl.ds(r0, c)
        log_a = loga_ref[rows, :]
        hi = log_a.astype(BF16)
        lo = (log_a - hi.astype(F32)).astype(BF16)
        bcum = jnp.dot(tri2, jnp.concatenate([hi, lo], axis=0), preferred_element_type=F32)
        b_last = bcum[c - 1:c, :]
        kk = k_ref[rows, :].astype(F32)
        q_e = q_ref[rows, :].astype(F32) * (GLA_DK ** -0.5) * jnp.exp(bcum)
        k_e = (kk * jnp.exp(-bcum)).astype(BF16)
        k_dec = (kk * jnp.exp(b_last - bcum)).astype(BF16)
        decay = jnp.exp(b_last)
        heads = range(GLA_HEADS)
        pair = lambda x, h: x[:, (h // 2) * LANES:(h // 2 + 1) * LANES]
        hsl = lambda h: slice(h * GLA_DV, (h + 1) * GLA_DV)
        qm = [jnp.where(head_mask[h % 2], pair(q_e, h), 0.0).astype(BF16) for h in heads]
        vb = [v_ref[rows, hsl(h)] for h in heads]
        s = [lax.dot_general(qm[h], pair(k_e, h), NT_DIMS, preferred_element_type=F32)
             for h in heads]
        st = [state_ref[h] for h in heads]
        o_inter = [lax.dot_general(qm[h], st[h].astype(BF16), NT_DIMS,
                                   preferred_element_type=F32) for h in heads]
        u_t = [lax.dot_general(vb[h], pair(k_dec, h), TN_DIMS, preferred_element_type=F32)
               for h in heads]
        for h in heads:
            state_ref[h] = pair(decay, h) * st[h] + u_t[h]
        o_intra = [jnp.dot(jnp.where(causal, s[h], 0.0).astype(BF16), vb[h],
                           preferred_element_type=F32) for h in heads]
        for h in heads:
            o = o_intra[h] + o_inter[h]
            y = o * _rms_scale(o) * gn_ref[...]
            g = gate_ref[rows, hsl(h)].astype(F32)
            y = y * (g * (1.0 / (1.0 + jnp.exp(-g))))
            o_ref[rows, hsl(h)] = y.astype(o_ref.dtype)
        return carry

    lax.fori_loop(0, tb // c, chunk, 0, unroll=4)


def _gla(proj, xb, ss, w_lr, wg, bg, gn, *, batch, seq, tb=512):
    m = proj.shape[0]
    d = xb.shape[1]
    tb = min(tb, seq)
    nt = seq // tb
    hk = GLA_HEADS * GLA_DK
    hv = GLA_HEADS * GLA_DV
    row = lambda b, t: b * nt + t
    return pl.pallas_call(
        functools.partial(_gla_kernel, tb=tb),
        grid=(batch, nt),
        in_specs=[pl.BlockSpec((tb, hk), lambda b, t: (row(b, t), 0)),
                  pl.BlockSpec((tb, hk), lambda b, t: (row(b, t), 1)),
                  pl.BlockSpec((tb, hv), lambda b, t: (row(b, t), 1)),
                  pl.BlockSpec((tb, hv), lambda b, t: (row(b, t), 2)),
                  pl.BlockSpec((tb, d), lambda b, t: (row(b, t), 0)),
                  pl.BlockSpec((ss.shape[0], tb, LANES), lambda b, t: (0, row(b, t), 0)),
                  pl.BlockSpec((d, LANES), lambda b, t: (0, 0)),
                  pl.BlockSpec((LANES, hk), lambda b, t: (0, 0)),
                  pl.BlockSpec((1, hk), lambda b, t: (0, 0)),
                  pl.BlockSpec((1, GLA_DV), lambda b, t: (0, 0))],
        out_specs=pl.BlockSpec((tb, hv), lambda b, t: (row(b, t), 0)),
        out_shape=jax.ShapeDtypeStruct((m, hv), BF16),
        scratch_shapes=[pltpu.VMEM((GLA_HEADS, GLA_DV, LANES), F32),
                        pltpu.VMEM((tb, hk), F32)],
        compiler_params=_params("parallel", "arbitrary"),
        name="gla",
    )(proj, proj, proj, proj, xb, ss, w_lr, wg, bg, gn)


def _sb_kernel(q_ref, k_ref, v_ref, w_ref, gq_ref, gk_ref, gn_ref, o_ref, qn_ref, kn_ref,
               *, tq, tk, seq):
    wmat = w_ref[...]
    nd = tq // tk
    rowi = lax.broadcasted_iota(jnp.int32, (tk, tk), 0)
    coli = lax.broadcasted_iota(jnp.int32, (tk, tk), 1)
    below_diag = jnp.concatenate([coli < rowi] * nd, axis=0)

    def head_norm(i, _):
        rows = pl.ds(pl.multiple_of(i * tq, tq), tq)
        y = q_ref[rows, :].astype(F32)
        qn_ref[rows, :] = (y * _rms_scale(y) * gq_ref[...]).astype(BF16)
        y = k_ref[rows, :].astype(F32)
        kn_ref[rows, :] = (y * _rms_scale(y) * gk_ref[...]).astype(BF16)
        return 0

    lax.fori_loop(0, seq // tq, head_norm, 0)

    group = lambda x, g: x[g * tk:(g + 1) * tk]

    def scores(q, q0, j):
        first = [q0 + (g - j) * tk for g in range(nd)]
        start = [pl.multiple_of(jnp.maximum(f, 0), tk) for f in first]
        z = jnp.concatenate(
            [lax.dot_general(group(q, g), kn_ref[pl.ds(start[g], tk), :], NT_DIMS,
                             preferred_element_type=F32) for g in range(nd)], axis=0)
        log_beta = jnp.minimum(z, 0.0) - jnp.log2(1.0 + jnp.exp2(-jnp.abs(z)))
        return first, start, log_beta, log_beta - z

    def suffix_sums(log_1m):
        hi = log_1m.astype(BF16)
        lo = (log_1m - hi.astype(F32)).astype(BF16)
        return jnp.dot(jnp.concatenate([hi, lo], axis=1), wmat, preferred_element_type=F32)

    def retire(carry, first):
        return jnp.concatenate(
            [jnp.where(first[g] >= 0, group(carry, g), SB_DEAD) for g in range(nd)], axis=0)

    def weigh(a, start):
        a = a.astype(BF16)
        return jnp.concatenate(
            [jnp.dot(group(a, g), v_ref[pl.ds(start[g], tk), :], preferred_element_type=F32)
             for g in range(nd)], axis=0)

    def sweeps(q, q0, j, n, carry, acc, mask):
        sc = [scores(q, q0, j + t) for t in range(n)]
        s2 = []
        for t, (_, _, _, log_1m) in enumerate(sc):
            if t == 0 and mask is not None:
                log_1m = jnp.where(mask, log_1m, 0.0)
            s2.append(suffix_sums(log_1m))
        for t, (first, start, log_beta, _) in enumerate(sc):
            diagonal = t == 0 and mask is not None
            if not diagonal:
                carry = retire(carry, first)
            a = jnp.exp2(log_beta + s2[t][:, :tk] + carry)
            if diagonal:
                a = jnp.where(mask, a, 0.0)
            carry = carry + s2[t][:, tk:]
            acc = acc + weigh(a, start)
        return carry, acc

    def live(c):
        return (jnp.max(c) > -SB_EXIT_LOG2).astype(jnp.int32)

    def qblock(i, _):
        q0 = pl.multiple_of(i * tq, tq)
        q = qn_ref[pl.ds(q0, tq), :]
        carry = jnp.zeros((tq, tk), F32)
        acc = jnp.zeros((tq, SB_DH), F32)
        carry, acc = sweeps(q, q0, 0, SB_PEELED_SWEEPS, carry, acc, below_diag)

        def cond(st):
            j, go, _, _ = st
            return jnp.logical_and(go > 0, j < (i + 1) * nd)

        def body(st):
            j, _, carry, acc = st
            carry, acc = sweeps(q, q0, j, SB_LOOP_SWEEPS, carry, acc, None)
            return j + SB_LOOP_SWEEPS, live(carry), carry, acc

        _, _, _, acc = lax.while_loop(cond, body, (SB_PEELED_SWEEPS, live(carry), carry, acc))
        o_ref[pl.ds(q0, tq), :] = (acc * _rms_scale(acc) * gn_ref[...]).astype(o_ref.dtype)
        return 0

    lax.fori_loop(0, seq // tq, qblock, 0)


def _sb(proj, gq, gk, gn, *, col0, batch, seq, tq=1024, tk=128):
    m = proj.shape[0]
    tq = min(tq, seq)
    assert tq % tk == 0 and seq % tq == 0
    jj = lax.broadcasted_iota(jnp.int32, (2 * tk, 2 * tk), 0) % tk
    ss = lax.broadcasted_iota(jnp.int32, (2 * tk, 2 * tk), 1)
    wmat = jnp.logical_or(ss >= tk, jj > ss).astype(BF16)
    vec = pl.BlockSpec((1, SB_DH), lambda b, h: (0, 0))
    return pl.pallas_call(
        functools.partial(_sb_kernel, tq=tq, tk=tk, seq=seq),
        grid=(batch, SB_HEADS),
        in_specs=[pl.BlockSpec((seq, SB_DH), lambda b, h: (b, col0 + h)),
                  pl.BlockSpec((seq, SB_DH), lambda b, h: (b, col0 + SB_HEADS + h)),
                  pl.BlockSpec((seq, SB_DH), lambda b, h: (b, col0 + 2 * SB_HEADS + h)),
                  pl.BlockSpec((2 * tk, 2 * tk), lambda b, h: (0, 0)),
                  vec, vec, vec],
        out_specs=pl.BlockSpec((seq, SB_DH), lambda b, h: (b, h)),
        out_shape=jax.ShapeDtypeStruct((m, SB_HEADS * SB_DH), BF16),
        scratch_shapes=[pltpu.VMEM((seq, SB_DH), BF16), pltpu.VMEM((seq, SB_DH), BF16)],
        compiler_params=_params("parallel", "parallel"),
        name="stickbreak",
    )(proj, proj, proj, wmat, gq, gk, gn)


def _layer(x, xb, ss, l, next_gain, w_in, w_gate_up, b_gate, gla_out_norm, sb_q_norm,
           sb_k_norm, sb_out_norm, w_o, mlp_norm, w_up, w_down, *, batch, seq):
    hk = GLA_HEADS * GLA_DK
    hv = GLA_HEADS * GLA_DV
    n_gla = 2 * hk + 2 * hv
    pad = LANES - 3 * GLA_GATE_RANK
    w_gla = w_in[l, :, :n_gla].astype(BF16)
    w_sb = w_in[l, :, n_gla + GLA_GATE_RANK:].astype(BF16)
    w_lr = jnp.pad(jnp.tile(w_in[l, :, n_gla:n_gla + GLA_GATE_RANK], (1, 3)), ((0, 0), (0, pad)))
    wg = jnp.pad(jnp.tile(w_gate_up[l], (3, 1)), ((0, pad), (0, 0)))
    gq = (sb_q_norm[l] * (math.log2(math.e) / math.sqrt(SB_DH))).reshape(1, SB_DH)

    p_gla = _matmul(_identity, xb, ss, w_gla, out_dtype=BF16, name="inproj_gla")
    p_sb = _matmul(_identity, xb, ss, w_sb, out_dtype=BF16, name="inproj_sb")
    o_gla = _gla(p_gla, xb, ss, w_lr, wg, b_gate[l].reshape(1, hk),
                 gla_out_norm[l].reshape(1, GLA_DV), batch=batch, seq=seq)
    o_sb = _sb(p_sb, gq, sb_k_norm[l].reshape(1, SB_DH), sb_out_norm[l].reshape(1, SB_DH),
               col0=0, batch=batch, seq=seq)
    x, xb, ss = _oproj(o_gla, o_sb, w_o, x, mlp_norm[l], layer=l)
    up = _matmul(_relu2, xb, ss, w_up, layer=l, out_dtype=BF16, name="mlp_up")
    if next_gain is None:
        return _down(up, w_down, x, None, layer=l), None, None
    return _down(up, w_down, x, next_gain, layer=l)


def kernel(x, attn_norm, w_in, w_gate_up, b_gate, gla_out_norm, sb_q_norm, sb_k_norm,
           sb_out_norm, w_o, mlp_norm, w_up, w_down):
    batch, seq, d = x.shape
    depth = w_in.shape[0]
    y = x.reshape(batch * seq, d)
    yb, ss = _prenorm(y, attn_norm[0])
    for l in range(depth):
        next_gain = attn_norm[l + 1] if l + 1 < depth else None
        y, yb, ss = _layer(y, yb, ss, l, next_gain, w_in, w_gate_up, b_gate, gla_out_norm,
                           sb_q_norm, sb_k_norm, sb_out_norm, w_o, mlp_norm, w_up, w_down,
                           batch=batch, seq=seq)
    return y.reshape(batch, seq, d)
```

```python
import functools
import math

import jax
import jax.numpy as jnp
from jax import lax
from jax.experimental import pallas as pl
from jax.experimental.pallas import tpu as pltpu

EPS = 1e-6
LANES = 128

GLA_HEADS = 8
GLA_DK = 64
GLA_DV = 128
GLA_GATE_RANK = 16
GLA_GATE_TAU = 16.0
GLA_CHUNK = 64
SB_HEADS = 8
SB_DH = 128

SB_EXIT_LOG2 = 126.0
SB_DEAD = -1e30
SB_PEELED_SWEEPS = 2
SB_LOOP_SWEEPS = 1

VMEM_LIMIT = 56 * 1024 * 1024

F32 = jnp.float32
BF16 = jnp.bfloat16
NT_DIMS = (((1,), (1,)), ((), ()))
TN_DIMS = (((0,), (0,)), ((), ()))


def _params(*sem):
    return pltpu.CompilerParams(dimension_semantics=sem, vmem_limit_bytes=VMEM_LIMIT)


def _log_sigmoid(z):
    return jnp.minimum(z, 0.0) - jnp.log(1.0 + jnp.exp(-jnp.abs(z)))


def _rms_scale(y):
    return lax.rsqrt(jnp.mean(y * y, axis=-1, keepdims=True) + EPS)


def _sumsq(y):
    return jnp.broadcast_to(jnp.sum(y * y, axis=-1, keepdims=True), (y.shape[0], LANES))


def _row_scale(ss_ref, d):
    return lax.rsqrt(jnp.sum(ss_ref[...], axis=0) * (1.0 / d) + EPS)


def _emit_normed(y, g_ref, o_ref, xb_ref, ss_ref):
    o_ref[...] = y
    xb_ref[...] = (y * g_ref[...]).astype(BF16)
    ss_ref[...] = _sumsq(y)


def _prenorm_kernel(x_ref, g_ref, xb_ref, ss_ref):
    x = x_ref[...]
    xb_ref[...] = (x * g_ref[...]).astype(BF16)
    ss_ref[...] = _sumsq(x)


def _prenorm(x, gain, rows=256):
    m, d = x.shape
    rows = min(rows, m)
    return pl.pallas_call(
        _prenorm_kernel,
        grid=(m // rows,),
        in_specs=[pl.BlockSpec((rows, d), lambda i: (i, 0)),
                  pl.BlockSpec((1, d), lambda i: (0, 0))],
        out_specs=[pl.BlockSpec((rows, d), lambda i: (i, 0)),
                   pl.BlockSpec((None, rows, LANES), lambda i: (0, i, 0))],
        out_shape=[jax.ShapeDtypeStruct((m, d), BF16),
                   jax.ShapeDtypeStruct((1, m, LANES), F32)],
        compiler_params=_params("parallel"),
        name="prenorm",
    )(x, gain.reshape(1, d))


def _w_spec(w, rows, cols, index_map, layer):
    if w.ndim == 2:
        return pl.BlockSpec((rows, cols), index_map)
    return pl.BlockSpec((None, rows, cols), lambda *g: (layer,) + tuple(index_map(*g)))


def _mm_kernel(a_ref, w_ref, ss_ref, o_ref, *, post):
    acc = jnp.dot(a_ref[...], w_ref[...].astype(BF16), preferred_element_type=F32)
    r = _row_scale(ss_ref, a_ref.shape[1])
    for c in range(acc.shape[1] // LANES):
        sl = slice(c * LANES, (c + 1) * LANES)
        o_ref[:, sl] = post(acc[:, sl] * r).astype(o_ref.dtype)


def _identity(y):
    return y


def _relu2(y):
    return jnp.square(jnp.maximum(y, 0.0))


def _matmul(post, a, ss, w, *, layer=0, out_dtype, tm=1024, tn=1024, name):
    m, k = a.shape
    n = w.shape[-1]
    tm = min(tm, m)
    tn = min(tn, n)
    return pl.pallas_call(
        functools.partial(_mm_kernel, post=post),
        grid=(n // tn, m // tm),
        in_specs=[pl.BlockSpec((tm, k), lambda j, i: (i, 0)),
                  _w_spec(w, k, tn, lambda j, i: (0, j), layer),
                  pl.BlockSpec((ss.shape[0], tm, LANES), lambda j, i: (0, i, 0))],
        out_specs=pl.BlockSpec((tm, tn), lambda j, i: (i, j)),
        out_shape=jax.ShapeDtypeStruct((m, n), out_dtype),
        compiler_params=_params("parallel", "arbitrary"),
        name=name,
    )(a, w, ss)


def _normed_out(m, n, tm, tn, tile_map, slot_map):
    specs = [pl.BlockSpec((tm, tn), tile_map), pl.BlockSpec((tm, tn), tile_map),
             pl.BlockSpec((None, tm, LANES), slot_map)]
    shapes = [jax.ShapeDtypeStruct((m, n), F32), jax.ShapeDtypeStruct((m, n), BF16),
              jax.ShapeDtypeStruct((n // tn, m, LANES), F32)]
    return specs, shapes


def _oproj_kernel(a1_ref, a2_ref, w1_ref, w2_ref, x_ref, g_ref, o_ref, xb_ref, ss_ref):
    acc = jnp.dot(a1_ref[...], w1_ref[...].astype(BF16), preferred_element_type=F32)
    acc += jnp.dot(a2_ref[...], w2_ref[...].astype(BF16), preferred_element_type=F32)
    _emit_normed(x_ref[...] + acc, g_ref, o_ref, xb_ref, ss_ref)


def _oproj(a1, a2, w, x, gain, *, layer, tm=1024, tn=1024):
    m, kh = a1.shape
    n = w.shape[-1]
    tm = min(tm, m)
    out_specs, out_shape = _normed_out(m, n, tm, tn, lambda j, i: (i, j), lambda j, i: (j, i, 0))
    return pl.pallas_call(
        _oproj_kernel,
        grid=(n // tn, m // tm),
        in_specs=[pl.BlockSpec((tm, kh), lambda j, i: (i, 0)),
                  pl.BlockSpec((tm, kh), lambda j, i: (i, 0)),
                  _w_spec(w, kh, tn, lambda j, i: (0, j), layer),
                  _w_spec(w, kh, tn, lambda j, i: (1, j), layer),
                  pl.BlockSpec((tm, tn), lambda j, i: (i, j)),
                  pl.BlockSpec((1, tn), lambda j, i: (0, j))],
        out_specs=out_specs,
        out_shape=out_shape,
        compiler_params=_params("parallel", "arbitrary"),
        name="oproj",
    )(a1, a2, w, w, x, gain.reshape(1, n))


def _down_kernel(a_ref, w_ref, x_ref, *rest, normed):
    acc_ref = rest[-1]
    kk = pl.program_id(2)

    @pl.when(kk == 0)
    def _():
        acc_ref[...] = jnp.zeros_like(acc_ref)

    acc_ref[...] += jnp.dot(a_ref[...], w_ref[...].astype(BF16), preferred_element_type=F32)

    @pl.when(kk == pl.num_programs(2) - 1)
    def _():
        y = x_ref[...] + acc_ref[...]
        if normed:
            _emit_normed(y, *rest[:-1])
        else:
            rest[0][...] = y


def _down(a, w, x, gain, *, layer, tm=1024, tn=1024, tk=2048):
    m, k = a.shape
    n = w.shape[-1]
    tm = min(tm, m)
    tile_map = lambda i, j, kk: (i, j)
    in_specs = [pl.BlockSpec((tm, tk), lambda i, j, kk: (i, kk)),
                _w_spec(w, tk, tn, lambda i, j, kk: (kk, j), layer),
                pl.BlockSpec((tm, tn), tile_map)]
    args = [a, w, x]
    if gain is None:
        out_specs = pl.BlockSpec((tm, tn), tile_map)
        out_shape = jax.ShapeDtypeStruct((m, n), F32)
    else:
        in_specs.append(pl.BlockSpec((1, tn), lambda i, j, kk: (0, j)))
        args.append(gain.reshape(1, n))
        out_specs, out_shape = _normed_out(m, n, tm, tn, tile_map, lambda i, j, kk: (j, i, 0))
    return pl.pallas_call(
        functools.partial(_down_kernel, normed=gain is not None),
        grid=(m // tm, n // tn, k // tk),
        in_specs=in_specs,
        out_specs=out_specs,
        out_shape=out_shape,
        scratch_shapes=[pltpu.VMEM((tm, tn), F32)],
        compiler_params=_params("parallel", "arbitrary", "arbitrary"),
        name="down",
    )(*args)


def _gla_kernel(q_ref, k_ref, v_ref, gate_ref, xb_ref, ss_ref, wlr_ref, wg_ref, bg_ref, gn_ref,
                o_ref, state_ref, loga_ref, *, tb):
    @pl.when(pl.program_id(1) == 0)
    def _():
        state_ref[...] = jnp.zeros_like(state_ref)

    c = GLA_CHUNK
    row = lax.broadcasted_iota(jnp.int32, (c, c), 0)
    col = lax.broadcasted_iota(jnp.int32, (c, c), 1)
    causal = col <= row
    tri = jnp.where(causal, 1.0, 0.0).astype(BF16)
    tri2 = jnp.concatenate([tri, tri], axis=1)
    lane = lax.broadcasted_iota(jnp.int32, (c, LANES), 1)
    head_mask = (lane < GLA_DK, lane >= GLA_DK)

    r = GLA_GATE_RANK
    lr = _row_scale(ss_ref, xb_ref.shape[1]) * jnp.dot(
        xb_ref[...], wlr_ref[...].astype(BF16), preferred_element_type=F32)
    lr_hi = lr.astype(BF16)
    lr_lo = (lr - lr_hi.astype(F32)).astype(BF16)
    lr_lane = lax.broadcasted_iota(jnp.int32, lr.shape, 1)
    lr_mix = jnp.where(jnp.logical_and(lr_lane >= r, lr_lane < 2 * r), lr_lo, lr_hi)
    wg = wg_ref[...]
    wg_hi = wg.astype(BF16)
    wg_lo = (wg - wg_hi.astype(F32)).astype(BF16)
    wg_mix = jnp.where(lax.broadcasted_iota(jnp.int32, wg.shape, 0) >= 2 * r, wg_lo, wg_hi)
    logits = jnp.dot(lr_mix, wg_mix, preferred_element_type=F32) + bg_ref[...]
    loga_ref[...] = _log_sigmoid(logits) * (1.0 / GLA_GATE_TAU)

    def chunk(ci, carry):
        r0 = pl.multiple_of(ci * c, c)
        rows = pl.ds(r0, c)
        log_a = loga_ref[rows, :]
        hi = log_a.astype(BF16)
        lo = (log_a - hi.astype(F32)).astype(BF16)
        bcum = jnp.dot(tri2, jnp.concatenate([hi, lo], axis=0), preferred_element_type=F32)
        b_last = bcum[c - 1:c, :]
        kk = k_ref[rows, :].astype(F32)
        q_e = q_ref[rows, :].astype(F32) * (GLA_DK ** -0.5) * jnp.exp(bcum)
        k_e = (kk * jnp.exp(-bcum)).astype(BF16)
        k_dec = (kk * jnp.exp(b_last - bcum)).astype(BF16)
        decay = jnp.exp(b_last)
        heads = range(GLA_HEADS)
        pair = lambda x, h: x[:, (h // 2) * LANES:(h // 2 + 1) * LANES]
        hsl = lambda h: slice(h * GLA_DV, (h + 1) * GLA_DV)
        qm = [jnp.where(head_mask[h % 2], pair(q_e, h), 0.0).astype(BF16) for h in heads]
        vb = [v_ref[rows, hsl(h)] for h in heads]
        s = [lax.dot_general(qm[h], pair(k_e, h), NT_DIMS, preferred_element_type=F32)
             for h in heads]
        st = [state_ref[h] for h in heads]
        o_inter = [lax.dot_general(qm[h], st[h].astype(BF16), NT_DIMS,
                                   preferred_element_type=F32) for h in heads]
        u_t = [lax.dot_general(vb[h], pair(k_dec, h), TN_DIMS, preferred_element_type=F32)
               for h in heads]
        for h in heads:
            state_ref[h] = pair(decay, h) * st[h] + u_t[h]
        o_intra = [jnp.dot(jnp.where(causal, s[h], 0.0).astype(BF16), vb[h],
                           preferred_element_type=F32) for h in heads]
        for h in heads:
            o = o_intra[h] + o_inter[h]
            y = o * _rms_scale(o) * gn_ref[...]
            g = gate_ref[rows, hsl(h)].astype(F32)
            y = y * (g * (1.0 / (1.0 + jnp.exp(-g))))
            o_ref[rows, hsl(h)] = y.astype(o_ref.dtype)
        return carry

    lax.fori_loop(0, tb // c, chunk, 0, unroll=4)


def _gla(proj, xb, ss, w_lr, wg, bg, gn, *, batch, seq, tb=512):
    m = proj.shape[0]
    d = xb.shape[1]
    tb = min(tb, seq)
    nt = seq // tb
    hk = GLA_HEADS * GLA_DK
    hv = GLA_HEADS * GLA_DV
    row = lambda b, t: b * nt + t
    return pl.pallas_call(
        functools.partial(_gla_kernel, tb=tb),
        grid=(batch, nt),
        in_specs=[pl.BlockSpec((tb, hk), lambda b, t: (row(b, t), 0)),
                  pl.BlockSpec((tb, hk), lambda b, t: (row(b, t), 1)),
                  pl.BlockSpec((tb, hv), lambda b, t: (row(b, t), 1)),
                  pl.BlockSpec((tb, hv), lambda b, t: (row(b, t), 2)),
                  pl.BlockSpec((tb, d), lambda b, t: (row(b, t), 0)),
                  pl.BlockSpec((ss.shape[0], tb, LANES), lambda b, t: (0, row(b, t), 0)),
                  pl.BlockSpec((d, LANES), lambda b, t: (0, 0)),
                  pl.BlockSpec((LANES, hk), lambda b, t: (0, 0)),
                  pl.BlockSpec((1, hk), lambda b, t: (0, 0)),
                  pl.BlockSpec((1, GLA_DV), lambda b, t: (0, 0))],
        out_specs=pl.BlockSpec((tb, hv), lambda b, t: (row(b, t), 0)),
        out_shape=jax.ShapeDtypeStruct((m, hv), BF16),
        scratch_shapes=[pltpu.VMEM((GLA_HEADS, GLA_DV, LANES), F32),
                        pltpu.VMEM((tb, hk), F32)],
        compiler_params=_params("parallel", "arbitrary"),
        name="gla",
    )(proj, proj, proj, proj, xb, ss, w_lr, wg, bg, gn)


def _sb_kernel(q_ref, k_ref, v_ref, w_ref, gq_ref, gk_ref, gn_ref, o_ref, qn_ref, kn_ref,
               *, tq, tk, seq):
    wmat = w_ref[...]
    ones = wmat[:, tk:]
    nd = tq // tk
    rowi = lax.broadcasted_iota(jnp.int32, (tk, tk), 0)
    coli = lax.broadcasted_iota(jnp.int32, (tk, tk), 1)
    below_diag = jnp.concatenate([coli < rowi] * nd, axis=0)

    def normed(y, g_ref):
        ms = jnp.dot((y * y).astype(BF16), ones, preferred_element_type=F32) * (1.0 / SB_DH)
        return (y * lax.rsqrt(ms + EPS) * g_ref[...]).astype(BF16)

    def head_norm(i, _):
        rows = pl.ds(pl.multiple_of(i * tq, tq), tq)
        qn_ref[rows, :] = normed(q_ref[rows, :].astype(F32), gq_ref)
        kn_ref[rows, :] = normed(k_ref[rows, :].astype(F32), gk_ref)
        return 0

    lax.fori_loop(0, seq // tq, head_norm, 0)

    group = lambda x, g: x[g * tk:(g + 1) * tk]

    def scores(q, q0, j):
        first = [q0 + (g - j) * tk for g in range(nd)]
        start = [pl.multiple_of(jnp.maximum(f, 0), tk) for f in first]
        z = jnp.concatenate(
            [lax.dot_general(group(q, g), kn_ref[pl.ds(start[g], tk), :], NT_DIMS,
                             preferred_element_type=F32) for g in range(nd)], axis=0)
        log_beta = jnp.minimum(z, 0.0) - jnp.log2(1.0 + jnp.exp2(-jnp.abs(z)))
        return first, start, log_beta, log_beta - z

    def suffix_sums(log_1m):
        return jnp.dot(log_1m.astype(BF16), wmat, preferred_element_type=F32)

    def retire(carry, first, j):
        keeps = lambda g: isinstance(j, int) and g >= j
        return jnp.concatenate(
            [group(carry, g) if keeps(g) else jnp.where(first[g] >= 0, group(carry, g), SB_DEAD)
             for g in range(nd)], axis=0)

    def weigh(a, start):
        a = a.astype(BF16)
        return jnp.concatenate(
            [jnp.dot(group(a, g), v_ref[pl.ds(start[g], tk), :], preferred_element_type=F32)
             for g in range(nd)], axis=0)

    def sweeps(q, q0, j, n, carry, acc, mask):
        sc = [scores(q, q0, j + t) for t in range(n)]
        s2 = []
        for t, (_, _, _, log_1m) in enumerate(sc):
            if t == 0 and mask is not None:
                log_1m = jnp.where(mask, log_1m, 0.0)
            s2.append(suffix_sums(log_1m))
        for t, (first, start, log_beta, _) in enumerate(sc):
            diagonal = t == 0 and mask is not None
            if not diagonal:
                carry = retire(carry, first, j + t)
            a = jnp.exp2(log_beta + s2[t][:, :tk] + carry)
            if diagonal:
                a = jnp.where(mask, a, 0.0)
            carry = carry + s2[t][:, tk:]
            acc = acc + weigh(a, start)
        return carry, acc

    def live(c):
        return (jnp.max(c) > -SB_EXIT_LOG2).astype(jnp.int32)

    def qblock(i, _):
        q0 = pl.multiple_of(i * tq, tq)
        q = qn_ref[pl.ds(q0, tq), :]
        carry = jnp.zeros((tq, tk), F32)
        acc = jnp.zeros((tq, SB_DH), F32)
        carry, acc = sweeps(q, q0, 0, SB_PEELED_SWEEPS, carry, acc, below_diag)

        def cond(st):
            j, go, _, _ = st
            return jnp.logical_and(go > 0, j < (i + 1) * nd)

        def body(st):
            j, _, carry, acc = st
            carry, acc = sweeps(q, q0, j, SB_LOOP_SWEEPS, carry, acc, None)
            return j + SB_LOOP_SWEEPS, live(carry), carry, acc

        _, _, _, acc = lax.while_loop(cond, body, (SB_PEELED_SWEEPS, live(carry), carry, acc))
        o_ref[pl.ds(q0, tq), :] = (acc * _rms_scale(acc) * gn_ref[...]).astype(o_ref.dtype)
        return 0

    lax.fori_loop(0, seq // tq, qblock, 0)


def _sb(proj, gq, gk, gn, *, col0, batch, seq, tq=1024, tk=128):
    m = proj.shape[0]
    tq = min(tq, seq)
    assert tq % tk == 0 and seq % tq == 0
    jj = lax.broadcasted_iota(jnp.int32, (tk, 2 * tk), 0)
    ss = lax.broadcasted_iota(jnp.int32, (tk, 2 * tk), 1)
    wmat = jnp.logical_or(ss >= tk, jj > ss).astype(BF16)
    vec = pl.BlockSpec((1, SB_DH), lambda b, h: (0, 0))
    return pl.pallas_call(
        functools.partial(_sb_kernel, tq=tq, tk=tk, seq=seq),
        grid=(batch, SB_HEADS),
        in_specs=[pl.BlockSpec((seq, SB_DH), lambda b, h: (b, col0 + h)),
                  pl.BlockSpec((seq, SB_DH), lambda b, h: (b, col0 + SB_HEADS + h)),
                  pl.BlockSpec((seq, SB_DH), lambda b, h: (b, col0 + 2 * SB_HEADS + h)),
                  pl.BlockSpec((tk, 2 * tk), lambda b, h: (0, 0)),
                  vec, vec, vec],
        out_specs=pl.BlockSpec((seq, SB_DH), lambda b, h: (b, h)),
        out_shape=jax.ShapeDtypeStruct((m, SB_HEADS * SB_DH), BF16),
        scratch_shapes=[pltpu.VMEM((seq, SB_DH), BF16), pltpu.VMEM((seq, SB_DH), BF16)],
        compiler_params=_params("parallel", "parallel"),
        name="stickbreak",
    )(proj, proj, proj, wmat, gq, gk, gn)


def _layer(x, xb, ss, l, next_gain, w_in, w_gate_up, b_gate, gla_out_norm, sb_q_norm,
           sb_k_norm, sb_out_norm, w_o, mlp_norm, w_up, w_down, *, batch, seq):
    hk = GLA_HEADS * GLA_DK
    hv = GLA_HEADS * GLA_DV
    n_gla = 2 * hk + 2 * hv
    pad = LANES - 3 * GLA_GATE_RANK
    w_heads = jnp.concatenate([w_in[l, :, :n_gla], w_in[l, :, n_gla + GLA_GATE_RANK:]],
                              axis=1).astype(BF16)
    w_lr = jnp.pad(jnp.tile(w_in[l, :, n_gla:n_gla + GLA_GATE_RANK], (1, 3)), ((0, 0), (0, pad)))
    wg = jnp.pad(jnp.tile(w_gate_up[l], (3, 1)), ((0, pad), (0, 0)))
    gq = (sb_q_norm[l] * (math.log2(math.e) / math.sqrt(SB_DH))).reshape(1, SB_DH)

    proj = _matmul(_identity, xb, ss, w_heads, out_dtype=BF16, name="inproj")
    o_gla = _gla(proj, xb, ss, w_lr, wg, b_gate[l].reshape(1, hk),
                 gla_out_norm[l].reshape(1, GLA_DV), batch=batch, seq=seq)
    o_sb = _sb(proj, gq, sb_k_norm[l].reshape(1, SB_DH), sb_out_norm[l].reshape(1, SB_DH),
               col0=n_gla // LANES, batch=batch, seq=seq)
    x, xb, ss = _oproj(o_gla, o_sb, w_o, x, mlp_norm[l], layer=l)
    up = _matmul(_relu2, xb, ss, w_up, layer=l, out_dtype=BF16, name="mlp_up")
    if next_gain is None:
        return _down(up, w_down, x, None, layer=l), None, None
    return _down(up, w_down, x, next_gain, layer=l)


def kernel(x, attn_norm, w_in, w_gate_up, b_gate, gla_out_norm, sb_q_norm, sb_k_norm,
           sb_out_norm, w_o, mlp_norm, w_up, w_down):
    batch, seq, d = x.shape
    depth = w_in.shape[0]
    y = x.reshape(batch * seq, d)
    yb, ss = _prenorm(y, attn_norm[0])
    for l in range(depth):
        next_gain = attn_norm[l + 1] if l + 1 < depth else None
        y, yb, ss = _layer(y, yb, ss, l, next_gain, w_in, w_gate_up, b_gate, gla_out_norm,
                           sb_q_norm, sb_k_norm, sb_out_norm, w_o, mlp_norm, w_up, w_down,
                           batch=batch, seq=seq)
    return y.reshape(batch, seq, d)
```

```python
import functools
import math

import jax
import jax.numpy as jnp
from jax import lax
from jax.experimental import pallas as pl
from jax.experimental.pallas import tpu as pltpu

EPS = 1e-6
LANES = 128

GLA_HEADS = 8
GLA_DK = 64
GLA_DV = 128
GLA_GATE_RANK = 16
GLA_GATE_TAU = 16.0
GLA_CHUNK = 64
SB_HEADS = 8
SB_DH = 128

SB_EXIT_LOG2 = 126.0
SB_DEAD = -1e30
SB_PEELED_SWEEPS = 2
SB_LOOP_SWEEPS = 1

VMEM_LIMIT = 56 * 1024 * 1024

F32 = jnp.float32
BF16 = jnp.bfloat16
NT_DIMS = (((1,), (1,)), ((), ()))
TN_DIMS = (((0,), (0,)), ((), ()))


def _params(*sem):
    return pltpu.CompilerParams(dimension_semantics=sem, vmem_limit_bytes=VMEM_LIMIT)


def _log_sigmoid(z):
    return jnp.minimum(z, 0.0) - jnp.log(1.0 + jnp.exp(-jnp.abs(z)))


def _rms_scale(y):
    return lax.rsqrt(jnp.mean(y * y, axis=-1, keepdims=True) + EPS)


def _sumsq(y):
    return jnp.broadcast_to(jnp.sum(y * y, axis=-1, keepdims=True), (y.shape[0], LANES))


def _row_scale(ss_ref, d):
    return lax.rsqrt(jnp.sum(ss_ref[...], axis=0) * (1.0 / d) + EPS)


def _emit_normed(y, g_ref, o_ref, xb_ref, ss_ref):
    o_ref[...] = y
    xb_ref[...] = (y * g_ref[...]).astype(BF16)
    ss_ref[...] = _sumsq(y)


def _prenorm_kernel(x_ref, g_ref, xb_ref, ss_ref):
    x = x_ref[...]
    xb_ref[...] = (x * g_ref[...]).astype(BF16)
    ss_ref[...] = _sumsq(x)


def _prenorm(x, gain, rows=256):
    m, d = x.shape
    rows = min(rows, m)
    return pl.pallas_call(
        _prenorm_kernel,
        grid=(m // rows,),
        in_specs=[pl.BlockSpec((rows, d), lambda i: (i, 0)),
                  pl.BlockSpec((1, d), lambda i: (0, 0))],
        out_specs=[pl.BlockSpec((rows, d), lambda i: (i, 0)),
                   pl.BlockSpec((None, rows, LANES), lambda i: (0, i, 0))],
        out_shape=[jax.ShapeDtypeStruct((m, d), BF16),
                   jax.ShapeDtypeStruct((1, m, LANES), F32)],
        compiler_params=_params("parallel"),
        name="prenorm",
    )(x, gain.reshape(1, d))


def _w_spec(w, rows, cols, index_map, layer):
    if w.ndim == 2:
        return pl.BlockSpec((rows, cols), index_map)
    return pl.BlockSpec((None, rows, cols), lambda *g: (layer,) + tuple(index_map(*g)))


def _mm_kernel(a_ref, *refs, post, tiles):
    w_refs, ss_ref, o_ref = refs[:-2], refs[-2], refs[-1]

    def tile(w_ref):
        acc = jnp.dot(a_ref[...], w_ref[...].astype(BF16), preferred_element_type=F32)
        r = _row_scale(ss_ref, a_ref.shape[1])
        for c in range(acc.shape[1] // LANES):
            sl = slice(c * LANES, (c + 1) * LANES)
            o_ref[:, sl] = post(acc[:, sl] * r).astype(o_ref.dtype)

    if len(w_refs) == 1:
        tile(w_refs[0])
        return
    j = pl.program_id(0)
    first = 0
    for w_ref, nt in zip(w_refs, tiles):
        pl.when(jnp.logical_and(j >= first, j < first + nt))(functools.partial(tile, w_ref))
        first += nt


def _identity(y):
    return y


def _relu2(y):
    return jnp.square(jnp.maximum(y, 0.0))


def _matmul(post, a, ss, ws, *, layer=0, out_dtype, tm=1024, tn=1024, name):
    m, k = a.shape
    tm = min(tm, m)
    tiles = [w.shape[-1] // tn for w in ws]
    w_specs, first = [], 0
    for w, nt in zip(ws, tiles):
        col = functools.partial(lambda j, first, nt: jnp.clip(j - first, 0, nt - 1),
                                first=first, nt=nt)
        w_specs.append(_w_spec(w, k, tn, lambda j, i, col=col: (0, col(j)), layer))
        first += nt
    return pl.pallas_call(
        functools.partial(_mm_kernel, post=post, tiles=tiles),
        grid=(first, m // tm),
        in_specs=[pl.BlockSpec((tm, k), lambda j, i: (i, 0)), *w_specs,
                  pl.BlockSpec((ss.shape[0], tm, LANES), lambda j, i: (0, i, 0))],
        out_specs=pl.BlockSpec((tm, tn), lambda j, i: (i, j)),
        out_shape=jax.ShapeDtypeStruct((m, first * tn), out_dtype),
        compiler_params=_params("parallel", "arbitrary"),
        name=name,
    )(a, *ws, ss)


def _normed_out(m, n, tm, tn, tile_map, slot_map):
    specs = [pl.BlockSpec((tm, tn), tile_map), pl.BlockSpec((tm, tn), tile_map),
             pl.BlockSpec((None, tm, LANES), slot_map)]
    shapes = [jax.ShapeDtypeStruct((m, n), F32), jax.ShapeDtypeStruct((m, n), BF16),
              jax.ShapeDtypeStruct((n // tn, m, LANES), F32)]
    return specs, shapes


def _oproj_kernel(a1_ref, a2_ref, w1_ref, w2_ref, x_ref, g_ref, o_ref, xb_ref, ss_ref):
    acc = jnp.dot(a1_ref[...], w1_ref[...].astype(BF16), preferred_element_type=F32)
    acc += jnp.dot(a2_ref[...], w2_ref[...].astype(BF16), preferred_element_type=F32)
    _emit_normed(x_ref[...] + acc, g_ref, o_ref, xb_ref, ss_ref)


def _oproj(a1, a2, w, x, gain, *, layer, tm=1024, tn=1024):
    m, kh = a1.shape
    n = w.shape[-1]
    tm = min(tm, m)
    out_specs, out_shape = _normed_out(m, n, tm, tn, lambda j, i: (i, j), lambda j, i: (j, i, 0))
    return pl.pallas_call(
        _oproj_kernel,
        grid=(n // tn, m // tm),
        in_specs=[pl.BlockSpec((tm, kh), lambda j, i: (i, 0)),
                  pl.BlockSpec((tm, kh), lambda j, i: (i, 0)),
                  _w_spec(w, kh, tn, lambda j, i: (0, j), layer),
                  _w_spec(w, kh, tn, lambda j, i: (1, j), layer),
                  pl.BlockSpec((tm, tn), lambda j, i: (i, j)),
                  pl.BlockSpec((1, tn), lambda j, i: (0, j))],
        out_specs=out_specs,
        out_shape=out_shape,
        compiler_params=_params("parallel", "arbitrary"),
        name="oproj",
    )(a1, a2, w, w, x, gain.reshape(1, n))


def _down_kernel(a_ref, w_ref, x_ref, *rest, normed):
    acc_ref = rest[-1]
    kk = pl.program_id(2)

    @pl.when(kk == 0)
    def _():
        acc_ref[...] = jnp.zeros_like(acc_ref)

    acc_ref[...] += jnp.dot(a_ref[...], w_ref[...].astype(BF16), preferred_element_type=F32)

    @pl.when(kk == pl.num_programs(2) - 1)
    def _():
        y = x_ref[...] + acc_ref[...]
        if normed:
            _emit_normed(y, *rest[:-1])
        else:
            rest[0][...] = y


def _down(a, w, x, gain, *, layer, tm=1024, tn=1024, tk=2048):
    m, k = a.shape
    n = w.shape[-1]
    tm = min(tm, m)
    tile_map = lambda i, j, kk: (i, j)
    in_specs = [pl.BlockSpec((tm, tk), lambda i, j, kk: (i, kk)),
                _w_spec(w, tk, tn, lambda i, j, kk: (kk, j), layer),
                pl.BlockSpec((tm, tn), tile_map)]
    args = [a, w, x]
    if gain is None:
        out_specs = pl.BlockSpec((tm, tn), tile_map)
        out_shape = jax.ShapeDtypeStruct((m, n), F32)
    else:
        in_specs.append(pl.BlockSpec((1, tn), lambda i, j, kk: (0, j)))
        args.append(gain.reshape(1, n))
        out_specs, out_shape = _normed_out(m, n, tm, tn, tile_map, lambda i, j, kk: (j, i, 0))
    return pl.pallas_call(
        functools.partial(_down_kernel, normed=gain is not None),
        grid=(m // tm, n // tn, k // tk),
        in_specs=in_specs,
        out_specs=out_specs,
        out_shape=out_shape,
        scratch_shapes=[pltpu.VMEM((tm, tn), F32)],
        compiler_params=_params("parallel", "arbitrary", "arbitrary"),
        name="down",
    )(*args)


def _gla_kernel(q_ref, k_ref, v_ref, gate_ref, xb_ref, ss_ref, wlr_ref, wg_ref, bg_ref, gn_ref,
                o_ref, state_ref, loga_ref, *, tb):
    @pl.when(pl.program_id(1) == 0)
    def _():
        state_ref[...] = jnp.zeros_like(state_ref)

    c = GLA_CHUNK
    row = lax.broadcasted_iota(jnp.int32, (c, c), 0)
    col = lax.broadcasted_iota(jnp.int32, (c, c), 1)
    causal = col <= row
    tri = jnp.where(causal, 1.0, 0.0).astype(BF16)
    tri2 = jnp.concatenate([tri, tri], axis=1)
    lane = lax.broadcasted_iota(jnp.int32, (c, LANES), 1)
    head_mask = (lane < GLA_DK, lane >= GLA_DK)

    r = GLA_GATE_RANK
    lr = _row_scale(ss_ref, xb_ref.shape[1]) * jnp.dot(
        xb_ref[...], wlr_ref[...].astype(BF16), preferred_element_type=F32)
    lr_hi = lr.astype(BF16)
    lr_lo = (lr - lr_hi.astype(F32)).astype(BF16)
    lr_lane = lax.broadcasted_iota(jnp.int32, lr.shape, 1)
    lr_mix = jnp.where(jnp.logical_and(lr_lane >= r, lr_lane < 2 * r), lr_lo, lr_hi)
    wg = wg_ref[...]
    wg_hi = wg.astype(BF16)
    wg_lo = (wg - wg_hi.astype(F32)).astype(BF16)
    wg_mix = jnp.where(lax.broadcasted_iota(jnp.int32, wg.shape, 0) >= 2 * r, wg_lo, wg_hi)
    logits = jnp.dot(lr_mix, wg_mix, preferred_element_type=F32) + bg_ref[...]
    loga_ref[...] = _log_sigmoid(logits) * (1.0 / GLA_GATE_TAU)

    def chunk(ci, carry):
        r0 = pl.multiple_of(ci * c, c)
        rows = pl.ds(r0, c)
        log_a = loga_ref[rows, :]
        hi = log_a.astype(BF16)
        lo = (log_a - hi.astype(F32)).astype(BF16)
        bcum = jnp.dot(tri2, jnp.concatenate([hi, lo], axis=0), preferred_element_type=F32)
        b_last = bcum[c - 1:c, :]
        kk = k_ref[rows, :].astype(F32)
        q_e = q_ref[rows, :].astype(F32) * (GLA_DK ** -0.5) * jnp.exp(bcum)
        k_e = (kk * jnp.exp(-bcum)).astype(BF16)
        k_dec = (kk * jnp.exp(b_last - bcum)).astype(BF16)
        decay = jnp.exp(b_last)
        heads = range(GLA_HEADS)
        pair = lambda x, h: x[:, (h // 2) * LANES:(h // 2 + 1) * LANES]
        hsl = lambda h: slice(h * GLA_DV, (h + 1) * GLA_DV)
        qm = [jnp.where(head_mask[h % 2], pair(q_e, h), 0.0).astype(BF16) for h in heads]
        vb = [v_ref[rows, hsl(h)] for h in heads]
        s = [lax.dot_general(qm[h], pair(k_e, h), NT_DIMS, preferred_element_type=F32)
             for h in heads]
        st = [state_ref[h] for h in heads]
        o_inter = [lax.dot_general(qm[h], st[h].astype(BF16), NT_DIMS,
                                   preferred_element_type=F32) for h in heads]
        u_t = [lax.dot_general(vb[h], pair(k_dec, h), TN_DIMS, preferred_element_type=F32)
               for h in heads]
        for h in heads:
            state_ref[h] = pair(decay, h) * st[h] + u_t[h]
        o_intra = [jnp.dot(jnp.where(causal, s[h], 0.0).astype(BF16), vb[h],
                           preferred_element_type=F32) for h in heads]
        for h in heads:
            o = o_intra[h] + o_inter[h]
            y = o * _rms_scale(o) * gn_ref[...]
            g = gate_ref[rows, hsl(h)].astype(F32)
            y = y * (g * (1.0 / (1.0 + jnp.exp(-g))))
            o_ref[rows, hsl(h)] = y.astype(o_ref.dtype)
        return carry

    lax.fori_loop(0, tb // c, chunk, 0, unroll=4)


def _gla(proj, xb, ss, w_lr, wg, bg, gn, *, batch, seq, tb=512):
    m = proj.shape[0]
    d = xb.shape[1]
    tb = min(tb, seq)
    nt = seq // tb
    hk = GLA_HEADS * GLA_DK
    hv = GLA_HEADS * GLA_DV
    row = lambda b, t: b * nt + t
    return pl.pallas_call(
        functools.partial(_gla_kernel, tb=tb),
        grid=(batch, nt),
        in_specs=[pl.BlockSpec((tb, hk), lambda b, t: (row(b, t), 0)),
                  pl.BlockSpec((tb, hk), lambda b, t: (row(b, t), 1)),
                  pl.BlockSpec((tb, hv), lambda b, t: (row(b, t), 1)),
                  pl.BlockSpec((tb, hv), lambda b, t: (row(b, t), 2)),
                  pl.BlockSpec((tb, d), lambda b, t: (row(b, t), 0)),
                  pl.BlockSpec((ss.shape[0], tb, LANES), lambda b, t: (0, row(b, t), 0)),
                  pl.BlockSpec((d, LANES), lambda b, t: (0, 0)),
                  pl.BlockSpec((LANES, hk), lambda b, t: (0, 0)),
                  pl.BlockSpec((1, hk), lambda b, t: (0, 0)),
                  pl.BlockSpec((1, GLA_DV), lambda b, t: (0, 0))],
        out_specs=pl.BlockSpec((tb, hv), lambda b, t: (row(b, t), 0)),
        out_shape=jax.ShapeDtypeStruct((m, hv), BF16),
        scratch_shapes=[pltpu.VMEM((GLA_HEADS, GLA_DV, LANES), F32),
                        pltpu.VMEM((tb, hk), F32)],
        compiler_params=_params("parallel", "arbitrary"),
        name="gla",
    )(proj, proj, proj, proj, xb, ss, w_lr, wg, bg, gn)


def _sb_kernel(q_ref, k_ref, v_ref, w_ref, gq_ref, gk_ref, gn_ref, o_ref, qn_ref, kn_ref,
               *, tq, tk, seq):
    wmat = w_ref[...]
    ones = wmat[:, tk:]
    nd = tq // tk
    rowi = lax.broadcasted_iota(jnp.int32, (tk, tk), 0)
    coli = lax.broadcasted_iota(jnp.int32, (tk, tk), 1)
    below_diag = jnp.concatenate([coli < rowi] * nd, axis=0)

    def normed(y, g_ref):
        ms = jnp.dot((y * y).astype(BF16), ones, preferred_element_type=F32) * (1.0 / SB_DH)
        return (y * lax.rsqrt(ms + EPS) * g_ref[...]).astype(BF16)

    def head_norm(i, _):
        rows = pl.ds(pl.multiple_of(i * tq, tq), tq)
        qn_ref[rows, :] = normed(q_ref[rows, :].astype(F32), gq_ref)
        kn_ref[rows, :] = normed(k_ref[rows, :].astype(F32), gk_ref)
        return 0

    lax.fori_loop(0, seq // tq, head_norm, 0)

    group = lambda x, g: x[g * tk:(g + 1) * tk]

    def scores(q, q0, j):
        first = [q0 + (g - j) * tk for g in range(nd)]
        start = [pl.multiple_of(jnp.maximum(f, 0), tk) for f in first]
        z = jnp.concatenate(
            [lax.dot_general(group(q, g), kn_ref[pl.ds(start[g], tk), :], NT_DIMS,
                             preferred_element_type=F32) for g in range(nd)], axis=0)
        log_beta = jnp.minimum(z, 0.0) - jnp.log2(1.0 + jnp.exp2(-jnp.abs(z)))
        return first, start, log_beta, log_beta - z

    def suffix_sums(log_1m):
        return jnp.dot(log_1m.astype(BF16), wmat, preferred_element_type=F32)

    def retire(carry, first, j):
        keeps = lambda g: isinstance(j, int) and g >= j
        return jnp.concatenate(
            [group(carry, g) if keeps(g) else jnp.where(first[g] >= 0, group(carry, g), SB_DEAD)
             for g in range(nd)], axis=0)

    def weigh(a, start):
        a = a.astype(BF16)
        return jnp.concatenate(
            [jnp.dot(group(a, g), v_ref[pl.ds(start[g], tk), :], preferred_element_type=F32)
             for g in range(nd)], axis=0)

    def sweeps(q, q0, j, n, carry, acc, mask):
        sc = [scores(q, q0, j + t) for t in range(n)]
        s2 = []
        for t, (_, _, _, log_1m) in enumerate(sc):
            if t == 0 and mask is not None:
                log_1m = jnp.where(mask, log_1m, 0.0)
            s2.append(suffix_sums(log_1m))
        for t, (first, start, log_beta, _) in enumerate(sc):
            diagonal = t == 0 and mask is not None
            if not diagonal:
                carry = retire(carry, first, j + t)
            a = jnp.exp2(log_beta + s2[t][:, :tk] + carry)
            if diagonal:
                a = jnp.where(mask, a, 0.0)
            carry = carry + s2[t][:, tk:]
            acc = acc + weigh(a, start)
        return carry, acc

    def live(c):
        return (jnp.max(c) > -SB_EXIT_LOG2).astype(jnp.int32)

    def qblock(i, _):
        q0 = pl.multiple_of(i * tq, tq)
        q = qn_ref[pl.ds(q0, tq), :]
        carry = jnp.zeros((tq, tk), F32)
        acc = jnp.zeros((tq, SB_DH), F32)
        carry, acc = sweeps(q, q0, 0, SB_PEELED_SWEEPS, carry, acc, below_diag)

        def cond(st):
            j, go, _, _ = st
            return jnp.logical_and(go > 0, j < (i + 1) * nd)

        def body(st):
            j, _, carry, acc = st
            carry, acc = sweeps(q, q0, j, SB_LOOP_SWEEPS, carry, acc, None)
            return j + SB_LOOP_SWEEPS, live(carry), carry, acc

        _, _, _, acc = lax.while_loop(cond, body, (SB_PEELED_SWEEPS, live(carry), carry, acc))
        o_ref[pl.ds(q0, tq), :] = (acc * _rms_scale(acc) * gn_ref[...]).astype(o_ref.dtype)
        return 0

    lax.fori_loop(0, seq // tq, qblock, 0)


def _sb(proj, gq, gk, gn, *, col0, batch, seq, tq=1024, tk=128):
    m = proj.shape[0]
    tq = min(tq, seq)
    assert tq % tk == 0 and seq % tq == 0
    jj = lax.broadcasted_iota(jnp.int32, (tk, 2 * tk), 0)
    ss = lax.broadcasted_iota(jnp.int32, (tk, 2 * tk), 1)
    wmat = jnp.logical_or(ss >= tk, jj > ss).astype(BF16)
    vec = pl.BlockSpec((1, SB_DH), lambda b, h: (0, 0))
    return pl.pallas_call(
        functools.partial(_sb_kernel, tq=tq, tk=tk, seq=seq),
        grid=(batch, SB_HEADS),
        in_specs=[pl.BlockSpec((seq, SB_DH), lambda b, h: (b, col0 + h)),
                  pl.BlockSpec((seq, SB_DH), lambda b, h: (b, col0 + SB_HEADS + h)),
                  pl.BlockSpec((seq, SB_DH), lambda b, h: (b, col0 + 2 * SB_HEADS + h)),
                  pl.BlockSpec((tk, 2 * tk), lambda b, h: (0, 0)),
                  vec, vec, vec],
        out_specs=pl.BlockSpec((seq, SB_DH), lambda b, h: (b, h)),
        out_shape=jax.ShapeDtypeStruct((m, SB_HEADS * SB_DH), BF16),
        scratch_shapes=[pltpu.VMEM((seq, SB_DH), BF16), pltpu.VMEM((seq, SB_DH), BF16)],
        compiler_params=_params("parallel", "parallel"),
        name="stickbreak",
    )(proj, proj, proj, wmat, gq, gk, gn)


def _layer(x, xb, ss, l, next_gain, w_in, w_gate_up, b_gate, gla_out_norm, sb_q_norm,
           sb_k_norm, sb_out_norm, w_o, mlp_norm, w_up, w_down, *, batch, seq):
    hk = GLA_HEADS * GLA_DK
    hv = GLA_HEADS * GLA_DV
    n_gla = 2 * hk + 2 * hv
    pad = LANES - 3 * GLA_GATE_RANK
    w_gla = w_in[l, :, :n_gla]
    w_sb = w_in[l, :, n_gla + GLA_GATE_RANK:]
    w_lr = jnp.pad(jnp.tile(w_in[l, :, n_gla:n_gla + GLA_GATE_RANK], (1, 3)), ((0, 0), (0, pad)))
    wg = jnp.pad(jnp.tile(w_gate_up[l], (3, 1)), ((0, pad), (0, 0)))
    gq = (sb_q_norm[l] * (math.log2(math.e) / math.sqrt(SB_DH))).reshape(1, SB_DH)

    proj = _matmul(_identity, xb, ss, [w_gla, w_sb], out_dtype=BF16, name="inproj")
    o_gla = _gla(proj, xb, ss, w_lr, wg, b_gate[l].reshape(1, hk),
                 gla_out_norm[l].reshape(1, GLA_DV), batch=batch, seq=seq)
    o_sb = _sb(proj, gq, sb_k_norm[l].reshape(1, SB_DH), sb_out_norm[l].reshape(1, SB_DH),
               col0=n_gla // LANES, batch=batch, seq=seq)
    x, xb, ss = _oproj(o_gla, o_sb, w_o, x, mlp_norm[l], layer=l)
    up = _matmul(_relu2, xb, ss, [w_up], layer=l, out_dtype=BF16, name="mlp_up")
    if next_gain is None:
        return _down(up, w_down, x, None, layer=l), None, None
    return _down(up, w_down, x, next_gain, layer=l)


def kernel(x, attn_norm, w_in, w_gate_up, b_gate, gla_out_norm, sb_q_norm, sb_k_norm,
           sb_out_norm, w_o, mlp_norm, w_up, w_down):
    batch, seq, d = x.shape
    depth = w_in.shape[0]
    y = x.reshape(batch * seq, d)
    yb, ss = _prenorm(y, attn_norm[0])
    for l in range(depth):
        next_gain = attn_norm[l + 1] if l + 1 < depth else None
        y, yb, ss = _layer(y, yb, ss, l, next_gain, w_in, w_gate_up, b_gate, gla_out_norm,
                           sb_q_norm, sb_k_norm, sb_out_norm, w_o, mlp_norm, w_up, w_down,
                           batch=batch, seq=seq)
    return y.reshape(batch, seq, d)
```

```python
import functools
import math

import jax
import jax.numpy as jnp
from jax import lax
from jax.experimental import pallas as pl
from jax.experimental.pallas import tpu as pltpu

EPS = 1e-6
LANES = 128

GLA_HEADS = 8
GLA_DK = 64
GLA_DV = 128
GLA_GATE_RANK = 16
GLA_GATE_TAU = 16.0
GLA_CHUNK = 64
SB_HEADS = 8
SB_DH = 128

SB_EXIT_LOG2 = 126.0
SB_DEAD = -1e30
SB_PEELED_SWEEPS = 2
SB_LOOP_SWEEPS = 1

VMEM_LIMIT = 56 * 1024 * 1024

F32 = jnp.float32
BF16 = jnp.bfloat16
NT_DIMS = (((1,), (1,)), ((), ()))
TN_DIMS = (((0,), (0,)), ((), ()))


def _params(*sem):
    return pltpu.CompilerParams(dimension_semantics=sem, vmem_limit_bytes=VMEM_LIMIT)


def _log_sigmoid(z):
    return jnp.minimum(z, 0.0) - jnp.log(1.0 + jnp.exp(-jnp.abs(z)))


def _rms_scale(y):
    return lax.rsqrt(jnp.mean(y * y, axis=-1, keepdims=True) + EPS)


def _sumsq(y):
    return jnp.broadcast_to(jnp.sum(y * y, axis=-1, keepdims=True), (y.shape[0], LANES))


def _row_scale(ss_ref, d):
    return lax.rsqrt(jnp.sum(ss_ref[...], axis=0) * (1.0 / d) + EPS)


def _emit_normed(y, g_ref, o_ref, xb_ref, ss_ref):
    o_ref[...] = y
    xb_ref[...] = (y * g_ref[...]).astype(BF16)
    ss_ref[...] = _sumsq(y)


def _prenorm_kernel(x_ref, g_ref, xb_ref, ss_ref):
    x = x_ref[...]
    xb_ref[...] = (x * g_ref[...]).astype(BF16)
    ss_ref[...] = _sumsq(x)


def _prenorm(x, gain, rows=256):
    m, d = x.shape
    rows = min(rows, m)
    return pl.pallas_call(
        _prenorm_kernel,
        grid=(m // rows,),
        in_specs=[pl.BlockSpec((rows, d), lambda i: (i, 0)),
                  pl.BlockSpec((1, d), lambda i: (0, 0))],
        out_specs=[pl.BlockSpec((rows, d), lambda i: (i, 0)),
                   pl.BlockSpec((None, rows, LANES), lambda i: (0, i, 0))],
        out_shape=[jax.ShapeDtypeStruct((m, d), BF16),
                   jax.ShapeDtypeStruct((1, m, LANES), F32)],
        compiler_params=_params("parallel"),
        name="prenorm",
    )(x, gain.reshape(1, d))


def _w_spec(w, rows, cols, index_map, layer):
    if w.ndim == 2:
        return pl.BlockSpec((rows, cols), index_map)
    return pl.BlockSpec((None, rows, cols), lambda *g: (layer,) + tuple(index_map(*g)))


def _mm_kernel(a_ref, *refs, post, tiles):
    w_refs, ss_ref, o_ref = refs[:-2], refs[-2], refs[-1]

    def tile(w_ref):
        acc = jnp.dot(a_ref[...], w_ref[...].astype(BF16), preferred_element_type=F32)
        r = _row_scale(ss_ref, a_ref.shape[1])
        for c in range(acc.shape[1] // LANES):
            sl = slice(c * LANES, (c + 1) * LANES)
            o_ref[:, sl] = post(acc[:, sl] * r).astype(o_ref.dtype)

    if len(w_refs) == 1:
        tile(w_refs[0])
        return
    j = pl.program_id(0)
    first = 0
    for w_ref, nt in zip(w_refs, tiles):
        pl.when(jnp.logical_and(j >= first, j < first + nt))(functools.partial(tile, w_ref))
        first += nt


def _identity(y):
    return y


def _relu2(y):
    return jnp.square(jnp.maximum(y, 0.0))


def _matmul(post, a, ss, ws, *, layer=0, out_dtype, tm=1024, tn=1024, name):
    m, k = a.shape
    tm = min(tm, m)
    tiles = [w.shape[-1] // tn for w in ws]
    w_specs, first = [], 0
    for w, nt in zip(ws, tiles):
        col = functools.partial(lambda j, first, nt: jnp.clip(j - first, 0, nt - 1),
                                first=first, nt=nt)
        w_specs.append(_w_spec(w, k, tn, lambda j, i, col=col: (0, col(j)), layer))
        first += nt
    return pl.pallas_call(
        functools.partial(_mm_kernel, post=post, tiles=tiles),
        grid=(first, m // tm),
        in_specs=[pl.BlockSpec((tm, k), lambda j, i: (i, 0)), *w_specs,
                  pl.BlockSpec((ss.shape[0], tm, LANES), lambda j, i: (0, i, 0))],
        out_specs=pl.BlockSpec((tm, tn), lambda j, i: (i, j)),
        out_shape=jax.ShapeDtypeStruct((m, first * tn), out_dtype),
        compiler_params=_params("parallel", "arbitrary"),
        name=name,
    )(a, *ws, ss)


def _normed_out(m, n, tm, tn, tile_map, slot_map):
    specs = [pl.BlockSpec((tm, tn), tile_map), pl.BlockSpec((tm, tn), tile_map),
             pl.BlockSpec((None, tm, LANES), slot_map)]
    shapes = [jax.ShapeDtypeStruct((m, n), F32), jax.ShapeDtypeStruct((m, n), BF16),
              jax.ShapeDtypeStruct((n // tn, m, LANES), F32)]
    return specs, shapes


def _oproj_kernel(a1_ref, a2_ref, w1_ref, w2_ref, x_ref, g_ref, o_ref, xb_ref, ss_ref):
    acc = jnp.dot(a1_ref[...], w1_ref[...].astype(BF16), preferred_element_type=F32)
    acc += jnp.dot(a2_ref[...], w2_ref[...].astype(BF16), preferred_element_type=F32)
    _emit_normed(x_ref[...] + acc, g_ref, o_ref, xb_ref, ss_ref)


def _oproj(a1, a2, w, x, gain, *, layer, tm=1024, tn=1024):
    m, kh = a1.shape
    n = w.shape[-1]
    tm = min(tm, m)
    out_specs, out_shape = _normed_out(m, n, tm, tn, lambda j, i: (i, j), lambda j, i: (j, i, 0))
    return pl.pallas_call(
        _oproj_kernel,
        grid=(n // tn, m // tm),
        in_specs=[pl.BlockSpec((tm, kh), lambda j, i: (i, 0)),
                  pl.BlockSpec((tm, kh), lambda j, i: (i, 0)),
                  _w_spec(w, kh, tn, lambda j, i: (0, j), layer),
                  _w_spec(w, kh, tn, lambda j, i: (1, j), layer),
                  pl.BlockSpec((tm, tn), lambda j, i: (i, j)),
                  pl.BlockSpec((1, tn), lambda j, i: (0, j))],
        out_specs=out_specs,
        out_shape=out_shape,
        compiler_params=_params("parallel", "arbitrary"),
        name="oproj",
    )(a1, a2, w, w, x, gain.reshape(1, n))


def _down_kernel(a_ref, w_ref, x_ref, *rest, normed):
    acc_ref = rest[-1]
    kk = pl.program_id(2)

    @pl.when(kk == 0)
    def _():
        acc_ref[...] = jnp.zeros_like(acc_ref)

    acc_ref[...] += jnp.dot(a_ref[...], w_ref[...].astype(BF16), preferred_element_type=F32)

    @pl.when(kk == pl.num_programs(2) - 1)
    def _():
        y = x_ref[...] + acc_ref[...]
        if normed:
            _emit_normed(y, *rest[:-1])
        else:
            rest[0][...] = y


def _down(a, w, x, gain, *, layer, tm=1024, tn=1024, tk=2048):
    m, k = a.shape
    n = w.shape[-1]
    tm = min(tm, m)
    tile_map = lambda i, j, kk: (i, j)
    in_specs = [pl.BlockSpec((tm, tk), lambda i, j, kk: (i, kk)),
                _w_spec(w, tk, tn, lambda i, j, kk: (kk, j), layer),
                pl.BlockSpec((tm, tn), tile_map)]
    args = [a, w, x]
    if gain is None:
        out_specs = pl.BlockSpec((tm, tn), tile_map)
        out_shape = jax.ShapeDtypeStruct((m, n), F32)
    else:
        in_specs.append(pl.BlockSpec((1, tn), lambda i, j, kk: (0, j)))
        args.append(gain.reshape(1, n))
        out_specs, out_shape = _normed_out(m, n, tm, tn, tile_map, lambda i, j, kk: (j, i, 0))
    return pl.pallas_call(
        functools.partial(_down_kernel, normed=gain is not None),
        grid=(m // tm, n // tn, k // tk),
        in_specs=in_specs,
        out_specs=out_specs,
        out_shape=out_shape,
        scratch_shapes=[pltpu.VMEM((tm, tn), F32)],
        compiler_params=_params("parallel", "arbitrary", "arbitrary"),
        name="down",
    )(*args)


def _gla_kernel(q_ref, k_ref, v_ref, gate_ref, xb_ref, ss_ref, wlr_ref, wg_ref, bg_ref, gn_ref,
                o_ref, state_ref, loga_ref, *, tb):
    @pl.when(pl.program_id(1) == 0)
    def _():
        state_ref[...] = jnp.zeros_like(state_ref)

    c = GLA_CHUNK
    row = lax.broadcasted_iota(jnp.int32, (c, c), 0)
    col = lax.broadcasted_iota(jnp.int32, (c, c), 1)
    causal = col <= row
    tri = jnp.where(causal, 1.0, 0.0).astype(BF16)
    tri2 = jnp.concatenate([tri, tri], axis=1)
    lane = lax.broadcasted_iota(jnp.int32, (c, LANES), 1)
    head_mask = (lane < GLA_DK, lane >= GLA_DK)

    r = GLA_GATE_RANK
    lr = _row_scale(ss_ref, xb_ref.shape[1]) * jnp.dot(
        xb_ref[...], wlr_ref[...].astype(BF16), preferred_element_type=F32)
    lr_hi = lr.astype(BF16)
    lr_lo = (lr - lr_hi.astype(F32)).astype(BF16)
    lr_lane = lax.broadcasted_iota(jnp.int32, lr.shape, 1)
    lr_mix = jnp.where(jnp.logical_and(lr_lane >= r, lr_lane < 2 * r), lr_lo, lr_hi)
    wg = wg_ref[...]
    wg_hi = wg.astype(BF16)
    wg_lo = (wg - wg_hi.astype(F32)).astype(BF16)
    wg_mix = jnp.where(lax.broadcasted_iota(jnp.int32, wg.shape, 0) >= 2 * r, wg_lo, wg_hi)
    logits = jnp.dot(lr_mix, wg_mix, preferred_element_type=F32) + bg_ref[...]
    loga_ref[...] = _log_sigmoid(logits) * (1.0 / GLA_GATE_TAU)

    def chunk(ci, carry):
        r0 = pl.multiple_of(ci * c, c)
        rows = pl.ds(r0, c)
        log_a = loga_ref[rows, :]
        hi = log_a.astype(BF16)
        lo = (log_a - hi.astype(F32)).astype(BF16)
        bcum = jnp.dot(tri2, jnp.concatenate([hi, lo], axis=0), preferred_element_type=F32)
        b_last = bcum[c - 1:c, :]
        kk = k_ref[rows, :].astype(F32)
        q_e = q_ref[rows, :].astype(F32) * (GLA_DK ** -0.5) * jnp.exp(bcum)
        k_e = (kk * jnp.exp(-bcum)).astype(BF16)
        k_dec = (kk * jnp.exp(b_last - bcum)).astype(BF16)
        decay = jnp.exp(b_last)
        heads = range(GLA_HEADS)
        pair = lambda x, h: x[:, (h // 2) * LANES:(h // 2 + 1) * LANES]
        hsl = lambda h: slice(h * GLA_DV, (h + 1) * GLA_DV)
        qm = [jnp.where(head_mask[h % 2], pair(q_e, h), 0.0).astype(BF16) for h in heads]
        vb = [v_ref[rows, hsl(h)] for h in heads]
        s = [lax.dot_general(qm[h], pair(k_e, h), NT_DIMS, preferred_element_type=F32)
             for h in heads]
        st = [state_ref[h] for h in heads]
        o_inter = [lax.dot_general(qm[h], st[h].astype(BF16), NT_DIMS,
                                   preferred_element_type=F32) for h in heads]
        u_t = [lax.dot_general(vb[h], pair(k_dec, h), TN_DIMS, preferred_element_type=F32)
               for h in heads]
        for h in heads:
            state_ref[h] = pair(decay, h) * st[h] + u_t[h]
        o_intra = [jnp.dot(jnp.where(causal, s[h], 0.0).astype(BF16), vb[h],
                           preferred_element_type=F32) for h in heads]
        for h in heads:
            o = o_intra[h] + o_inter[h]
            y = o * _rms_scale(o) * gn_ref[...]
            g = gate_ref[rows, hsl(h)].astype(F32)
            y = y * (g * (1.0 / (1.0 + jnp.exp(-g))))
            o_ref[rows, hsl(h)] = y.astype(o_ref.dtype)
        return carry

    lax.fori_loop(0, tb // c, chunk, 0, unroll=4)


def _gla(proj, xb, ss, w_lr, wg, bg, gn, *, batch, seq, tb=512):
    m = proj.shape[0]
    d = xb.shape[1]
    tb = min(tb, seq)
    nt = seq // tb
    hk = GLA_HEADS * GLA_DK
    hv = GLA_HEADS * GLA_DV
    row = lambda b, t: b * nt + t
    return pl.pallas_call(
        functools.partial(_gla_kernel, tb=tb),
        grid=(batch, nt),
        in_specs=[pl.BlockSpec((tb, hk), lambda b, t: (row(b, t), 0)),
                  pl.BlockSpec((tb, hk), lambda b, t: (row(b, t), 1)),
                  pl.BlockSpec((tb, hv), lambda b, t: (row(b, t), 1)),
                  pl.BlockSpec((tb, hv), lambda b, t: (row(b, t), 2)),
                  pl.BlockSpec((tb, d), lambda b, t: (row(b, t), 0)),
                  pl.BlockSpec((ss.shape[0], tb, LANES), lambda b, t: (0, row(b, t), 0)),
                  pl.BlockSpec((d, LANES), lambda b, t: (0, 0)),
                  pl.BlockSpec((LANES, hk), lambda b, t: (0, 0)),
                  pl.BlockSpec((1, hk), lambda b, t: (0, 0)),
                  pl.BlockSpec((1, GLA_DV), lambda b, t: (0, 0))],
        out_specs=pl.BlockSpec((tb, hv), lambda b, t: (row(b, t), 0)),
        out_shape=jax.ShapeDtypeStruct((m, hv), BF16),
        scratch_shapes=[pltpu.VMEM((GLA_HEADS, GLA_DV, LANES), F32),
                        pltpu.VMEM((tb, hk), F32)],
        compiler_params=_params("parallel", "arbitrary"),
        name="gla",
    )(proj, proj, proj, proj, xb, ss, w_lr, wg, bg, gn)


def _sb_kernel(q_ref, k_ref, v_ref, w_ref, gq_ref, gk_ref, gn_ref, o_ref, qn_ref, kn_ref,
               *, tq, tk, seq):
    wmat = w_ref[...]
    ones = wmat[:, tk:]
    nd = tq // tk
    rowi = lax.broadcasted_iota(jnp.int32, (tk, tk), 0)
    coli = lax.broadcasted_iota(jnp.int32, (tk, tk), 1)
    below_diag = jnp.concatenate([coli < rowi] * nd, axis=0)

    def normed(y, g_ref):
        ms = jnp.dot((y * y).astype(BF16), ones, preferred_element_type=F32) * (1.0 / SB_DH)
        return (y * lax.rsqrt(ms + EPS) * g_ref[...]).astype(BF16)

    def head_norm(i, _):
        rows = pl.ds(pl.multiple_of(i * tq, tq), tq)
        qn_ref[rows, :] = normed(q_ref[rows, :].astype(F32), gq_ref)
        kn_ref[rows, :] = normed(k_ref[rows, :].astype(F32), gk_ref)
        return 0

    lax.fori_loop(0, seq // tq, head_norm, 0)

    group = lambda x, g: x[g * tk:(g + 1) * tk]

    def scores(q, q0, j):
        first = [q0 + (g - j) * tk for g in range(nd)]
        start = [pl.multiple_of(jnp.maximum(f, 0), tk) for f in first]
        z = jnp.concatenate(
            [lax.dot_general(group(q, g), kn_ref[pl.ds(start[g], tk), :], NT_DIMS,
                             preferred_element_type=F32) for g in range(nd)], axis=0)
        log_beta = jnp.minimum(z, 0.0) - jnp.log2(1.0 + jnp.exp2(-jnp.abs(z)))
        return first, start, log_beta, log_beta - z

    def suffix_sums(log_1m):
        return jnp.dot(log_1m.astype(BF16), wmat, preferred_element_type=F32)

    def retire(carry, first, j):
        keeps = lambda g: isinstance(j, int) and g >= j
        return jnp.concatenate(
            [group(carry, g) if keeps(g) else jnp.where(first[g] >= 0, group(carry, g), SB_DEAD)
             for g in range(nd)], axis=0)

    def weigh(a, start):
        a = a.astype(BF16)
        return jnp.concatenate(
            [jnp.dot(group(a, g), v_ref[pl.ds(start[g], tk), :], preferred_element_type=F32)
             for g in range(nd)], axis=0)

    def sweeps(q, q0, j, n, carry, acc, mask):
        sc = [scores(q, q0, j + t) for t in range(n)]
        s2 = []
        for t, (_, _, _, log_1m) in enumerate(sc):
            if t == 0 and mask is not None:
                log_1m = jnp.where(mask, log_1m, 0.0)
            s2.append(suffix_sums(log_1m))
        for t, (first, start, log_beta, _) in enumerate(sc):
            diagonal = t == 0 and mask is not None
            if not diagonal:
                carry = retire(carry, first, j + t)
            a = jnp.exp2(log_beta + s2[t][:, :tk] + carry)
            if diagonal:
                a = jnp.where(mask, a, 0.0)
            carry = carry + s2[t][:, tk:]
            acc = acc + weigh(a, start)
        return carry, acc

    def live(c):
        return (jnp.max(c) > -SB_EXIT_LOG2).astype(jnp.int32)

    def qblock(i, _):
        q0 = pl.multiple_of(i * tq, tq)
        q = qn_ref[pl.ds(q0, tq), :]
        carry = jnp.zeros((tq, tk), F32)
        acc = jnp.zeros((tq, SB_DH), F32)
        carry, acc = sweeps(q, q0, 0, SB_PEELED_SWEEPS, carry, acc, below_diag)

        def cond(st):
            j, go, _, _ = st
            return jnp.logical_and(go > 0, j < (i + 1) * nd)

        def body(st):
            j, _, carry, acc = st
            carry, acc = sweeps(q, q0, j, SB_LOOP_SWEEPS, carry, acc, None)
            return j + SB_LOOP_SWEEPS, live(carry), carry, acc

        _, _, _, acc = lax.while_loop(cond, body, (SB_PEELED_SWEEPS, live(carry), carry, acc))
        o_ref[pl.ds(q0, tq), :] = (acc * _rms_scale(acc) * gn_ref[...]).astype(o_ref.dtype)
        return 0

    lax.fori_loop(0, seq // tq, qblock, 0)


def _sb(proj, gq, gk, gn, *, col0, batch, seq, tq=1024, tk=128):
    m = proj.shape[0]
    tq = min(tq, seq)
    assert tq % tk == 0 and seq % tq == 0
    jj = lax.broadcasted_iota(jnp.int32, (tk, 2 * tk), 0)
    ss = lax.broadcasted_iota(jnp.int32, (tk, 2 * tk), 1)
    wmat = jnp.logical_or(ss >= tk, jj > ss).astype(BF16)
    vec = pl.BlockSpec((1, SB_DH), lambda b, h: (0, 0))
    return pl.pallas_call(
        functools.partial(_sb_kernel, tq=tq, tk=tk, seq=seq),
        grid=(batch, SB_HEADS),
        in_specs=[pl.BlockSpec((seq, SB_DH), lambda b, h: (b, col0 + h)),
                  pl.BlockSpec((seq, SB_DH), lambda b, h: (b, col0 + SB_HEADS + h)),
                  pl.BlockSpec((seq, SB_DH), lambda b, h: (b, col0 + 2 * SB_HEADS + h)),
                  pl.BlockSpec((tk, 2 * tk), lambda b, h: (0, 0)),
                  vec, vec, vec],
        out_specs=pl.BlockSpec((seq, SB_DH), lambda b, h: (b, h)),
        out_shape=jax.ShapeDtypeStruct((m, SB_HEADS * SB_DH), BF16),
        scratch_shapes=[pltpu.VMEM((seq, SB_DH), BF16), pltpu.VMEM((seq, SB_DH), BF16)],
        compiler_params=_params("parallel", "parallel"),
        name="stickbreak",
    )(proj, proj, proj, wmat, gq, gk, gn)


def _layer(x, xb, ss, l, next_gain, w_in, w_gate_up, b_gate, gla_out_norm, sb_q_norm,
           sb_k_norm, sb_out_norm, w_o, mlp_norm, w_up, w_down, *, batch, seq):
    hk = GLA_HEADS * GLA_DK
    hv = GLA_HEADS * GLA_DV
    n_gla = 2 * hk + 2 * hv
    pad = LANES - 3 * GLA_GATE_RANK
    w_gla = w_in[l, :, :n_gla]
    w_sb = w_in[l, :, n_gla + GLA_GATE_RANK:]
    w_lr = jnp.pad(jnp.tile(w_in[l, :, n_gla:n_gla + GLA_GATE_RANK], (1, 3)), ((0, 0), (0, pad)))
    wg = jnp.pad(jnp.tile(w_gate_up[l], (3, 1)), ((0, pad), (0, 0)))
    gq = (sb_q_norm[l] * (math.log2(math.e) / math.sqrt(SB_DH))).reshape(1, SB_DH)

    p_gla = _matmul(_identity, xb, ss, [w_gla], out_dtype=BF16, name="inproj_gla")
    p_sb = _matmul(_identity, xb, ss, [w_sb], out_dtype=BF16, name="inproj_sb")
    o_gla = _gla(p_gla, xb, ss, w_lr, wg, b_gate[l].reshape(1, hk),
                 gla_out_norm[l].reshape(1, GLA_DV), batch=batch, seq=seq)
    o_sb = _sb(p_sb, gq, sb_k_norm[l].reshape(1, SB_DH), sb_out_norm[l].reshape(1, SB_DH),
               col0=0, batch=batch, seq=seq)
    x, xb, ss = _oproj(o_gla, o_sb, w_o, x, mlp_norm[l], layer=l)
    up = _matmul(_relu2, xb, ss, [w_up], layer=l, out_dtype=BF16, tm=2048, name="mlp_up")
    if next_gain is None:
        return _down(up, w_down, x, None, layer=l), None, None
    return _down(up, w_down, x, next_gain, layer=l)


def kernel(x, attn_norm, w_in, w_gate_up, b_gate, gla_out_norm, sb_q_norm, sb_k_norm,
           sb_out_norm, w_o, mlp_norm, w_up, w_down):
    batch, seq, d = x.shape
    depth = w_in.shape[0]
    y = x.reshape(batch * seq, d)
    yb, ss = _prenorm(y, attn_norm[0])
    for l in range(depth):
        next_gain = attn_norm[l + 1] if l + 1 < depth else None
        y, yb, ss = _layer(y, yb, ss, l, next_gain, w_in, w_gate_up, b_gate, gla_out_norm,
                           sb_q_norm, sb_k_norm, sb_out_norm, w_o, mlp_norm, w_up, w_down,
                           batch=batch, seq=seq)
    return y.reshape(batch, seq, d)
```

```python
import functools
import math

import jax
import jax.numpy as jnp
from jax import lax
from jax.experimental import pallas as pl
from jax.experimental.pallas import tpu as pltpu

EPS = 1e-6
LANES = 128

GLA_HEADS = 8
GLA_DK = 64
GLA_DV = 128
GLA_GATE_RANK = 16
GLA_GATE_TAU = 16.0
GLA_CHUNK = 64
SB_HEADS = 8
SB_DH = 128

SB_EXIT_LOG2 = 126.0
SB_DEAD = -1e30
SB_PEELED_SWEEPS = 2
SB_LOOP_SWEEPS = 1

VMEM_LIMIT = 56 * 1024 * 1024

F32 = jnp.float32
BF16 = jnp.bfloat16
NT_DIMS = (((1,), (1,)), ((), ()))
TN_DIMS = (((0,), (0,)), ((), ()))


def _params(*sem):
    return pltpu.CompilerParams(dimension_semantics=sem, vmem_limit_bytes=VMEM_LIMIT)


def _log_sigmoid(z):
    return jnp.minimum(z, 0.0) - jnp.log(1.0 + jnp.exp(-jnp.abs(z)))


def _rms_scale(y):
    return lax.rsqrt(jnp.mean(y * y, axis=-1, keepdims=True) + EPS)


def _sumsq(y):
    return jnp.broadcast_to(jnp.sum(y * y, axis=-1, keepdims=True), (y.shape[0], LANES))


def _row_scale(ss_ref, d):
    return lax.rsqrt(jnp.sum(ss_ref[...], axis=0) * (1.0 / d) + EPS)


def _emit_normed(y, g_ref, o_ref, xb_ref, ss_ref):
    o_ref[...] = y
    xb_ref[...] = (y * g_ref[...]).astype(BF16)
    ss_ref[...] = _sumsq(y)


def _prenorm_kernel(x_ref, g_ref, xb_ref, ss_ref):
    x = x_ref[...]
    xb_ref[...] = (x * g_ref[...]).astype(BF16)
    ss_ref[...] = _sumsq(x)


def _prenorm(x, gain, rows=256):
    m, d = x.shape
    rows = min(rows, m)
    return pl.pallas_call(
        _prenorm_kernel,
        grid=(m // rows,),
        in_specs=[pl.BlockSpec((rows, d), lambda i: (i, 0)),
                  pl.BlockSpec((1, d), lambda i: (0, 0))],
        out_specs=[pl.BlockSpec((rows, d), lambda i: (i, 0)),
                   pl.BlockSpec((None, rows, LANES), lambda i: (0, i, 0))],
        out_shape=[jax.ShapeDtypeStruct((m, d), BF16),
                   jax.ShapeDtypeStruct((1, m, LANES), F32)],
        compiler_params=_params("parallel"),
        name="prenorm",
    )(x, gain.reshape(1, d))


def _w_spec(w, rows, cols, index_map, layer):
    if w.ndim == 2:
        return pl.BlockSpec((rows, cols), index_map)
    return pl.BlockSpec((None, rows, cols), lambda *g: (layer,) + tuple(index_map(*g)))


def _mm_kernel(a_ref, w_ref, ss_ref, *rest, post):
    o_ref = rest[-1] if len(rest) == 1 else rest[1]
    acc = jnp.dot(a_ref[...], w_ref[...].astype(BF16), preferred_element_type=F32)
    r = _row_scale(ss_ref, a_ref.shape[1])
    for c in range(acc.shape[1] // LANES):
        sl = slice(c * LANES, (c + 1) * LANES)
        o_ref[:, sl] = post(acc[:, sl] * r).astype(o_ref.dtype)
    if len(rest) == 3:
        rest[2][...] = rest[0][...].astype(BF16)


def _identity(y):
    return y


def _relu2(y):
    return jnp.square(jnp.maximum(y, 0.0))


def _matmul(post, a, ss, w, *, layer=0, side=None, out_dtype, tm=1024, tn=1024, name):
    m, k = a.shape
    n = w.shape[-1]
    tm = min(tm, m)
    tn = min(tn, n)
    nj, ni = n // tn, m // tm
    in_specs = [pl.BlockSpec((tm, k), lambda j, i: (i, 0)),
                _w_spec(w, k, tn, lambda j, i: (0, j), layer),
                pl.BlockSpec((ss.shape[0], tm, LANES), lambda j, i: (0, i, 0))]
    out_specs = pl.BlockSpec((tm, tn), lambda j, i: (i, j))
    out_shape = jax.ShapeDtypeStruct((m, n), out_dtype)
    args = [a, w, ss]
    if side is not None:
        rows, cols = side.shape[1] // (nj * ni), side.shape[2]
        in_specs.append(pl.BlockSpec((None, rows, cols), lambda j, i: (layer, j * ni + i, 0)))
        out_specs = [out_specs, pl.BlockSpec((rows, cols), lambda j, i: (j * ni + i, 0))]
        out_shape = [out_shape, jax.ShapeDtypeStruct(side.shape[1:], BF16)]
        args.append(side)
    return pl.pallas_call(
        functools.partial(_mm_kernel, post=post),
        grid=(nj, ni),
        in_specs=in_specs,
        out_specs=out_specs,
        out_shape=out_shape,
        compiler_params=_params("parallel", "arbitrary"),
        name=name,
    )(*args)


def _normed_out(m, n, tm, tn, tile_map, slot_map):
    specs = [pl.BlockSpec((tm, tn), tile_map), pl.BlockSpec((tm, tn), tile_map),
             pl.BlockSpec((None, tm, LANES), slot_map)]
    shapes = [jax.ShapeDtypeStruct((m, n), F32), jax.ShapeDtypeStruct((m, n), BF16),
              jax.ShapeDtypeStruct((n // tn, m, LANES), F32)]
    return specs, shapes


def _oproj_kernel(a1_ref, a2_ref, w1_ref, w2_ref, x_ref, g_ref, o_ref, xb_ref, ss_ref):
    acc = jnp.dot(a1_ref[...], w1_ref[...].astype(BF16), preferred_element_type=F32)
    acc += jnp.dot(a2_ref[...], w2_ref[...].astype(BF16), preferred_element_type=F32)
    _emit_normed(x_ref[...] + acc, g_ref, o_ref, xb_ref, ss_ref)


def _oproj(a1, a2, w, x, gain, *, layer, tm=1024, tn=1024):
    m, kh = a1.shape
    n = w.shape[-1]
    tm = min(tm, m)
    out_specs, out_shape = _normed_out(m, n, tm, tn, lambda j, i: (i, j), lambda j, i: (j, i, 0))
    return pl.pallas_call(
        _oproj_kernel,
        grid=(n // tn, m // tm),
        in_specs=[pl.BlockSpec((tm, kh), lambda j, i: (i, 0)),
                  pl.BlockSpec((tm, kh), lambda j, i: (i, 0)),
                  _w_spec(w, kh, tn, lambda j, i: (0, j), layer),
                  _w_spec(w, kh, tn, lambda j, i: (1, j), layer),
                  pl.BlockSpec((tm, tn), lambda j, i: (i, j)),
                  pl.BlockSpec((1, tn), lambda j, i: (0, j))],
        out_specs=out_specs,
        out_shape=out_shape,
        compiler_params=_params("parallel", "arbitrary"),
        name="oproj",
    )(a1, a2, w, w, x, gain.reshape(1, n))


def _down_kernel(a_ref, w_ref, x_ref, *rest, normed):
    acc_ref = rest[-1]
    kk = pl.program_id(2)

    @pl.when(kk == 0)
    def _():
        acc_ref[...] = jnp.zeros_like(acc_ref)

    acc_ref[...] += jnp.dot(a_ref[...], w_ref[...].astype(BF16), preferred_element_type=F32)

    @pl.when(kk == pl.num_programs(2) - 1)
    def _():
        y = x_ref[...] + acc_ref[...]
        if normed:
            _emit_normed(y, *rest[:-1])
        else:
            rest[0][...] = y


def _down(a, w, x, gain, *, layer, tm=1024, tn=1024, tk=2048):
    m, k = a.shape
    n = w.shape[-1]
    tm = min(tm, m)
    tile_map = lambda i, j, kk: (i, j)
    in_specs = [pl.BlockSpec((tm, tk), lambda i, j, kk: (i, kk)),
                _w_spec(w, tk, tn, lambda i, j, kk: (kk, j), layer),
                pl.BlockSpec((tm, tn), tile_map)]
    args = [a, w, x]
    if gain is None:
        out_specs = pl.BlockSpec((tm, tn), tile_map)
        out_shape = jax.ShapeDtypeStruct((m, n), F32)
    else:
        in_specs.append(pl.BlockSpec((1, tn), lambda i, j, kk: (0, j)))
        args.append(gain.reshape(1, n))
        out_specs, out_shape = _normed_out(m, n, tm, tn, tile_map, lambda i, j, kk: (j, i, 0))
    return pl.pallas_call(
        functools.partial(_down_kernel, normed=gain is not None),
        grid=(m // tm, n // tn, k // tk),
        in_specs=in_specs,
        out_specs=out_specs,
        out_shape=out_shape,
        scratch_shapes=[pltpu.VMEM((tm, tn), F32)],
        compiler_params=_params("parallel", "arbitrary", "arbitrary"),
        name="down",
    )(*args)


def _gla_kernel(q_ref, k_ref, v_ref, gate_ref, xb_ref, ss_ref, wlr_ref, wg_ref, bg_ref, gn_ref,
                o_ref, state_ref, loga_ref, *, tb):
    @pl.when(pl.program_id(1) == 0)
    def _():
        state_ref[...] = jnp.zeros_like(state_ref)

    c = GLA_CHUNK
    row = lax.broadcasted_iota(jnp.int32, (c, c), 0)
    col = lax.broadcasted_iota(jnp.int32, (c, c), 1)
    causal = col <= row
    tri = jnp.where(causal, 1.0, 0.0).astype(BF16)
    tri2 = jnp.concatenate([tri, tri], axis=1)
    lane = lax.broadcasted_iota(jnp.int32, (c, LANES), 1)
    head_mask = (lane < GLA_DK, lane >= GLA_DK)

    r = GLA_GATE_RANK
    lr = _row_scale(ss_ref, xb_ref.shape[1]) * jnp.dot(
        xb_ref[...], wlr_ref[...].astype(BF16), preferred_element_type=F32)
    lr_hi = lr.astype(BF16)
    lr_lo = (lr - lr_hi.astype(F32)).astype(BF16)
    lr_lane = lax.broadcasted_iota(jnp.int32, lr.shape, 1)
    lr_mix = jnp.where(jnp.logical_and(lr_lane >= r, lr_lane < 2 * r), lr_lo, lr_hi)
    wg = wg_ref[...]
    wg_hi = wg.astype(BF16)
    wg_lo = (wg - wg_hi.astype(F32)).astype(BF16)
    wg_mix = jnp.where(lax.broadcasted_iota(jnp.int32, wg.shape, 0) >= 2 * r, wg_lo, wg_hi)
    logits = jnp.dot(lr_mix, wg_mix, preferred_element_type=F32) + bg_ref[...]
    loga_ref[...] = _log_sigmoid(logits) * (1.0 / GLA_GATE_TAU)

    def chunk(ci, carry):
        r0 = pl.multiple_of(ci * c, c)
        rows = pl.ds(r0, c)
        log_a = loga_ref[rows, :]
        hi = log_a.astype(BF16)
        lo = (log_a - hi.astype(F32)).astype(BF16)
        bcum = jnp.dot(tri2, jnp.concatenate([hi, lo], axis=0), preferred_element_type=F32)
        b_last = bcum[c - 1:c, :]
        kk = k_ref[rows, :].astype(F32)
        q_e = q_ref[rows, :].astype(F32) * (GLA_DK ** -0.5) * jnp.exp(bcum)
        k_e = (kk * jnp.exp(-bcum)).astype(BF16)
        k_dec = (kk * jnp.exp(b_last - bcum)).astype(BF16)
        decay = jnp.exp(b_last)
        heads = range(GLA_HEADS)
        pair = lambda x, h: x[:, (h // 2) * LANES:(h // 2 + 1) * LANES]
        hsl = lambda h: slice(h * GLA_DV, (h + 1) * GLA_DV)
        qm = [jnp.where(head_mask[h % 2], pair(q_e, h), 0.0).astype(BF16) for h in heads]
        vb = [v_ref[rows, hsl(h)] for h in heads]
        s = [lax.dot_general(qm[h], pair(k_e, h), NT_DIMS, preferred_element_type=F32)
             for h in heads]
        st = [state_ref[h] for h in heads]
        o_inter = [lax.dot_general(qm[h], st[h].astype(BF16), NT_DIMS,
                                   preferred_element_type=F32) for h in heads]
        u_t = [lax.dot_general(vb[h], pair(k_dec, h), TN_DIMS, preferred_element_type=F32)
               for h in heads]
        for h in heads:
            state_ref[h] = pair(decay, h) * st[h] + u_t[h]
        o_intra = [jnp.dot(jnp.where(causal, s[h], 0.0).astype(BF16), vb[h],
                           preferred_element_type=F32) for h in heads]
        for h in heads:
            o = o_intra[h] + o_inter[h]
            y = o * _rms_scale(o) * gn_ref[...]
            g = gate_ref[rows, hsl(h)].astype(F32)
            y = y * (g * (1.0 / (1.0 + jnp.exp(-g))))
            o_ref[rows, hsl(h)] = y.astype(o_ref.dtype)
        return carry

    lax.fori_loop(0, tb // c, chunk, 0, unroll=4)


def _gla(proj, xb, ss, w_lr, wg, bg, gn, *, batch, seq, tb=512):
    m = proj.shape[0]
    d = xb.shape[1]
    tb = min(tb, seq)
    nt = seq // tb
    hk = GLA_HEADS * GLA_DK
    hv = GLA_HEADS * GLA_DV
    row = lambda b, t: b * nt + t
    return pl.pallas_call(
        functools.partial(_gla_kernel, tb=tb),
        grid=(batch, nt),
        in_specs=[pl.BlockSpec((tb, hk), lambda b, t: (row(b, t), 0)),
                  pl.BlockSpec((tb, hk), lambda b, t: (row(b, t), 1)),
                  pl.BlockSpec((tb, hv), lambda b, t: (row(b, t), 1)),
                  pl.BlockSpec((tb, hv), lambda b, t: (row(b, t), 2)),
                  pl.BlockSpec((tb, d), lambda b, t: (row(b, t), 0)),
                  pl.BlockSpec((ss.shape[0], tb, LANES), lambda b, t: (0, row(b, t), 0)),
                  pl.BlockSpec((d, LANES), lambda b, t: (0, 0)),
                  pl.BlockSpec((LANES, hk), lambda b, t: (0, 0)),
                  pl.BlockSpec((1, hk), lambda b, t: (0, 0)),
                  pl.BlockSpec((1, GLA_DV), lambda b, t: (0, 0))],
        out_specs=pl.BlockSpec((tb, hv), lambda b, t: (row(b, t), 0)),
        out_shape=jax.ShapeDtypeStruct((m, hv), BF16),
        scratch_shapes=[pltpu.VMEM((GLA_HEADS, GLA_DV, LANES), F32),
                        pltpu.VMEM((tb, hk), F32)],
        compiler_params=_params("parallel", "arbitrary"),
        name="gla",
    )(proj, proj, proj, proj, xb, ss, w_lr, wg, bg, gn)


def _sb_kernel(q_ref, k_ref, v_ref, w_ref, gq_ref, gk_ref, gn_ref, o_ref, qn_ref, kn_ref,
               *, tq, tk, seq):
    wmat = w_ref[...]
    ones = wmat[:, tk:]
    nd = tq // tk
    rowi = lax.broadcasted_iota(jnp.int32, (tk, tk), 0)
    coli = lax.broadcasted_iota(jnp.int32, (tk, tk), 1)
    below_diag = jnp.concatenate([coli < rowi] * nd, axis=0)

    def normed(y, g_ref):
        ms = jnp.dot((y * y).astype(BF16), ones, preferred_element_type=F32) * (1.0 / SB_DH)
        return (y * lax.rsqrt(ms + EPS) * g_ref[...]).astype(BF16)

    def head_norm(i, _):
        rows = pl.ds(pl.multiple_of(i * tq, tq), tq)
        qn_ref[rows, :] = normed(q_ref[rows, :].astype(F32), gq_ref)
        kn_ref[rows, :] = normed(k_ref[rows, :].astype(F32), gk_ref)
        return 0

    lax.fori_loop(0, seq // tq, head_norm, 0)

    group = lambda x, g: x[g * tk:(g + 1) * tk]

    def scores(q, q0, j):
        first = [q0 + (g - j) * tk for g in range(nd)]
        start = [pl.multiple_of(jnp.maximum(f, 0), tk) for f in first]
        z = jnp.concatenate(
            [lax.dot_general(group(q, g), kn_ref[pl.ds(start[g], tk), :], NT_DIMS,
                             preferred_element_type=F32) for g in range(nd)], axis=0)
        log_beta = jnp.minimum(z, 0.0) - jnp.log2(1.0 + jnp.exp2(-jnp.abs(z)))
        return first, start, log_beta, log_beta - z

    def suffix_sums(log_1m):
        return jnp.dot(log_1m.astype(BF16), wmat, preferred_element_type=F32)

    def retire(carry, first, j):
        keeps = lambda g: isinstance(j, int) and g >= j
        return jnp.concatenate(
            [group(carry, g) if keeps(g) else jnp.where(first[g] >= 0, group(carry, g), SB_DEAD)
             for g in range(nd)], axis=0)

    def weigh(a, start):
        a = a.astype(BF16)
        return jnp.concatenate(
            [jnp.dot(group(a, g), v_ref[pl.ds(start[g], tk), :], preferred_element_type=F32)
             for g in range(nd)], axis=0)

    def sweeps(q, q0, j, n, carry, acc, mask):
        sc = [scores(q, q0, j + t) for t in range(n)]
        s2 = []
        for t, (_, _, _, log_1m) in enumerate(sc):
            if t == 0 and mask is not None:
                log_1m = jnp.where(mask, log_1m, 0.0)
            s2.append(suffix_sums(log_1m))
        for t, (first, start, log_beta, _) in enumerate(sc):
            diagonal = t == 0 and mask is not None
            if not diagonal:
                carry = retire(carry, first, j + t)
            a = jnp.exp2(log_beta + s2[t][:, :tk] + carry)
            if diagonal:
                a = jnp.where(mask, a, 0.0)
            carry = carry + s2[t][:, tk:]
            acc = acc + weigh(a, start)
        return carry, acc

    def live(c):
        return (jnp.max(c) > -SB_EXIT_LOG2).astype(jnp.int32)

    def qblock(i, _):
        q0 = pl.multiple_of(i * tq, tq)
        q = qn_ref[pl.ds(q0, tq), :]
        carry = jnp.zeros((tq, tk), F32)
        acc = jnp.zeros((tq, SB_DH), F32)
        carry, acc = sweeps(q, q0, 0, SB_PEELED_SWEEPS, carry, acc, below_diag)

        def cond(st):
            j, go, _, _ = st
            return jnp.logical_and(go > 0, j < (i + 1) * nd)

        def body(st):
            j, _, carry, acc = st
            carry, acc = sweeps(q, q0, j, SB_LOOP_SWEEPS, carry, acc, None)
            return j + SB_LOOP_SWEEPS, live(carry), carry, acc

        _, _, _, acc = lax.while_loop(cond, body, (SB_PEELED_SWEEPS, live(carry), carry, acc))
        o_ref[pl.ds(q0, tq), :] = (acc * _rms_scale(acc) * gn_ref[...]).astype(o_ref.dtype)
        return 0

    lax.fori_loop(0, seq // tq, qblock, 0)


def _sb(proj, gq, gk, gn, *, col0, batch, seq, tq=1024, tk=128):
    m = proj.shape[0]
    tq = min(tq, seq)
    assert tq % tk == 0 and seq % tq == 0
    jj = lax.broadcasted_iota(jnp.int32, (tk, 2 * tk), 0)
    ss = lax.broadcasted_iota(jnp.int32, (tk, 2 * tk), 1)
    wmat = jnp.logical_or(ss >= tk, jj > ss).astype(BF16)
    vec = pl.BlockSpec((1, SB_DH), lambda b, h: (0, 0))
    return pl.pallas_call(
        functools.partial(_sb_kernel, tq=tq, tk=tk, seq=seq),
        grid=(batch, SB_HEADS),
        in_specs=[pl.BlockSpec((seq, SB_DH), lambda b, h: (b, col0 + h)),
                  pl.BlockSpec((seq, SB_DH), lambda b, h: (b, col0 + SB_HEADS + h)),
                  pl.BlockSpec((seq, SB_DH), lambda b, h: (b, col0 + 2 * SB_HEADS + h)),
                  pl.BlockSpec((tk, 2 * tk), lambda b, h: (0, 0)),
                  vec, vec, vec],
        out_specs=pl.BlockSpec((seq, SB_DH), lambda b, h: (b, h)),
        out_shape=jax.ShapeDtypeStruct((m, SB_HEADS * SB_DH), BF16),
        scratch_shapes=[pltpu.VMEM((seq, SB_DH), BF16), pltpu.VMEM((seq, SB_DH), BF16)],
        compiler_params=_params("parallel", "parallel"),
        name="stickbreak",
    )(proj, proj, proj, wmat, gq, gk, gn)


def _layer(x, xb, ss, l, next_gain, w_in, w_gate_up, b_gate, gla_out_norm, sb_q_norm,
           sb_k_norm, sb_out_norm, w_o, mlp_norm, w_up, w_down, *, batch, seq):
    hk = GLA_HEADS * GLA_DK
    hv = GLA_HEADS * GLA_DV
    n_gla = 2 * hk + 2 * hv
    pad = LANES - 3 * GLA_GATE_RANK
    w_gla = w_in[l, :, :n_gla].astype(BF16)
    w_sb = w_in[l, :, n_gla + GLA_GATE_RANK:].astype(BF16)
    w_lr = jnp.pad(jnp.tile(w_in[l, :, n_gla:n_gla + GLA_GATE_RANK], (1, 3)), ((0, 0), (0, pad)))
    wg = jnp.pad(jnp.tile(w_gate_up[l], (3, 1)), ((0, pad), (0, 0)))
    gq = (sb_q_norm[l] * (math.log2(math.e) / math.sqrt(SB_DH))).reshape(1, SB_DH)

    p_gla = _matmul(_identity, xb, ss, w_gla, out_dtype=BF16, name="inproj_gla")
    p_sb = _matmul(_identity, xb, ss, w_sb, out_dtype=BF16, name="inproj_sb")
    o_gla = _gla(p_gla, xb, ss, w_lr, wg, b_gate[l].reshape(1, hk),
                 gla_out_norm[l].reshape(1, GLA_DV), batch=batch, seq=seq)
    o_sb = _sb(p_sb, gq, sb_k_norm[l].reshape(1, SB_DH), sb_out_norm[l].reshape(1, SB_DH),
               col0=0, batch=batch, seq=seq)
    x, xb, ss = _oproj(o_gla, o_sb, w_o, x, mlp_norm[l], layer=l)
    up, w_down_b = _matmul(_relu2, xb, ss, w_up, layer=l, side=w_down, out_dtype=BF16,
                           name="mlp_up")
    if next_gain is None:
        return _down(up, w_down_b, x, None, layer=l), None, None
    return _down(up, w_down_b, x, next_gain, layer=l)


def kernel(x, attn_norm, w_in, w_gate_up, b_gate, gla_out_norm, sb_q_norm, sb_k_norm,
           sb_out_norm, w_o, mlp_norm, w_up, w_down):
    batch, seq, d = x.shape
    depth = w_in.shape[0]
    y = x.reshape(batch * seq, d)
    yb, ss = _prenorm(y, attn_norm[0])
    for l in range(depth):
        next_gain = attn_norm[l + 1] if l + 1 < depth else None
        y, yb, ss = _layer(y, yb, ss, l, next_gain, w_in, w_gate_up, b_gate, gla_out_norm,
                           sb_q_norm, sb_k_norm, sb_out_norm, w_o, mlp_norm, w_up, w_down,
                           batch=batch, seq=seq)
    return y.reshape(batch, seq, d)
```

```python
import functools
import math

import jax
import jax.numpy as jnp
from jax import lax
from jax.experimental import pallas as pl
from jax.experimental.pallas import tpu as pltpu

EPS = 1e-6
LANES = 128

GLA_HEADS = 8
GLA_DK = 64
GLA_DV = 128
GLA_GATE_RANK = 16
GLA_GATE_TAU = 16.0
GLA_CHUNK = 64
SB_HEADS = 8
SB_DH = 128

SB_EXIT_LOG2 = 126.0
SB_DEAD = -1e30
SB_PEELED_SWEEPS = 2
SB_LOOP_SWEEPS = 1

VMEM_LIMIT = 56 * 1024 * 1024

F32 = jnp.float32
BF16 = jnp.bfloat16
NT_DIMS = (((1,), (1,)), ((), ()))
TN_DIMS = (((0,), (0,)), ((), ()))


def _params(*sem):
    return pltpu.CompilerParams(dimension_semantics=sem, vmem_limit_bytes=VMEM_LIMIT)


def _log_sigmoid(z):
    return jnp.minimum(z, 0.0) - jnp.log(1.0 + jnp.exp(-jnp.abs(z)))


def _rms_scale(y):
    return lax.rsqrt(jnp.mean(y * y, axis=-1, keepdims=True) + EPS)


def _sumsq(y):
    return jnp.broadcast_to(jnp.sum(y * y, axis=-1, keepdims=True), (y.shape[0], LANES))


def _row_scale(ss_ref, d):
    return lax.rsqrt(jnp.sum(ss_ref[...], axis=0) * (1.0 / d) + EPS)


def _emit_normed(y, g_ref, o_ref, xb_ref, ss_ref):
    o_ref[...] = y
    xb_ref[...] = (y * g_ref[...]).astype(BF16)
    ss_ref[...] = _sumsq(y)


def _prenorm_kernel(x_ref, g_ref, xb_ref, ss_ref):
    x = x_ref[...]
    xb_ref[...] = (x * g_ref[...]).astype(BF16)
    ss_ref[...] = _sumsq(x)


def _prenorm(x, gain, rows=256):
    m, d = x.shape
    rows = min(rows, m)
    return pl.pallas_call(
        _prenorm_kernel,
        grid=(m // rows,),
        in_specs=[pl.BlockSpec((rows, d), lambda i: (i, 0)),
                  pl.BlockSpec((1, d), lambda i: (0, 0))],
        out_specs=[pl.BlockSpec((rows, d), lambda i: (i, 0)),
                   pl.BlockSpec((None, rows, LANES), lambda i: (0, i, 0))],
        out_shape=[jax.ShapeDtypeStruct((m, d), BF16),
                   jax.ShapeDtypeStruct((1, m, LANES), F32)],
        compiler_params=_params("parallel"),
        name="prenorm",
    )(x, gain.reshape(1, d))


def _w_spec(w, rows, cols, index_map, layer):
    if w.ndim == 2:
        return pl.BlockSpec((rows, cols), index_map)
    return pl.BlockSpec((None, rows, cols), lambda *g: (layer,) + tuple(index_map(*g)))


def _mm_kernel(a_ref, w_ref, ss_ref, *rest, post):
    o_ref = rest[-1] if len(rest) == 1 else rest[1]
    acc = jnp.dot(a_ref[...], w_ref[...].astype(BF16), preferred_element_type=F32)
    r = _row_scale(ss_ref, a_ref.shape[1])
    for c in range(acc.shape[1] // LANES):
        sl = slice(c * LANES, (c + 1) * LANES)
        o_ref[:, sl] = post(acc[:, sl] * r).astype(o_ref.dtype)
    if len(rest) == 3:
        rest[2][...] = rest[0][...].astype(BF16)


def _identity(y):
    return y


def _relu2(y):
    return jnp.square(jnp.maximum(y, 0.0))


def _matmul(post, a, ss, w, *, layer=0, side=None, out_dtype, tm=1024, tn=1024, name):
    m, k = a.shape
    n = w.shape[-1]
    tm = min(tm, m)
    tn = min(tn, n)
    nj, ni = n // tn, m // tm
    in_specs = [pl.BlockSpec((tm, k), lambda j, i: (i, 0)),
                _w_spec(w, k, tn, lambda j, i: (0, j), layer),
                pl.BlockSpec((ss.shape[0], tm, LANES), lambda j, i: (0, i, 0))]
    out_specs = pl.BlockSpec((tm, tn), lambda j, i: (i, j))
    out_shape = jax.ShapeDtypeStruct((m, n), out_dtype)
    args = [a, w, ss]
    if side is not None:
        rows, cols = side.shape[1] // (nj * ni), side.shape[2]
        in_specs.append(pl.BlockSpec((None, rows, cols), lambda j, i: (layer, j * ni + i, 0)))
        out_specs = [out_specs, pl.BlockSpec((rows, cols), lambda j, i: (j * ni + i, 0))]
        out_shape = [out_shape, jax.ShapeDtypeStruct(side.shape[1:], BF16)]
        args.append(side)
    return pl.pallas_call(
        functools.partial(_mm_kernel, post=post),
        grid=(nj, ni),
        in_specs=in_specs,
        out_specs=out_specs,
        out_shape=out_shape,
        compiler_params=_params("parallel", "arbitrary"),
        name=name,
    )(*args)


def _normed_out(m, n, tm, tn, tile_map, slot_map):
    specs = [pl.BlockSpec((tm, tn), tile_map), pl.BlockSpec((tm, tn), tile_map),
             pl.BlockSpec((None, tm, LANES), slot_map)]
    shapes = [jax.ShapeDtypeStruct((m, n), F32), jax.ShapeDtypeStruct((m, n), BF16),
              jax.ShapeDtypeStruct((n // tn, m, LANES), F32)]
    return specs, shapes


def _oproj_kernel(a1_ref, a2_ref, w1_ref, w2_ref, x_ref, g_ref, o_ref, xb_ref, ss_ref):
    acc = jnp.dot(a1_ref[...], w1_ref[...].astype(BF16), preferred_element_type=F32)
    acc += jnp.dot(a2_ref[...], w2_ref[...].astype(BF16), preferred_element_type=F32)
    _emit_normed(x_ref[...] + acc, g_ref, o_ref, xb_ref, ss_ref)


def _oproj(a1, a2, w, x, gain, *, layer, tm=1024, tn=1024):
    m, kh = a1.shape
    n = w.shape[-1]
    tm = min(tm, m)
    out_specs, out_shape = _normed_out(m, n, tm, tn, lambda j, i: (i, j), lambda j, i: (j, i, 0))
    return pl.pallas_call(
        _oproj_kernel,
        grid=(n // tn, m // tm),
        in_specs=[pl.BlockSpec((tm, kh), lambda j, i: (i, 0)),
                  pl.BlockSpec((tm, kh), lambda j, i: (i, 0)),
                  _w_spec(w, kh, tn, lambda j, i: (0, j), layer),
                  _w_spec(w, kh, tn, lambda j, i: (1, j), layer),
                  pl.BlockSpec((tm, tn), lambda j, i: (i, j)),
                  pl.BlockSpec((1, tn), lambda j, i: (0, j))],
        out_specs=out_specs,
        out_shape=out_shape,
        compiler_params=_params("parallel", "arbitrary"),
        name="oproj",
    )(a1, a2, w, w, x, gain.reshape(1, n))


def _down_kernel(a_ref, w_ref, x_ref, *rest, normed):
    acc_ref = rest[-1]
    kk = pl.program_id(2)

    @pl.when(kk == 0)
    def _():
        acc_ref[...] = jnp.zeros_like(acc_ref)

    acc_ref[...] += jnp.dot(a_ref[...], w_ref[...].astype(BF16), preferred_element_type=F32)

    @pl.when(kk == pl.num_programs(2) - 1)
    def _():
        y = x_ref[...] + acc_ref[...]
        if normed:
            _emit_normed(y, *rest[:-1])
        else:
            rest[0][...] = y


def _down(a, w, x, gain, *, layer, tm=1024, tn=1024, tk=2048):
    m, k = a.shape
    n = w.shape[-1]
    tm = min(tm, m)
    tile_map = lambda i, j, kk: (i, j)
    in_specs = [pl.BlockSpec((tm, tk), lambda i, j, kk: (i, kk)),
                _w_spec(w, tk, tn, lambda i, j, kk: (kk, j), layer),
                pl.BlockSpec((tm, tn), tile_map)]
    args = [a, w, x]
    if gain is None:
        out_specs = pl.BlockSpec((tm, tn), tile_map)
        out_shape = jax.ShapeDtypeStruct((m, n), F32)
    else:
        in_specs.append(pl.BlockSpec((1, tn), lambda i, j, kk: (0, j)))
        args.append(gain.reshape(1, n))
        out_specs, out_shape = _normed_out(m, n, tm, tn, tile_map, lambda i, j, kk: (j, i, 0))
    return pl.pallas_call(
        functools.partial(_down_kernel, normed=gain is not None),
        grid=(m // tm, n // tn, k // tk),
        in_specs=in_specs,
        out_specs=out_specs,
        out_shape=out_shape,
        scratch_shapes=[pltpu.VMEM((tm, tn), F32)],
        compiler_params=_params("parallel", "arbitrary", "arbitrary"),
        name="down",
    )(*args)


def _gla_kernel(q_ref, k_ref, v_ref, gate_ref, xb_ref, ss_ref, wlr_ref, wg_ref, bg_ref, gn_ref,
                o_ref, state_ref, loga_ref, *, tb):
    @pl.when(pl.program_id(1) == 0)
    def _():
        state_ref[...] = jnp.zeros_like(state_ref)

    c = GLA_CHUNK
    row = lax.broadcasted_iota(jnp.int32, (c, c), 0)
    col = lax.broadcasted_iota(jnp.int32, (c, c), 1)
    causal = col <= row
    tri = jnp.where(causal, 1.0, 0.0).astype(BF16)
    tri2 = jnp.concatenate([tri, tri], axis=1)
    lane = lax.broadcasted_iota(jnp.int32, (c, LANES), 1)
    head_mask = (lane < GLA_DK, lane >= GLA_DK)

    r = GLA_GATE_RANK
    lr = _row_scale(ss_ref, xb_ref.shape[1]) * jnp.dot(
        xb_ref[...], wlr_ref[...].astype(BF16), preferred_element_type=F32)
    lr_hi = lr.astype(BF16)
    lr_lo = (lr - lr_hi.astype(F32)).astype(BF16)
    lr_lane = lax.broadcasted_iota(jnp.int32, lr.shape, 1)
    lr_mix = jnp.where(jnp.logical_and(lr_lane >= r, lr_lane < 2 * r), lr_lo, lr_hi)
    wg = wg_ref[...]
    wg_hi = wg.astype(BF16)
    wg_lo = (wg - wg_hi.astype(F32)).astype(BF16)
    wg_mix = jnp.where(lax.broadcasted_iota(jnp.int32, wg.shape, 0) >= 2 * r, wg_lo, wg_hi)
    logits = jnp.dot(lr_mix, wg_mix, preferred_element_type=F32) + bg_ref[...]
    loga_ref[...] = _log_sigmoid(logits) * (1.0 / GLA_GATE_TAU)

    def chunk(ci, carry):
        r0 = pl.multiple_of(ci * c, c)
        rows = pl.ds(r0, c)
        log_a = loga_ref[rows, :]
        hi = log_a.astype(BF16)
        lo = (log_a - hi.astype(F32)).astype(BF16)
        bcum = jnp.dot(tri2, jnp.concatenate([hi, lo], axis=0), preferred_element_type=F32)
        b_last = bcum[c - 1:c, :]
        kk = k_ref[rows, :].astype(F32)
        q_e = q_ref[rows, :].astype(F32) * (GLA_DK ** -0.5) * jnp.exp(bcum)
        k_e = (kk * jnp.exp(-bcum)).astype(BF16)
        k_dec = (kk * jnp.exp(b_last - bcum)).astype(BF16)
        decay = jnp.exp(b_last)
        heads = range(GLA_HEADS)
        pair = lambda x, h: x[:, (h // 2) * LANES:(h // 2 + 1) * LANES]
        hsl = lambda h: slice(h * GLA_DV, (h + 1) * GLA_DV)
        qm = [jnp.where(head_mask[h % 2], pair(q_e, h), 0.0).astype(BF16) for h in heads]
        vb = [v_ref[rows, hsl(h)] for h in heads]
        s = [lax.dot_general(qm[h], pair(k_e, h), NT_DIMS, preferred_element_type=F32)
             for h in heads]
        st = [state_ref[h] for h in heads]
        o_inter = [lax.dot_general(qm[h], st[h].astype(BF16), NT_DIMS,
                                   preferred_element_type=F32) for h in heads]
        u_t = [lax.dot_general(vb[h], pair(k_dec, h), TN_DIMS, preferred_element_type=F32)
               for h in heads]
        for h in heads:
            state_ref[h] = pair(decay, h) * st[h] + u_t[h]
        o_intra = [jnp.dot(jnp.where(causal, s[h], 0.0).astype(BF16), vb[h],
                           preferred_element_type=F32) for h in heads]
        for h in heads:
            o = o_intra[h] + o_inter[h]
            y = o * _rms_scale(o) * gn_ref[...]
            g = gate_ref[rows, hsl(h)].astype(F32)
            y = y * (g * (1.0 / (1.0 + jnp.exp(-g))))
            o_ref[rows, hsl(h)] = y.astype(o_ref.dtype)
        return carry

    lax.fori_loop(0, tb // c, chunk, 0, unroll=4)


def _gla(proj, xb, ss, w_lr, wg, bg, gn, *, batch, seq, tb=512):
    m = proj.shape[0]
    d = xb.shape[1]
    tb = min(tb, seq)
    nt = seq // tb
    hk = GLA_HEADS * GLA_DK
    hv = GLA_HEADS * GLA_DV
    row = lambda b, t: b * nt + t
    return pl.pallas_call(
        functools.partial(_gla_kernel, tb=tb),
        grid=(batch, nt),
        in_specs=[pl.BlockSpec((tb, hk), lambda b, t: (row(b, t), 0)),
                  pl.BlockSpec((tb, hk), lambda b, t: (row(b, t), 1)),
                  pl.BlockSpec((tb, hv), lambda b, t: (row(b, t), 1)),
                  pl.BlockSpec((tb, hv), lambda b, t: (row(b, t), 2)),
                  pl.BlockSpec((tb, d), lambda b, t: (row(b, t), 0)),
                  pl.BlockSpec((ss.shape[0], tb, LANES), lambda b, t: (0, row(b, t), 0)),
                  pl.BlockSpec((d, LANES), lambda b, t: (0, 0)),
                  pl.BlockSpec((LANES, hk), lambda b, t: (0, 0)),
                  pl.BlockSpec((1, hk), lambda b, t: (0, 0)),
                  pl.BlockSpec((1, GLA_DV), lambda b, t: (0, 0))],
        out_specs=pl.BlockSpec((tb, hv), lambda b, t: (row(b, t), 0)),
        out_shape=jax.ShapeDtypeStruct((m, hv), BF16),
        scratch_shapes=[pltpu.VMEM((GLA_HEADS, GLA_DV, LANES), F32),
                        pltpu.VMEM((tb, hk), F32)],
        compiler_params=_params("parallel", "arbitrary"),
        name="gla",
    )(proj, proj, proj, proj, xb, ss, w_lr, wg, bg, gn)


def _sb_kernel(q_ref, k_ref, v_ref, w_ref, gq_ref, gk_ref, gn_ref, o_ref, qn_ref, kn_ref,
               *, tq, tk, seq):
    wmat = w_ref[...]
    ones = wmat[:, tk:]
    nd = tq // tk
    rowi = lax.broadcasted_iota(jnp.int32, (tk, tk), 0)
    coli = lax.broadcasted_iota(jnp.int32, (tk, tk), 1)
    below_diag = jnp.concatenate([coli < rowi] * nd, axis=0)

    def normed(y, g_ref):
        ms = jnp.dot((y * y).astype(BF16), ones, preferred_element_type=F32) * (1.0 / SB_DH)
        return (y * lax.rsqrt(ms + EPS) * g_ref[...]).astype(BF16)

    def head_norm(i, _):
        rows = pl.ds(pl.multiple_of(i * tq, tq), tq)
        qn_ref[rows, :] = normed(q_ref[rows, :].astype(F32), gq_ref)
        kn_ref[rows, :] = normed(k_ref[rows, :].astype(F32), gk_ref)
        return 0

    lax.fori_loop(0, seq // tq, head_norm, 0)

    group = lambda x, g: x[g * tk:(g + 1) * tk]

    def scores(q, q0, j):
        first = [q0 + (g - j) * tk for g in range(nd)]
        start = [pl.multiple_of(jnp.maximum(f, 0), tk) for f in first]
        z = jnp.concatenate(
            [lax.dot_general(group(q, g), kn_ref[pl.ds(start[g], tk), :], NT_DIMS,
                             preferred_element_type=F32) for g in range(nd)], axis=0)
        log_beta = jnp.minimum(z, 0.0) - jnp.log2(1.0 + jnp.exp2(-jnp.abs(z)))
        return first, start, log_beta, log_beta - z

    def suffix_sums(log_1m):
        return jnp.dot(log_1m.astype(BF16), wmat, preferred_element_type=F32)

    def retire(carry, first, j):
        keeps = lambda g: isinstance(j, int) and g >= j
        return jnp.concatenate(
            [group(carry, g) if keeps(g) else jnp.where(first[g] >= 0, group(carry, g), SB_DEAD)
             for g in range(nd)], axis=0)

    def weigh(a, start):
        a = a.astype(BF16)
        return jnp.concatenate(
            [jnp.dot(group(a, g), v_ref[pl.ds(start[g], tk), :], preferred_element_type=F32)
             for g in range(nd)], axis=0)

    def sweeps(q, q0, j, n, carry, acc, mask):
        sc = [scores(q, q0, j + t) for t in range(n)]
        s2 = []
        for t, (_, _, _, log_1m) in enumerate(sc):
            if t == 0 and mask is not None:
                log_1m = jnp.where(mask, log_1m, 0.0)
            s2.append(suffix_sums(log_1m))
        for t, (first, start, log_beta, _) in enumerate(sc):
            diagonal = t == 0 and mask is not None
            if not diagonal:
                carry = retire(carry, first, j + t)
            a = jnp.exp2(log_beta + s2[t][:, :tk] + carry)
            if diagonal:
                a = jnp.where(mask, a, 0.0)
            carry = carry + s2[t][:, tk:]
            acc = acc + weigh(a, start)
        return carry, acc

    def live(c):
        return (jnp.max(c) > -SB_EXIT_LOG2).astype(jnp.int32)

    def qblock(i, _):
        q0 = pl.multiple_of(i * tq, tq)
        q = qn_ref[pl.ds(q0, tq), :]
        carry = jnp.zeros((tq, tk), F32)
        acc = jnp.zeros((tq, SB_DH), F32)
        carry, acc = sweeps(q, q0, 0, SB_PEELED_SWEEPS, carry, acc, below_diag)

        def cond(st):
            j, go, _, _ = st
            return jnp.logical_and(go > 0, j < (i + 1) * nd)

        def body(st):
            j, _, carry, acc = st
            carry, acc = sweeps(q, q0, j, SB_LOOP_SWEEPS, carry, acc, None)
            return j + SB_LOOP_SWEEPS, live(carry), carry, acc

        _, _, _, acc = lax.while_loop(cond, body, (SB_PEELED_SWEEPS, live(carry), carry, acc))
        o_ref[pl.ds(q0, tq), :] = (acc * _rms_scale(acc) * gn_ref[...]).astype(o_ref.dtype)
        return 0

    lax.fori_loop(0, seq // tq, qblock, 0)


def _sb(proj, gq, gk, gn, *, col0, batch, seq, tq=2048, tk=128):
    m = proj.shape[0]
    tq = min(tq, seq)
    assert tq % tk == 0 and seq % tq == 0
    jj = lax.broadcasted_iota(jnp.int32, (tk, 2 * tk), 0)
    ss = lax.broadcasted_iota(jnp.int32, (tk, 2 * tk), 1)
    wmat = jnp.logical_or(ss >= tk, jj > ss).astype(BF16)
    vec = pl.BlockSpec((1, SB_DH), lambda b, h: (0, 0))
    return pl.pallas_call(
        functools.partial(_sb_kernel, tq=tq, tk=tk, seq=seq),
        grid=(batch, SB_HEADS),
        in_specs=[pl.BlockSpec((seq, SB_DH), lambda b, h: (b, col0 + h)),
                  pl.BlockSpec((seq, SB_DH), lambda b, h: (b, col0 + SB_HEADS + h)),
                  pl.BlockSpec((seq, SB_DH), lambda b, h: (b, col0 + 2 * SB_HEADS + h)),
                  pl.BlockSpec((tk, 2 * tk), lambda b, h: (0, 0)),
                  vec, vec, vec],
        out_specs=pl.BlockSpec((seq, SB_DH), lambda b, h: (b, h)),
        out_shape=jax.ShapeDtypeStruct((m, SB_HEADS * SB_DH), BF16),
        scratch_shapes=[pltpu.VMEM((seq, SB_DH), BF16), pltpu.VMEM((seq, SB_DH), BF16)],
        compiler_params=_params("parallel", "parallel"),
        name="stickbreak",
    )(proj, proj, proj, wmat, gq, gk, gn)


def _layer(x, xb, ss, l, next_gain, w_in, w_gate_up, b_gate, gla_out_norm, sb_q_norm,
           sb_k_norm, sb_out_norm, w_o, mlp_norm, w_up, w_down, *, batch, seq):
    hk = GLA_HEADS * GLA_DK
    hv = GLA_HEADS * GLA_DV
    n_gla = 2 * hk + 2 * hv
    pad = LANES - 3 * GLA_GATE_RANK
    w_gla = w_in[l, :, :n_gla].astype(BF16)
    w_sb = w_in[l, :, n_gla + GLA_GATE_RANK:].astype(BF16)
    w_lr = jnp.pad(jnp.tile(w_in[l, :, n_gla:n_gla + GLA_GATE_RANK], (1, 3)), ((0, 0), (0, pad)))
    wg = jnp.pad(jnp.tile(w_gate_up[l], (3, 1)), ((0, pad), (0, 0)))
    gq = (sb_q_norm[l] * (math.log2(math.e) / math.sqrt(SB_DH))).reshape(1, SB_DH)

    p_gla = _matmul(_identity, xb, ss, w_gla, out_dtype=BF16, name="inproj_gla")
    p_sb = _matmul(_identity, xb, ss, w_sb, out_dtype=BF16, name="inproj_sb")
    o_gla = _gla(p_gla, xb, ss, w_lr, wg, b_gate[l].reshape(1, hk),
                 gla_out_norm[l].reshape(1, GLA_DV), batch=batch, seq=seq)
    o_sb = _sb(p_sb, gq, sb_k_norm[l].reshape(1, SB_DH), sb_out_norm[l].reshape(1, SB_DH),
               col0=0, batch=batch, seq=seq)
    x, xb, ss = _oproj(o_gla, o_sb, w_o, x, mlp_norm[l], layer=l)
    up, w_down_b = _matmul(_relu2, xb, ss, w_up, layer=l, side=w_down, out_dtype=BF16,
                           name="mlp_up")
    if next_gain is None:
        return _down(up, w_down_b, x, None, layer=l), None, None
    return _down(up, w_down_b, x, next_gain, layer=l)


def kernel(x, attn_norm, w_in, w_gate_up, b_gate, gla_out_norm, sb_q_norm, sb_k_norm,
           sb_out_norm, w_o, mlp_norm, w_up, w_down):
    batch, seq, d = x.shape
    depth = w_in.shape[0]
    y = x.reshape(batch * seq, d)
    yb, ss = _prenorm(y, attn_norm[0])
    for l in range(depth):
        next_gain = attn_norm[l + 1] if l + 1 < depth else None
        y, yb, ss = _layer(y, yb, ss, l, next_gain, w_in, w_gate_up, b_gate, gla_out_norm,
                           sb_q_norm, sb_k_norm, sb_out_norm, w_o, mlp_norm, w_up, w_down,
                           batch=batch, seq=seq)
    return y.reshape(batch, seq, d)
```

```python
import functools
import math

import jax
import jax.numpy as jnp
from jax import lax
from jax.experimental import pallas as pl
from jax.experimental.pallas import tpu as pltpu

EPS = 1e-6
LANES = 128

GLA_HEADS = 8
GLA_DK = 64
GLA_DV = 128
GLA_GATE_RANK = 16
GLA_GATE_TAU = 16.0
GLA_CHUNK = 64
SB_HEADS = 8
SB_DH = 128

SB_EXIT_LOG2 = 126.0
SB_DEAD = -1e30
SB_PEELED_SWEEPS = 2
SB_LOOP_SWEEPS = 1

V7X_VMEM_BYTES = 64 * 1024 * 1024
VMEM_LIMIT = V7X_VMEM_BYTES * 7 // 8

F32 = jnp.float32
BF16 = jnp.bfloat16
NT_DIMS = (((1,), (1,)), ((), ()))
TN_DIMS = (((0,), (0,)), ((), ()))


def _params(*sem):
    return pltpu.CompilerParams(dimension_semantics=sem, vmem_limit_bytes=VMEM_LIMIT)


def _log_sigmoid(z):
    return jnp.minimum(z, 0.0) - jnp.log(1.0 + jnp.exp(-jnp.abs(z)))


def _rms_scale(y):
    return lax.rsqrt(jnp.mean(y * y, axis=-1, keepdims=True) + EPS)


def _sumsq(y):
    return jnp.broadcast_to(jnp.sum(y * y, axis=-1, keepdims=True), (y.shape[0], LANES))


def _row_scale(ss_ref, d):
    return lax.rsqrt(jnp.sum(ss_ref[...], axis=0) * (1.0 / d) + EPS)


def _emit_normed(y, g_ref, o_ref, xb_ref, ss_ref):
    o_ref[...] = y
    xb_ref[...] = (y * g_ref[...]).astype(BF16)
    ss_ref[...] = _sumsq(y)


def _prenorm_kernel(x_ref, g_ref, xb_ref, ss_ref):
    x = x_ref[...]
    xb_ref[...] = (x * g_ref[...]).astype(BF16)
    ss_ref[...] = _sumsq(x)


def _prenorm(x, gain, rows=256):
    m, d = x.shape
    rows = min(rows, m)
    return pl.pallas_call(
        _prenorm_kernel,
        grid=(m // rows,),
        in_specs=[pl.BlockSpec((rows, d), lambda i: (i, 0)),
                  pl.BlockSpec((1, d), lambda i: (0, 0))],
        out_specs=[pl.BlockSpec((rows, d), lambda i: (i, 0)),
                   pl.BlockSpec((None, rows, LANES), lambda i: (0, i, 0))],
        out_shape=[jax.ShapeDtypeStruct((m, d), BF16),
                   jax.ShapeDtypeStruct((1, m, LANES), F32)],
        compiler_params=_params("parallel"),
        name="prenorm",
    )(x, gain.reshape(1, d))


def _w_spec(w, rows, cols, index_map, layer):
    if w.ndim == 2:
        return pl.BlockSpec((rows, cols), index_map)
    return pl.BlockSpec((None, rows, cols), lambda *g: (layer,) + tuple(index_map(*g)))


def _mm_kernel(a_ref, w_ref, ss_ref, *rest, post):
    o_ref = rest[-1] if len(rest) == 1 else rest[1]
    acc = jnp.dot(a_ref[...], w_ref[...].astype(BF16), preferred_element_type=F32)
    r = _row_scale(ss_ref, a_ref.shape[1])
    for c in range(acc.shape[1] // LANES):
        sl = slice(c * LANES, (c + 1) * LANES)
        o_ref[:, sl] = post(acc[:, sl] * r).astype(o_ref.dtype)
    if len(rest) == 3:
        rest[2][...] = rest[0][...].astype(BF16)


def _identity(y):
    return y


def _relu2(y):
    return jnp.square(jnp.maximum(y, 0.0))


def _matmul(post, a, ss, w, *, layer=0, side=None, out_dtype, tm=1024, tn=1024, name):
    m, k = a.shape
    n = w.shape[-1]
    tm = min(tm, m)
    tn = min(tn, n)
    nj, ni = n // tn, m // tm
    in_specs = [pl.BlockSpec((tm, k), lambda j, i: (i, 0)),
                _w_spec(w, k, tn, lambda j, i: (0, j), layer),
                pl.BlockSpec((ss.shape[0], tm, LANES), lambda j, i: (0, i, 0))]
    out_specs = pl.BlockSpec((tm, tn), lambda j, i: (i, j))
    out_shape = jax.ShapeDtypeStruct((m, n), out_dtype)
    args = [a, w, ss]
    if side is not None:
        rows, cols = side.shape[1] // (nj * ni), side.shape[2]
        in_specs.append(pl.BlockSpec((None, rows, cols), lambda j, i: (layer, j * ni + i, 0)))
        out_specs = [out_specs, pl.BlockSpec((rows, cols), lambda j, i: (j * ni + i, 0))]
        out_shape = [out_shape, jax.ShapeDtypeStruct(side.shape[1:], BF16)]
        args.append(side)
    return pl.pallas_call(
        functools.partial(_mm_kernel, post=post),
        grid=(nj, ni),
        in_specs=in_specs,
        out_specs=out_specs,
        out_shape=out_shape,
        compiler_params=_params("parallel", "arbitrary"),
        name=name,
    )(*args)


def _normed_out(m, n, tm, tn, tile_map, slot_map):
    specs = [pl.BlockSpec((tm, tn), tile_map), pl.BlockSpec((tm, tn), tile_map),
             pl.BlockSpec((None, tm, LANES), slot_map)]
    shapes = [jax.ShapeDtypeStruct((m, n), F32), jax.ShapeDtypeStruct((m, n), BF16),
              jax.ShapeDtypeStruct((n // tn, m, LANES), F32)]
    return specs, shapes


def _oproj_kernel(a1_ref, a2_ref, w1_ref, w2_ref, x_ref, g_ref, o_ref, xb_ref, ss_ref):
    acc = jnp.dot(a1_ref[...], w1_ref[...].astype(BF16), preferred_element_type=F32)
    acc += jnp.dot(a2_ref[...], w2_ref[...].astype(BF16), preferred_element_type=F32)
    _emit_normed(x_ref[...] + acc, g_ref, o_ref, xb_ref, ss_ref)


def _oproj(a1, a2, w, x, gain, *, layer, tm=1024, tn=1024):
    m, kh = a1.shape
    n = w.shape[-1]
    tm = min(tm, m)
    out_specs, out_shape = _normed_out(m, n, tm, tn, lambda j, i: (i, j), lambda j, i: (j, i, 0))
    return pl.pallas_call(
        _oproj_kernel,
        grid=(n // tn, m // tm),
        in_specs=[pl.BlockSpec((tm, kh), lambda j, i: (i, 0)),
                  pl.BlockSpec((tm, kh), lambda j, i: (i, 0)),
                  _w_spec(w, kh, tn, lambda j, i: (0, j), layer),
                  _w_spec(w, kh, tn, lambda j, i: (1, j), layer),
                  pl.BlockSpec((tm, tn), lambda j, i: (i, j)),
                  pl.BlockSpec((1, tn), lambda j, i: (0, j))],
        out_specs=out_specs,
        out_shape=out_shape,
        compiler_params=_params("parallel", "arbitrary"),
        name="oproj",
    )(a1, a2, w, w, x, gain.reshape(1, n))


def _down_kernel(a_ref, w_ref, x_ref, *rest, normed):
    acc_ref = rest[-1]
    kk = pl.program_id(2)

    @pl.when(kk == 0)
    def _():
        acc_ref[...] = jnp.zeros_like(acc_ref)

    acc_ref[...] += jnp.dot(a_ref[...], w_ref[...].astype(BF16), preferred_element_type=F32)

    @pl.when(kk == pl.num_programs(2) - 1)
    def _():
        y = x_ref[...] + acc_ref[...]
        if normed:
            _emit_normed(y, *rest[:-1])
        else:
            rest[0][...] = y


def _down(a, w, x, gain, *, layer, tm=1024, tn=1024, tk=2048):
    m, k = a.shape
    n = w.shape[-1]
    tm = min(tm, m)
    tile_map = lambda i, j, kk: (i, j)
    in_specs = [pl.BlockSpec((tm, tk), lambda i, j, kk: (i, kk)),
                _w_spec(w, tk, tn, lambda i, j, kk: (kk, j), layer),
                pl.BlockSpec((tm, tn), tile_map)]
    args = [a, w, x]
    if gain is None:
        out_specs = pl.BlockSpec((tm, tn), tile_map)
        out_shape = jax.ShapeDtypeStruct((m, n), F32)
    else:
        in_specs.append(pl.BlockSpec((1, tn), lambda i, j, kk: (0, j)))
        args.append(gain.reshape(1, n))
        out_specs, out_shape = _normed_out(m, n, tm, tn, tile_map, lambda i, j, kk: (j, i, 0))
    return pl.pallas_call(
        functools.partial(_down_kernel, normed=gain is not None),
        grid=(m // tm, n // tn, k // tk),
        in_specs=in_specs,
        out_specs=out_specs,
        out_shape=out_shape,
        scratch_shapes=[pltpu.VMEM((tm, tn), F32)],
        compiler_params=_params("parallel", "arbitrary", "arbitrary"),
        name="down",
    )(*args)


def _gla_kernel(q_ref, k_ref, v_ref, gate_ref, xb_ref, ss_ref, wlr_ref, wg_ref, bg_ref, gn_ref,
                o_ref, state_ref, loga_ref, *, tb):
    @pl.when(pl.program_id(1) == 0)
    def _():
        state_ref[...] = jnp.zeros_like(state_ref)

    c = GLA_CHUNK
    row = lax.broadcasted_iota(jnp.int32, (c, c), 0)
    col = lax.broadcasted_iota(jnp.int32, (c, c), 1)
    causal = col <= row
    tri = jnp.where(causal, 1.0, 0.0).astype(BF16)
    tri2 = jnp.concatenate([tri, tri], axis=1)
    lane = lax.broadcasted_iota(jnp.int32, (c, LANES), 1)
    head_mask = (lane < GLA_DK, lane >= GLA_DK)

    r = GLA_GATE_RANK
    lr = _row_scale(ss_ref, xb_ref.shape[1]) * jnp.dot(
        xb_ref[...], wlr_ref[...].astype(BF16), preferred_element_type=F32)
    lr_hi = lr.astype(BF16)
    lr_lo = (lr - lr_hi.astype(F32)).astype(BF16)
    lr_lane = lax.broadcasted_iota(jnp.int32, lr.shape, 1)
    lr_mix = jnp.where(jnp.logical_and(lr_lane >= r, lr_lane < 2 * r), lr_lo, lr_hi)
    wg = wg_ref[...]
    wg_hi = wg.astype(BF16)
    wg_lo = (wg - wg_hi.astype(F32)).astype(BF16)
    wg_mix = jnp.where(lax.broadcasted_iota(jnp.int32, wg.shape, 0) >= 2 * r, wg_lo, wg_hi)
    logits = jnp.dot(lr_mix, wg_mix, preferred_element_type=F32) + bg_ref[...]
    loga_ref[...] = _log_sigmoid(logits) * (1.0 / GLA_GATE_TAU)

    def chunk(ci, carry):
        r0 = pl.multiple_of(ci * c, c)
        rows = pl.ds(r0, c)
        log_a = loga_ref[rows, :]
        hi = log_a.astype(BF16)
        lo = (log_a - hi.astype(F32)).astype(BF16)
        bcum = jnp.dot(tri2, jnp.concatenate([hi, lo], axis=0), preferred_element_type=F32)
        b_last = bcum[c - 1:c, :]
        kk = k_ref[rows, :].astype(F32)
        q_e = q_ref[rows, :].astype(F32) * (GLA_DK ** -0.5) * jnp.exp(bcum)
        k_e = (kk * jnp.exp(-bcum)).astype(BF16)
        k_dec = (kk * jnp.exp(b_last - bcum)).astype(BF16)
        decay = jnp.exp(b_last)
        heads = range(GLA_HEADS)
        pair = lambda x, h: x[:, (h // 2) * LANES:(h // 2 + 1) * LANES]
        hsl = lambda h: slice(h * GLA_DV, (h + 1) * GLA_DV)
        qm = [jnp.where(head_mask[h % 2], pair(q_e, h), 0.0).astype(BF16) for h in heads]
        vb = [v_ref[rows, hsl(h)] for h in heads]
        s = [lax.dot_general(qm[h], pair(k_e, h), NT_DIMS, preferred_element_type=F32)
             for h in heads]
        st = [state_ref[h] for h in heads]
        o_inter = [lax.dot_general(qm[h], st[h].astype(BF16), NT_DIMS,
                                   preferred_element_type=F32) for h in heads]
        u_t = [lax.dot_general(vb[h], pair(k_dec, h), TN_DIMS, preferred_element_type=F32)
               for h in heads]
        for h in heads:
            state_ref[h] = pair(decay, h) * st[h] + u_t[h]
        o_intra = [jnp.dot(jnp.where(causal, s[h], 0.0).astype(BF16), vb[h],
                           preferred_element_type=F32) for h in heads]
        for h in heads:
            o = o_intra[h] + o_inter[h]
            y = o * _rms_scale(o) * gn_ref[...]
            g = gate_ref[rows, hsl(h)].astype(F32)
            y = y * (g * (1.0 / (1.0 + jnp.exp(-g))))
            o_ref[rows, hsl(h)] = y.astype(o_ref.dtype)
        return carry

    lax.fori_loop(0, tb // c, chunk, 0, unroll=4)


def _gla(proj, xb, ss, w_lr, wg, bg, gn, *, batch, seq, tb=512):
    m = proj.shape[0]
    d = xb.shape[1]
    tb = min(tb, seq)
    nt = seq // tb
    hk = GLA_HEADS * GLA_DK
    hv = GLA_HEADS * GLA_DV
    row = lambda b, t: b * nt + t
    return pl.pallas_call(
        functools.partial(_gla_kernel, tb=tb),
        grid=(batch, nt),
        in_specs=[pl.BlockSpec((tb, hk), lambda b, t: (row(b, t), 0)),
                  pl.BlockSpec((tb, hk), lambda b, t: (row(b, t), 1)),
                  pl.BlockSpec((tb, hv), lambda b, t: (row(b, t), 1)),
                  pl.BlockSpec((tb, hv), lambda b, t: (row(b, t), 2)),
                  pl.BlockSpec((tb, d), lambda b, t: (row(b, t), 0)),
                  pl.BlockSpec((ss.shape[0], tb, LANES), lambda b, t: (0, row(b, t), 0)),
                  pl.BlockSpec((d, LANES), lambda b, t: (0, 0)),
                  pl.BlockSpec((LANES, hk), lambda b, t: (0, 0)),
                  pl.BlockSpec((1, hk), lambda b, t: (0, 0)),
                  pl.BlockSpec((1, GLA_DV), lambda b, t: (0, 0))],
        out_specs=pl.BlockSpec((tb, hv), lambda b, t: (row(b, t), 0)),
        out_shape=jax.ShapeDtypeStruct((m, hv), BF16),
        scratch_shapes=[pltpu.VMEM((GLA_HEADS, GLA_DV, LANES), F32),
                        pltpu.VMEM((tb, hk), F32)],
        compiler_params=_params("parallel", "arbitrary"),
        name="gla",
    )(proj, proj, proj, proj, xb, ss, w_lr, wg, bg, gn)


def _sb_kernel(q_ref, k_ref, v_ref, w_ref, gq_ref, gk_ref, gn_ref, o_ref, qn_ref, kn_ref,
               *, tq, tk, seq):
    wmat = w_ref[...]
    ones = wmat[:, tk:]
    nd = tq // tk
    rowi = lax.broadcasted_iota(jnp.int32, (tk, tk), 0)
    coli = lax.broadcasted_iota(jnp.int32, (tk, tk), 1)
    below_diag = jnp.concatenate([coli < rowi] * nd, axis=0)

    def normed(y, g_ref):
        ms = jnp.dot((y * y).astype(BF16), ones, preferred_element_type=F32) * (1.0 / SB_DH)
        return (y * lax.rsqrt(ms + EPS) * g_ref[...]).astype(BF16)

    def head_norm(i, _):
        rows = pl.ds(pl.multiple_of(i * tq, tq), tq)
        qn_ref[rows, :] = normed(q_ref[rows, :].astype(F32), gq_ref)
        kn_ref[rows, :] = normed(k_ref[rows, :].astype(F32), gk_ref)
        return 0

    lax.fori_loop(0, seq // tq, head_norm, 0)

    group = lambda x, g: x[g * tk:(g + 1) * tk]

    def scores(q, q0, j):
        first = [q0 + (g - j) * tk for g in range(nd)]
        start = [pl.multiple_of(jnp.maximum(f, 0), tk) for f in first]
        z = jnp.concatenate(
            [lax.dot_general(group(q, g), kn_ref[pl.ds(start[g], tk), :], NT_DIMS,
                             preferred_element_type=F32) for g in range(nd)], axis=0)
        log_beta = jnp.minimum(z, 0.0) - jnp.log2(1.0 + jnp.exp2(-jnp.abs(z)))
        return first, start, log_beta, log_beta - z

    def suffix_sums(log_1m):
        return jnp.dot(log_1m.astype(BF16), wmat, preferred_element_type=F32)

    def retire(carry, first, j):
        keeps = lambda g: isinstance(j, int) and g >= j
        return jnp.concatenate(
            [group(carry, g) if keeps(g) else jnp.where(first[g] >= 0, group(carry, g), SB_DEAD)
             for g in range(nd)], axis=0)

    def weigh(a, start):
        a = a.astype(BF16)
        return jnp.concatenate(
            [jnp.dot(group(a, g), v_ref[pl.ds(start[g], tk), :], preferred_element_type=F32)
             for g in range(nd)], axis=0)

    def sweeps(q, q0, j, n, carry, acc, mask):
        sc = [scores(q, q0, j + t) for t in range(n)]
        s2 = []
        for t, (_, _, _, log_1m) in enumerate(sc):
            if t == 0 and mask is not None:
                log_1m = jnp.where(mask, log_1m, 0.0)
            s2.append(suffix_sums(log_1m))
        for t, (first, start, log_beta, _) in enumerate(sc):
            diagonal = t == 0 and mask is not None
            if not diagonal:
                carry = retire(carry, first, j + t)
            a = jnp.exp2(log_beta + s2[t][:, :tk] + carry)
            if diagonal:
                a = jnp.where(mask, a, 0.0)
            carry = carry + s2[t][:, tk:]
            acc = acc + weigh(a, start)
        return carry, acc

    def live(c):
        return (jnp.max(c) > -SB_EXIT_LOG2).astype(jnp.int32)

    def qblock(i, _):
        q0 = pl.multiple_of(i * tq, tq)
        q = qn_ref[pl.ds(q0, tq), :]
        carry = jnp.zeros((tq, tk), F32)
        acc = jnp.zeros((tq, SB_DH), F32)
        carry, acc = sweeps(q, q0, 0, SB_PEELED_SWEEPS, carry, acc, below_diag)

        def cond(st):
            j, go, _, _ = st
            return jnp.logical_and(go > 0, j < (i + 1) * nd)

        def body(st):
            j, _, carry, acc = st
            carry, acc = sweeps(q, q0, j, SB_LOOP_SWEEPS, carry, acc, None)
            return j + SB_LOOP_SWEEPS, live(carry), carry, acc

        _, _, _, acc = lax.while_loop(cond, body, (SB_PEELED_SWEEPS, live(carry), carry, acc))
        o_ref[pl.ds(q0, tq), :] = (acc * _rms_scale(acc) * gn_ref[...]).astype(o_ref.dtype)
        return 0

    lax.fori_loop(0, seq // tq, qblock, 0)


def _sb(proj, gq, gk, gn, *, col0, batch, seq, tq=4096, tk=128):
    m = proj.shape[0]
    tq = min(tq, seq)
    assert tq % tk == 0 and seq % tq == 0
    jj = lax.broadcasted_iota(jnp.int32, (tk, 2 * tk), 0)
    ss = lax.broadcasted_iota(jnp.int32, (tk, 2 * tk), 1)
    wmat = jnp.logical_or(ss >= tk, jj > ss).astype(BF16)
    vec = pl.BlockSpec((1, SB_DH), lambda b, h: (0, 0))
    return pl.pallas_call(
        functools.partial(_sb_kernel, tq=tq, tk=tk, seq=seq),
        grid=(batch, SB_HEADS),
        in_specs=[pl.BlockSpec((seq, SB_DH), lambda b, h: (b, col0 + h)),
                  pl.BlockSpec((seq, SB_DH), lambda b, h: (b, col0 + SB_HEADS + h)),
                  pl.BlockSpec((seq, SB_DH), lambda b, h: (b, col0 + 2 * SB_HEADS + h)),
                  pl.BlockSpec((tk, 2 * tk), lambda b, h: (0, 0)),
                  vec, vec, vec],
        out_specs=pl.BlockSpec((seq, SB_DH), lambda b, h: (b, h)),
        out_shape=jax.ShapeDtypeStruct((m, SB_HEADS * SB_DH), BF16),
        scratch_shapes=[pltpu.VMEM((seq, SB_DH), BF16), pltpu.VMEM((seq, SB_DH), BF16)],
        compiler_params=_params("parallel", "parallel"),
        name="stickbreak",
    )(proj, proj, proj, wmat, gq, gk, gn)


def _layer(x, xb, ss, l, next_gain, w_in, w_gate_up, b_gate, gla_out_norm, sb_q_norm,
           sb_k_norm, sb_out_norm, w_o, mlp_norm, w_up, w_down, *, batch, seq):
    hk = GLA_HEADS * GLA_DK
    hv = GLA_HEADS * GLA_DV
    n_gla = 2 * hk + 2 * hv
    pad = LANES - 3 * GLA_GATE_RANK
    w_gla = w_in[l, :, :n_gla].astype(BF16)
    w_sb = w_in[l, :, n_gla + GLA_GATE_RANK:].astype(BF16)
    w_lr = jnp.pad(jnp.tile(w_in[l, :, n_gla:n_gla + GLA_GATE_RANK], (1, 3)), ((0, 0), (0, pad)))
    wg = jnp.pad(jnp.tile(w_gate_up[l], (3, 1)), ((0, pad), (0, 0)))
    gq = (sb_q_norm[l] * (math.log2(math.e) / math.sqrt(SB_DH))).reshape(1, SB_DH)

    p_gla = _matmul(_identity, xb, ss, w_gla, out_dtype=BF16, name="inproj_gla")
    p_sb = _matmul(_identity, xb, ss, w_sb, out_dtype=BF16, name="inproj_sb")
    o_gla = _gla(p_gla, xb, ss, w_lr, wg, b_gate[l].reshape(1, hk),
                 gla_out_norm[l].reshape(1, GLA_DV), batch=batch, seq=seq)
    o_sb = _sb(p_sb, gq, sb_k_norm[l].reshape(1, SB_DH), sb_out_norm[l].reshape(1, SB_DH),
               col0=0, batch=batch, seq=seq)
    x, xb, ss = _oproj(o_gla, o_sb, w_o, x, mlp_norm[l], layer=l)
    up, w_down_b = _matmul(_relu2, xb, ss, w_up, layer=l, side=w_down, out_dtype=BF16,
                           name="mlp_up")
    if next_gain is None:
        return _down(up, w_down_b, x, None, layer=l), None, None
    return _down(up, w_down_b, x, next_gain, layer=l)


def kernel(x, attn_norm, w_in, w_gate_up, b_gate, gla_out_norm, sb_q_norm, sb_k_norm,
           sb_out_norm, w_o, mlp_norm, w_up, w_down):
    batch, seq, d = x.shape
    depth = w_in.shape[0]
    y = x.reshape(batch * seq, d)
    yb, ss = _prenorm(y, attn_norm[0])
    for l in range(depth):
        next_gain = attn_norm[l + 1] if l + 1 < depth else None
        y, yb, ss = _layer(y, yb, ss, l, next_gain, w_in, w_gate_up, b_gate, gla_out_norm,
                           sb_q_norm, sb_k_norm, sb_out_norm, w_o, mlp_norm, w_up, w_down,
                           batch=batch, seq=seq)
    return y.reshape(batch, seq, d)
```

```python
import functools
import math

import jax
import jax.numpy as jnp
from jax import lax
from jax.experimental import pallas as pl
from jax.experimental.pallas import tpu as pltpu

EPS = 1e-6
LANES = 128

GLA_HEADS = 8
GLA_DK = 64
GLA_DV = 128
GLA_GATE_RANK = 16
GLA_GATE_TAU = 16.0
GLA_CHUNK = 64
SB_HEADS = 8
SB_DH = 128

SB_EXIT_LOG2 = 126.0
SB_DEAD = -1e30
SB_PEELED_SWEEPS = 2
SB_LOOP_SWEEPS = 1

V7X_VMEM_BYTES = 64 * 1024 * 1024
VMEM_LIMIT = V7X_VMEM_BYTES * 7 // 8

F32 = jnp.float32
BF16 = jnp.bfloat16
NT_DIMS = (((1,), (1,)), ((), ()))
TN_DIMS = (((0,), (0,)), ((), ()))


def _params(*sem):
    return pltpu.CompilerParams(dimension_semantics=sem, vmem_limit_bytes=VMEM_LIMIT)


def _log_sigmoid(z):
    return jnp.minimum(z, 0.0) - jnp.log(1.0 + jnp.exp(-jnp.abs(z)))


def _rms_scale(y):
    return lax.rsqrt(jnp.mean(y * y, axis=-1, keepdims=True) + EPS)


def _sumsq(y):
    return jnp.broadcast_to(jnp.sum(y * y, axis=-1, keepdims=True), (y.shape[0], LANES))


def _row_scale(ss_ref, d):
    return lax.rsqrt(jnp.sum(ss_ref[...], axis=0) * (1.0 / d) + EPS)


def _emit_normed(y, g_ref, o_ref, xb_ref, ss_ref):
    o_ref[...] = y
    xb_ref[...] = (y * g_ref[...]).astype(BF16)
    ss_ref[...] = _sumsq(y)


def _prenorm_kernel(x_ref, g_ref, xb_ref, ss_ref):
    x = x_ref[...]
    xb_ref[...] = (x * g_ref[...]).astype(BF16)
    ss_ref[...] = _sumsq(x)


def _prenorm(x, gain, rows=512):
    m, d = x.shape
    rows = min(rows, m)
    return pl.pallas_call(
        _prenorm_kernel,
        grid=(m // rows,),
        in_specs=[pl.BlockSpec((rows, d), lambda i: (i, 0)),
                  pl.BlockSpec((1, d), lambda i: (0, 0))],
        out_specs=[pl.BlockSpec((rows, d), lambda i: (i, 0)),
                   pl.BlockSpec((None, rows, LANES), lambda i: (0, i, 0))],
        out_shape=[jax.ShapeDtypeStruct((m, d), BF16),
                   jax.ShapeDtypeStruct((1, m, LANES), F32)],
        compiler_params=_params("parallel"),
        name="prenorm",
    )(x, gain.reshape(1, d))


def _w_spec(w, rows, cols, index_map, layer):
    if w.ndim == 2:
        return pl.BlockSpec((rows, cols), index_map)
    return pl.BlockSpec((None, rows, cols), lambda *g: (layer,) + tuple(index_map(*g)))


def _mm_kernel(a_ref, w_ref, ss_ref, *rest, post):
    o_ref = rest[-1] if len(rest) == 1 else rest[1]
    acc = jnp.dot(a_ref[...], w_ref[...].astype(BF16), preferred_element_type=F32)
    r = _row_scale(ss_ref, a_ref.shape[1])
    for c in range(acc.shape[1] // LANES):
        sl = slice(c * LANES, (c + 1) * LANES)
        o_ref[:, sl] = post(acc[:, sl] * r).astype(o_ref.dtype)
    if len(rest) == 3:
        rest[2][...] = rest[0][...].astype(BF16)


def _identity(y):
    return y


def _relu2(y):
    return jnp.square(jnp.maximum(y, 0.0))


def _matmul(post, a, ss, w, *, layer=0, side=None, out_dtype, tm=1024, tn=1024, name):
    m, k = a.shape
    n = w.shape[-1]
    tm = min(tm, m)
    tn = min(tn, n)
    nj, ni = n // tn, m // tm
    in_specs = [pl.BlockSpec((tm, k), lambda j, i: (i, 0)),
                _w_spec(w, k, tn, lambda j, i: (0, j), layer),
                pl.BlockSpec((ss.shape[0], tm, LANES), lambda j, i: (0, i, 0))]
    out_specs = pl.BlockSpec((tm, tn), lambda j, i: (i, j))
    out_shape = jax.ShapeDtypeStruct((m, n), out_dtype)
    args = [a, w, ss]
    if side is not None:
        rows, cols = side.shape[1] // (nj * ni), side.shape[2]
        in_specs.append(pl.BlockSpec((None, rows, cols), lambda j, i: (layer, j * ni + i, 0)))
        out_specs = [out_specs, pl.BlockSpec((rows, cols), lambda j, i: (j * ni + i, 0))]
        out_shape = [out_shape, jax.ShapeDtypeStruct(side.shape[1:], BF16)]
        args.append(side)
    return pl.pallas_call(
        functools.partial(_mm_kernel, post=post),
        grid=(nj, ni),
        in_specs=in_specs,
        out_specs=out_specs,
        out_shape=out_shape,
        compiler_params=_params("parallel", "arbitrary"),
        name=name,
    )(*args)


def _normed_out(m, n, tm, tn, tile_map, slot_map):
    specs = [pl.BlockSpec((tm, tn), tile_map), pl.BlockSpec((tm, tn), tile_map),
             pl.BlockSpec((None, tm, LANES), slot_map)]
    shapes = [jax.ShapeDtypeStruct((m, n), F32), jax.ShapeDtypeStruct((m, n), BF16),
              jax.ShapeDtypeStruct((n // tn, m, LANES), F32)]
    return specs, shapes


def _oproj_kernel(a1_ref, a2_ref, w1_ref, w2_ref, x_ref, g_ref, o_ref, xb_ref, ss_ref):
    acc = jnp.dot(a1_ref[...], w1_ref[...], preferred_element_type=F32)
    acc += jnp.dot(a2_ref[...], w2_ref[...], preferred_element_type=F32)
    _emit_normed(x_ref[...] + acc, g_ref, o_ref, xb_ref, ss_ref)


def _oproj(a1, a2, w, x, gain, *, tm=512):
    m, kh = a1.shape
    n = w.shape[-1]
    tm = min(tm, m)
    out_specs, out_shape = _normed_out(m, n, tm, n, lambda i: (i, 0), lambda i: (0, i, 0))
    return pl.pallas_call(
        _oproj_kernel,
        grid=(m // tm,),
        in_specs=[pl.BlockSpec((tm, kh), lambda i: (i, 0)),
                  pl.BlockSpec((tm, kh), lambda i: (i, 0)),
                  pl.BlockSpec((kh, n), lambda i: (0, 0)),
                  pl.BlockSpec((kh, n), lambda i: (1, 0)),
                  pl.BlockSpec((tm, n), lambda i: (i, 0)),
                  pl.BlockSpec((1, n), lambda i: (0, 0))],
        out_specs=out_specs,
        out_shape=out_shape,
        compiler_params=_params("parallel"),
        name="oproj",
    )(a1, a2, w, w, x, gain.reshape(1, n))


def _down_kernel(a_ref, w_ref, x_ref, *rest, normed):
    acc_ref = rest[-1]
    kk = pl.program_id(2)

    @pl.when(kk == 0)
    def _():
        acc_ref[...] = jnp.zeros_like(acc_ref)

    acc_ref[...] += jnp.dot(a_ref[...], w_ref[...].astype(BF16), preferred_element_type=F32)

    @pl.when(kk == pl.num_programs(2) - 1)
    def _():
        y = x_ref[...] + acc_ref[...]
        if normed:
            _emit_normed(y, *rest[:-1])
        else:
            rest[0][...] = y


def _down(a, w, x, gain, *, layer, tm=1024, tn=1024, tk=2048):
    m, k = a.shape
    n = w.shape[-1]
    tm = min(tm, m)
    tile_map = lambda i, j, kk: (i, j)
    in_specs = [pl.BlockSpec((tm, tk), lambda i, j, kk: (i, kk)),
                _w_spec(w, tk, tn, lambda i, j, kk: (kk, j), layer),
                pl.BlockSpec((tm, tn), tile_map)]
    args = [a, w, x]
    if gain is None:
        out_specs = pl.BlockSpec((tm, tn), tile_map)
        out_shape = jax.ShapeDtypeStruct((m, n), F32)
    else:
        in_specs.append(pl.BlockSpec((1, tn), lambda i, j, kk: (0, j)))
        args.append(gain.reshape(1, n))
        out_specs, out_shape = _normed_out(m, n, tm, tn, tile_map, lambda i, j, kk: (j, i, 0))
    return pl.pallas_call(
        functools.partial(_down_kernel, normed=gain is not None),
        grid=(m // tm, n // tn, k // tk),
        in_specs=in_specs,
        out_specs=out_specs,
        out_shape=out_shape,
        scratch_shapes=[pltpu.VMEM((tm, tn), F32)],
        compiler_params=_params("parallel", "arbitrary", "arbitrary"),
        name="down",
    )(*args)


def _gla_kernel(q_ref, k_ref, v_ref, gate_ref, xb_ref, ss_ref, wlr_ref, wg_ref, bg_ref, gn_ref,
                o_ref, state_ref, loga_ref, *, tb):
    @pl.when(pl.program_id(1) == 0)
    def _():
        state_ref[...] = jnp.zeros_like(state_ref)

    c = GLA_CHUNK
    row = lax.broadcasted_iota(jnp.int32, (c, c), 0)
    col = lax.broadcasted_iota(jnp.int32, (c, c), 1)
    causal = col <= row
    tri = jnp.where(causal, 1.0, 0.0).astype(BF16)
    tri2 = jnp.concatenate([tri, tri], axis=1)
    lane = lax.broadcasted_iota(jnp.int32, (c, LANES), 1)
    head_mask = (lane < GLA_DK, lane >= GLA_DK)

    r = GLA_GATE_RANK
    lr = _row_scale(ss_ref, xb_ref.shape[1]) * jnp.dot(
        xb_ref[...], wlr_ref[...].astype(BF16), preferred_element_type=F32)
    lr_hi = lr.astype(BF16)
    lr_lo = (lr - lr_hi.astype(F32)).astype(BF16)
    lr_lane = lax.broadcasted_iota(jnp.int32, lr.shape, 1)
    lr_mix = jnp.where(jnp.logical_and(lr_lane >= r, lr_lane < 2 * r), lr_lo, lr_hi)
    wg = wg_ref[...]
    wg_hi = wg.astype(BF16)
    wg_lo = (wg - wg_hi.astype(F32)).astype(BF16)
    wg_mix = jnp.where(lax.broadcasted_iota(jnp.int32, wg.shape, 0) >= 2 * r, wg_lo, wg_hi)
    logits = jnp.dot(lr_mix, wg_mix, preferred_element_type=F32) + bg_ref[...]
    loga_ref[...] = _log_sigmoid(logits) * (1.0 / GLA_GATE_TAU)

    def chunk(ci, carry):
        r0 = pl.multiple_of(ci * c, c)
        rows = pl.ds(r0, c)
        log_a = loga_ref[rows, :]
        hi = log_a.astype(BF16)
        lo = (log_a - hi.astype(F32)).astype(BF16)
        bcum = jnp.dot(tri2, jnp.concatenate([hi, lo], axis=0), preferred_element_type=F32)
        b_last = bcum[c - 1:c, :]
        kk = k_ref[rows, :].astype(F32)
        q_e = q_ref[rows, :].astype(F32) * (GLA_DK ** -0.5) * jnp.exp(bcum)
        k_e = (kk * jnp.exp(-bcum)).astype(BF16)
        k_dec = (kk * jnp.exp(b_last - bcum)).astype(BF16)
        decay = jnp.exp(b_last)
        heads = range(GLA_HEADS)
        pair = lambda x, h: x[:, (h // 2) * LANES:(h // 2 + 1) * LANES]
        hsl = lambda h: slice(h * GLA_DV, (h + 1) * GLA_DV)
        qm = [jnp.where(head_mask[h % 2], pair(q_e, h), 0.0).astype(BF16) for h in heads]
        vb = [v_ref[rows, hsl(h)] for h in heads]
        s = [lax.dot_general(qm[h], pair(k_e, h), NT_DIMS, preferred_element_type=F32)
             for h in heads]
        st = [state_ref[h] for h in heads]
        o_inter = [lax.dot_general(qm[h], st[h].astype(BF16), NT_DIMS,
                                   preferred_element_type=F32) for h in heads]
        u_t = [lax.dot_general(vb[h], pair(k_dec, h), TN_DIMS, preferred_element_type=F32)
               for h in heads]
        for h in heads:
            state_ref[h] = pair(decay, h) * st[h] + u_t[h]
        o_intra = [jnp.dot(jnp.where(causal, s[h], 0.0).astype(BF16), vb[h],
                           preferred_element_type=F32) for h in heads]
        for h in heads:
            o = o_intra[h] + o_inter[h]
            y = o * _rms_scale(o) * gn_ref[...]
            g = gate_ref[rows, hsl(h)].astype(F32)
            y = y * (g * (1.0 / (1.0 + jnp.exp(-g))))
            o_ref[rows, hsl(h)] = y.astype(o_ref.dtype)
        return carry

    lax.fori_loop(0, tb // c, chunk, 0, unroll=4)


def _gla(proj, xb, ss, w_lr, wg, bg, gn, *, batch, seq, tb=512):
    m = proj.shape[0]
    d = xb.shape[1]
    tb = min(tb, seq)
    nt = seq // tb
    hk = GLA_HEADS * GLA_DK
    hv = GLA_HEADS * GLA_DV
    row = lambda b, t: b * nt + t
    return pl.pallas_call(
        functools.partial(_gla_kernel, tb=tb),
        grid=(batch, nt),
        in_specs=[pl.BlockSpec((tb, hk), lambda b, t: (row(b, t), 0)),
                  pl.BlockSpec((tb, hk), lambda b, t: (row(b, t), 1)),
                  pl.BlockSpec((tb, hv), lambda b, t: (row(b, t), 1)),
                  pl.BlockSpec((tb, hv), lambda b, t: (row(b, t), 2)),
                  pl.BlockSpec((tb, d), lambda b, t: (row(b, t), 0)),
                  pl.BlockSpec((ss.shape[0], tb, LANES), lambda b, t: (0, row(b, t), 0)),
                  pl.BlockSpec((d, LANES), lambda b, t: (0, 0)),
                  pl.BlockSpec((LANES, hk), lambda b, t: (0, 0)),
                  pl.BlockSpec((1, hk), lambda b, t: (0, 0)),
                  pl.BlockSpec((1, GLA_DV), lambda b, t: (0, 0))],
        out_specs=pl.BlockSpec((tb, hv), lambda b, t: (row(b, t), 0)),
        out_shape=jax.ShapeDtypeStruct((m, hv), BF16),
        scratch_shapes=[pltpu.VMEM((GLA_HEADS, GLA_DV, LANES), F32),
                        pltpu.VMEM((tb, hk), F32)],
        compiler_params=_params("parallel", "arbitrary"),
        name="gla",
    )(proj, proj, proj, proj, xb, ss, w_lr, wg, bg, gn)


def _sb_kernel(q_ref, k_ref, v_ref, w_ref, gq_ref, gk_ref, gn_ref, side_ref, o_ref, side_b_ref,
               qn_ref, kn_ref, *, tq, tk, seq):
    wmat = w_ref[...]
    ones = wmat[:, tk:]
    nd = tq // tk
    rowi = lax.broadcasted_iota(jnp.int32, (tk, tk), 0)
    coli = lax.broadcasted_iota(jnp.int32, (tk, tk), 1)
    below_diag = jnp.concatenate([coli < rowi] * nd, axis=0)

    def normed(y, g_ref):
        ms = jnp.dot((y * y).astype(BF16), ones, preferred_element_type=F32) * (1.0 / SB_DH)
        return (y * lax.rsqrt(ms + EPS) * g_ref[...]).astype(BF16)

    def head_norm(i, _):
        rows = pl.ds(pl.multiple_of(i * tq, tq), tq)
        qn_ref[rows, :] = normed(q_ref[rows, :].astype(F32), gq_ref)
        kn_ref[rows, :] = normed(k_ref[rows, :].astype(F32), gk_ref)
        return 0

    lax.fori_loop(0, seq // tq, head_norm, 0)

    group = lambda x, g: x[g * tk:(g + 1) * tk]

    def scores(q, q0, j):
        first = [q0 + (g - j) * tk for g in range(nd)]
        start = [pl.multiple_of(jnp.maximum(f, 0), tk) for f in first]
        z = jnp.concatenate(
            [lax.dot_general(group(q, g), kn_ref[pl.ds(start[g], tk), :], NT_DIMS,
                             preferred_element_type=F32) for g in range(nd)], axis=0)
        log_beta = jnp.minimum(z, 0.0) - jnp.log2(1.0 + jnp.exp2(-jnp.abs(z)))
        return first, start, log_beta, log_beta - z

    def suffix_sums(log_1m):
        return jnp.dot(log_1m.astype(BF16), wmat, preferred_element_type=F32)

    def retire(carry, first, j):
        keeps = lambda g: isinstance(j, int) and g >= j
        return jnp.concatenate(
            [group(carry, g) if keeps(g) else jnp.where(first[g] >= 0, group(carry, g), SB_DEAD)
             for g in range(nd)], axis=0)

    def weigh(a, start):
        a = a.astype(BF16)
        return jnp.concatenate(
            [jnp.dot(group(a, g), v_ref[pl.ds(start[g], tk), :], preferred_element_type=F32)
             for g in range(nd)], axis=0)

    def sweeps(q, q0, j, n, carry, acc, mask):
        sc = [scores(q, q0, j + t) for t in range(n)]
        s2 = []
        for t, (_, _, _, log_1m) in enumerate(sc):
            if t == 0 and mask is not None:
                log_1m = jnp.where(mask, log_1m, 0.0)
            s2.append(suffix_sums(log_1m))
        for t, (first, start, log_beta, _) in enumerate(sc):
            diagonal = t == 0 and mask is not None
            if not diagonal:
                carry = retire(carry, first, j + t)
            a = jnp.exp2(log_beta + s2[t][:, :tk] + carry)
            if diagonal:
                a = jnp.where(mask, a, 0.0)
            carry = carry + s2[t][:, tk:]
            acc = acc + weigh(a, start)
        return carry, acc

    def live(c):
        return (jnp.max(c) > -SB_EXIT_LOG2).astype(jnp.int32)

    def qblock(i, _):
        q0 = pl.multiple_of(i * tq, tq)
        q = qn_ref[pl.ds(q0, tq), :]
        carry = jnp.zeros((tq, tk), F32)
        acc = jnp.zeros((tq, SB_DH), F32)
        carry, acc = sweeps(q, q0, 0, SB_PEELED_SWEEPS, carry, acc, below_diag)

        def cond(st):
            j, go, _, _ = st
            return jnp.logical_and(go > 0, j < (i + 1) * nd)

        def body(st):
            j, _, carry, acc = st
            carry, acc = sweeps(q, q0, j, SB_LOOP_SWEEPS, carry, acc, None)
            return j + SB_LOOP_SWEEPS, live(carry), carry, acc

        _, _, _, acc = lax.while_loop(cond, body, (SB_PEELED_SWEEPS, live(carry), carry, acc))
        o_ref[pl.ds(q0, tq), :] = (acc * _rms_scale(acc) * gn_ref[...]).astype(o_ref.dtype)
        return 0

    lax.fori_loop(0, seq // tq, qblock, 0)
    side_b_ref[...] = side_ref[...].astype(BF16)


def _sb(proj, gq, gk, gn, side, *, layer, col0, batch, seq, tq=4096, tk=128):
    m = proj.shape[0]
    tq = min(tq, seq)
    assert tq % tk == 0 and seq % tq == 0
    jj = lax.broadcasted_iota(jnp.int32, (tk, 2 * tk), 0)
    ss = lax.broadcasted_iota(jnp.int32, (tk, 2 * tk), 1)
    wmat = jnp.logical_or(ss >= tk, jj > ss).astype(BF16)
    vec = pl.BlockSpec((1, SB_DH), lambda b, h: (0, 0))
    rows, cols = side.shape[1] // (batch * SB_HEADS), side.shape[2]
    return pl.pallas_call(
        functools.partial(_sb_kernel, tq=tq, tk=tk, seq=seq),
        grid=(batch, SB_HEADS),
        in_specs=[pl.BlockSpec((seq, SB_DH), lambda b, h: (b, col0 + h)),
                  pl.BlockSpec((seq, SB_DH), lambda b, h: (b, col0 + SB_HEADS + h)),
                  pl.BlockSpec((seq, SB_DH), lambda b, h: (b, col0 + 2 * SB_HEADS + h)),
                  pl.BlockSpec((tk, 2 * tk), lambda b, h: (0, 0)),
                  vec, vec, vec,
                  pl.BlockSpec((None, rows, cols), lambda b, h: (layer, b * SB_HEADS + h, 0))],
        out_specs=[pl.BlockSpec((seq, SB_DH), lambda b, h: (b, h)),
                   pl.BlockSpec((rows, cols), lambda b, h: (b * SB_HEADS + h, 0))],
        out_shape=[jax.ShapeDtypeStruct((m, SB_HEADS * SB_DH), BF16),
                   jax.ShapeDtypeStruct(side.shape[1:], BF16)],
        scratch_shapes=[pltpu.VMEM((seq, SB_DH), BF16), pltpu.VMEM((seq, SB_DH), BF16)],
        compiler_params=_params("parallel", "parallel"),
        name="stickbreak",
    )(proj, proj, proj, wmat, gq, gk, gn, side)


def _layer(x, xb, ss, l, next_gain, w_in, w_gate_up, b_gate, gla_out_norm, sb_q_norm,
           sb_k_norm, sb_out_norm, w_o, mlp_norm, w_up, w_down, *, batch, seq):
    hk = GLA_HEADS * GLA_DK
    hv = GLA_HEADS * GLA_DV
    n_gla = 2 * hk + 2 * hv
    pad = LANES - 3 * GLA_GATE_RANK
    w_gla = w_in[l, :, :n_gla].astype(BF16)
    w_sb = w_in[l, :, n_gla + GLA_GATE_RANK:].astype(BF16)
    w_lr = jnp.pad(jnp.tile(w_in[l, :, n_gla:n_gla + GLA_GATE_RANK], (1, 3)), ((0, 0), (0, pad)))
    wg = jnp.pad(jnp.tile(w_gate_up[l], (3, 1)), ((0, pad), (0, 0)))
    gq = (sb_q_norm[l] * (math.log2(math.e) / math.sqrt(SB_DH))).reshape(1, SB_DH)

    p_gla = _matmul(_identity, xb, ss, w_gla, out_dtype=BF16, name="inproj_gla")
    p_sb = _matmul(_identity, xb, ss, w_sb, out_dtype=BF16, name="inproj_sb")
    o_gla = _gla(p_gla, xb, ss, w_lr, wg, b_gate[l].reshape(1, hk),
                 gla_out_norm[l].reshape(1, GLA_DV), batch=batch, seq=seq)
    o_sb, w_o_b = _sb(p_sb, gq, sb_k_norm[l].reshape(1, SB_DH), sb_out_norm[l].reshape(1, SB_DH),
                      w_o, layer=l, col0=0, batch=batch, seq=seq)
    x, xb, ss = _oproj(o_gla, o_sb, w_o_b, x, mlp_norm[l])
    up, w_down_b = _matmul(_relu2, xb, ss, w_up, layer=l, side=w_down, out_dtype=BF16,
                           name="mlp_up")
    if next_gain is None:
        return _down(up, w_down_b, x, None, layer=l), None, None
    return _down(up, w_down_b, x, next_gain, layer=l)


def kernel(x, attn_norm, w_in, w_gate_up, b_gate, gla_out_norm, sb_q_norm, sb_k_norm,
           sb_out_norm, w_o, mlp_norm, w_up, w_down):
    batch, seq, d = x.shape
    depth = w_in.shape[0]
    y = x.reshape(batch * seq, d)
    yb, ss = _prenorm(y, attn_norm[0])
    for l in range(depth):
        next_gain = attn_norm[l + 1] if l + 1 < depth else None
        y, yb, ss = _layer(y, yb, ss, l, next_gain, w_in, w_gate_up, b_gate, gla_out_norm,
                           sb_q_norm, sb_k_norm, sb_out_norm, w_o, mlp_norm, w_up, w_down,
                           batch=batch, seq=seq)
    return y.reshape(batch, seq, d)
```

```python
import functools
import math

import jax
import jax.numpy as jnp
from jax import lax
from jax.experimental import pallas as pl
from jax.experimental.pallas import tpu as pltpu

EPS = 1e-6
LANES = 128

GLA_HEADS = 8
GLA_DK = 64
GLA_DV = 128
GLA_GATE_RANK = 16
GLA_GATE_TAU = 16.0
GLA_CHUNK = 64
GLA_CHUNKS_PER_TRIP = 4
SB_HEADS = 8
SB_DH = 128

SB_EXIT_LOG2 = 126.0
SB_DEAD = -1e30
SB_PEELED_SWEEPS = 2
SB_LOOP_SWEEPS = 1

V7X_VMEM_BYTES = 64 * 1024 * 1024
VMEM_LIMIT = V7X_VMEM_BYTES * 7 // 8

F32 = jnp.float32
BF16 = jnp.bfloat16
NT_DIMS = (((1,), (1,)), ((), ()))
TN_DIMS = (((0,), (0,)), ((), ()))


def _params(*sem):
    return pltpu.CompilerParams(dimension_semantics=sem, vmem_limit_bytes=VMEM_LIMIT)


def _log_sigmoid(z):
    return jnp.minimum(z, 0.0) - jnp.log(1.0 + jnp.exp(-jnp.abs(z)))


def _rms_scale(y):
    return lax.rsqrt(jnp.mean(y * y, axis=-1, keepdims=True) + EPS)


def _sumsq(y):
    return jnp.broadcast_to(jnp.sum(y * y, axis=-1, keepdims=True), (y.shape[0], LANES))


def _row_scale(ss_ref, d):
    return lax.rsqrt(jnp.sum(ss_ref[...], axis=0) * (1.0 / d) + EPS)


def _emit_normed(y, g_ref, o_ref, xb_ref, ss_ref):
    o_ref[...] = y
    xb_ref[...] = (y * g_ref[...]).astype(BF16)
    ss_ref[...] = _sumsq(y)


def _prenorm_kernel(x_ref, g_ref, xb_ref, ss_ref):
    x = x_ref[...]
    xb_ref[...] = (x * g_ref[...]).astype(BF16)
    ss_ref[...] = _sumsq(x)


def _prenorm(x, gain, rows=512):
    m, d = x.shape
    rows = min(rows, m)
    return pl.pallas_call(
        _prenorm_kernel,
        grid=(m // rows,),
        in_specs=[pl.BlockSpec((rows, d), lambda i: (i, 0)),
                  pl.BlockSpec((1, d), lambda i: (0, 0))],
        out_specs=[pl.BlockSpec((rows, d), lambda i: (i, 0)),
                   pl.BlockSpec((None, rows, LANES), lambda i: (0, i, 0))],
        out_shape=[jax.ShapeDtypeStruct((m, d), BF16),
                   jax.ShapeDtypeStruct((1, m, LANES), F32)],
        compiler_params=_params("parallel"),
        name="prenorm",
    )(x, gain.reshape(1, d))


def _w_spec(w, rows, cols, index_map, layer):
    if w.ndim == 2:
        return pl.BlockSpec((rows, cols), index_map)
    return pl.BlockSpec((None, rows, cols), lambda *g: (layer,) + tuple(index_map(*g)))


def _mm_kernel(a_ref, w_ref, ss_ref, *rest, post):
    o_ref = rest[-1] if len(rest) == 1 else rest[1]
    acc = jnp.dot(a_ref[...], w_ref[...].astype(BF16), preferred_element_type=F32)
    r = _row_scale(ss_ref, a_ref.shape[1])
    for c in range(acc.shape[1] // LANES):
        sl = slice(c * LANES, (c + 1) * LANES)
        o_ref[:, sl] = post(acc[:, sl] * r).astype(o_ref.dtype)
    if len(rest) == 3:
        rest[2][...] = rest[0][...].astype(BF16)


def _identity(y):
    return y


def _relu2(y):
    return jnp.square(jnp.maximum(y, 0.0))


def _matmul(post, a, ss, w, *, layer=0, side=None, out_dtype, tm=1024, tn=1024, name):
    m, k = a.shape
    n = w.shape[-1]
    tm = min(tm, m)
    tn = min(tn, n)
    nj, ni = n // tn, m // tm
    in_specs = [pl.BlockSpec((tm, k), lambda j, i: (i, 0)),
                _w_spec(w, k, tn, lambda j, i: (0, j), layer),
                pl.BlockSpec((ss.shape[0], tm, LANES), lambda j, i: (0, i, 0))]
    out_specs = pl.BlockSpec((tm, tn), lambda j, i: (i, j))
    out_shape = jax.ShapeDtypeStruct((m, n), out_dtype)
    args = [a, w, ss]
    if side is not None:
        rows, cols = side.shape[1] // (nj * ni), side.shape[2]
        in_specs.append(pl.BlockSpec((None, rows, cols), lambda j, i: (layer, j * ni + i, 0)))
        out_specs = [out_specs, pl.BlockSpec((rows, cols), lambda j, i: (j * ni + i, 0))]
        out_shape = [out_shape, jax.ShapeDtypeStruct(side.shape[1:], BF16)]
        args.append(side)
    return pl.pallas_call(
        functools.partial(_mm_kernel, post=post),
        grid=(nj, ni),
        in_specs=in_specs,
        out_specs=out_specs,
        out_shape=out_shape,
        compiler_params=_params("parallel", "arbitrary"),
        name=name,
    )(*args)


def _normed_out(m, n, tm, tn, tile_map, slot_map):
    specs = [pl.BlockSpec((tm, tn), tile_map), pl.BlockSpec((tm, tn), tile_map),
             pl.BlockSpec((None, tm, LANES), slot_map)]
    shapes = [jax.ShapeDtypeStruct((m, n), F32), jax.ShapeDtypeStruct((m, n), BF16),
              jax.ShapeDtypeStruct((n // tn, m, LANES), F32)]
    return specs, shapes


def _oproj_kernel(a1_ref, a2_ref, w1_ref, w2_ref, x_ref, g_ref, o_ref, xb_ref, ss_ref):
    acc = jnp.dot(a1_ref[...], w1_ref[...], preferred_element_type=F32)
    acc += jnp.dot(a2_ref[...], w2_ref[...], preferred_element_type=F32)
    _emit_normed(x_ref[...] + acc, g_ref, o_ref, xb_ref, ss_ref)


def _oproj(a1, a2, w, x, gain, *, tm=512):
    m, kh = a1.shape
    n = w.shape[-1]
    tm = min(tm, m)
    out_specs, out_shape = _normed_out(m, n, tm, n, lambda i: (i, 0), lambda i: (0, i, 0))
    return pl.pallas_call(
        _oproj_kernel,
        grid=(m // tm,),
        in_specs=[pl.BlockSpec((tm, kh), lambda i: (i, 0)),
                  pl.BlockSpec((tm, kh), lambda i: (i, 0)),
                  pl.BlockSpec((kh, n), lambda i: (0, 0)),
                  pl.BlockSpec((kh, n), lambda i: (1, 0)),
                  pl.BlockSpec((tm, n), lambda i: (i, 0)),
                  pl.BlockSpec((1, n), lambda i: (0, 0))],
        out_specs=out_specs,
        out_shape=out_shape,
        compiler_params=_params("parallel"),
        name="oproj",
    )(a1, a2, w, w, x, gain.reshape(1, n))


def _down_kernel(a_ref, w_ref, x_ref, *rest, normed):
    acc_ref = rest[-1]
    kk = pl.program_id(2)

    @pl.when(kk == 0)
    def _():
        acc_ref[...] = jnp.zeros_like(acc_ref)

    acc_ref[...] += jnp.dot(a_ref[...], w_ref[...].astype(BF16), preferred_element_type=F32)

    @pl.when(kk == pl.num_programs(2) - 1)
    def _():
        y = x_ref[...] + acc_ref[...]
        if normed:
            _emit_normed(y, *rest[:-1])
        else:
            rest[0][...] = y


def _down(a, w, x, gain, *, layer, tm=1024, tn=1024, tk=2048):
    m, k = a.shape
    n = w.shape[-1]
    tm = min(tm, m)
    tile_map = lambda i, j, kk: (i, j)
    in_specs = [pl.BlockSpec((tm, tk), lambda i, j, kk: (i, kk)),
                _w_spec(w, tk, tn, lambda i, j, kk: (kk, j), layer),
                pl.BlockSpec((tm, tn), tile_map)]
    args = [a, w, x]
    if gain is None:
        out_specs = pl.BlockSpec((tm, tn), tile_map)
        out_shape = jax.ShapeDtypeStruct((m, n), F32)
    else:
        in_specs.append(pl.BlockSpec((1, tn), lambda i, j, kk: (0, j)))
        args.append(gain.reshape(1, n))
        out_specs, out_shape = _normed_out(m, n, tm, tn, tile_map, lambda i, j, kk: (j, i, 0))
    return pl.pallas_call(
        functools.partial(_down_kernel, normed=gain is not None),
        grid=(m // tm, n // tn, k // tk),
        in_specs=in_specs,
        out_specs=out_specs,
        out_shape=out_shape,
        scratch_shapes=[pltpu.VMEM((tm, tn), F32)],
        compiler_params=_params("parallel", "arbitrary", "arbitrary"),
        name="down",
    )(*args)


def _gla_kernel(q_ref, k_ref, v_ref, gate_ref, xb_ref, ss_ref, wlr_ref, wg_ref, bg_ref, gn_ref,
                o_ref, state_ref, qe_ref, ke_ref, kdec_ref, decay_ref, *, tb):
    @pl.when(pl.program_id(1) == 0)
    def _():
        state_ref[...] = jnp.zeros_like(state_ref)

    c = GLA_CHUNK
    row = lax.broadcasted_iota(jnp.int32, (c, c), 0)
    col = lax.broadcasted_iota(jnp.int32, (c, c), 1)
    causal = col <= row
    tri = jnp.where(causal, 1.0, 0.0).astype(BF16)
    tri2 = jnp.concatenate([tri, tri], axis=1)
    lane = lax.broadcasted_iota(jnp.int32, (c, LANES), 1)
    head_mask = (lane < GLA_DK, lane >= GLA_DK)

    r = GLA_GATE_RANK
    lr = _row_scale(ss_ref, xb_ref.shape[1]) * jnp.dot(
        xb_ref[...], wlr_ref[...].astype(BF16), preferred_element_type=F32)
    lr_hi = lr.astype(BF16)
    lr_lo = (lr - lr_hi.astype(F32)).astype(BF16)
    lr_lane = lax.broadcasted_iota(jnp.int32, lr.shape, 1)
    lr_mix = jnp.where(jnp.logical_and(lr_lane >= r, lr_lane < 2 * r), lr_lo, lr_hi)
    wg = wg_ref[...]
    wg_hi = wg.astype(BF16)
    wg_lo = (wg - wg_hi.astype(F32)).astype(BF16)
    wg_mix = jnp.where(lax.broadcasted_iota(jnp.int32, wg.shape, 0) >= 2 * r, wg_lo, wg_hi)
    logits = jnp.dot(lr_mix, wg_mix, preferred_element_type=F32) + bg_ref[...]
    log_a = _log_sigmoid(logits) * (1.0 / GLA_GATE_TAU)

    hi = log_a.astype(BF16)
    lo = (log_a - hi.astype(F32)).astype(BF16)
    n_chunks = tb // c
    bcum_c = [jnp.dot(tri2, jnp.concatenate([hi[t * c:(t + 1) * c], lo[t * c:(t + 1) * c]],
                                            axis=0), preferred_element_type=F32)
              for t in range(n_chunks)]
    bcum = jnp.concatenate(bcum_c, axis=0)
    b_last = jnp.concatenate(
        [jnp.broadcast_to(b[c - 1:c, :], b.shape) for b in bcum_c], axis=0)
    kk = k_ref[...].astype(F32)
    qe_ref[...] = (q_ref[...].astype(F32) * (GLA_DK ** -0.5) * jnp.exp(bcum)).astype(BF16)
    ke_ref[...] = (kk * jnp.exp(-bcum)).astype(BF16)
    kdec_ref[...] = (kk * jnp.exp(b_last - bcum)).astype(BF16)
    decay_ref[...] = jnp.concatenate([jnp.exp(b[c - 1:c, :]) for b in bcum_c], axis=0)

    def chunk(ci, carry):
        r0 = pl.multiple_of(ci * c, c)
        rows = pl.ds(r0, c)
        q_e = qe_ref[rows, :]
        k_e = ke_ref[rows, :]
        k_dec = kdec_ref[rows, :]
        decay = decay_ref[pl.ds(ci, 1), :]
        heads = range(GLA_HEADS)
        pair = lambda x, h: x[:, (h // 2) * LANES:(h // 2 + 1) * LANES]
        hsl = lambda h: slice(h * GLA_DV, (h + 1) * GLA_DV)
        zero = jnp.zeros((), BF16)
        qm = [jnp.where(head_mask[h % 2], pair(q_e, h), zero) for h in heads]
        vb = [v_ref[rows, hsl(h)] for h in heads]
        s = [lax.dot_general(qm[h], pair(k_e, h), NT_DIMS, preferred_element_type=F32)
             for h in heads]
        st = [state_ref[h] for h in heads]
        o_inter = [lax.dot_general(qm[h], st[h].astype(BF16), NT_DIMS,
                                   preferred_element_type=F32) for h in heads]
        u_t = [lax.dot_general(vb[h], pair(k_dec, h), TN_DIMS, preferred_element_type=F32)
               for h in heads]
        for h in heads:
            state_ref[h] = pair(decay, h) * st[h] + u_t[h]
        o_intra = [jnp.dot(jnp.where(causal, s[h], 0.0).astype(BF16), vb[h],
                           preferred_element_type=F32) for h in heads]
        for h in heads:
            o = o_intra[h] + o_inter[h]
            y = o * _rms_scale(o) * gn_ref[...]
            g = gate_ref[rows, hsl(h)].astype(F32)
            y = y * (g * (1.0 / (1.0 + jnp.exp(-g))))
            o_ref[rows, hsl(h)] = y.astype(o_ref.dtype)
        return carry

    lax.fori_loop(0, tb // c, chunk, 0, unroll=GLA_CHUNKS_PER_TRIP)


def _gla(proj, xb, ss, w_lr, wg, bg, gn, *, batch, seq, tb=512):
    m = proj.shape[0]
    d = xb.shape[1]
    tb = min(tb, seq)
    nt = seq // tb
    hk = GLA_HEADS * GLA_DK
    hv = GLA_HEADS * GLA_DV
    row = lambda b, t: b * nt + t
    return pl.pallas_call(
        functools.partial(_gla_kernel, tb=tb),
        grid=(batch, nt),
        in_specs=[pl.BlockSpec((tb, hk), lambda b, t: (row(b, t), 0)),
                  pl.BlockSpec((tb, hk), lambda b, t: (row(b, t), 1)),
                  pl.BlockSpec((tb, hv), lambda b, t: (row(b, t), 1)),
                  pl.BlockSpec((tb, hv), lambda b, t: (row(b, t), 2)),
                  pl.BlockSpec((tb, d), lambda b, t: (row(b, t), 0)),
                  pl.BlockSpec((ss.shape[0], tb, LANES), lambda b, t: (0, row(b, t), 0)),
                  pl.BlockSpec((d, LANES), lambda b, t: (0, 0)),
                  pl.BlockSpec((LANES, hk), lambda b, t: (0, 0)),
                  pl.BlockSpec((1, hk), lambda b, t: (0, 0)),
                  pl.BlockSpec((1, GLA_DV), lambda b, t: (0, 0))],
        out_specs=pl.BlockSpec((tb, hv), lambda b, t: (row(b, t), 0)),
        out_shape=jax.ShapeDtypeStruct((m, hv), BF16),
        scratch_shapes=[pltpu.VMEM((GLA_HEADS, GLA_DV, LANES), F32),
                        pltpu.VMEM((tb, hk), BF16), pltpu.VMEM((tb, hk), BF16),
                        pltpu.VMEM((tb, hk), BF16), pltpu.VMEM((tb // GLA_CHUNK, hk), F32)],
        compiler_params=_params("parallel", "arbitrary"),
        name="gla",
    )(proj, proj, proj, proj, xb, ss, w_lr, wg, bg, gn)


def _sb_kernel(q_ref, k_ref, v_ref, w_ref, gq_ref, gk_ref, gn_ref, side_ref, o_ref, side_b_ref,
               qn_ref, kn_ref, *, tq, tk, seq):
    wmat = w_ref[...]
    ones = wmat[:, tk:]
    nd = tq // tk
    rowi = lax.broadcasted_iota(jnp.int32, (tk, tk), 0)
    coli = lax.broadcasted_iota(jnp.int32, (tk, tk), 1)
    below_diag = jnp.concatenate([coli < rowi] * nd, axis=0)

    def normed(y, g_ref):
        ms = jnp.dot((y * y).astype(BF16), ones, preferred_element_type=F32) * (1.0 / SB_DH)
        return (y * lax.rsqrt(ms + EPS) * g_ref[...]).astype(BF16)

    def head_norm(i, _):
        rows = pl.ds(pl.multiple_of(i * tq, tq), tq)
        qn_ref[rows, :] = normed(q_ref[rows, :].astype(F32), gq_ref)
        kn_ref[rows, :] = normed(k_ref[rows, :].astype(F32), gk_ref)
        return 0

    lax.fori_loop(0, seq // tq, head_norm, 0)

    group = lambda x, g: x[g * tk:(g + 1) * tk]

    def scores(q, q0, j):
        first = [q0 + (g - j) * tk for g in range(nd)]
        start = [pl.multiple_of(jnp.maximum(f, 0), tk) for f in first]
        z = jnp.concatenate(
            [lax.dot_general(group(q, g), kn_ref[pl.ds(start[g], tk), :], NT_DIMS,
                             preferred_element_type=F32) for g in range(nd)], axis=0)
        log_beta = jnp.minimum(z, 0.0) - jnp.log2(1.0 + jnp.exp2(-jnp.abs(z)))
        return first, start, log_beta, log_beta - z

    def suffix_sums(log_1m):
        return jnp.dot(log_1m.astype(BF16), wmat, preferred_element_type=F32)

    def retire(carry, first, j):
        keeps = lambda g: isinstance(j, int) and g >= j
        return jnp.concatenate(
            [group(carry, g) if keeps(g) else jnp.where(first[g] >= 0, group(carry, g), SB_DEAD)
             for g in range(nd)], axis=0)

    def weigh(a, start):
        a = a.astype(BF16)
        return jnp.concatenate(
            [jnp.dot(group(a, g), v_ref[pl.ds(start[g], tk), :], preferred_element_type=F32)
             for g in range(nd)], axis=0)

    def sweeps(q, q0, j, n, carry, acc, mask):
        sc = [scores(q, q0, j + t) for t in range(n)]
        s2 = []
        for t, (_, _, _, log_1m) in enumerate(sc):
            if t == 0 and mask is not None:
                log_1m = jnp.where(mask, log_1m, 0.0)
            s2.append(suffix_sums(log_1m))
        for t, (first, start, log_beta, _) in enumerate(sc):
            diagonal = t == 0 and mask is not None
            if not diagonal:
                carry = retire(carry, first, j + t)
            a = jnp.exp2(log_beta + s2[t][:, :tk] + carry)
            if diagonal:
                a = jnp.where(mask, a, 0.0)
            carry = carry + s2[t][:, tk:]
            acc = acc + weigh(a, start)
        return carry, acc

    def live(c):
        return (jnp.max(c) > -SB_EXIT_LOG2).astype(jnp.int32)

    def qblock(i, _):
        q0 = pl.multiple_of(i * tq, tq)
        q = qn_ref[pl.ds(q0, tq), :]
        carry = jnp.zeros((tq, tk), F32)
        acc = jnp.zeros((tq, SB_DH), F32)
        carry, acc = sweeps(q, q0, 0, SB_PEELED_SWEEPS, carry, acc, below_diag)

        def cond(st):
            j, go, _, _ = st
            return jnp.logical_and(go > 0, j < (i + 1) * nd)

        def body(st):
            j, _, carry, acc = st
            carry, acc = sweeps(q, q0, j, SB_LOOP_SWEEPS, carry, acc, None)
            return j + SB_LOOP_SWEEPS, live(carry), carry, acc

        _, _, _, acc = lax.while_loop(cond, body, (SB_PEELED_SWEEPS, live(carry), carry, acc))
        o_ref[pl.ds(q0, tq), :] = (acc * _rms_scale(acc) * gn_ref[...]).astype(o_ref.dtype)
        return 0

    lax.fori_loop(0, seq // tq, qblock, 0)
    side_b_ref[...] = side_ref[...].astype(BF16)


def _sb(proj, gq, gk, gn, side, *, layer, col0, batch, seq, tq=4096, tk=128):
    m = proj.shape[0]
    tq = min(tq, seq)
    assert tq % tk == 0 and seq % tq == 0
    jj = lax.broadcasted_iota(jnp.int32, (tk, 2 * tk), 0)
    ss = lax.broadcasted_iota(jnp.int32, (tk, 2 * tk), 1)
    wmat = jnp.logical_or(ss >= tk, jj > ss).astype(BF16)
    vec = pl.BlockSpec((1, SB_DH), lambda b, h: (0, 0))
    rows, cols = side.shape[1] // (batch * SB_HEADS), side.shape[2]
    return pl.pallas_call(
        functools.partial(_sb_kernel, tq=tq, tk=tk, seq=seq),
        grid=(batch, SB_HEADS),
        in_specs=[pl.BlockSpec((seq, SB_DH), lambda b, h: (b, col0 + h)),
                  pl.BlockSpec((seq, SB_DH), lambda b, h: (b, col0 + SB_HEADS + h)),
                  pl.BlockSpec((seq, SB_DH), lambda b, h: (b, col0 + 2 * SB_HEADS + h)),
                  pl.BlockSpec((tk, 2 * tk), lambda b, h: (0, 0)),
                  vec, vec, vec,
                  pl.BlockSpec((None, rows, cols), lambda b, h: (layer, b * SB_HEADS + h, 0))],
        out_specs=[pl.BlockSpec((seq, SB_DH), lambda b, h: (b, h)),
                   pl.BlockSpec((rows, cols), lambda b, h: (b * SB_HEADS + h, 0))],
        out_shape=[jax.ShapeDtypeStruct((m, SB_HEADS * SB_DH), BF16),
                   jax.ShapeDtypeStruct(side.shape[1:], BF16)],
        scratch_shapes=[pltpu.VMEM((seq, SB_DH), BF16), pltpu.VMEM((seq, SB_DH), BF16)],
        compiler_params=_params("parallel", "parallel"),
        name="stickbreak",
    )(proj, proj, proj, wmat, gq, gk, gn, side)


def _layer(x, xb, ss, l, next_gain, w_in, w_gate_up, b_gate, gla_out_norm, sb_q_norm,
           sb_k_norm, sb_out_norm, w_o, mlp_norm, w_up, w_down, *, batch, seq):
    hk = GLA_HEADS * GLA_DK
    hv = GLA_HEADS * GLA_DV
    n_gla = 2 * hk + 2 * hv
    pad = LANES - 3 * GLA_GATE_RANK
    w_gla = w_in[l, :, :n_gla].astype(BF16)
    w_sb = w_in[l, :, n_gla + GLA_GATE_RANK:].astype(BF16)
    w_lr = jnp.pad(jnp.tile(w_in[l, :, n_gla:n_gla + GLA_GATE_RANK], (1, 3)), ((0, 0), (0, pad)))
    wg = jnp.pad(jnp.tile(w_gate_up[l], (3, 1)), ((0, pad), (0, 0)))
    gq = (sb_q_norm[l] * (math.log2(math.e) / math.sqrt(SB_DH))).reshape(1, SB_DH)

    p_gla = _matmul(_identity, xb, ss, w_gla, out_dtype=BF16, name="inproj_gla")
    p_sb = _matmul(_identity, xb, ss, w_sb, out_dtype=BF16, name="inproj_sb")
    o_gla = _gla(p_gla, xb, ss, w_lr, wg, b_gate[l].reshape(1, hk),
                 gla_out_norm[l].reshape(1, GLA_DV), batch=batch, seq=seq)
    o_sb, w_o_b = _sb(p_sb, gq, sb_k_norm[l].reshape(1, SB_DH), sb_out_norm[l].reshape(1, SB_DH),
                      w_o, layer=l, col0=0, batch=batch, seq=seq)
    x, xb, ss = _oproj(o_gla, o_sb, w_o_b, x, mlp_norm[l])
    up, w_down_b = _matmul(_relu2, xb, ss, w_up, layer=l, side=w_down, out_dtype=BF16,
                           name="mlp_up")
    if next_gain is None:
        return _down(up, w_down_b, x, None, layer=l), None, None
    return _down(up, w_down_b, x, next_gain, layer=l)


def kernel(x, attn_norm, w_in, w_gate_up, b_gate, gla_out_norm, sb_q_norm, sb_k_norm,
           sb_out_norm, w_o, mlp_norm, w_up, w_down):
    batch, seq, d = x.shape
    depth = w_in.shape[0]
    y = x.reshape(batch * seq, d)
    yb, ss = _prenorm(y, attn_norm[0])
    for l in range(depth):
        next_gain = attn_norm[l + 1] if l + 1 < depth else None
        y, yb, ss = _layer(y, yb, ss, l, next_gain, w_in, w_gate_up, b_gate, gla_out_norm,
                           sb_q_norm, sb_k_norm, sb_out_norm, w_o, mlp_norm, w_up, w_down,
                           batch=batch, seq=seq)
    return y.reshape(batch, seq, d)
```

```python
import functools
import math

import jax
import jax.numpy as jnp
from jax import lax
from jax.experimental import pallas as pl
from jax.experimental.pallas import tpu as pltpu

EPS = 1e-6
LANES = 128

GLA_HEADS = 8
GLA_DK = 64
GLA_DV = 128
GLA_GATE_RANK = 16
GLA_GATE_TAU = 16.0
GLA_CHUNK = 64
GLA_CHUNKS_PER_TRIP = 4
SB_HEADS = 8
SB_DH = 128

SB_EXIT_LOG2 = 126.0
SB_DEAD = -1e30
SB_PEELED_SWEEPS = 2
SB_LOOP_SWEEPS = 1

V7X_VMEM_BYTES = 64 * 1024 * 1024
VMEM_LIMIT = V7X_VMEM_BYTES * 7 // 8

F32 = jnp.float32
BF16 = jnp.bfloat16
NT_DIMS = (((1,), (1,)), ((), ()))
TN_DIMS = (((0,), (0,)), ((), ()))


def _params(*sem):
    return pltpu.CompilerParams(dimension_semantics=sem, vmem_limit_bytes=VMEM_LIMIT)


def _log_sigmoid(z):
    return jnp.minimum(z, 0.0) - jnp.log(1.0 + jnp.exp(-jnp.abs(z)))


def _rms_scale(y):
    return lax.rsqrt(jnp.mean(y * y, axis=-1, keepdims=True) + EPS)


def _sumsq(y):
    return jnp.broadcast_to(jnp.sum(y * y, axis=-1, keepdims=True), (y.shape[0], LANES))


def _row_scale(ss_ref, d):
    return lax.rsqrt(jnp.sum(ss_ref[...], axis=0) * (1.0 / d) + EPS)


def _emit_normed(y, g_ref, o_ref, xb_ref, ss_ref):
    o_ref[...] = y
    xb_ref[...] = (y * g_ref[...]).astype(BF16)
    ss_ref[...] = _sumsq(y)


def _prenorm_kernel(x_ref, g_ref, xb_ref, ss_ref):
    x = x_ref[...]
    xb_ref[...] = (x * g_ref[...]).astype(BF16)
    ss_ref[...] = _sumsq(x)


def _prenorm(x, gain, rows=512):
    m, d = x.shape
    rows = min(rows, m)
    return pl.pallas_call(
        _prenorm_kernel,
        grid=(m // rows,),
        in_specs=[pl.BlockSpec((rows, d), lambda i: (i, 0)),
                  pl.BlockSpec((1, d), lambda i: (0, 0))],
        out_specs=[pl.BlockSpec((rows, d), lambda i: (i, 0)),
                   pl.BlockSpec((None, rows, LANES), lambda i: (0, i, 0))],
        out_shape=[jax.ShapeDtypeStruct((m, d), BF16),
                   jax.ShapeDtypeStruct((1, m, LANES), F32)],
        compiler_params=_params("parallel"),
        name="prenorm",
    )(x, gain.reshape(1, d))


def _w_spec(w, rows, cols, index_map, layer):
    if w.ndim == 2:
        return pl.BlockSpec((rows, cols), index_map)
    return pl.BlockSpec((None, rows, cols), lambda *g: (layer,) + tuple(index_map(*g)))


def _mm_kernel(a_ref, w_ref, ss_ref, *rest, post):
    o_ref = rest[-1] if len(rest) == 1 else rest[1]
    acc = jnp.dot(a_ref[...], w_ref[...].astype(BF16), preferred_element_type=F32)
    r = _row_scale(ss_ref, a_ref.shape[1])
    for c in range(acc.shape[1] // LANES):
        sl = slice(c * LANES, (c + 1) * LANES)
        o_ref[:, sl] = post(acc[:, sl] * r).astype(o_ref.dtype)
    if len(rest) == 3:
        rest[2][...] = rest[0][...].astype(BF16)


def _identity(y):
    return y


def _relu2(y):
    return jnp.square(jnp.maximum(y, 0.0))


def _matmul(post, a, ss, w, *, n=None, layer=0, side=None, out_dtype, tm=1024, tn=1024, name):
    m, k = a.shape
    n = w.shape[-1] if n is None else n
    tm = min(tm, m)
    tn = min(tn, n)
    nj, ni = n // tn, m // tm
    in_specs = [pl.BlockSpec((tm, k), lambda j, i: (i, 0)),
                _w_spec(w, k, tn, lambda j, i: (0, j), layer),
                pl.BlockSpec((ss.shape[0], tm, LANES), lambda j, i: (0, i, 0))]
    out_specs = pl.BlockSpec((tm, tn), lambda j, i: (i, j))
    out_shape = jax.ShapeDtypeStruct((m, n), out_dtype)
    args = [a, w, ss]
    if side is not None:
        rows, cols = side.shape[1] // (nj * ni), side.shape[2]
        in_specs.append(pl.BlockSpec((None, rows, cols), lambda j, i: (layer, j * ni + i, 0)))
        out_specs = [out_specs, pl.BlockSpec((rows, cols), lambda j, i: (j * ni + i, 0))]
        out_shape = [out_shape, jax.ShapeDtypeStruct(side.shape[1:], BF16)]
        args.append(side)
    return pl.pallas_call(
        functools.partial(_mm_kernel, post=post),
        grid=(nj, ni),
        in_specs=in_specs,
        out_specs=out_specs,
        out_shape=out_shape,
        compiler_params=_params("parallel", "arbitrary"),
        name=name,
    )(*args)


def _normed_out(m, n, tm, tn, tile_map, slot_map):
    specs = [pl.BlockSpec((tm, tn), tile_map), pl.BlockSpec((tm, tn), tile_map),
             pl.BlockSpec((None, tm, LANES), slot_map)]
    shapes = [jax.ShapeDtypeStruct((m, n), F32), jax.ShapeDtypeStruct((m, n), BF16),
              jax.ShapeDtypeStruct((n // tn, m, LANES), F32)]
    return specs, shapes


def _oproj_kernel(a1_ref, a2_ref, w1_ref, w2_ref, x_ref, g_ref, o_ref, xb_ref, ss_ref):
    acc = jnp.dot(a1_ref[...], w1_ref[...], preferred_element_type=F32)
    acc += jnp.dot(a2_ref[...], w2_ref[...], preferred_element_type=F32)
    _emit_normed(x_ref[...] + acc, g_ref, o_ref, xb_ref, ss_ref)


def _oproj(a1, a2, w, x, gain, *, tm=512):
    m, kh = a1.shape
    n = w.shape[-1]
    tm = min(tm, m)
    out_specs, out_shape = _normed_out(m, n, tm, n, lambda i: (i, 0), lambda i: (0, i, 0))
    return pl.pallas_call(
        _oproj_kernel,
        grid=(m // tm,),
        in_specs=[pl.BlockSpec((tm, kh), lambda i: (i, 0)),
                  pl.BlockSpec((tm, kh), lambda i: (i, 0)),
                  pl.BlockSpec((kh, n), lambda i: (0, 0)),
                  pl.BlockSpec((kh, n), lambda i: (1, 0)),
                  pl.BlockSpec((tm, n), lambda i: (i, 0)),
                  pl.BlockSpec((1, n), lambda i: (0, 0))],
        out_specs=out_specs,
        out_shape=out_shape,
        compiler_params=_params("parallel"),
        name="oproj",
    )(a1, a2, w, w, x, gain.reshape(1, n))


def _down_kernel(a_ref, w_ref, x_ref, *rest, normed):
    acc_ref = rest[-1]
    kk = pl.program_id(2)

    @pl.when(kk == 0)
    def _():
        acc_ref[...] = jnp.zeros_like(acc_ref)

    acc_ref[...] += jnp.dot(a_ref[...], w_ref[...].astype(BF16), preferred_element_type=F32)

    @pl.when(kk == pl.num_programs(2) - 1)
    def _():
        y = x_ref[...] + acc_ref[...]
        if normed:
            _emit_normed(y, *rest[:-1])
        else:
            rest[0][...] = y


def _down(a, w, x, gain, *, layer, tm=1024, tn=1024, tk=2048):
    m, k = a.shape
    n = w.shape[-1]
    tm = min(tm, m)
    tile_map = lambda i, j, kk: (i, j)
    in_specs = [pl.BlockSpec((tm, tk), lambda i, j, kk: (i, kk)),
                _w_spec(w, tk, tn, lambda i, j, kk: (kk, j), layer),
                pl.BlockSpec((tm, tn), tile_map)]
    args = [a, w, x]
    if gain is None:
        out_specs = pl.BlockSpec((tm, tn), tile_map)
        out_shape = jax.ShapeDtypeStruct((m, n), F32)
    else:
        in_specs.append(pl.BlockSpec((1, tn), lambda i, j, kk: (0, j)))
        args.append(gain.reshape(1, n))
        out_specs, out_shape = _normed_out(m, n, tm, tn, tile_map, lambda i, j, kk: (j, i, 0))
    return pl.pallas_call(
        functools.partial(_down_kernel, normed=gain is not None),
        grid=(m // tm, n // tn, k // tk),
        in_specs=in_specs,
        out_specs=out_specs,
        out_shape=out_shape,
        scratch_shapes=[pltpu.VMEM((tm, tn), F32)],
        compiler_params=_params("parallel", "arbitrary", "arbitrary"),
        name="down",
    )(*args)


def _gla_kernel(q_ref, k_ref, v_ref, gate_ref, xb_ref, ss_ref, wlr_ref, wg_ref, bg_ref, gn_ref,
                o_ref, state_ref, qe_ref, ke_ref, kdec_ref, decay_ref, *, tb):
    @pl.when(pl.program_id(1) == 0)
    def _():
        state_ref[...] = jnp.zeros_like(state_ref)

    c = GLA_CHUNK
    row = lax.broadcasted_iota(jnp.int32, (c, c), 0)
    col = lax.broadcasted_iota(jnp.int32, (c, c), 1)
    causal = col <= row
    tri = jnp.where(causal, 1.0, 0.0).astype(BF16)
    tri2 = jnp.concatenate([tri, tri], axis=1)
    lane = lax.broadcasted_iota(jnp.int32, (c, LANES), 1)
    head_mask = (lane < GLA_DK, lane >= GLA_DK)

    r = GLA_GATE_RANK
    lr = _row_scale(ss_ref, xb_ref.shape[1]) * jnp.dot(
        xb_ref[...], wlr_ref[...].astype(BF16), preferred_element_type=F32)
    lr_hi = lr.astype(BF16)
    lr_lo = (lr - lr_hi.astype(F32)).astype(BF16)
    lr_lane = lax.broadcasted_iota(jnp.int32, lr.shape, 1)
    lr_mix = jnp.where(jnp.logical_and(lr_lane >= r, lr_lane < 2 * r), lr_lo, lr_hi)
    wg = wg_ref[...]
    wg_hi = wg.astype(BF16)
    wg_lo = (wg - wg_hi.astype(F32)).astype(BF16)
    wg_mix = jnp.where(lax.broadcasted_iota(jnp.int32, wg.shape, 0) >= 2 * r, wg_lo, wg_hi)
    logits = jnp.dot(lr_mix, wg_mix, preferred_element_type=F32) + bg_ref[...]
    log_a = _log_sigmoid(logits) * (1.0 / GLA_GATE_TAU)

    hi = log_a.astype(BF16)
    lo = (log_a - hi.astype(F32)).astype(BF16)
    n_chunks = tb // c
    bcum_c = [jnp.dot(tri2, jnp.concatenate([hi[t * c:(t + 1) * c], lo[t * c:(t + 1) * c]],
                                            axis=0), preferred_element_type=F32)
              for t in range(n_chunks)]
    bcum = jnp.concatenate(bcum_c, axis=0)
    b_last = jnp.concatenate(
        [jnp.broadcast_to(b[c - 1:c, :], b.shape) for b in bcum_c], axis=0)
    kk = k_ref[...].astype(F32)
    qe_ref[...] = (q_ref[...].astype(F32) * (GLA_DK ** -0.5) * jnp.exp(bcum)).astype(BF16)
    ke_ref[...] = (kk * jnp.exp(-bcum)).astype(BF16)
    kdec_ref[...] = (kk * jnp.exp(b_last - bcum)).astype(BF16)
    decay_ref[...] = jnp.concatenate([jnp.exp(b[c - 1:c, :]) for b in bcum_c], axis=0)

    def chunk(ci, carry):
        r0 = pl.multiple_of(ci * c, c)
        rows = pl.ds(r0, c)
        q_e = qe_ref[rows, :]
        k_e = ke_ref[rows, :]
        k_dec = kdec_ref[rows, :]
        decay = decay_ref[pl.ds(ci, 1), :]
        heads = range(GLA_HEADS)
        pair = lambda x, h: x[:, (h // 2) * LANES:(h // 2 + 1) * LANES]
        hsl = lambda h: slice(h * GLA_DV, (h + 1) * GLA_DV)
        zero = jnp.zeros((), BF16)
        qm = [jnp.where(head_mask[h % 2], pair(q_e, h), zero) for h in heads]
        vb = [v_ref[rows, hsl(h)] for h in heads]
        s = [lax.dot_general(qm[h], pair(k_e, h), NT_DIMS, preferred_element_type=F32)
             for h in heads]
        st = [state_ref[h] for h in heads]
        o_inter = [lax.dot_general(qm[h], st[h].astype(BF16), NT_DIMS,
                                   preferred_element_type=F32) for h in heads]
        u_t = [lax.dot_general(vb[h], pair(k_dec, h), TN_DIMS, preferred_element_type=F32)
               for h in heads]
        for h in heads:
            state_ref[h] = pair(decay, h) * st[h] + u_t[h]
        o_intra = [jnp.dot(jnp.where(causal, s[h], 0.0).astype(BF16), vb[h],
                           preferred_element_type=F32) for h in heads]
        for h in heads:
            o = o_intra[h] + o_inter[h]
            y = o * _rms_scale(o) * gn_ref[...]
            g = gate_ref[rows, hsl(h)].astype(F32)
            y = y * (g * (1.0 / (1.0 + jnp.exp(-g))))
            o_ref[rows, hsl(h)] = y.astype(o_ref.dtype)
        return carry

    lax.fori_loop(0, tb // c, chunk, 0, unroll=GLA_CHUNKS_PER_TRIP)


def _gla(proj, xb, ss, w_lr, wg, bg, gn, *, batch, seq, tb=512):
    m = proj.shape[0]
    d = xb.shape[1]
    tb = min(tb, seq)
    nt = seq // tb
    hk = GLA_HEADS * GLA_DK
    hv = GLA_HEADS * GLA_DV
    row = lambda b, t: b * nt + t
    return pl.pallas_call(
        functools.partial(_gla_kernel, tb=tb),
        grid=(batch, nt),
        in_specs=[pl.BlockSpec((tb, hk), lambda b, t: (row(b, t), 0)),
                  pl.BlockSpec((tb, hk), lambda b, t: (row(b, t), 1)),
                  pl.BlockSpec((tb, hv), lambda b, t: (row(b, t), 1)),
                  pl.BlockSpec((tb, hv), lambda b, t: (row(b, t), 2)),
                  pl.BlockSpec((tb, d), lambda b, t: (row(b, t), 0)),
                  pl.BlockSpec((ss.shape[0], tb, LANES), lambda b, t: (0, row(b, t), 0)),
                  pl.BlockSpec((d, LANES), lambda b, t: (0, 0)),
                  pl.BlockSpec((LANES, hk), lambda b, t: (0, 0)),
                  pl.BlockSpec((1, hk), lambda b, t: (0, 0)),
                  pl.BlockSpec((1, GLA_DV), lambda b, t: (0, 0))],
        out_specs=pl.BlockSpec((tb, hv), lambda b, t: (row(b, t), 0)),
        out_shape=jax.ShapeDtypeStruct((m, hv), BF16),
        scratch_shapes=[pltpu.VMEM((GLA_HEADS, GLA_DV, LANES), F32),
                        pltpu.VMEM((tb, hk), BF16), pltpu.VMEM((tb, hk), BF16),
                        pltpu.VMEM((tb, hk), BF16), pltpu.VMEM((tb // GLA_CHUNK, hk), F32)],
        compiler_params=_params("parallel", "arbitrary"),
        name="gla",
    )(proj, proj, proj, proj, xb, ss, w_lr, wg, bg, gn)


def _sb_kernel(q_ref, k_ref, v_ref, w_ref, gq_ref, gk_ref, gn_ref, side_ref, o_ref, side_b_ref,
               qn_ref, kn_ref, *, tq, tk, seq):
    wmat = w_ref[...]
    ones = wmat[:, tk:]
    nd = tq // tk
    rowi = lax.broadcasted_iota(jnp.int32, (tk, tk), 0)
    coli = lax.broadcasted_iota(jnp.int32, (tk, tk), 1)
    below_diag = jnp.concatenate([coli < rowi] * nd, axis=0)

    def normed(y, g_ref):
        ms = jnp.dot((y * y).astype(BF16), ones, preferred_element_type=F32) * (1.0 / SB_DH)
        return (y * lax.rsqrt(ms + EPS) * g_ref[...]).astype(BF16)

    def head_norm(i, _):
        rows = pl.ds(pl.multiple_of(i * tq, tq), tq)
        qn_ref[rows, :] = normed(q_ref[rows, :].astype(F32), gq_ref)
        kn_ref[rows, :] = normed(k_ref[rows, :].astype(F32), gk_ref)
        return 0

    lax.fori_loop(0, seq // tq, head_norm, 0)

    group = lambda x, g: x[g * tk:(g + 1) * tk]

    def scores(q, q0, j):
        first = [q0 + (g - j) * tk for g in range(nd)]
        start = [pl.multiple_of(jnp.maximum(f, 0), tk) for f in first]
        z = jnp.concatenate(
            [lax.dot_general(group(q, g), kn_ref[pl.ds(start[g], tk), :], NT_DIMS,
                             preferred_element_type=F32) for g in range(nd)], axis=0)
        log_beta = jnp.minimum(z, 0.0) - jnp.log2(1.0 + jnp.exp2(-jnp.abs(z)))
        return first, start, log_beta, log_beta - z

    def suffix_sums(log_1m):
        return jnp.dot(log_1m.astype(BF16), wmat, preferred_element_type=F32)

    def retire(carry, first, j):
        keeps = lambda g: isinstance(j, int) and g >= j
        return jnp.concatenate(
            [group(carry, g) if keeps(g) else jnp.where(first[g] >= 0, group(carry, g), SB_DEAD)
             for g in range(nd)], axis=0)

    def weigh(a, start):
        a = a.astype(BF16)
        return jnp.concatenate(
            [jnp.dot(group(a, g), v_ref[pl.ds(start[g], tk), :], preferred_element_type=F32)
             for g in range(nd)], axis=0)

    def sweeps(q, q0, j, n, carry, acc, mask):
        sc = [scores(q, q0, j + t) for t in range(n)]
        s2 = []
        for t, (_, _, _, log_1m) in enumerate(sc):
            if t == 0 and mask is not None:
                log_1m = jnp.where(mask, log_1m, 0.0)
            s2.append(suffix_sums(log_1m))
        for t, (first, start, log_beta, _) in enumerate(sc):
            diagonal = t == 0 and mask is not None
            if not diagonal:
                carry = retire(carry, first, j + t)
            a = jnp.exp2(log_beta + s2[t][:, :tk] + carry)
            if diagonal:
                a = jnp.where(mask, a, 0.0)
            carry = carry + s2[t][:, tk:]
            acc = acc + weigh(a, start)
        return carry, acc

    def live(c):
        return (jnp.max(c) > -SB_EXIT_LOG2).astype(jnp.int32)

    def qblock(i, _):
        q0 = pl.multiple_of(i * tq, tq)
        q = qn_ref[pl.ds(q0, tq), :]
        carry = jnp.zeros((tq, tk), F32)
        acc = jnp.zeros((tq, SB_DH), F32)
        carry, acc = sweeps(q, q0, 0, SB_PEELED_SWEEPS, carry, acc, below_diag)

        def cond(st):
            j, go, _, _ = st
            return jnp.logical_and(go > 0, j < (i + 1) * nd)

        def body(st):
            j, _, carry, acc = st
            carry, acc = sweeps(q, q0, j, SB_LOOP_SWEEPS, carry, acc, None)
            return j + SB_LOOP_SWEEPS, live(carry), carry, acc

        _, _, _, acc = lax.while_loop(cond, body, (SB_PEELED_SWEEPS, live(carry), carry, acc))
        o_ref[pl.ds(q0, tq), :] = (acc * _rms_scale(acc) * gn_ref[...]).astype(o_ref.dtype)
        return 0

    lax.fori_loop(0, seq // tq, qblock, 0)
    side_b_ref[...] = side_ref[...].astype(BF16)


def _sb(proj, gq, gk, gn, side, *, layer, col0, batch, seq, tq=4096, tk=128):
    m = proj.shape[0]
    tq = min(tq, seq)
    assert tq % tk == 0 and seq % tq == 0
    jj = lax.broadcasted_iota(jnp.int32, (tk, 2 * tk), 0)
    ss = lax.broadcasted_iota(jnp.int32, (tk, 2 * tk), 1)
    wmat = jnp.logical_or(ss >= tk, jj > ss).astype(BF16)
    vec = pl.BlockSpec((1, SB_DH), lambda b, h: (0, 0))
    rows, cols = side.shape[1] // (batch * SB_HEADS), side.shape[2]
    return pl.pallas_call(
        functools.partial(_sb_kernel, tq=tq, tk=tk, seq=seq),
        grid=(batch, SB_HEADS),
        in_specs=[pl.BlockSpec((seq, SB_DH), lambda b, h: (b, col0 + h)),
                  pl.BlockSpec((seq, SB_DH), lambda b, h: (b, col0 + SB_HEADS + h)),
                  pl.BlockSpec((seq, SB_DH), lambda b, h: (b, col0 + 2 * SB_HEADS + h)),
                  pl.BlockSpec((tk, 2 * tk), lambda b, h: (0, 0)),
                  vec, vec, vec,
                  pl.BlockSpec((None, rows, cols), lambda b, h: (layer, b * SB_HEADS + h, 0))],
        out_specs=[pl.BlockSpec((seq, SB_DH), lambda b, h: (b, h)),
                   pl.BlockSpec((rows, cols), lambda b, h: (b * SB_HEADS + h, 0))],
        out_shape=[jax.ShapeDtypeStruct((m, SB_HEADS * SB_DH), BF16),
                   jax.ShapeDtypeStruct(side.shape[1:], BF16)],
        scratch_shapes=[pltpu.VMEM((seq, SB_DH), BF16), pltpu.VMEM((seq, SB_DH), BF16)],
        compiler_params=_params("parallel", "parallel"),
        name="stickbreak",
    )(proj, proj, proj, wmat, gq, gk, gn, side)


def _layer(x, xb, ss, l, next_gain, w_in_b, w_in, w_gate_up, b_gate, gla_out_norm, sb_q_norm,
           sb_k_norm, sb_out_norm, w_o, mlp_norm, w_up, w_down, *, batch, seq):
    hk = GLA_HEADS * GLA_DK
    hv = GLA_HEADS * GLA_DV
    n_gla = 2 * hk + 2 * hv
    pad = LANES - 3 * GLA_GATE_RANK
    w_lr = jnp.pad(jnp.tile(w_in[l, :, n_gla:n_gla + GLA_GATE_RANK], (1, 3)), ((0, 0), (0, pad)))
    wg = jnp.pad(jnp.tile(w_gate_up[l], (3, 1)), ((0, pad), (0, 0)))
    gq = (sb_q_norm[l] * (math.log2(math.e) / math.sqrt(SB_DH))).reshape(1, SB_DH)

    p_gla = _matmul(_identity, xb, ss, w_in_b, n=n_gla, layer=l, out_dtype=BF16,
                    name="inproj_gla")
    p_sb = _matmul(_identity, xb, ss, w_in_b[l, :, n_gla + GLA_GATE_RANK:], out_dtype=BF16,
                   name="inproj_sb")
    o_gla = _gla(p_gla, xb, ss, w_lr, wg, b_gate[l].reshape(1, hk),
                 gla_out_norm[l].reshape(1, GLA_DV), batch=batch, seq=seq)
    o_sb, w_o_b = _sb(p_sb, gq, sb_k_norm[l].reshape(1, SB_DH), sb_out_norm[l].reshape(1, SB_DH),
                      w_o, layer=l, col0=0, batch=batch, seq=seq)
    x, xb, ss = _oproj(o_gla, o_sb, w_o_b, x, mlp_norm[l])
    up, w_down_b = _matmul(_relu2, xb, ss, w_up, layer=l, side=w_down, out_dtype=BF16,
                           name="mlp_up")
    if next_gain is None:
        return _down(up, w_down_b, x, None, layer=l), None, None
    return _down(up, w_down_b, x, next_gain, layer=l)


def kernel(x, attn_norm, w_in, w_gate_up, b_gate, gla_out_norm, sb_q_norm, sb_k_norm,
           sb_out_norm, w_o, mlp_norm, w_up, w_down):
    batch, seq, d = x.shape
    depth = w_in.shape[0]
    y = x.reshape(batch * seq, d)
    w_in_b = w_in.astype(BF16)
    yb, ss = _prenorm(y, attn_norm[0])
    for l in range(depth):
        next_gain = attn_norm[l + 1] if l + 1 < depth else None
        y, yb, ss = _layer(y, yb, ss, l, next_gain, w_in_b, w_in, w_gate_up, b_gate, gla_out_norm,
                           sb_q_norm, sb_k_norm, sb_out_norm, w_o, mlp_norm, w_up, w_down,
                           batch=batch, seq=seq)
    return y.reshape(batch, seq, d)
```

```python
import functools
import math

import jax
import jax.numpy as jnp
from jax import lax
from jax.experimental import pallas as pl
from jax.experimental.pallas import tpu as pltpu

EPS = 1e-6
LANES = 128

GLA_HEADS = 8
GLA_DK = 64
GLA_DV = 128
GLA_GATE_RANK = 16
GLA_GATE_TAU = 16.0
GLA_CHUNK = 64
GLA_CHUNKS_PER_TRIP = 4
SB_HEADS = 8
SB_DH = 128

SB_EXIT_LOG2 = 126.0
SB_DEAD = -1e30
SB_PEELED_SWEEPS = 2
SB_LOOP_SWEEPS = 1

V7X_VMEM_BYTES = 64 * 1024 * 1024
VMEM_LIMIT = V7X_VMEM_BYTES * 7 // 8

F32 = jnp.float32
BF16 = jnp.bfloat16
NT_DIMS = (((1,), (1,)), ((), ()))
TN_DIMS = (((0,), (0,)), ((), ()))


def _params(*sem):
    return pltpu.CompilerParams(dimension_semantics=sem, vmem_limit_bytes=VMEM_LIMIT)


def _log_sigmoid(z):
    return jnp.minimum(z, 0.0) - jnp.log(1.0 + jnp.exp(-jnp.abs(z)))


def _rms_scale(y):
    return lax.rsqrt(jnp.mean(y * y, axis=-1, keepdims=True) + EPS)


def _sumsq(y):
    return jnp.broadcast_to(jnp.sum(y * y, axis=-1, keepdims=True), (y.shape[0], LANES))


def _normed_operand(a_ref, aux_ref):
    d = a_ref.shape[1]
    if a_ref.dtype == BF16:
        return a_ref[...], lax.rsqrt(jnp.sum(aux_ref[...], axis=0) * (1.0 / d) + EPS)
    x = a_ref[...]
    return (x * aux_ref[...]).astype(BF16), lax.rsqrt(_sumsq(x) * (1.0 / d) + EPS)


def _normed_specs(a, aux, rows, row_map):
    a_spec = pl.BlockSpec((rows, a.shape[1]), lambda *g: (row_map(*g), 0))
    if a.dtype == BF16:
        return a_spec, pl.BlockSpec((aux.shape[0], rows, LANES), lambda *g: (0, row_map(*g), 0))
    return a_spec, pl.BlockSpec((1, a.shape[1]), lambda *g: (0, 0))


def _emit_normed(y, g_ref, o_ref, xb_ref, ss_ref):
    o_ref[...] = y
    xb_ref[...] = (y * g_ref[...]).astype(BF16)
    ss_ref[...] = _sumsq(y)


def _w_spec(w, rows, cols, index_map, layer):
    if w.ndim == 2:
        return pl.BlockSpec((rows, cols), index_map)
    return pl.BlockSpec((None, rows, cols), lambda *g: (layer,) + tuple(index_map(*g)))


def _mm_kernel(a_ref, aux_ref, w_ref, *rest, post):
    o_ref = rest[-1] if len(rest) == 1 else rest[1]
    xb, r = _normed_operand(a_ref, aux_ref)
    acc = jnp.dot(xb, w_ref[...].astype(BF16), preferred_element_type=F32)
    for c in range(acc.shape[1] // LANES):
        sl = slice(c * LANES, (c + 1) * LANES)
        o_ref[:, sl] = post(acc[:, sl] * r).astype(o_ref.dtype)
    if len(rest) == 3:
        rest[2][...] = rest[0][...].astype(BF16)


def _identity(y):
    return y


def _relu2(y):
    return jnp.square(jnp.maximum(y, 0.0))


def _matmul(post, a, aux, w, *, n=None, layer=0, side=None, out_dtype, tm=1024, tn=1024, name):
    m, k = a.shape
    n = w.shape[-1] if n is None else n
    tm = min(tm, m)
    tn = min(tn, n)
    nj, ni = n // tn, m // tm
    in_specs = [*_normed_specs(a, aux, tm, lambda j, i: i),
                _w_spec(w, k, tn, lambda j, i: (0, j), layer)]
    out_specs = pl.BlockSpec((tm, tn), lambda j, i: (i, j))
    out_shape = jax.ShapeDtypeStruct((m, n), out_dtype)
    args = [a, aux, w]
    if side is not None:
        rows, cols = side.shape[1] // (nj * ni), side.shape[2]
        in_specs.append(pl.BlockSpec((None, rows, cols), lambda j, i: (layer, j * ni + i, 0)))
        out_specs = [out_specs, pl.BlockSpec((rows, cols), lambda j, i: (j * ni + i, 0))]
        out_shape = [out_shape, jax.ShapeDtypeStruct(side.shape[1:], BF16)]
        args.append(side)
    return pl.pallas_call(
        functools.partial(_mm_kernel, post=post),
        grid=(nj, ni),
        in_specs=in_specs,
        out_specs=out_specs,
        out_shape=out_shape,
        compiler_params=_params("parallel", "arbitrary"),
        name=name,
    )(*args)


def _normed_out(m, n, tm, tn, tile_map, slot_map):
    specs = [pl.BlockSpec((tm, tn), tile_map), pl.BlockSpec((tm, tn), tile_map),
             pl.BlockSpec((None, tm, LANES), slot_map)]
    shapes = [jax.ShapeDtypeStruct((m, n), F32), jax.ShapeDtypeStruct((m, n), BF16),
              jax.ShapeDtypeStruct((n // tn, m, LANES), F32)]
    return specs, shapes


def _oproj_kernel(a1_ref, a2_ref, w1_ref, w2_ref, x_ref, g_ref, o_ref, xb_ref, ss_ref):
    acc = jnp.dot(a1_ref[...], w1_ref[...], preferred_element_type=F32)
    acc += jnp.dot(a2_ref[...], w2_ref[...], preferred_element_type=F32)
    _emit_normed(x_ref[...] + acc, g_ref, o_ref, xb_ref, ss_ref)


def _oproj(a1, a2, w, x, gain, *, tm=512):
    m, kh = a1.shape
    n = w.shape[-1]
    tm = min(tm, m)
    out_specs, out_shape = _normed_out(m, n, tm, n, lambda i: (i, 0), lambda i: (0, i, 0))
    return pl.pallas_call(
        _oproj_kernel,
        grid=(m // tm,),
        in_specs=[pl.BlockSpec((tm, kh), lambda i: (i, 0)),
                  pl.BlockSpec((tm, kh), lambda i: (i, 0)),
                  pl.BlockSpec((kh, n), lambda i: (0, 0)),
                  pl.BlockSpec((kh, n), lambda i: (1, 0)),
                  pl.BlockSpec((tm, n), lambda i: (i, 0)),
                  pl.BlockSpec((1, n), lambda i: (0, 0))],
        out_specs=out_specs,
        out_shape=out_shape,
        compiler_params=_params("parallel"),
        name="oproj",
    )(a1, a2, w, w, x, gain.reshape(1, n))


def _down_kernel(a_ref, w_ref, x_ref, *rest, normed):
    acc_ref = rest[-1]
    kk = pl.program_id(2)

    @pl.when(kk == 0)
    def _():
        acc_ref[...] = jnp.zeros_like(acc_ref)

    acc_ref[...] += jnp.dot(a_ref[...], w_ref[...].astype(BF16), preferred_element_type=F32)

    @pl.when(kk == pl.num_programs(2) - 1)
    def _():
        y = x_ref[...] + acc_ref[...]
        if normed:
            _emit_normed(y, *rest[:-1])
        else:
            rest[0][...] = y


def _down(a, w, x, gain, *, layer, tm=1024, tn=1024, tk=2048):
    m, k = a.shape
    n = w.shape[-1]
    tm = min(tm, m)
    tile_map = lambda i, j, kk: (i, j)
    in_specs = [pl.BlockSpec((tm, tk), lambda i, j, kk: (i, kk)),
                _w_spec(w, tk, tn, lambda i, j, kk: (kk, j), layer),
                pl.BlockSpec((tm, tn), tile_map)]
    args = [a, w, x]
    if gain is None:
        out_specs = pl.BlockSpec((tm, tn), tile_map)
        out_shape = jax.ShapeDtypeStruct((m, n), F32)
    else:
        in_specs.append(pl.BlockSpec((1, tn), lambda i, j, kk: (0, j)))
        args.append(gain.reshape(1, n))
        out_specs, out_shape = _normed_out(m, n, tm, tn, tile_map, lambda i, j, kk: (j, i, 0))
    return pl.pallas_call(
        functools.partial(_down_kernel, normed=gain is not None),
        grid=(m // tm, n // tn, k // tk),
        in_specs=in_specs,
        out_specs=out_specs,
        out_shape=out_shape,
        scratch_shapes=[pltpu.VMEM((tm, tn), F32)],
        compiler_params=_params("parallel", "arbitrary", "arbitrary"),
        name="down",
    )(*args)


def _gla_kernel(q_ref, k_ref, v_ref, gate_ref, a_ref, aux_ref, wlr_ref, wg_ref, bg_ref, gn_ref,
                o_ref, state_ref, qe_ref, ke_ref, kdec_ref, decay_ref, *, tb):
    @pl.when(pl.program_id(1) == 0)
    def _():
        state_ref[...] = jnp.zeros_like(state_ref)

    c = GLA_CHUNK
    row = lax.broadcasted_iota(jnp.int32, (c, c), 0)
    col = lax.broadcasted_iota(jnp.int32, (c, c), 1)
    causal = col <= row
    tri = jnp.where(causal, 1.0, 0.0).astype(BF16)
    tri2 = jnp.concatenate([tri, tri], axis=1)
    lane = lax.broadcasted_iota(jnp.int32, (c, LANES), 1)
    head_mask = (lane < GLA_DK, lane >= GLA_DK)

    r = GLA_GATE_RANK
    xb, x_scale = _normed_operand(a_ref, aux_ref)
    lr = x_scale * jnp.dot(xb, wlr_ref[...].astype(BF16), preferred_element_type=F32)
    lr_hi = lr.astype(BF16)
    lr_lo = (lr - lr_hi.astype(F32)).astype(BF16)
    lr_lane = lax.broadcasted_iota(jnp.int32, lr.shape, 1)
    lr_mix = jnp.where(jnp.logical_and(lr_lane >= r, lr_lane < 2 * r), lr_lo, lr_hi)
    wg = wg_ref[...]
    wg_hi = wg.astype(BF16)
    wg_lo = (wg - wg_hi.astype(F32)).astype(BF16)
    wg_mix = jnp.where(lax.broadcasted_iota(jnp.int32, wg.shape, 0) >= 2 * r, wg_lo, wg_hi)
    logits = jnp.dot(lr_mix, wg_mix, preferred_element_type=F32) + bg_ref[...]
    log_a = _log_sigmoid(logits) * (1.0 / GLA_GATE_TAU)

    hi = log_a.astype(BF16)
    lo = (log_a - hi.astype(F32)).astype(BF16)
    n_chunks = tb // c
    bcum_c = [jnp.dot(tri2, jnp.concatenate([hi[t * c:(t + 1) * c], lo[t * c:(t + 1) * c]],
                                            axis=0), preferred_element_type=F32)
              for t in range(n_chunks)]
    bcum = jnp.concatenate(bcum_c, axis=0)
    b_last = jnp.concatenate(
        [jnp.broadcast_to(b[c - 1:c, :], b.shape) for b in bcum_c], axis=0)
    kk = k_ref[...].astype(F32)
    qe_ref[...] = (q_ref[...].astype(F32) * (GLA_DK ** -0.5) * jnp.exp(bcum)).astype(BF16)
    ke_ref[...] = (kk * jnp.exp(-bcum)).astype(BF16)
    kdec_ref[...] = (kk * jnp.exp(b_last - bcum)).astype(BF16)
    decay_ref[...] = jnp.concatenate([jnp.exp(b[c - 1:c, :]) for b in bcum_c], axis=0)

    def chunk(ci, carry):
        r0 = pl.multiple_of(ci * c, c)
        rows = pl.ds(r0, c)
        q_e = qe_ref[rows, :]
        k_e = ke_ref[rows, :]
        k_dec = kdec_ref[rows, :]
        decay = decay_ref[pl.ds(ci, 1), :]
        heads = range(GLA_HEADS)
        pair = lambda x, h: x[:, (h // 2) * LANES:(h // 2 + 1) * LANES]
        hsl = lambda h: slice(h * GLA_DV, (h + 1) * GLA_DV)
        zero = jnp.zeros((), BF16)
        qm = [jnp.where(head_mask[h % 2], pair(q_e, h), zero) for h in heads]
        vb = [v_ref[rows, hsl(h)] for h in heads]
        s = [lax.dot_general(qm[h], pair(k_e, h), NT_DIMS, preferred_element_type=F32)
             for h in heads]
        st = [state_ref[h] for h in heads]
        o_inter = [lax.dot_general(qm[h], st[h].astype(BF16), NT_DIMS,
                                   preferred_element_type=F32) for h in heads]
        u_t = [lax.dot_general(vb[h], pair(k_dec, h), TN_DIMS, preferred_element_type=F32)
               for h in heads]
        for h in heads:
            state_ref[h] = pair(decay, h) * st[h] + u_t[h]
        o_intra = [jnp.dot(jnp.where(causal, s[h], 0.0).astype(BF16), vb[h],
                           preferred_element_type=F32) for h in heads]
        for h in heads:
            o = o_intra[h] + o_inter[h]
            y = o * _rms_scale(o) * gn_ref[...]
            g = gate_ref[rows, hsl(h)].astype(F32)
            y = y * (g * (1.0 / (1.0 + jnp.exp(-g))))
            o_ref[rows, hsl(h)] = y.astype(o_ref.dtype)
        return carry

    lax.fori_loop(0, tb // c, chunk, 0, unroll=GLA_CHUNKS_PER_TRIP)


def _gla(proj, a, aux, w_lr, wg, bg, gn, *, batch, seq, tb=512):
    m = proj.shape[0]
    d = a.shape[1]
    tb = min(tb, seq)
    nt = seq // tb
    hk = GLA_HEADS * GLA_DK
    hv = GLA_HEADS * GLA_DV
    row = lambda b, t: b * nt + t
    return pl.pallas_call(
        functools.partial(_gla_kernel, tb=tb),
        grid=(batch, nt),
        in_specs=[pl.BlockSpec((tb, hk), lambda b, t: (row(b, t), 0)),
                  pl.BlockSpec((tb, hk), lambda b, t: (row(b, t), 1)),
                  pl.BlockSpec((tb, hv), lambda b, t: (row(b, t), 1)),
                  pl.BlockSpec((tb, hv), lambda b, t: (row(b, t), 2)),
                  *_normed_specs(a, aux, tb, row),
                  pl.BlockSpec((d, LANES), lambda b, t: (0, 0)),
                  pl.BlockSpec((LANES, hk), lambda b, t: (0, 0)),
                  pl.BlockSpec((1, hk), lambda b, t: (0, 0)),
                  pl.BlockSpec((1, GLA_DV), lambda b, t: (0, 0))],
        out_specs=pl.BlockSpec((tb, hv), lambda b, t: (row(b, t), 0)),
        out_shape=jax.ShapeDtypeStruct((m, hv), BF16),
        scratch_shapes=[pltpu.VMEM((GLA_HEADS, GLA_DV, LANES), F32),
                        pltpu.VMEM((tb, hk), BF16), pltpu.VMEM((tb, hk), BF16),
                        pltpu.VMEM((tb, hk), BF16), pltpu.VMEM((tb // GLA_CHUNK, hk), F32)],
        compiler_params=_params("parallel", "arbitrary"),
        name="gla",
    )(proj, proj, proj, proj, a, aux, w_lr, wg, bg, gn)


def _sb_kernel(q_ref, k_ref, v_ref, w_ref, gq_ref, gk_ref, gn_ref, side_ref, o_ref, side_b_ref,
               qn_ref, kn_ref, *, tq, tk, seq):
    wmat = w_ref[...]
    ones = wmat[:, tk:]
    nd = tq // tk
    rowi = lax.broadcasted_iota(jnp.int32, (tk, tk), 0)
    coli = lax.broadcasted_iota(jnp.int32, (tk, tk), 1)
    below_diag = jnp.concatenate([coli < rowi] * nd, axis=0)

    def normed(y, g_ref):
        ms = jnp.dot((y * y).astype(BF16), ones, preferred_element_type=F32) * (1.0 / SB_DH)
        return (y * lax.rsqrt(ms + EPS) * g_ref[...]).astype(BF16)

    def head_norm(i, _):
        rows = pl.ds(pl.multiple_of(i * tq, tq), tq)
        qn_ref[rows, :] = normed(q_ref[rows, :].astype(F32), gq_ref)
        kn_ref[rows, :] = normed(k_ref[rows, :].astype(F32), gk_ref)
        return 0

    lax.fori_loop(0, seq // tq, head_norm, 0)

    group = lambda x, g: x[g * tk:(g + 1) * tk]

    def scores(q, q0, j):
        first = [q0 + (g - j) * tk for g in range(nd)]
        start = [pl.multiple_of(jnp.maximum(f, 0), tk) for f in first]
        z = jnp.concatenate(
            [lax.dot_general(group(q, g), kn_ref[pl.ds(start[g], tk), :], NT_DIMS,
                             preferred_element_type=F32) for g in range(nd)], axis=0)
        log_beta = jnp.minimum(z, 0.0) - jnp.log2(1.0 + jnp.exp2(-jnp.abs(z)))
        return first, start, log_beta, log_beta - z

    def suffix_sums(log_1m):
        return jnp.dot(log_1m.astype(BF16), wmat, preferred_element_type=F32)

    def retire(carry, first, j):
        keeps = lambda g: isinstance(j, int) and g >= j
        return jnp.concatenate(
            [group(carry, g) if keeps(g) else jnp.where(first[g] >= 0, group(carry, g), SB_DEAD)
             for g in range(nd)], axis=0)

    def weigh(a, start):
        a = a.astype(BF16)
        return jnp.concatenate(
            [jnp.dot(group(a, g), v_ref[pl.ds(start[g], tk), :], preferred_element_type=F32)
             for g in range(nd)], axis=0)

    def sweeps(q, q0, j, n, carry, acc, mask):
        sc = [scores(q, q0, j + t) for t in range(n)]
        s2 = []
        for t, (_, _, _, log_1m) in enumerate(sc):
            if t == 0 and mask is not None:
                log_1m = jnp.where(mask, log_1m, 0.0)
            s2.append(suffix_sums(log_1m))
        for t, (first, start, log_beta, _) in enumerate(sc):
            diagonal = t == 0 and mask is not None
            if not diagonal:
                carry = retire(carry, first, j + t)
            a = jnp.exp2(log_beta + s2[t][:, :tk] + carry)
            if diagonal:
                a = jnp.where(mask, a, 0.0)
            carry = carry + s2[t][:, tk:]
            acc = acc + weigh(a, start)
        return carry, acc

    def live(c):
        return (jnp.max(c) > -SB_EXIT_LOG2).astype(jnp.int32)

    def qblock(i, _):
        q0 = pl.multiple_of(i * tq, tq)
        q = qn_ref[pl.ds(q0, tq), :]
        carry = jnp.zeros((tq, tk), F32)
        acc = jnp.zeros((tq, SB_DH), F32)
        carry, acc = sweeps(q, q0, 0, SB_PEELED_SWEEPS, carry, acc, below_diag)

        def cond(st):
            j, go, _, _ = st
            return jnp.logical_and(go > 0, j < (i + 1) * nd)

        def body(st):
            j, _, carry, acc = st
            carry, acc = sweeps(q, q0, j, SB_LOOP_SWEEPS, carry, acc, None)
            return j + SB_LOOP_SWEEPS, live(carry), carry, acc

        _, _, _, acc = lax.while_loop(cond, body, (SB_PEELED_SWEEPS, live(carry), carry, acc))
        o_ref[pl.ds(q0, tq), :] = (acc * _rms_scale(acc) * gn_ref[...]).astype(o_ref.dtype)
        return 0

    lax.fori_loop(0, seq // tq, qblock, 0)
    side_b_ref[...] = side_ref[...].astype(BF16)


def _sb(proj, gq, gk, gn, side, *, layer, col0, batch, seq, tq=4096, tk=128):
    m = proj.shape[0]
    tq = min(tq, seq)
    assert tq % tk == 0 and seq % tq == 0
    jj = lax.broadcasted_iota(jnp.int32, (tk, 2 * tk), 0)
    ss = lax.broadcasted_iota(jnp.int32, (tk, 2 * tk), 1)
    wmat = jnp.logical_or(ss >= tk, jj > ss).astype(BF16)
    vec = pl.BlockSpec((1, SB_DH), lambda b, h: (0, 0))
    rows, cols = side.shape[1] // (batch * SB_HEADS), side.shape[2]
    return pl.pallas_call(
        functools.partial(_sb_kernel, tq=tq, tk=tk, seq=seq),
        grid=(batch, SB_HEADS),
        in_specs=[pl.BlockSpec((seq, SB_DH), lambda b, h: (b, col0 + h)),
                  pl.BlockSpec((seq, SB_DH), lambda b, h: (b, col0 + SB_HEADS + h)),
                  pl.BlockSpec((seq, SB_DH), lambda b, h: (b, col0 + 2 * SB_HEADS + h)),
                  pl.BlockSpec((tk, 2 * tk), lambda b, h: (0, 0)),
                  vec, vec, vec,
                  pl.BlockSpec((None, rows, cols), lambda b, h: (layer, b * SB_HEADS + h, 0))],
        out_specs=[pl.BlockSpec((seq, SB_DH), lambda b, h: (b, h)),
                   pl.BlockSpec((rows, cols), lambda b, h: (b * SB_HEADS + h, 0))],
        out_shape=[jax.ShapeDtypeStruct((m, SB_HEADS * SB_DH), BF16),
                   jax.ShapeDtypeStruct(side.shape[1:], BF16)],
        scratch_shapes=[pltpu.VMEM((seq, SB_DH), BF16), pltpu.VMEM((seq, SB_DH), BF16)],
        compiler_params=_params("parallel", "parallel"),
        name="stickbreak",
    )(proj, proj, proj, wmat, gq, gk, gn, side)


def _layer(x, xb, ss, l, next_gain, w_in_b, w_in, w_gate_up, b_gate, gla_out_norm, sb_q_norm,
           sb_k_norm, sb_out_norm, w_o, mlp_norm, w_up, w_down, *, batch, seq):
    hk = GLA_HEADS * GLA_DK
    hv = GLA_HEADS * GLA_DV
    n_gla = 2 * hk + 2 * hv
    pad = LANES - 3 * GLA_GATE_RANK
    w_lr = jnp.pad(jnp.tile(w_in[l, :, n_gla:n_gla + GLA_GATE_RANK], (1, 3)), ((0, 0), (0, pad)))
    wg = jnp.pad(jnp.tile(w_gate_up[l], (3, 1)), ((0, pad), (0, 0)))
    gq = (sb_q_norm[l] * (math.log2(math.e) / math.sqrt(SB_DH))).reshape(1, SB_DH)

    p_gla = _matmul(_identity, xb, ss, w_in_b, n=n_gla, layer=l, out_dtype=BF16,
                    name="inproj_gla")
    p_sb = _matmul(_identity, xb, ss, w_in_b[l, :, n_gla + GLA_GATE_RANK:], out_dtype=BF16,
                   name="inproj_sb")
    o_gla = _gla(p_gla, xb, ss, w_lr, wg, b_gate[l].reshape(1, hk),
                 gla_out_norm[l].reshape(1, GLA_DV), batch=batch, seq=seq)
    o_sb, w_o_b = _sb(p_sb, gq, sb_k_norm[l].reshape(1, SB_DH), sb_out_norm[l].reshape(1, SB_DH),
                      w_o, layer=l, col0=0, batch=batch, seq=seq)
    x, xb, ss = _oproj(o_gla, o_sb, w_o_b, x, mlp_norm[l])
    up, w_down_b = _matmul(_relu2, xb, ss, w_up, layer=l, side=w_down, out_dtype=BF16,
                           name="mlp_up")
    if next_gain is None:
        return _down(up, w_down_b, x, None, layer=l), None, None
    return _down(up, w_down_b, x, next_gain, layer=l)


def kernel(x, attn_norm, w_in, w_gate_up, b_gate, gla_out_norm, sb_q_norm, sb_k_norm,
           sb_out_norm, w_o, mlp_norm, w_up, w_down):
    batch, seq, d = x.shape
    depth = w_in.shape[0]
    y = x.reshape(batch * seq, d)
    w_in_b = w_in.astype(BF16)
    yb, ss = y, attn_norm[0].reshape(1, d)
    for l in range(depth):
        next_gain = attn_norm[l + 1] if l + 1 < depth else None
        y, yb, ss = _layer(y, yb, ss, l, next_gain, w_in_b, w_in, w_gate_up, b_gate, gla_out_norm,
                           sb_q_norm, sb_k_norm, sb_out_norm, w_o, mlp_norm, w_up, w_down,
                           batch=batch, seq=seq)
    return y.reshape(batch, seq, d)
```

```python
import functools
import math

import jax
import jax.numpy as jnp
from jax import lax
from jax.experimental import pallas as pl
from jax.experimental.pallas import tpu as pltpu

EPS = 1e-6
LANES = 128

GLA_HEADS = 8
GLA_DK = 64
GLA_DV = 128
GLA_GATE_RANK = 16
GLA_GATE_TAU = 16.0
GLA_CHUNK = 64
GLA_CHUNKS_PER_TRIP = 4
SB_HEADS = 8
SB_DH = 128

SB_EXIT_LOG2 = 126.0
SB_DEAD = -1e30
SB_PEELED_SWEEPS = 2
SB_LOOP_SWEEPS = 1

V7X_VMEM_BYTES = 64 * 1024 * 1024
VMEM_LIMIT = V7X_VMEM_BYTES * 7 // 8

F32 = jnp.float32
BF16 = jnp.bfloat16
NT_DIMS = (((1,), (1,)), ((), ()))
TN_DIMS = (((0,), (0,)), ((), ()))


def _params(*sem):
    return pltpu.CompilerParams(dimension_semantics=sem, vmem_limit_bytes=VMEM_LIMIT)


def _log_sigmoid(z):
    return jnp.minimum(z, 0.0) - jnp.log(1.0 + jnp.exp(-jnp.abs(z)))


def _rms_scale(y):
    return lax.rsqrt(jnp.mean(y * y, axis=-1, keepdims=True) + EPS)


def _sumsq(y):
    return jnp.broadcast_to(jnp.sum(y * y, axis=-1, keepdims=True), (y.shape[0], LANES))


def _normed_operand(a_ref, aux_ref):
    d = a_ref.shape[1]
    if a_ref.dtype == BF16:
        return a_ref[...], lax.rsqrt(jnp.sum(aux_ref[...], axis=0) * (1.0 / d) + EPS)
    x = a_ref[...]
    return (x * aux_ref[...]).astype(BF16), lax.rsqrt(_sumsq(x) * (1.0 / d) + EPS)


def _normed_specs(a, aux, rows, row_map):
    a_spec = pl.BlockSpec((rows, a.shape[1]), lambda *g: (row_map(*g), 0))
    if a.dtype == BF16:
        return a_spec, pl.BlockSpec((aux.shape[0], rows, LANES), lambda *g: (0, row_map(*g), 0))
    return a_spec, pl.BlockSpec((1, a.shape[1]), lambda *g: (0, 0))


def _emit_normed(y, g_ref, o_ref, xb_ref, ss_ref):
    o_ref[...] = y
    xb_ref[...] = (y * g_ref[...]).astype(BF16)
    ss_ref[...] = _sumsq(y)


def _w_spec(w, rows, cols, index_map, layer):
    if w.ndim == 2:
        return pl.BlockSpec((rows, cols), index_map)
    return pl.BlockSpec((None, rows, cols), lambda *g: (layer,) + tuple(index_map(*g)))


def _mm_kernel(a_ref, aux_ref, w_ref, *rest, post):
    o_ref = rest[-1] if len(rest) == 1 else rest[1]
    xb, r = _normed_operand(a_ref, aux_ref)
    acc = jnp.dot(xb, w_ref[...].astype(BF16), preferred_element_type=F32)
    for c in range(acc.shape[1] // LANES):
        sl = slice(c * LANES, (c + 1) * LANES)
        o_ref[:, sl] = post(acc[:, sl] * r).astype(o_ref.dtype)
    if len(rest) == 3:
        rest[2][...] = rest[0][...].astype(BF16)


def _identity(y):
    return y


def _relu2(y):
    return jnp.square(jnp.maximum(y, 0.0))


def _matmul(post, a, aux, w, *, n=None, layer=0, side=None, out_dtype, tm=1024, tn=1024, name):
    m, k = a.shape
    n = w.shape[-1] if n is None else n
    tm = min(tm, m)
    tn = min(tn, n)
    nj, ni = n // tn, m // tm
    in_specs = [*_normed_specs(a, aux, tm, lambda j, i: i),
                _w_spec(w, k, tn, lambda j, i: (0, j), layer)]
    out_specs = pl.BlockSpec((tm, tn), lambda j, i: (i, j))
    out_shape = jax.ShapeDtypeStruct((m, n), out_dtype)
    args = [a, aux, w]
    if side is not None:
        rows, cols = side.shape[1] // (nj * ni), side.shape[2]
        in_specs.append(pl.BlockSpec((None, rows, cols), lambda j, i: (layer, j * ni + i, 0)))
        out_specs = [out_specs, pl.BlockSpec((rows, cols), lambda j, i: (j * ni + i, 0))]
        out_shape = [out_shape, jax.ShapeDtypeStruct(side.shape[1:], BF16)]
        args.append(side)
    return pl.pallas_call(
        functools.partial(_mm_kernel, post=post),
        grid=(nj, ni),
        in_specs=in_specs,
        out_specs=out_specs,
        out_shape=out_shape,
        compiler_params=_params("parallel", "arbitrary"),
        name=name,
    )(*args)


def _normed_out(m, n, tm, tn, tile_map, slot_map):
    specs = [pl.BlockSpec((tm, tn), tile_map), pl.BlockSpec((tm, tn), tile_map),
             pl.BlockSpec((None, tm, LANES), slot_map)]
    shapes = [jax.ShapeDtypeStruct((m, n), F32), jax.ShapeDtypeStruct((m, n), BF16),
              jax.ShapeDtypeStruct((n // tn, m, LANES), F32)]
    return specs, shapes


def _oproj_kernel(a1_ref, a2_ref, w1_ref, w2_ref, x_ref, g_ref, o_ref, xb_ref, ss_ref):
    acc = jnp.dot(a1_ref[...], w1_ref[...], preferred_element_type=F32)
    acc += jnp.dot(a2_ref[...], w2_ref[...], preferred_element_type=F32)
    _emit_normed(x_ref[...] + acc, g_ref, o_ref, xb_ref, ss_ref)


def _oproj(a1, a2, w, x, gain, *, tm=512):
    m, kh = a1.shape
    n = w.shape[-1]
    tm = min(tm, m)
    out_specs, out_shape = _normed_out(m, n, tm, n, lambda i: (i, 0), lambda i: (0, i, 0))
    return pl.pallas_call(
        _oproj_kernel,
        grid=(m // tm,),
        in_specs=[pl.BlockSpec((tm, kh), lambda i: (i, 0)),
                  pl.BlockSpec((tm, kh), lambda i: (i, 0)),
                  pl.BlockSpec((kh, n), lambda i: (0, 0)),
                  pl.BlockSpec((kh, n), lambda i: (1, 0)),
                  pl.BlockSpec((tm, n), lambda i: (i, 0)),
                  pl.BlockSpec((1, n), lambda i: (0, 0))],
        out_specs=out_specs,
        out_shape=out_shape,
        compiler_params=_params("parallel"),
        name="oproj",
    )(a1, a2, w, w, x, gain.reshape(1, n))


def _down_kernel(a_ref, w_ref, x_ref, *rest, normed):
    acc_ref = rest[-1]
    kk = pl.program_id(2)

    @pl.when(kk == 0)
    def _():
        acc_ref[...] = jnp.zeros_like(acc_ref)

    acc_ref[...] += jnp.dot(a_ref[...], w_ref[...].astype(BF16), preferred_element_type=F32)

    @pl.when(kk == pl.num_programs(2) - 1)
    def _():
        y = x_ref[...] + acc_ref[...]
        if normed:
            _emit_normed(y, *rest[:-1])
        else:
            rest[0][...] = y


def _down(a, w, x, gain, *, layer, tm=1024, tn=1024, tk=2048):
    m, k = a.shape
    n = w.shape[-1]
    tm = min(tm, m)
    tile_map = lambda i, j, kk: (i, j)
    in_specs = [pl.BlockSpec((tm, tk), lambda i, j, kk: (i, kk)),
                _w_spec(w, tk, tn, lambda i, j, kk: (kk, j), layer),
                pl.BlockSpec((tm, tn), tile_map)]
    args = [a, w, x]
    if gain is None:
        out_specs = pl.BlockSpec((tm, tn), tile_map)
        out_shape = jax.ShapeDtypeStruct((m, n), F32)
    else:
        in_specs.append(pl.BlockSpec((1, tn), lambda i, j, kk: (0, j)))
        args.append(gain.reshape(1, n))
        out_specs, out_shape = _normed_out(m, n, tm, tn, tile_map, lambda i, j, kk: (j, i, 0))
    return pl.pallas_call(
        functools.partial(_down_kernel, normed=gain is not None),
        grid=(m // tm, n // tn, k // tk),
        in_specs=in_specs,
        out_specs=out_specs,
        out_shape=out_shape,
        scratch_shapes=[pltpu.VMEM((tm, tn), F32)],
        compiler_params=_params("parallel", "arbitrary", "arbitrary"),
        name="down",
    )(*args)


def _gla_kernel(q_ref, k_ref, v_ref, gate_ref, a_ref, aux_ref, wlr_ref, wg_ref, bg_ref, gn_ref,
                o_ref, state_ref, qe_ref, ke_ref, kdec_ref, decay_ref, *, tb):
    @pl.when(pl.program_id(1) == 0)
    def _():
        state_ref[...] = jnp.zeros_like(state_ref)

    c = GLA_CHUNK
    row = lax.broadcasted_iota(jnp.int32, (c, c), 0)
    col = lax.broadcasted_iota(jnp.int32, (c, c), 1)
    causal = col <= row
    tri = jnp.where(causal, 1.0, 0.0).astype(BF16)
    tri2 = jnp.concatenate([tri, tri], axis=1)
    lane = lax.broadcasted_iota(jnp.int32, (c, LANES), 1)
    head_mask = (lane < GLA_DK, lane >= GLA_DK)

    r = GLA_GATE_RANK
    xb, x_scale = _normed_operand(a_ref, aux_ref)
    lr = x_scale * jnp.dot(xb, wlr_ref[...].astype(BF16), preferred_element_type=F32)
    lr_hi = lr.astype(BF16)
    lr_lo = (lr - lr_hi.astype(F32)).astype(BF16)
    lr_lane = lax.broadcasted_iota(jnp.int32, lr.shape, 1)
    lr_mix = jnp.where(jnp.logical_and(lr_lane >= r, lr_lane < 2 * r), lr_lo, lr_hi)
    wg = wg_ref[...]
    wg_hi = wg.astype(BF16)
    wg_lo = (wg - wg_hi.astype(F32)).astype(BF16)
    wg_mix = jnp.where(lax.broadcasted_iota(jnp.int32, wg.shape, 0) >= 2 * r, wg_lo, wg_hi)
    logits = jnp.dot(lr_mix, wg_mix, preferred_element_type=F32) + bg_ref[...]
    log_a = _log_sigmoid(logits) * (1.0 / GLA_GATE_TAU)

    hi = log_a.astype(BF16)
    lo = (log_a - hi.astype(F32)).astype(BF16)
    n_chunks = tb // c
    bcum_c = [jnp.dot(tri2, jnp.concatenate([hi[t * c:(t + 1) * c], lo[t * c:(t + 1) * c]],
                                            axis=0), preferred_element_type=F32)
              for t in range(n_chunks)]
    bcum = jnp.concatenate(bcum_c, axis=0)
    b_last = jnp.concatenate(
        [jnp.broadcast_to(b[c - 1:c, :], b.shape) for b in bcum_c], axis=0)
    kk = k_ref[...].astype(F32)
    qe_ref[...] = (q_ref[...].astype(F32) * (GLA_DK ** -0.5) * jnp.exp(bcum)).astype(BF16)
    ke_ref[...] = (kk * jnp.exp(-bcum)).astype(BF16)
    kdec_ref[...] = (kk * jnp.exp(b_last - bcum)).astype(BF16)
    decay_ref[...] = jnp.concatenate([jnp.exp(b[c - 1:c, :]) for b in bcum_c], axis=0)

    def chunk(ci, carry):
        r0 = pl.multiple_of(ci * c, c)
        rows = pl.ds(r0, c)
        q_e = qe_ref[rows, :]
        k_e = ke_ref[rows, :]
        k_dec = kdec_ref[rows, :]
        decay = decay_ref[pl.ds(ci, 1), :]
        heads = range(GLA_HEADS)
        pair = lambda x, h: x[:, (h // 2) * LANES:(h // 2 + 1) * LANES]
        hsl = lambda h: slice(h * GLA_DV, (h + 1) * GLA_DV)
        zero = jnp.zeros((), BF16)
        qm = [jnp.where(head_mask[h % 2], pair(q_e, h), zero) for h in heads]
        vb = [v_ref[rows, hsl(h)] for h in heads]
        s = [lax.dot_general(qm[h], pair(k_e, h), NT_DIMS, preferred_element_type=F32)
             for h in heads]
        st = [state_ref[h] for h in heads]
        o_inter = [lax.dot_general(qm[h], st[h].astype(BF16), NT_DIMS,
                                   preferred_element_type=F32) for h in heads]
        u_t = [lax.dot_general(vb[h], pair(k_dec, h), TN_DIMS, preferred_element_type=F32)
               for h in heads]
        for h in heads:
            state_ref[h] = pair(decay, h) * st[h] + u_t[h]
        o_intra = [jnp.dot(jnp.where(causal, s[h], 0.0).astype(BF16), vb[h],
                           preferred_element_type=F32) for h in heads]
        for h in heads:
            o = o_intra[h] + o_inter[h]
            y = o * _rms_scale(o) * gn_ref[...]
            g = gate_ref[rows, hsl(h)].astype(F32)
            y = y * (g * (1.0 / (1.0 + jnp.exp(-g))))
            o_ref[rows, hsl(h)] = y.astype(o_ref.dtype)
        return carry

    lax.fori_loop(0, tb // c, chunk, 0, unroll=GLA_CHUNKS_PER_TRIP)


def _gla(proj, a, aux, w_lr, wg, bg, gn, *, batch, seq, tb=512):
    m = proj.shape[0]
    d = a.shape[1]
    tb = min(tb, seq)
    nt = seq // tb
    hk = GLA_HEADS * GLA_DK
    hv = GLA_HEADS * GLA_DV
    row = lambda b, t: b * nt + t
    return pl.pallas_call(
        functools.partial(_gla_kernel, tb=tb),
        grid=(batch, nt),
        in_specs=[pl.BlockSpec((tb, hk), lambda b, t: (row(b, t), 0)),
                  pl.BlockSpec((tb, hk), lambda b, t: (row(b, t), 1)),
                  pl.BlockSpec((tb, hv), lambda b, t: (row(b, t), 1)),
                  pl.BlockSpec((tb, hv), lambda b, t: (row(b, t), 2)),
                  *_normed_specs(a, aux, tb, row),
                  pl.BlockSpec((d, LANES), lambda b, t: (0, 0)),
                  pl.BlockSpec((LANES, hk), lambda b, t: (0, 0)),
                  pl.BlockSpec((1, hk), lambda b, t: (0, 0)),
                  pl.BlockSpec((1, GLA_DV), lambda b, t: (0, 0))],
        out_specs=pl.BlockSpec((tb, hv), lambda b, t: (row(b, t), 0)),
        out_shape=jax.ShapeDtypeStruct((m, hv), BF16),
        scratch_shapes=[pltpu.VMEM((GLA_HEADS, GLA_DV, LANES), F32),
                        pltpu.VMEM((tb, hk), BF16), pltpu.VMEM((tb, hk), BF16),
                        pltpu.VMEM((tb, hk), BF16), pltpu.VMEM((tb // GLA_CHUNK, hk), F32)],
        compiler_params=_params("parallel", "arbitrary"),
        name="gla",
    )(proj, proj, proj, proj, a, aux, w_lr, wg, bg, gn)


def _sb_kernel(q_ref, k_ref, v_ref, w_ref, gq_ref, gk_ref, gn_ref, side_ref, o_ref, side_b_ref,
               qn_ref, kn_ref, *, tq, tk, seq):
    wmat = w_ref[...]
    lane_mean = jnp.full((SB_DH, SB_DH), 1.0 / SB_DH, BF16)
    nd = tq // tk
    rowi = lax.broadcasted_iota(jnp.int32, (tk, tk), 0)
    coli = lax.broadcasted_iota(jnp.int32, (tk, tk), 1)
    below_diag = jnp.concatenate([coli < rowi] * nd, axis=0)

    def normed(y, g_ref):
        ms = jnp.dot((y * y).astype(BF16), lane_mean, preferred_element_type=F32)
        return (y * (lax.rsqrt(ms + EPS) * g_ref[...])).astype(BF16)

    def head_norm(i, _):
        rows = pl.ds(pl.multiple_of(i * tq, tq), tq)
        qn_ref[rows, :] = normed(q_ref[rows, :].astype(F32), gq_ref)
        kn_ref[rows, :] = normed(k_ref[rows, :].astype(F32), gk_ref)
        return 0

    lax.fori_loop(0, seq // tq, head_norm, 0)

    group = lambda x, g: x[g * tk:(g + 1) * tk]

    def scores(q, q0, j):
        first = [q0 + (g - j) * tk for g in range(nd)]
        start = [pl.multiple_of(jnp.maximum(f, 0), tk) for f in first]
        z = jnp.concatenate(
            [lax.dot_general(group(q, g), kn_ref[pl.ds(start[g], tk), :], NT_DIMS,
                             preferred_element_type=F32) for g in range(nd)], axis=0)
        log_beta = jnp.minimum(z, 0.0) - jnp.log2(1.0 + jnp.exp2(-jnp.abs(z)))
        return first, start, log_beta, log_beta - z

    def suffix_sums(log_1m):
        return jnp.dot(log_1m.astype(BF16), wmat, preferred_element_type=F32)

    def retire(carry, first, j):
        keeps = lambda g: isinstance(j, int) and g >= j
        return jnp.concatenate(
            [group(carry, g) if keeps(g) else jnp.where(first[g] >= 0, group(carry, g), SB_DEAD)
             for g in range(nd)], axis=0)

    def weigh(a, start):
        a = a.astype(BF16)
        return jnp.concatenate(
            [jnp.dot(group(a, g), v_ref[pl.ds(start[g], tk), :], preferred_element_type=F32)
             for g in range(nd)], axis=0)

    def sweeps(q, q0, j, n, carry, acc, mask):
        sc = [scores(q, q0, j + t) for t in range(n)]
        s2 = []
        for t, (_, _, _, log_1m) in enumerate(sc):
            if t == 0 and mask is not None:
                log_1m = jnp.where(mask, log_1m, 0.0)
            s2.append(suffix_sums(log_1m))
        for t, (first, start, log_beta, _) in enumerate(sc):
            diagonal = t == 0 and mask is not None
            if not diagonal:
                carry = retire(carry, first, j + t)
            prefix = s2[t]
            carry = carry + jnp.broadcast_to(prefix[:, tk - 1:tk], prefix.shape)
            a = jnp.exp2(log_beta - prefix + carry)
            if diagonal:
                a = jnp.where(mask, a, 0.0)
            acc = acc + weigh(a, start)
        return carry, acc

    def live(c):
        return (jnp.max(c) > -SB_EXIT_LOG2).astype(jnp.int32)

    def qblock(i, _):
        q0 = pl.multiple_of(i * tq, tq)
        q = qn_ref[pl.ds(q0, tq), :]
        carry = jnp.zeros((tq, tk), F32)
        acc = jnp.zeros((tq, SB_DH), F32)
        carry, acc = sweeps(q, q0, 0, SB_PEELED_SWEEPS, carry, acc, below_diag)

        def cond(st):
            j, go, _, _ = st
            return jnp.logical_and(go > 0, j < (i + 1) * nd)

        def body(st):
            j, _, carry, acc = st
            carry, acc = sweeps(q, q0, j, SB_LOOP_SWEEPS, carry, acc, None)
            return j + SB_LOOP_SWEEPS, live(carry), carry, acc

        _, _, _, acc = lax.while_loop(cond, body, (SB_PEELED_SWEEPS, live(carry), carry, acc))
        o_ref[pl.ds(q0, tq), :] = (acc * _rms_scale(acc) * gn_ref[...]).astype(o_ref.dtype)
        return 0

    lax.fori_loop(0, seq // tq, qblock, 0)
    side_b_ref[...] = side_ref[...].astype(BF16)


def _sb(proj, gq, gk, gn, side, *, layer, col0, batch, seq, tq=4096, tk=128):
    m = proj.shape[0]
    tq = min(tq, seq)
    assert tq % tk == 0 and seq % tq == 0
    jj = lax.broadcasted_iota(jnp.int32, (tk, tk), 0)
    ss = lax.broadcasted_iota(jnp.int32, (tk, tk), 1)
    wmat = (jj <= ss).astype(BF16)
    vec = pl.BlockSpec((1, SB_DH), lambda b, h: (0, 0))
    rows, cols = side.shape[1] // (batch * SB_HEADS), side.shape[2]
    return pl.pallas_call(
        functools.partial(_sb_kernel, tq=tq, tk=tk, seq=seq),
        grid=(batch, SB_HEADS),
        in_specs=[pl.BlockSpec((seq, SB_DH), lambda b, h: (b, col0 + h)),
                  pl.BlockSpec((seq, SB_DH), lambda b, h: (b, col0 + SB_HEADS + h)),
                  pl.BlockSpec((seq, SB_DH), lambda b, h: (b, col0 + 2 * SB_HEADS + h)),
                  pl.BlockSpec((tk, tk), lambda b, h: (0, 0)),
                  vec, vec, vec,
                  pl.BlockSpec((None, rows, cols), lambda b, h: (layer, b * SB_HEADS + h, 0))],
        out_specs=[pl.BlockSpec((seq, SB_DH), lambda b, h: (b, h)),
                   pl.BlockSpec((rows, cols), lambda b, h: (b * SB_HEADS + h, 0))],
        out_shape=[jax.ShapeDtypeStruct((m, SB_HEADS * SB_DH), BF16),
                   jax.ShapeDtypeStruct(side.shape[1:], BF16)],
        scratch_shapes=[pltpu.VMEM((seq, SB_DH), BF16), pltpu.VMEM((seq, SB_DH), BF16)],
        compiler_params=_params("parallel", "parallel"),
        name="stickbreak",
    )(proj, proj, proj, wmat, gq, gk, gn, side)


def _layer(x, xb, ss, l, next_gain, w_in_b, w_in, w_gate_up, b_gate, gla_out_norm, sb_q_norm,
           sb_k_norm, sb_out_norm, w_o, mlp_norm, w_up, w_down, *, batch, seq):
    hk = GLA_HEADS * GLA_DK
    hv = GLA_HEADS * GLA_DV
    n_gla = 2 * hk + 2 * hv
    pad = LANES - 3 * GLA_GATE_RANK
    w_lr = jnp.pad(jnp.tile(w_in[l, :, n_gla:n_gla + GLA_GATE_RANK], (1, 3)), ((0, 0), (0, pad)))
    wg = jnp.pad(jnp.tile(w_gate_up[l], (3, 1)), ((0, pad), (0, 0)))
    gq = (sb_q_norm[l] * (math.log2(math.e) / math.sqrt(SB_DH))).reshape(1, SB_DH)

    p_gla = _matmul(_identity, xb, ss, w_in_b, n=n_gla, layer=l, out_dtype=BF16,
                    name="inproj_gla")
    p_sb = _matmul(_identity, xb, ss, w_in_b[l, :, n_gla + GLA_GATE_RANK:], out_dtype=BF16,
                   name="inproj_sb")
    o_gla = _gla(p_gla, xb, ss, w_lr, wg, b_gate[l].reshape(1, hk),
                 gla_out_norm[l].reshape(1, GLA_DV), batch=batch, seq=seq)
    o_sb, w_o_b = _sb(p_sb, gq, sb_k_norm[l].reshape(1, SB_DH), sb_out_norm[l].reshape(1, SB_DH),
                      w_o, layer=l, col0=0, batch=batch, seq=seq)
    x, xb, ss = _oproj(o_gla, o_sb, w_o_b, x, mlp_norm[l])
    up, w_down_b = _matmul(_relu2, xb, ss, w_up, layer=l, side=w_down, out_dtype=BF16,
                           name="mlp_up")
    if next_gain is None:
        return _down(up, w_down_b, x, None, layer=l), None, None
    return _down(up, w_down_b, x, next_gain, layer=l)


def kernel(x, attn_norm, w_in, w_gate_up, b_gate, gla_out_norm, sb_q_norm, sb_k_norm,
           sb_out_norm, w_o, mlp_norm, w_up, w_down):
    batch, seq, d = x.shape
    depth = w_in.shape[0]
    y = x.reshape(batch * seq, d)
    w_in_b = w_in.astype(BF16)
    yb, ss = y, attn_norm[0].reshape(1, d)
    for l in range(depth):
        next_gain = attn_norm[l + 1] if l + 1 < depth else None
        y, yb, ss = _layer(y, yb, ss, l, next_gain, w_in_b, w_in, w_gate_up, b_gate, gla_out_norm,
                           sb_q_norm, sb_k_norm, sb_out_norm, w_o, mlp_norm, w_up, w_down,
                           batch=batch, seq=seq)
    return y.reshape(batch, seq, d)
```

```python
import functools
import math

import jax
import jax.numpy as jnp
from jax import lax
from jax.experimental import pallas as pl
from jax.experimental.pallas import tpu as pltpu

EPS = 1e-6
LANES = 128

GLA_HEADS = 8
GLA_DK = 64
GLA_DV = 128
GLA_GATE_RANK = 16
GLA_GATE_TAU = 16.0
GLA_CHUNK = 64
GLA_CHUNKS_PER_TRIP = 4
SB_HEADS = 8
SB_DH = 128

SB_EXIT_LOG2 = 126.0
SB_DEAD = -1e30
SB_PEELED_SWEEPS = 2

V7X_VMEM_BYTES = 64 * 1024 * 1024
VMEM_LIMIT = V7X_VMEM_BYTES * 7 // 8

F32 = jnp.float32
BF16 = jnp.bfloat16
NT_DIMS = (((1,), (1,)), ((), ()))
TN_DIMS = (((0,), (0,)), ((), ()))


def _params(*sem):
    return pltpu.CompilerParams(dimension_semantics=sem, vmem_limit_bytes=VMEM_LIMIT)


def _log_sigmoid(z):
    return jnp.minimum(z, 0.0) - jnp.log(1.0 + jnp.exp(-jnp.abs(z)))


def _rms_scale(y):
    return lax.rsqrt(jnp.mean(y * y, axis=-1, keepdims=True) + EPS)


def _sumsq(y):
    return jnp.broadcast_to(jnp.sum(y * y, axis=-1, keepdims=True), (y.shape[0], LANES))


def _normed_operand(a_ref, aux_ref):
    d = a_ref.shape[1]
    if a_ref.dtype == BF16:
        return a_ref[...], lax.rsqrt(jnp.sum(aux_ref[...], axis=0) * (1.0 / d) + EPS)
    x = a_ref[...]
    return (x * aux_ref[...]).astype(BF16), lax.rsqrt(_sumsq(x) * (1.0 / d) + EPS)


def _normed_specs(a, aux, rows, row_map):
    a_spec = pl.BlockSpec((rows, a.shape[1]), lambda *g: (row_map(*g), 0))
    if a.dtype == BF16:
        return a_spec, pl.BlockSpec((aux.shape[0], rows, LANES), lambda *g: (0, row_map(*g), 0))
    return a_spec, pl.BlockSpec((1, a.shape[1]), lambda *g: (0, 0))


def _emit_normed(y, g_ref, o_ref, xb_ref, ss_ref):
    o_ref[...] = y
    xb_ref[...] = (y * g_ref[...]).astype(BF16)
    ss_ref[...] = _sumsq(y)


def _w_spec(w, rows, cols, index_map, layer):
    if w.ndim == 2:
        return pl.BlockSpec((rows, cols), index_map)
    return pl.BlockSpec((None, rows, cols), lambda *g: (layer,) + tuple(index_map(*g)))


def _mm_kernel(a_ref, aux_ref, w_ref, *rest, post):
    o_ref = rest[-1] if len(rest) == 1 else rest[1]
    xb, r = _normed_operand(a_ref, aux_ref)
    acc = jnp.dot(xb, w_ref[...].astype(BF16), preferred_element_type=F32)
    for c in range(acc.shape[1] // LANES):
        sl = slice(c * LANES, (c + 1) * LANES)
        o_ref[:, sl] = post(acc[:, sl] * r).astype(o_ref.dtype)
    if len(rest) == 3:
        rest[2][...] = rest[0][...].astype(BF16)


def _identity(y):
    return y


def _relu2(y):
    return jnp.square(jnp.maximum(y, 0.0))


def _matmul(post, a, aux, w, *, n=None, layer=0, side=None, out_dtype, tm=1024, tn=1024, name):
    m, k = a.shape
    n = w.shape[-1] if n is None else n
    tm = min(tm, m)
    tn = min(tn, n)
    nj, ni = n // tn, m // tm
    in_specs = [*_normed_specs(a, aux, tm, lambda j, i: i),
                _w_spec(w, k, tn, lambda j, i: (0, j), layer)]
    out_specs = pl.BlockSpec((tm, tn), lambda j, i: (i, j))
    out_shape = jax.ShapeDtypeStruct((m, n), out_dtype)
    args = [a, aux, w]
    if side is not None:
        rows, cols = side.shape[1] // (nj * ni), side.shape[2]
        in_specs.append(pl.BlockSpec((None, rows, cols), lambda j, i: (layer, j * ni + i, 0)))
        out_specs = [out_specs, pl.BlockSpec((rows, cols), lambda j, i: (j * ni + i, 0))]
        out_shape = [out_shape, jax.ShapeDtypeStruct(side.shape[1:], BF16)]
        args.append(side)
    return pl.pallas_call(
        functools.partial(_mm_kernel, post=post),
        grid=(nj, ni),
        in_specs=in_specs,
        out_specs=out_specs,
        out_shape=out_shape,
        compiler_params=_params("parallel", "arbitrary"),
        name=name,
    )(*args)


def _normed_out(m, n, tm, tn, tile_map, slot_map):
    specs = [pl.BlockSpec((tm, tn), tile_map), pl.BlockSpec((tm, tn), tile_map),
             pl.BlockSpec((None, tm, LANES), slot_map)]
    shapes = [jax.ShapeDtypeStruct((m, n), F32), jax.ShapeDtypeStruct((m, n), BF16),
              jax.ShapeDtypeStruct((n // tn, m, LANES), F32)]
    return specs, shapes


def _oproj_kernel(a1_ref, a2_ref, w1_ref, w2_ref, x_ref, g_ref, o_ref, xb_ref, ss_ref):
    acc = jnp.dot(a1_ref[...], w1_ref[...], preferred_element_type=F32)
    acc += jnp.dot(a2_ref[...], w2_ref[...], preferred_element_type=F32)
    _emit_normed(x_ref[...] + acc, g_ref, o_ref, xb_ref, ss_ref)


def _oproj(a1, a2, w, x, gain, *, tm=512):
    m, kh = a1.shape
    n = w.shape[-1]
    tm = min(tm, m)
    out_specs, out_shape = _normed_out(m, n, tm, n, lambda i: (i, 0), lambda i: (0, i, 0))
    return pl.pallas_call(
        _oproj_kernel,
        grid=(m // tm,),
        in_specs=[pl.BlockSpec((tm, kh), lambda i: (i, 0)),
                  pl.BlockSpec((tm, kh), lambda i: (i, 0)),
                  pl.BlockSpec((kh, n), lambda i: (0, 0)),
                  pl.BlockSpec((kh, n), lambda i: (1, 0)),
                  pl.BlockSpec((tm, n), lambda i: (i, 0)),
                  pl.BlockSpec((1, n), lambda i: (0, 0))],
        out_specs=out_specs,
        out_shape=out_shape,
        compiler_params=_params("parallel"),
        name="oproj",
    )(a1, a2, w, w, x, gain.reshape(1, n))


def _down_kernel(a_ref, w_ref, x_ref, *rest, normed):
    acc_ref = rest[-1]
    kk = pl.program_id(2)

    @pl.when(kk == 0)
    def _():
        acc_ref[...] = jnp.zeros_like(acc_ref)

    acc_ref[...] += jnp.dot(a_ref[...], w_ref[...].astype(BF16), preferred_element_type=F32)

    @pl.when(kk == pl.num_programs(2) - 1)
    def _():
        y = x_ref[...] + acc_ref[...]
        if normed:
            _emit_normed(y, *rest[:-1])
        else:
            rest[0][...] = y


def _down(a, w, x, gain, *, layer, tm=1024, tn=1024, tk=2048):
    m, k = a.shape
    n = w.shape[-1]
    tm = min(tm, m)
    tile_map = lambda i, j, kk: (i, j)
    in_specs = [pl.BlockSpec((tm, tk), lambda i, j, kk: (i, kk)),
                _w_spec(w, tk, tn, lambda i, j, kk: (kk, j), layer),
                pl.BlockSpec((tm, tn), tile_map)]
    args = [a, w, x]
    if gain is None:
        out_specs = pl.BlockSpec((tm, tn), tile_map)
        out_shape = jax.ShapeDtypeStruct((m, n), F32)
    else:
        in_specs.append(pl.BlockSpec((1, tn), lambda i, j, kk: (0, j)))
        args.append(gain.reshape(1, n))
        out_specs, out_shape = _normed_out(m, n, tm, tn, tile_map, lambda i, j, kk: (j, i, 0))
    return pl.pallas_call(
        functools.partial(_down_kernel, normed=gain is not None),
        grid=(m // tm, n // tn, k // tk),
        in_specs=in_specs,
        out_specs=out_specs,
        out_shape=out_shape,
        scratch_shapes=[pltpu.VMEM((tm, tn), F32)],
        compiler_params=_params("parallel", "arbitrary", "arbitrary"),
        name="down",
    )(*args)


def _gla_kernel(q_ref, k_ref, v_ref, gate_ref, a_ref, aux_ref, wlr_ref, wg_ref, bg_ref, gn_ref,
                o_ref, state_ref, qe_ref, ke_ref, kdec_ref, decay_ref, *, tb):
    @pl.when(pl.program_id(1) == 0)
    def _():
        state_ref[...] = jnp.zeros_like(state_ref)

    c = GLA_CHUNK
    row = lax.broadcasted_iota(jnp.int32, (c, c), 0)
    col = lax.broadcasted_iota(jnp.int32, (c, c), 1)
    causal = col <= row
    tri = jnp.where(causal, 1.0, 0.0).astype(BF16)
    tri2 = jnp.concatenate([tri, tri], axis=1)
    lane = lax.broadcasted_iota(jnp.int32, (c, LANES), 1)
    head_mask = (lane < GLA_DK, lane >= GLA_DK)

    r = GLA_GATE_RANK
    xb, x_scale = _normed_operand(a_ref, aux_ref)
    lr = x_scale * jnp.dot(xb, wlr_ref[...].astype(BF16), preferred_element_type=F32)
    lr_hi = lr.astype(BF16)
    lr_lo = (lr - lr_hi.astype(F32)).astype(BF16)
    lr_lane = lax.broadcasted_iota(jnp.int32, lr.shape, 1)
    lr_mix = jnp.where(jnp.logical_and(lr_lane >= r, lr_lane < 2 * r), lr_lo, lr_hi)
    wg = wg_ref[...]
    wg_hi = wg.astype(BF16)
    wg_lo = (wg - wg_hi.astype(F32)).astype(BF16)
    wg_mix = jnp.where(lax.broadcasted_iota(jnp.int32, wg.shape, 0) >= 2 * r, wg_lo, wg_hi)
    logits = jnp.dot(lr_mix, wg_mix, preferred_element_type=F32) + bg_ref[...]
    log_a = _log_sigmoid(logits) * (1.0 / GLA_GATE_TAU)

    hi = log_a.astype(BF16)
    lo = (log_a - hi.astype(F32)).astype(BF16)
    n_chunks = tb // c
    bcum_c = [jnp.dot(tri2, jnp.concatenate([hi[t * c:(t + 1) * c], lo[t * c:(t + 1) * c]],
                                            axis=0), preferred_element_type=F32)
              for t in range(n_chunks)]
    bcum = jnp.concatenate(bcum_c, axis=0)
    b_last = jnp.concatenate(
        [jnp.broadcast_to(b[c - 1:c, :], b.shape) for b in bcum_c], axis=0)
    kk = k_ref[...].astype(F32)
    qe_ref[...] = (q_ref[...].astype(F32) * (GLA_DK ** -0.5) * jnp.exp(bcum)).astype(BF16)
    ke_ref[...] = (kk * jnp.exp(-bcum)).astype(BF16)
    kdec_ref[...] = (kk * jnp.exp(b_last - bcum)).astype(BF16)
    decay_ref[...] = jnp.concatenate([jnp.exp(b[c - 1:c, :]) for b in bcum_c], axis=0)

    def chunk(ci, carry):
        r0 = pl.multiple_of(ci * c, c)
        rows = pl.ds(r0, c)
        q_e = qe_ref[rows, :]
        k_e = ke_ref[rows, :]
        k_dec = kdec_ref[rows, :]
        decay = decay_ref[pl.ds(ci, 1), :]
        heads = range(GLA_HEADS)
        pair = lambda x, h: x[:, (h // 2) * LANES:(h // 2 + 1) * LANES]
        hsl = lambda h: slice(h * GLA_DV, (h + 1) * GLA_DV)
        zero = jnp.zeros((), BF16)
        qm = [jnp.where(head_mask[h % 2], pair(q_e, h), zero) for h in heads]
        vb = [v_ref[rows, hsl(h)] for h in heads]
        s = [lax.dot_general(qm[h], pair(k_e, h), NT_DIMS, preferred_element_type=F32)
             for h in heads]
        st = [state_ref[h] for h in heads]
        o_inter = [lax.dot_general(qm[h], st[h].astype(BF16), NT_DIMS,
                                   preferred_element_type=F32) for h in heads]
        u_t = [lax.dot_general(vb[h], pair(k_dec, h), TN_DIMS, preferred_element_type=F32)
               for h in heads]
        for h in heads:
            state_ref[h] = pair(decay, h) * st[h] + u_t[h]
        o_intra = [jnp.dot(jnp.where(causal, s[h], 0.0).astype(BF16), vb[h],
                           preferred_element_type=F32) for h in heads]
        for h in heads:
            o = o_intra[h] + o_inter[h]
            y = o * _rms_scale(o) * gn_ref[...]
            g = gate_ref[rows, hsl(h)].astype(F32)
            y = y * (g * (1.0 / (1.0 + jnp.exp(-g))))
            o_ref[rows, hsl(h)] = y.astype(o_ref.dtype)
        return carry

    lax.fori_loop(0, tb // c, chunk, 0, unroll=GLA_CHUNKS_PER_TRIP)


def _gla(proj, a, aux, w_lr, wg, bg, gn, *, batch, seq, tb=1024):
    m = proj.shape[0]
    d = a.shape[1]
    tb = min(tb, seq)
    nt = seq // tb
    hk = GLA_HEADS * GLA_DK
    hv = GLA_HEADS * GLA_DV
    row = lambda b, t: b * nt + t
    return pl.pallas_call(
        functools.partial(_gla_kernel, tb=tb),
        grid=(batch, nt),
        in_specs=[pl.BlockSpec((tb, hk), lambda b, t: (row(b, t), 0)),
                  pl.BlockSpec((tb, hk), lambda b, t: (row(b, t), 1)),
                  pl.BlockSpec((tb, hv), lambda b, t: (row(b, t), 1)),
                  pl.BlockSpec((tb, hv), lambda b, t: (row(b, t), 2)),
                  *_normed_specs(a, aux, tb, row),
                  pl.BlockSpec((d, LANES), lambda b, t: (0, 0)),
                  pl.BlockSpec((LANES, hk), lambda b, t: (0, 0)),
                  pl.BlockSpec((1, hk), lambda b, t: (0, 0)),
                  pl.BlockSpec((1, GLA_DV), lambda b, t: (0, 0))],
        out_specs=pl.BlockSpec((tb, hv), lambda b, t: (row(b, t), 0)),
        out_shape=jax.ShapeDtypeStruct((m, hv), BF16),
        scratch_shapes=[pltpu.VMEM((GLA_HEADS, GLA_DV, LANES), F32),
                        pltpu.VMEM((tb, hk), BF16), pltpu.VMEM((tb, hk), BF16),
                        pltpu.VMEM((tb, hk), BF16), pltpu.VMEM((tb // GLA_CHUNK, hk), F32)],
        compiler_params=_params("parallel", "arbitrary"),
        name="gla",
    )(proj, proj, proj, proj, a, aux, w_lr, wg, bg, gn)


def _sb_kernel(q_ref, k_ref, v_ref, w_ref, gq_ref, gk_ref, gn_ref, side_ref, o_ref, side_b_ref,
               qn_ref, kn_ref, *, tq, tk, seq):
    wmat = w_ref[...]
    lane_mean = jnp.full((SB_DH, SB_DH), 1.0 / SB_DH, BF16)
    nd = tq // tk
    rowi = lax.broadcasted_iota(jnp.int32, (tk, tk), 0)
    coli = lax.broadcasted_iota(jnp.int32, (tk, tk), 1)
    below_diag = jnp.concatenate([coli < rowi] * nd, axis=0)

    def normed(y, g_ref):
        ms = jnp.dot((y * y).astype(BF16), lane_mean, preferred_element_type=F32)
        return (y * (lax.rsqrt(ms + EPS) * g_ref[...])).astype(BF16)

    def head_norm(i, _):
        rows = pl.ds(pl.multiple_of(i * tq, tq), tq)
        qn_ref[rows, :] = normed(q_ref[rows, :].astype(F32), gq_ref)
        kn_ref[rows, :] = normed(k_ref[rows, :].astype(F32), gk_ref)
        return 0

    lax.fori_loop(0, seq // tq, head_norm, 0)

    group = lambda x, g: x[g * tk:(g + 1) * tk]

    def scores(q, q0, j):
        first = [q0 + (g - j) * tk for g in range(nd)]
        start = [pl.multiple_of(jnp.maximum(f, 0), tk) for f in first]
        z = jnp.concatenate(
            [lax.dot_general(group(q, g), kn_ref[pl.ds(start[g], tk), :], NT_DIMS,
                             preferred_element_type=F32) for g in range(nd)], axis=0)
        log_beta = jnp.minimum(z, 0.0) - jnp.log2(1.0 + jnp.exp2(-jnp.abs(z)))
        return first, start, log_beta, log_beta - z

    def suffix_sums(log_1m):
        return jnp.dot(log_1m.astype(BF16), wmat, preferred_element_type=F32)

    def retire(carry, first, j):
        keeps = lambda g: isinstance(j, int) and g >= j
        return jnp.concatenate(
            [group(carry, g) if keeps(g) else jnp.where(first[g] >= 0, group(carry, g), SB_DEAD)
             for g in range(nd)], axis=0)

    def weigh(a, start):
        a = a.astype(BF16)
        return jnp.concatenate(
            [jnp.dot(group(a, g), v_ref[pl.ds(start[g], tk), :], preferred_element_type=F32)
             for g in range(nd)], axis=0)

    def sweeps(q, q0, j, n, carry, acc, mask):
        sc = [scores(q, q0, j + t) for t in range(n)]
        s2 = []
        for t, (_, _, _, log_1m) in enumerate(sc):
            if t == 0 and mask is not None:
                log_1m = jnp.where(mask, log_1m, 0.0)
            s2.append(suffix_sums(log_1m))
        for t, (first, start, log_beta, _) in enumerate(sc):
            diagonal = t == 0 and mask is not None
            if not diagonal:
                carry = retire(carry, first, j + t)
            prefix = s2[t]
            carry = carry + jnp.broadcast_to(prefix[:, tk - 1:tk], prefix.shape)
            a = jnp.exp2(log_beta - prefix + carry)
            if diagonal:
                a = jnp.where(mask, a, 0.0)
            acc = acc + weigh(a, start)
        return carry, acc

    def live(c):
        return (jnp.max(c) > -SB_EXIT_LOG2).astype(jnp.int32)

    def qblock(i, _):
        q0 = pl.multiple_of(i * tq, tq)
        q = qn_ref[pl.ds(q0, tq), :]
        carry = jnp.zeros((tq, tk), F32)
        acc = jnp.zeros((tq, SB_DH), F32)
        carry, acc = sweeps(q, q0, 0, SB_PEELED_SWEEPS, carry, acc, below_diag)

        def cond(st):
            j, go, _, _ = st
            return jnp.logical_and(go > 0, j < (i + 1) * nd)

        def body(st):
            j, _, carry, acc = st
            carry, acc = sweeps(q, q0, j, 1, carry, acc, None)
            return j + 1, live(carry), carry, acc

        _, _, _, acc = lax.while_loop(cond, body, (SB_PEELED_SWEEPS, live(carry), carry, acc))
        o_ref[pl.ds(q0, tq), :] = (acc * _rms_scale(acc) * gn_ref[...]).astype(o_ref.dtype)
        return 0

    lax.fori_loop(0, seq // tq, qblock, 0)
    side_b_ref[...] = side_ref[...].astype(BF16)


def _sb(proj, gq, gk, gn, side, *, layer, col0, batch, seq, tq=4096, tk=128):
    m = proj.shape[0]
    tq = min(tq, seq)
    assert tq % tk == 0 and seq % tq == 0
    jj = lax.broadcasted_iota(jnp.int32, (tk, tk), 0)
    ss = lax.broadcasted_iota(jnp.int32, (tk, tk), 1)
    wmat = (jj <= ss).astype(BF16)
    vec = pl.BlockSpec((1, SB_DH), lambda b, h: (0, 0))
    rows, cols = side.shape[1] // (batch * SB_HEADS), side.shape[2]
    return pl.pallas_call(
        functools.partial(_sb_kernel, tq=tq, tk=tk, seq=seq),
        grid=(batch, SB_HEADS),
        in_specs=[pl.BlockSpec((seq, SB_DH), lambda b, h: (b, col0 + h)),
                  pl.BlockSpec((seq, SB_DH), lambda b, h: (b, col0 + SB_HEADS + h)),
                  pl.BlockSpec((seq, SB_DH), lambda b, h: (b, col0 + 2 * SB_HEADS + h)),
                  pl.BlockSpec((tk, tk), lambda b, h: (0, 0)),
                  vec, vec, vec,
                  pl.BlockSpec((None, rows, cols), lambda b, h: (layer, b * SB_HEADS + h, 0))],
        out_specs=[pl.BlockSpec((seq, SB_DH), lambda b, h: (b, h)),
                   pl.BlockSpec((rows, cols), lambda b, h: (b * SB_HEADS + h, 0))],
        out_shape=[jax.ShapeDtypeStruct((m, SB_HEADS * SB_DH), BF16),
                   jax.ShapeDtypeStruct(side.shape[1:], BF16)],
        scratch_shapes=[pltpu.VMEM((seq, SB_DH), BF16), pltpu.VMEM((seq, SB_DH), BF16)],
        compiler_params=_params("parallel", "parallel"),
        name="stickbreak",
    )(proj, proj, proj, wmat, gq, gk, gn, side)


def _layer(x, xb, ss, l, next_gain, w_in_b, w_in, w_gate_up, b_gate, gla_out_norm, sb_q_norm,
           sb_k_norm, sb_out_norm, w_o, mlp_norm, w_up, w_down, *, batch, seq):
    hk = GLA_HEADS * GLA_DK
    hv = GLA_HEADS * GLA_DV
    n_gla = 2 * hk + 2 * hv
    pad = LANES - 3 * GLA_GATE_RANK
    w_lr = jnp.pad(jnp.tile(w_in[l, :, n_gla:n_gla + GLA_GATE_RANK], (1, 3)), ((0, 0), (0, pad)))
    wg = jnp.pad(jnp.tile(w_gate_up[l], (3, 1)), ((0, pad), (0, 0)))
    gq = (sb_q_norm[l] * (math.log2(math.e) / math.sqrt(SB_DH))).reshape(1, SB_DH)

    p_gla = _matmul(_identity, xb, ss, w_in_b, n=n_gla, layer=l, out_dtype=BF16,
                    name="inproj_gla")
    p_sb = _matmul(_identity, xb, ss, w_in_b[l, :, n_gla + GLA_GATE_RANK:], out_dtype=BF16,
                   name="inproj_sb")
    o_gla = _gla(p_gla, xb, ss, w_lr, wg, b_gate[l].reshape(1, hk),
                 gla_out_norm[l].reshape(1, GLA_DV), batch=batch, seq=seq)
    o_sb, w_o_b = _sb(p_sb, gq, sb_k_norm[l].reshape(1, SB_DH), sb_out_norm[l].reshape(1, SB_DH),
                      w_o, layer=l, col0=0, batch=batch, seq=seq)
    x, xb, ss = _oproj(o_gla, o_sb, w_o_b, x, mlp_norm[l])
    up, w_down_b = _matmul(_relu2, xb, ss, w_up, layer=l, side=w_down, out_dtype=BF16,
                           name="mlp_up")
    if next_gain is None:
        return _down(up, w_down_b, x, None, layer=l), None, None
    return _down(up, w_down_b, x, next_gain, layer=l)


def kernel(x, attn_norm, w_in, w_gate_up, b_gate, gla_out_norm, sb_q_norm, sb_k_norm,
           sb_out_norm, w_o, mlp_norm, w_up, w_down):
    batch, seq, d = x.shape
    depth = w_in.shape[0]
    y = x.reshape(batch * seq, d)
    w_in_b = w_in.astype(BF16)
    yb, ss = y, attn_norm[0].reshape(1, d)
    for l in range(depth):
        next_gain = attn_norm[l + 1] if l + 1 < depth else None
        y, yb, ss = _layer(y, yb, ss, l, next_gain, w_in_b, w_in, w_gate_up, b_gate, gla_out_norm,
                           sb_q_norm, sb_k_norm, sb_out_norm, w_o, mlp_norm, w_up, w_down,
                           batch=batch, seq=seq)
    return y.reshape(batch, seq, d)
```

```python
import functools
import math

import jax
import jax.numpy as jnp
from jax import lax
from jax.experimental import pallas as pl
from jax.experimental.pallas import tpu as pltpu

EPS = 1e-6
LANES = 128

GLA_HEADS = 8
GLA_DK = 64
GLA_DV = 128
GLA_GATE_RANK = 16
GLA_GATE_TAU = 16.0
GLA_CHUNK = 64
GLA_CHUNKS_PER_TRIP = 4
SB_HEADS = 8
SB_DH = 128

SB_EXIT_LOG2 = 126.0
SB_DEAD = -1e30
SB_PEELED_SWEEPS = 2
SB_LOOP_SWEEPS = 1

V7X_VMEM_BYTES = 64 * 1024 * 1024
VMEM_LIMIT = V7X_VMEM_BYTES * 7 // 8
DOWN_EPILOGUE_CHUNKS = 4

F32 = jnp.float32
BF16 = jnp.bfloat16
NT_DIMS = (((1,), (1,)), ((), ()))
TN_DIMS = (((0,), (0,)), ((), ()))


def _params(*sem):
    return pltpu.CompilerParams(dimension_semantics=sem, vmem_limit_bytes=VMEM_LIMIT)


def _log_sigmoid(z):
    return jnp.minimum(z, 0.0) - jnp.log(1.0 + jnp.exp(-jnp.abs(z)))


def _rms_scale(y):
    return lax.rsqrt(jnp.mean(y * y, axis=-1, keepdims=True) + EPS)


def _sumsq(y):
    return jnp.broadcast_to(jnp.sum(y * y, axis=-1, keepdims=True), (y.shape[0], LANES))


def _normed_operand(a_ref, aux_ref):
    d = a_ref.shape[1]
    if a_ref.dtype == BF16:
        return a_ref[...], lax.rsqrt(jnp.sum(aux_ref[...], axis=0) * (1.0 / d) + EPS)
    x = a_ref[...]
    return (x * aux_ref[...]).astype(BF16), lax.rsqrt(_sumsq(x) * (1.0 / d) + EPS)


def _normed_specs(a, aux, rows, row_map):
    a_spec = pl.BlockSpec((rows, a.shape[1]), lambda *g: (row_map(*g), 0))
    if a.dtype == BF16:
        return a_spec, pl.BlockSpec((aux.shape[0], rows, LANES), lambda *g: (0, row_map(*g), 0))
    return a_spec, pl.BlockSpec((1, a.shape[1]), lambda *g: (0, 0))


def _emit_normed(y, g_ref, o_ref, xb_ref, ss_ref):
    o_ref[...] = y
    xb_ref[...] = (y * g_ref[...]).astype(BF16)
    ss_ref[...] = _sumsq(y)


def _w_spec(w, rows, cols, index_map, layer):
    if w.ndim == 2:
        return pl.BlockSpec((rows, cols), index_map)
    return pl.BlockSpec((None, rows, cols), lambda *g: (layer,) + tuple(index_map(*g)))


def _mm_kernel(a_ref, aux_ref, w_ref, *rest, post):
    o_ref = rest[-1] if len(rest) == 1 else rest[1]
    xb, r = _normed_operand(a_ref, aux_ref)
    acc = jnp.dot(xb, w_ref[...].astype(BF16), preferred_element_type=F32)
    for c in range(acc.shape[1] // LANES):
        sl = slice(c * LANES, (c + 1) * LANES)
        o_ref[:, sl] = post(acc[:, sl] * r).astype(o_ref.dtype)
    if len(rest) == 3:
        rest[2][...] = rest[0][...].astype(BF16)


def _identity(y):
    return y


def _relu2(y):
    return jnp.square(jnp.maximum(y, 0.0))


def _matmul(post, a, aux, w, *, n=None, layer=0, side=None, out_dtype, tm=1024, tn=1024, name):
    m, k = a.shape
    n = w.shape[-1] if n is None else n
    tm = min(tm, m)
    tn = min(tn, n)
    nj, ni = n // tn, m // tm
    in_specs = [*_normed_specs(a, aux, tm, lambda j, i: i),
                _w_spec(w, k, tn, lambda j, i: (0, j), layer)]
    out_specs = pl.BlockSpec((tm, tn), lambda j, i: (i, j))
    out_shape = jax.ShapeDtypeStruct((m, n), out_dtype)
    args = [a, aux, w]
    if side is not None:
        rows, cols = side.shape[1] // (nj * ni), side.shape[2]
        in_specs.append(pl.BlockSpec((None, rows, cols), lambda j, i: (layer, j * ni + i, 0)))
        out_specs = [out_specs, pl.BlockSpec((rows, cols), lambda j, i: (j * ni + i, 0))]
        out_shape = [out_shape, jax.ShapeDtypeStruct(side.shape[1:], BF16)]
        args.append(side)
    return pl.pallas_call(
        functools.partial(_mm_kernel, post=post),
        grid=(nj, ni),
        in_specs=in_specs,
        out_specs=out_specs,
        out_shape=out_shape,
        compiler_params=_params("parallel", "arbitrary"),
        name=name,
    )(*args)


def _normed_out(m, n, tm, tn, tile_map, slot_map):
    specs = [pl.BlockSpec((tm, tn), tile_map), pl.BlockSpec((tm, tn), tile_map),
             pl.BlockSpec((None, tm, LANES), slot_map)]
    shapes = [jax.ShapeDtypeStruct((m, n), F32), jax.ShapeDtypeStruct((m, n), BF16),
              jax.ShapeDtypeStruct((n // tn, m, LANES), F32)]
    return specs, shapes


def _oproj_kernel(a1_ref, a2_ref, w1_ref, w2_ref, x_ref, g_ref, o_ref, xb_ref, ss_ref):
    acc = jnp.dot(a1_ref[...], w1_ref[...], preferred_element_type=F32)
    acc += jnp.dot(a2_ref[...], w2_ref[...], preferred_element_type=F32)
    _emit_normed(x_ref[...] + acc, g_ref, o_ref, xb_ref, ss_ref)


def _oproj(a1, a2, w, x, gain, *, tm=512):
    m, kh = a1.shape
    n = w.shape[-1]
    tm = min(tm, m)
    out_specs, out_shape = _normed_out(m, n, tm, n, lambda i: (i, 0), lambda i: (0, i, 0))
    return pl.pallas_call(
        _oproj_kernel,
        grid=(m // tm,),
        in_specs=[pl.BlockSpec((tm, kh), lambda i: (i, 0)),
                  pl.BlockSpec((tm, kh), lambda i: (i, 0)),
                  pl.BlockSpec((kh, n), lambda i: (0, 0)),
                  pl.BlockSpec((kh, n), lambda i: (1, 0)),
                  pl.BlockSpec((tm, n), lambda i: (i, 0)),
                  pl.BlockSpec((1, n), lambda i: (0, 0))],
        out_specs=out_specs,
        out_shape=out_shape,
        compiler_params=_params("parallel"),
        name="oproj",
    )(a1, a2, w, w, x, gain.reshape(1, n))


def _down_kernel(a_ref, w_ref, x_ref, *rest, normed):
    acc_ref = rest[-1]
    kk = pl.program_id(2)
    last = pl.num_programs(2) - 1

    @pl.when(kk == 0)
    def _():
        acc_ref[...] = jnp.zeros_like(acc_ref)

    @pl.when(kk < last)
    def _():
        acc_ref[...] += jnp.dot(a_ref[...], w_ref[...].astype(BF16),
                                preferred_element_type=F32)

    @pl.when(kk == last)
    def _():
        wb = w_ref[...].astype(BF16)
        tm = acc_ref.shape[0]
        rb = tm // DOWN_EPILOGUE_CHUNKS
        for c in range(DOWN_EPILOGUE_CHUNKS):
            rows = slice(c * rb, (c + 1) * rb)
            y = x_ref[rows, :] + (acc_ref[rows, :] + jnp.dot(
                a_ref[rows, :], wb, preferred_element_type=F32))
            if normed:
                g_ref, o_ref, xb_ref, ss_ref = rest[:-1]
                o_ref[rows, :] = y
                xb_ref[rows, :] = (y * g_ref[...]).astype(BF16)
                ss_ref[rows, :] = _sumsq(y)
            else:
                rest[0][rows, :] = y


def _down(a, w, x, gain, *, layer, tm=1024, tn=1024, tk=2048):
    m, k = a.shape
    n = w.shape[-1]
    tm = min(tm, m)
    tile_map = lambda i, j, kk: (i, j)
    in_specs = [pl.BlockSpec((tm, tk), lambda i, j, kk: (i, kk)),
                _w_spec(w, tk, tn, lambda i, j, kk: (kk, j), layer),
                pl.BlockSpec((tm, tn), tile_map)]
    args = [a, w, x]
    if gain is None:
        out_specs = pl.BlockSpec((tm, tn), tile_map)
        out_shape = jax.ShapeDtypeStruct((m, n), F32)
    else:
        in_specs.append(pl.BlockSpec((1, tn), lambda i, j, kk: (0, j)))
        args.append(gain.reshape(1, n))
        out_specs, out_shape = _normed_out(m, n, tm, tn, tile_map, lambda i, j, kk: (j, i, 0))
    return pl.pallas_call(
        functools.partial(_down_kernel, normed=gain is not None),
        grid=(m // tm, n // tn, k // tk),
        in_specs=in_specs,
        out_specs=out_specs,
        out_shape=out_shape,
        scratch_shapes=[pltpu.VMEM((tm, tn), F32)],
        compiler_params=_params("parallel", "arbitrary", "arbitrary"),
        name="down",
    )(*args)


def _gla_kernel(q_ref, k_ref, v_ref, gate_ref, a_ref, aux_ref, wlr_ref, wg_ref, bg_ref, gn_ref,
                o_ref, state_ref, qe_ref, ke_ref, kdec_ref, decay_ref, *, tb):
    @pl.when(pl.program_id(1) == 0)
    def _():
        state_ref[...] = jnp.zeros_like(state_ref)

    c = GLA_CHUNK
    row = lax.broadcasted_iota(jnp.int32, (c, c), 0)
    col = lax.broadcasted_iota(jnp.int32, (c, c), 1)
    causal = col <= row
    tri = jnp.where(causal, 1.0, 0.0).astype(BF16)
    tri2 = jnp.concatenate([tri, tri], axis=1)
    lane = lax.broadcasted_iota(jnp.int32, (c, LANES), 1)
    head_mask = (lane < GLA_DK, lane >= GLA_DK)

    r = GLA_GATE_RANK
    xb, x_scale = _normed_operand(a_ref, aux_ref)
    lr = x_scale * jnp.dot(xb, wlr_ref[...].astype(BF16), preferred_element_type=F32)
    lr_hi = lr.astype(BF16)
    lr_lo = (lr - lr_hi.astype(F32)).astype(BF16)
    lr_lane = lax.broadcasted_iota(jnp.int32, lr.shape, 1)
    lr_mix = jnp.where(jnp.logical_and(lr_lane >= r, lr_lane < 2 * r), lr_lo, lr_hi)
    wg = wg_ref[...]
    wg_hi = wg.astype(BF16)
    wg_lo = (wg - wg_hi.astype(F32)).astype(BF16)
    wg_mix = jnp.where(lax.broadcasted_iota(jnp.int32, wg.shape, 0) >= 2 * r, wg_lo, wg_hi)
    logits = jnp.dot(lr_mix, wg_mix, preferred_element_type=F32) + bg_ref[...]
    log_a = _log_sigmoid(logits) * (1.0 / GLA_GATE_TAU)

    hi = log_a.astype(BF16)
    lo = (log_a - hi.astype(F32)).astype(BF16)
    n_chunks = tb // c
    bcum_c = [jnp.dot(tri2, jnp.concatenate([hi[t * c:(t + 1) * c], lo[t * c:(t + 1) * c]],
                                            axis=0), preferred_element_type=F32)
              for t in range(n_chunks)]
    bcum = jnp.concatenate(bcum_c, axis=0)
    b_last = jnp.concatenate(
        [jnp.broadcast_to(b[c - 1:c, :], b.shape) for b in bcum_c], axis=0)
    kk = k_ref[...].astype(F32)
    qe_ref[...] = (q_ref[...].astype(F32) * (GLA_DK ** -0.5) * jnp.exp(bcum)).astype(BF16)
    ke_ref[...] = (kk * jnp.exp(-bcum)).astype(BF16)
    kdec_ref[...] = (kk * jnp.exp(b_last - bcum)).astype(BF16)
    decay_ref[...] = jnp.concatenate([jnp.exp(b[c - 1:c, :]) for b in bcum_c], axis=0)

    def chunk(ci, carry):
        r0 = pl.multiple_of(ci * c, c)
        rows = pl.ds(r0, c)
        q_e = qe_ref[rows, :]
        k_e = ke_ref[rows, :]
        k_dec = kdec_ref[rows, :]
        decay = decay_ref[pl.ds(ci, 1), :]
        heads = range(GLA_HEADS)
        pair = lambda x, h: x[:, (h // 2) * LANES:(h // 2 + 1) * LANES]
        hsl = lambda h: slice(h * GLA_DV, (h + 1) * GLA_DV)
        zero = jnp.zeros((), BF16)
        qm = [jnp.where(head_mask[h % 2], pair(q_e, h), zero) for h in heads]
        vb = [v_ref[rows, hsl(h)] for h in heads]
        s = [lax.dot_general(qm[h], pair(k_e, h), NT_DIMS, preferred_element_type=F32)
             for h in heads]
        st = [state_ref[h] for h in heads]
        o_inter = [lax.dot_general(qm[h], st[h].astype(BF16), NT_DIMS,
                                   preferred_element_type=F32) for h in heads]
        u_t = [lax.dot_general(vb[h], pair(k_dec, h), TN_DIMS, preferred_element_type=F32)
               for h in heads]
        for h in heads:
            state_ref[h] = pair(decay, h) * st[h] + u_t[h]
        o_intra = [jnp.dot(jnp.where(causal, s[h], 0.0).astype(BF16), vb[h],
                           preferred_element_type=F32) for h in heads]
        for h in heads:
            o = o_intra[h] + o_inter[h]
            y = o * _rms_scale(o) * gn_ref[...]
            g = gate_ref[rows, hsl(h)].astype(F32)
            y = y * (g * (1.0 / (1.0 + jnp.exp(-g))))
            o_ref[rows, hsl(h)] = y.astype(o_ref.dtype)
        return carry

    lax.fori_loop(0, tb // c, chunk, 0, unroll=GLA_CHUNKS_PER_TRIP)


def _gla(proj, a, aux, w_lr, wg, bg, gn, *, batch, seq, tb=512):
    m = proj.shape[0]
    d = a.shape[1]
    tb = min(tb, seq)
    nt = seq // tb
    hk = GLA_HEADS * GLA_DK
    hv = GLA_HEADS * GLA_DV
    row = lambda b, t: b * nt + t
    return pl.pallas_call(
        functools.partial(_gla_kernel, tb=tb),
        grid=(batch, nt),
        in_specs=[pl.BlockSpec((tb, hk), lambda b, t: (row(b, t), 0)),
                  pl.BlockSpec((tb, hk), lambda b, t: (row(b, t), 1)),
                  pl.BlockSpec((tb, hv), lambda b, t: (row(b, t), 1)),
                  pl.BlockSpec((tb, hv), lambda b, t: (row(b, t), 2)),
                  *_normed_specs(a, aux, tb, row),
                  pl.BlockSpec((d, LANES), lambda b, t: (0, 0)),
                  pl.BlockSpec((LANES, hk), lambda b, t: (0, 0)),
                  pl.BlockSpec((1, hk), lambda b, t: (0, 0)),
                  pl.BlockSpec((1, GLA_DV), lambda b, t: (0, 0))],
        out_specs=pl.BlockSpec((tb, hv), lambda b, t: (row(b, t), 0)),
        out_shape=jax.ShapeDtypeStruct((m, hv), BF16),
        scratch_shapes=[pltpu.VMEM((GLA_HEADS, GLA_DV, LANES), F32),
                        pltpu.VMEM((tb, hk), BF16), pltpu.VMEM((tb, hk), BF16),
                        pltpu.VMEM((tb, hk), BF16), pltpu.VMEM((tb // GLA_CHUNK, hk), F32)],
        compiler_params=_params("parallel", "arbitrary"),
        name="gla",
    )(proj, proj, proj, proj, a, aux, w_lr, wg, bg, gn)


def _sb_kernel(q_ref, k_ref, v_ref, w_ref, gq_ref, gk_ref, gn_ref, side_ref, o_ref, side_b_ref,
               qn_ref, kn_ref, *, tq, tk, seq):
    wmat = w_ref[...]
    lane_mean = jnp.full((SB_DH, SB_DH), 1.0 / SB_DH, BF16)
    nd = tq // tk
    rowi = lax.broadcasted_iota(jnp.int32, (tk, tk), 0)
    coli = lax.broadcasted_iota(jnp.int32, (tk, tk), 1)
    below_diag = jnp.concatenate([coli < rowi] * nd, axis=0)

    def normed(y, g_ref):
        ms = jnp.dot((y * y).astype(BF16), lane_mean, preferred_element_type=F32)
        return (y * (lax.rsqrt(ms + EPS) * g_ref[...])).astype(BF16)

    def head_norm(i, _):
        rows = pl.ds(pl.multiple_of(i * tq, tq), tq)
        qn_ref[rows, :] = normed(q_ref[rows, :].astype(F32), gq_ref)
        kn_ref[rows, :] = normed(k_ref[rows, :].astype(F32), gk_ref)
        return 0

    lax.fori_loop(0, seq // tq, head_norm, 0)

    group = lambda x, g: x[g * tk:(g + 1) * tk]

    def scores(q, q0, j):
        first = [q0 + (g - j) * tk for g in range(nd)]
        start = [pl.multiple_of(jnp.maximum(f, 0), tk) for f in first]
        z = jnp.concatenate(
            [lax.dot_general(group(q, g), kn_ref[pl.ds(start[g], tk), :], NT_DIMS,
                             preferred_element_type=F32) for g in range(nd)], axis=0)
        log_beta = jnp.minimum(z, 0.0) - jnp.log2(1.0 + jnp.exp2(-jnp.abs(z)))
        return first, start, log_beta, log_beta - z

    def suffix_sums(log_1m):
        return jnp.dot(log_1m.astype(BF16), wmat, preferred_element_type=F32)

    def retire(carry, first, j):
        keeps = lambda g: isinstance(j, int) and g >= j
        return jnp.concatenate(
            [group(carry, g) if keeps(g) else jnp.where(first[g] >= 0, group(carry, g), SB_DEAD)
             for g in range(nd)], axis=0)

    def weigh(a, start):
        a = a.astype(BF16)
        return jnp.concatenate(
            [jnp.dot(group(a, g), v_ref[pl.ds(start[g], tk), :], preferred_element_type=F32)
             for g in range(nd)], axis=0)

    def sweeps(q, q0, j, n, carry, acc, mask):
        sc = [scores(q, q0, j + t) for t in range(n)]
        s2 = []
        for t, (_, _, _, log_1m) in enumerate(sc):
            if t == 0 and mask is not None:
                log_1m = jnp.where(mask, log_1m, 0.0)
            s2.append(suffix_sums(log_1m))
        for t, (first, start, log_beta, _) in enumerate(sc):
            diagonal = t == 0 and mask is not None
            if not diagonal:
                carry = retire(carry, first, j + t)
            prefix = s2[t]
            carry = carry + jnp.broadcast_to(prefix[:, tk - 1:tk], prefix.shape)
            a = jnp.exp2(log_beta - prefix + carry)
            if diagonal:
                a = jnp.where(mask, a, 0.0)
            acc = acc + weigh(a, start)
        return carry, acc

    def live(c):
        return (jnp.max(c) > -SB_EXIT_LOG2).astype(jnp.int32)

    def qblock(i, _):
        q0 = pl.multiple_of(i * tq, tq)
        q = qn_ref[pl.ds(q0, tq), :]
        carry = jnp.zeros((tq, tk), F32)
        acc = jnp.zeros((tq, SB_DH), F32)
        carry, acc = sweeps(q, q0, 0, SB_PEELED_SWEEPS, carry, acc, below_diag)

        def cond(st):
            j, go, _, _ = st
            return jnp.logical_and(go > 0, j < (i + 1) * nd)

        def body(st):
            j, _, carry, acc = st
            carry, acc = sweeps(q, q0, j, SB_LOOP_SWEEPS, carry, acc, None)
            return j + SB_LOOP_SWEEPS, live(carry), carry, acc

        _, _, _, acc = lax.while_loop(cond, body, (SB_PEELED_SWEEPS, live(carry), carry, acc))
        o_ref[pl.ds(q0, tq), :] = (acc * _rms_scale(acc) * gn_ref[...]).astype(o_ref.dtype)
        return 0

    lax.fori_loop(0, seq // tq, qblock, 0)
    side_b_ref[...] = side_ref[...].astype(BF16)


def _sb(proj, gq, gk, gn, side, *, layer, col0, batch, seq, tq=4096, tk=128):
    m = proj.shape[0]
    tq = min(tq, seq)
    assert tq % tk == 0 and seq % tq == 0
    jj = lax.broadcasted_iota(jnp.int32, (tk, tk), 0)
    ss = lax.broadcasted_iota(jnp.int32, (tk, tk), 1)
    wmat = (jj <= ss).astype(BF16)
    vec = pl.BlockSpec((1, SB_DH), lambda b, h: (0, 0))
    rows, cols = side.shape[1] // (batch * SB_HEADS), side.shape[2]
    return pl.pallas_call(
        functools.partial(_sb_kernel, tq=tq, tk=tk, seq=seq),
        grid=(batch, SB_HEADS),
        in_specs=[pl.BlockSpec((seq, SB_DH), lambda b, h: (b, col0 + h)),
                  pl.BlockSpec((seq, SB_DH), lambda b, h: (b, col0 + SB_HEADS + h)),
                  pl.BlockSpec((seq, SB_DH), lambda b, h: (b, col0 + 2 * SB_HEADS + h)),
                  pl.BlockSpec((tk, tk), lambda b, h: (0, 0)),
                  vec, vec, vec,
                  pl.BlockSpec((None, rows, cols), lambda b, h: (layer, b * SB_HEADS + h, 0))],
        out_specs=[pl.BlockSpec((seq, SB_DH), lambda b, h: (b, h)),
                   pl.BlockSpec((rows, cols), lambda b, h: (b * SB_HEADS + h, 0))],
        out_shape=[jax.ShapeDtypeStruct((m, SB_HEADS * SB_DH), BF16),
                   jax.ShapeDtypeStruct(side.shape[1:], BF16)],
        scratch_shapes=[pltpu.VMEM((seq, SB_DH), BF16), pltpu.VMEM((seq, SB_DH), BF16)],
        compiler_params=_params("parallel", "parallel"),
        name="stickbreak",
    )(proj, proj, proj, wmat, gq, gk, gn, side)


def _layer(x, xb, ss, l, next_gain, w_in_b, w_in, w_gate_up, b_gate, gla_out_norm, sb_q_norm,
           sb_k_norm, sb_out_norm, w_o, mlp_norm, w_up, w_down, *, batch, seq):
    hk = GLA_HEADS * GLA_DK
    hv = GLA_HEADS * GLA_DV
    n_gla = 2 * hk + 2 * hv
    pad = LANES - 3 * GLA_GATE_RANK
    w_lr = jnp.pad(jnp.tile(w_in[l, :, n_gla:n_gla + GLA_GATE_RANK], (1, 3)), ((0, 0), (0, pad)))
    wg = jnp.pad(jnp.tile(w_gate_up[l], (3, 1)), ((0, pad), (0, 0)))
    gq = (sb_q_norm[l] * (math.log2(math.e) / math.sqrt(SB_DH))).reshape(1, SB_DH)

    p_gla = _matmul(_identity, xb, ss, w_in_b, n=n_gla, layer=l, out_dtype=BF16,
                    name="inproj_gla")
    p_sb = _matmul(_identity, xb, ss, w_in_b[l, :, n_gla + GLA_GATE_RANK:], out_dtype=BF16,
                   name="inproj_sb")
    o_gla = _gla(p_gla, xb, ss, w_lr, wg, b_gate[l].reshape(1, hk),
                 gla_out_norm[l].reshape(1, GLA_DV), batch=batch, seq=seq)
    o_sb, w_o_b = _sb(p_sb, gq, sb_k_norm[l].reshape(1, SB_DH), sb_out_norm[l].reshape(1, SB_DH),
                      w_o, layer=l, col0=0, batch=batch, seq=seq)
    x, xb, ss = _oproj(o_gla, o_sb, w_o_b, x, mlp_norm[l])
    up, w_down_b = _matmul(_relu2, xb, ss, w_up, layer=l, side=w_down, out_dtype=BF16,
                           name="mlp_up")
    if next_gain is None:
        return _down(up, w_down_b, x, None, layer=l), None, None
    return _down(up, w_down_b, x, next_gain, layer=l)


def kernel(x, attn_norm, w_in, w_gate_up, b_gate, gla_out_norm, sb_q_norm, sb_k_norm,
           sb_out_norm, w_o, mlp_norm, w_up, w_down):
    batch, seq, d = x.shape
    depth = w_in.shape[0]
    y = x.reshape(batch * seq, d)
    w_in_b = w_in.astype(BF16)
    yb, ss = y, attn_norm[0].reshape(1, d)
    for l in range(depth):
        next_gain = attn_norm[l + 1] if l + 1 < depth else None
        y, yb, ss = _layer(y, yb, ss, l, next_gain, w_in_b, w_in, w_gate_up, b_gate, gla_out_norm,
                           sb_q_norm, sb_k_norm, sb_out_norm, w_o, mlp_norm, w_up, w_down,
                           batch=batch, seq=seq)
    return y.reshape(batch, seq, d)
```

```python
import functools
import math

import jax
import jax.numpy as jnp
from jax import lax
from jax.experimental import pallas as pl
from jax.experimental.pallas import tpu as pltpu

EPS = 1e-6
LANES = 128

GLA_HEADS = 8
GLA_DK = 64
GLA_DV = 128
GLA_GATE_RANK = 16
GLA_GATE_TAU = 16.0
GLA_CHUNK = 64
GLA_CHUNKS_PER_TRIP = 4
SB_HEADS = 8
SB_DH = 128

SB_EXIT_LOG2 = 126.0
SB_DEAD = -1e30

V7X_VMEM_BYTES = 64 * 1024 * 1024
VMEM_LIMIT = V7X_VMEM_BYTES * 7 // 8
DOWN_EPILOGUE_CHUNKS = 4

F32 = jnp.float32
BF16 = jnp.bfloat16
NT_DIMS = (((1,), (1,)), ((), ()))
TN_DIMS = (((0,), (0,)), ((), ()))


def _params(*sem):
    return pltpu.CompilerParams(dimension_semantics=sem, vmem_limit_bytes=VMEM_LIMIT)


def _log_sigmoid(z):
    return jnp.minimum(z, 0.0) - jnp.log(1.0 + jnp.exp(-jnp.abs(z)))


def _rms_scale(y):
    return lax.rsqrt(jnp.mean(y * y, axis=-1, keepdims=True) + EPS)


def _sumsq(y):
    return jnp.broadcast_to(jnp.sum(y * y, axis=-1, keepdims=True), (y.shape[0], LANES))


def _normed_operand(a_ref, aux_ref):
    d = a_ref.shape[1]
    if a_ref.dtype == BF16:
        return a_ref[...], lax.rsqrt(jnp.sum(aux_ref[...], axis=0) * (1.0 / d) + EPS)
    x = a_ref[...]
    return (x * aux_ref[...]).astype(BF16), lax.rsqrt(_sumsq(x) * (1.0 / d) + EPS)


def _normed_specs(a, aux, rows, row_map):
    a_spec = pl.BlockSpec((rows, a.shape[1]), lambda *g: (row_map(*g), 0))
    if a.dtype == BF16:
        return a_spec, pl.BlockSpec((aux.shape[0], rows, LANES), lambda *g: (0, row_map(*g), 0))
    return a_spec, pl.BlockSpec((1, a.shape[1]), lambda *g: (0, 0))


def _emit_normed(y, g_ref, o_ref, xb_ref, ss_ref):
    o_ref[...] = y
    xb_ref[...] = (y * g_ref[...]).astype(BF16)
    ss_ref[...] = _sumsq(y)


def _w_spec(w, rows, cols, index_map, layer):
    if w.ndim == 2:
        return pl.BlockSpec((rows, cols), index_map)
    return pl.BlockSpec((None, rows, cols), lambda *g: (layer,) + tuple(index_map(*g)))


def _mm_kernel(a_ref, aux_ref, w_ref, *rest, post):
    o_ref = rest[-1] if len(rest) == 1 else rest[1]
    xb, r = _normed_operand(a_ref, aux_ref)
    acc = jnp.dot(xb, w_ref[...].astype(BF16), preferred_element_type=F32)
    for c in range(acc.shape[1] // LANES):
        sl = slice(c * LANES, (c + 1) * LANES)
        o_ref[:, sl] = post(acc[:, sl] * r).astype(o_ref.dtype)
    if len(rest) == 3:
        rest[2][...] = rest[0][...].astype(BF16)


def _identity(y):
    return y


def _relu2(y):
    return jnp.square(jnp.maximum(y, 0.0))


def _matmul(post, a, aux, w, *, n=None, layer=0, side=None, out_dtype, tm=1024, tn=1024, name):
    m, k = a.shape
    n = w.shape[-1] if n is None else n
    tm = min(tm, m)
    tn = min(tn, n)
    nj, ni = n // tn, m // tm
    in_specs = [*_normed_specs(a, aux, tm, lambda j, i: i),
                _w_spec(w, k, tn, lambda j, i: (0, j), layer)]
    out_specs = pl.BlockSpec((tm, tn), lambda j, i: (i, j))
    out_shape = jax.ShapeDtypeStruct((m, n), out_dtype)
    args = [a, aux, w]
    if side is not None:
        rows, cols = side.shape[1] // (nj * ni), side.shape[2]
        in_specs.append(pl.BlockSpec((None, rows, cols), lambda j, i: (layer, j * ni + i, 0)))
        out_specs = [out_specs, pl.BlockSpec((rows, cols), lambda j, i: (j * ni + i, 0))]
        out_shape = [out_shape, jax.ShapeDtypeStruct(side.shape[1:], BF16)]
        args.append(side)
    return pl.pallas_call(
        functools.partial(_mm_kernel, post=post),
        grid=(nj, ni),
        in_specs=in_specs,
        out_specs=out_specs,
        out_shape=out_shape,
        compiler_params=_params("parallel", "arbitrary"),
        name=name,
    )(*args)


def _normed_out(m, n, tm, tn, tile_map, slot_map):
    specs = [pl.BlockSpec((tm, tn), tile_map), pl.BlockSpec((tm, tn), tile_map),
             pl.BlockSpec((None, tm, LANES), slot_map)]
    shapes = [jax.ShapeDtypeStruct((m, n), F32), jax.ShapeDtypeStruct((m, n), BF16),
              jax.ShapeDtypeStruct((n // tn, m, LANES), F32)]
    return specs, shapes


def _oproj_kernel(a1_ref, a2_ref, w1_ref, w2_ref, x_ref, g_ref, o_ref, xb_ref, ss_ref):
    acc = jnp.dot(a1_ref[...], w1_ref[...], preferred_element_type=F32)
    acc += jnp.dot(a2_ref[...], w2_ref[...], preferred_element_type=F32)
    _emit_normed(x_ref[...] + acc, g_ref, o_ref, xb_ref, ss_ref)


def _oproj(a1, a2, w, x, gain, *, tm=512):
    m, kh = a1.shape
    n = w.shape[-1]
    tm = min(tm, m)
    out_specs, out_shape = _normed_out(m, n, tm, n, lambda i: (i, 0), lambda i: (0, i, 0))
    return pl.pallas_call(
        _oproj_kernel,
        grid=(m // tm,),
        in_specs=[pl.BlockSpec((tm, kh), lambda i: (i, 0)),
                  pl.BlockSpec((tm, kh), lambda i: (i, 0)),
                  pl.BlockSpec((kh, n), lambda i: (0, 0)),
                  pl.BlockSpec((kh, n), lambda i: (1, 0)),
                  pl.BlockSpec((tm, n), lambda i: (i, 0)),
                  pl.BlockSpec((1, n), lambda i: (0, 0))],
        out_specs=out_specs,
        out_shape=out_shape,
        compiler_params=_params("parallel"),
        name="oproj",
    )(a1, a2, w, w, x, gain.reshape(1, n))


def _down_kernel(a_ref, w_ref, x_ref, *rest, normed):
    acc_ref = rest[-1]
    kk = pl.program_id(2)
    last = pl.num_programs(2) - 1

    @pl.when(kk == 0)
    def _():
        acc_ref[...] = jnp.zeros_like(acc_ref)

    @pl.when(kk < last)
    def _():
        acc_ref[...] += jnp.dot(a_ref[...], w_ref[...].astype(BF16),
                                preferred_element_type=F32)

    @pl.when(kk == last)
    def _():
        wb = w_ref[...].astype(BF16)
        tm = acc_ref.shape[0]
        rb = tm // DOWN_EPILOGUE_CHUNKS
        for c in range(DOWN_EPILOGUE_CHUNKS):
            rows = slice(c * rb, (c + 1) * rb)
            y = x_ref[rows, :] + (acc_ref[rows, :] + jnp.dot(
                a_ref[rows, :], wb, preferred_element_type=F32))
            if normed:
                g_ref, o_ref, xb_ref, ss_ref = rest[:-1]
                o_ref[rows, :] = y
                xb_ref[rows, :] = (y * g_ref[...]).astype(BF16)
                ss_ref[rows, :] = _sumsq(y)
            else:
                rest[0][rows, :] = y


def _down(a, w, x, gain, *, layer, tm=1024, tn=1024, tk=2048):
    m, k = a.shape
    n = w.shape[-1]
    tm = min(tm, m)
    tile_map = lambda i, j, kk: (i, j)
    in_specs = [pl.BlockSpec((tm, tk), lambda i, j, kk: (i, kk)),
                _w_spec(w, tk, tn, lambda i, j, kk: (kk, j), layer),
                pl.BlockSpec((tm, tn), tile_map)]
    args = [a, w, x]
    if gain is None:
        out_specs = pl.BlockSpec((tm, tn), tile_map)
        out_shape = jax.ShapeDtypeStruct((m, n), F32)
    else:
        in_specs.append(pl.BlockSpec((1, tn), lambda i, j, kk: (0, j)))
        args.append(gain.reshape(1, n))
        out_specs, out_shape = _normed_out(m, n, tm, tn, tile_map, lambda i, j, kk: (j, i, 0))
    return pl.pallas_call(
        functools.partial(_down_kernel, normed=gain is not None),
        grid=(m // tm, n // tn, k // tk),
        in_specs=in_specs,
        out_specs=out_specs,
        out_shape=out_shape,
        scratch_shapes=[pltpu.VMEM((tm, tn), F32)],
        compiler_params=_params("parallel", "arbitrary", "arbitrary"),
        name="down",
    )(*args)


def _gla_kernel(q_ref, k_ref, v_ref, gate_ref, a_ref, aux_ref, wlr_ref, wg_ref, bg_ref, gn_ref,
                o_ref, state_ref, qe_ref, ke_ref, kdec_ref, decay_ref, *, tb):
    @pl.when(pl.program_id(1) == 0)
    def _():
        state_ref[...] = jnp.zeros_like(state_ref)

    c = GLA_CHUNK
    row = lax.broadcasted_iota(jnp.int32, (c, c), 0)
    col = lax.broadcasted_iota(jnp.int32, (c, c), 1)
    causal = col <= row
    tri = jnp.where(causal, 1.0, 0.0).astype(BF16)
    tri2 = jnp.concatenate([tri, tri], axis=1)
    lane = lax.broadcasted_iota(jnp.int32, (c, LANES), 1)
    head_mask = (lane < GLA_DK, lane >= GLA_DK)

    r = GLA_GATE_RANK
    xb, x_scale = _normed_operand(a_ref, aux_ref)
    lr = x_scale * jnp.dot(xb, wlr_ref[...].astype(BF16), preferred_element_type=F32)
    lr_hi = lr.astype(BF16)
    lr_lo = (lr - lr_hi.astype(F32)).astype(BF16)
    lr_lane = lax.broadcasted_iota(jnp.int32, lr.shape, 1)
    lr_mix = jnp.where(jnp.logical_and(lr_lane >= r, lr_lane < 2 * r), lr_lo, lr_hi)
    wg = wg_ref[...]
    wg_hi = wg.astype(BF16)
    wg_lo = (wg - wg_hi.astype(F32)).astype(BF16)
    wg_mix = jnp.where(lax.broadcasted_iota(jnp.int32, wg.shape, 0) >= 2 * r, wg_lo, wg_hi)
    logits = jnp.dot(lr_mix, wg_mix, preferred_element_type=F32) + bg_ref[...]
    log_a = _log_sigmoid(logits) * (1.0 / GLA_GATE_TAU)

    hi = log_a.astype(BF16)
    lo = (log_a - hi.astype(F32)).astype(BF16)
    n_chunks = tb // c
    bcum_c = [jnp.dot(tri2, jnp.concatenate([hi[t * c:(t + 1) * c], lo[t * c:(t + 1) * c]],
                                            axis=0), preferred_element_type=F32)
              for t in range(n_chunks)]
    bcum = jnp.concatenate(bcum_c, axis=0)
    b_last = jnp.concatenate(
        [jnp.broadcast_to(b[c - 1:c, :], b.shape) for b in bcum_c], axis=0)
    kk = k_ref[...].astype(F32)
    qe_ref[...] = (q_ref[...].astype(F32) * (GLA_DK ** -0.5) * jnp.exp(bcum)).astype(BF16)
    ke_ref[...] = (kk * jnp.exp(-bcum)).astype(BF16)
    kdec_ref[...] = (kk * jnp.exp(b_last - bcum)).astype(BF16)
    decay_ref[...] = jnp.concatenate([jnp.exp(b[c - 1:c, :]) for b in bcum_c], axis=0)

    def chunk(ci, carry):
        r0 = pl.multiple_of(ci * c, c)
        rows = pl.ds(r0, c)
        q_e = qe_ref[rows, :]
        k_e = ke_ref[rows, :]
        k_dec = kdec_ref[rows, :]
        decay = decay_ref[pl.ds(ci, 1), :]
        heads = range(GLA_HEADS)
        pair = lambda x, h: x[:, (h // 2) * LANES:(h // 2 + 1) * LANES]
        hsl = lambda h: slice(h * GLA_DV, (h + 1) * GLA_DV)
        zero = jnp.zeros((), BF16)
        qm = [jnp.where(head_mask[h % 2], pair(q_e, h), zero) for h in heads]
        vb = [v_ref[rows, hsl(h)] for h in heads]
        s = [lax.dot_general(qm[h], pair(k_e, h), NT_DIMS, preferred_element_type=F32)
             for h in heads]
        st = [state_ref[h] for h in heads]
        o_inter = [lax.dot_general(qm[h], st[h].astype(BF16), NT_DIMS,
                                   preferred_element_type=F32) for h in heads]
        u_t = [lax.dot_general(vb[h], pair(k_dec, h), TN_DIMS, preferred_element_type=F32)
               for h in heads]
        for h in heads:
            state_ref[h] = pair(decay, h) * st[h] + u_t[h]
        o_intra = [jnp.dot(jnp.where(causal, s[h], 0.0).astype(BF16), vb[h],
                           preferred_element_type=F32) for h in heads]
        for h in heads:
            o = o_intra[h] + o_inter[h]
            y = o * _rms_scale(o) * gn_ref[...]
            g = gate_ref[rows, hsl(h)].astype(F32)
            y = y * (g * (1.0 / (1.0 + jnp.exp(-g))))
            o_ref[rows, hsl(h)] = y.astype(o_ref.dtype)
        return carry

    lax.fori_loop(0, tb // c, chunk, 0, unroll=GLA_CHUNKS_PER_TRIP)


def _gla(proj, a, aux, w_lr, wg, bg, gn, *, batch, seq, tb=512):
    m = proj.shape[0]
    d = a.shape[1]
    tb = min(tb, seq)
    nt = seq // tb
    hk = GLA_HEADS * GLA_DK
    hv = GLA_HEADS * GLA_DV
    row = lambda b, t: b * nt + t
    return pl.pallas_call(
        functools.partial(_gla_kernel, tb=tb),
        grid=(batch, nt),
        in_specs=[pl.BlockSpec((tb, hk), lambda b, t: (row(b, t), 0)),
                  pl.BlockSpec((tb, hk), lambda b, t: (row(b, t), 1)),
                  pl.BlockSpec((tb, hv), lambda b, t: (row(b, t), 1)),
                  pl.BlockSpec((tb, hv), lambda b, t: (row(b, t), 2)),
                  *_normed_specs(a, aux, tb, row),
                  pl.BlockSpec((d, LANES), lambda b, t: (0, 0)),
                  pl.BlockSpec((LANES, hk), lambda b, t: (0, 0)),
                  pl.BlockSpec((1, hk), lambda b, t: (0, 0)),
                  pl.BlockSpec((1, GLA_DV), lambda b, t: (0, 0))],
        out_specs=pl.BlockSpec((tb, hv), lambda b, t: (row(b, t), 0)),
        out_shape=jax.ShapeDtypeStruct((m, hv), BF16),
        scratch_shapes=[pltpu.VMEM((GLA_HEADS, GLA_DV, LANES), F32),
                        pltpu.VMEM((tb, hk), BF16), pltpu.VMEM((tb, hk), BF16),
                        pltpu.VMEM((tb, hk), BF16), pltpu.VMEM((tb // GLA_CHUNK, hk), F32)],
        compiler_params=_params("parallel", "arbitrary"),
        name="gla",
    )(proj, proj, proj, proj, a, aux, w_lr, wg, bg, gn)


def _sb_kernel(q_ref, k_ref, v_ref, w_ref, gq_ref, gk_ref, gn_ref, side_ref, o_ref, side_b_ref,
               qn_ref, kn_ref, *, tk, seq):
    wmat2 = w_ref[...]
    wmat = wmat2[:tk, :tk]
    lane_mean = jnp.full((SB_DH, SB_DH), 1.0 / SB_DH, BF16)
    nd = seq // tk
    rowi = lax.broadcasted_iota(jnp.int32, (tk, tk), 0)
    coli = lax.broadcasted_iota(jnp.int32, (tk, tk), 1)
    below_diag = coli < rowi

    def normed(y, g_ref):
        ms = jnp.dot((y * y).astype(BF16), lane_mean, preferred_element_type=F32)
        return (y * (lax.rsqrt(ms + EPS) * g_ref[...])).astype(BF16)

    qn_ref[...] = normed(q_ref[...].astype(F32), gq_ref)
    kn_ref[...] = normed(k_ref[...].astype(F32), gk_ref)

    def log_betas(z):
        log_beta = jnp.minimum(z, 0.0) - jnp.log2(1.0 + jnp.exp2(-jnp.abs(z)))
        return log_beta, log_beta - z

    def last_column(prefix, width):
        return jnp.broadcast_to(prefix[:, width - 1:width], (prefix.shape[0], tk))

    q = qn_ref[...]
    log_beta, log_1m = log_betas(lax.dot_general(q[:tk], kn_ref[:tk, :], NT_DIMS,
                                                 preferred_element_type=F32))
    prefix = jnp.dot(jnp.where(below_diag, log_1m, 0.0).astype(BF16), wmat,
                     preferred_element_type=F32)
    a = jnp.where(below_diag, jnp.exp2(log_beta - prefix + last_column(prefix, tk)), 0.0)
    acc0 = jnp.dot(a.astype(BF16), v_ref[:tk, :], preferred_element_type=F32)
    window = lambda ref, g: ref[(g - 1) * tk:(g + 1) * tk, :]
    z = jnp.concatenate(
        [lax.dot_general(q[g * tk:(g + 1) * tk], window(kn_ref, g), NT_DIMS,
                         preferred_element_type=F32) for g in range(1, nd)], axis=0)
    visible = jnp.concatenate(
        [jnp.concatenate([jnp.ones((tk, tk), jnp.bool_), below_diag], axis=1)] * (nd - 1), axis=0)
    log_beta, log_1m = log_betas(z)
    prefix = jnp.dot(jnp.where(visible, log_1m, 0.0).astype(BF16), wmat2,
                     preferred_element_type=F32)
    carry_diag = last_column(prefix, 2 * tk)
    carry = carry_diag + last_column(prefix, tk)
    a = jnp.exp2(log_beta - prefix + jnp.concatenate([carry, carry_diag], axis=1))
    a = jnp.where(visible, a, 0.0).astype(BF16)
    acc = jnp.concatenate(
        [acc0] + [jnp.dot(a[(g - 1) * tk:g * tk], window(v_ref, g), preferred_element_type=F32)
                  for g in range(1, nd)], axis=0)
    carry = jnp.concatenate([jnp.full((tk, tk), SB_DEAD, F32), carry], axis=0)

    group = lambda x, g: x[g * tk:(g + 1) * tk]

    def sweep(j, carry, acc):
        first = [(g - j) * tk for g in range(nd)]
        start = [pl.multiple_of(jnp.maximum(f, 0), tk) for f in first]
        z = jnp.concatenate(
            [lax.dot_general(group(q, g), kn_ref[pl.ds(start[g], tk), :], NT_DIMS,
                             preferred_element_type=F32) for g in range(nd)], axis=0)
        log_beta, log_1m = log_betas(z)
        prefix = jnp.dot(log_1m.astype(BF16), wmat, preferred_element_type=F32)
        carry = jnp.concatenate(
            [jnp.where(first[g] >= 0, group(carry, g), SB_DEAD) for g in range(nd)], axis=0)
        carry = carry + last_column(prefix, tk)
        a = jnp.exp2(log_beta - prefix + carry).astype(BF16)
        pv = jnp.concatenate(
            [jnp.dot(group(a, g), v_ref[pl.ds(start[g], tk), :], preferred_element_type=F32)
             for g in range(nd)], axis=0)
        return carry, acc + pv

    def live(c):
        return (jnp.max(c) > -SB_EXIT_LOG2).astype(jnp.int32)

    def cond(st):
        j, go, _, _ = st
        return jnp.logical_and(go > 0, j < nd)

    def body(st):
        j, _, carry, acc = st
        carry, acc = sweep(j, carry, acc)
        return j + 1, live(carry), carry, acc

    _, _, _, acc = lax.while_loop(cond, body, (2, live(carry), carry, acc))
    o_ref[...] = (acc * _rms_scale(acc) * gn_ref[...]).astype(o_ref.dtype)
    side_b_ref[...] = side_ref[...].astype(BF16)


def _sb(proj, gq, gk, gn, side, *, layer, col0, batch, seq, tk=128):
    m = proj.shape[0]
    assert seq % tk == 0 and seq >= 2 * tk
    jj = lax.broadcasted_iota(jnp.int32, (2 * tk, 2 * tk), 0)
    ss = lax.broadcasted_iota(jnp.int32, (2 * tk, 2 * tk), 1)
    wmat = jnp.logical_and(jj <= ss, jj // tk == ss // tk).astype(BF16)
    vec = pl.BlockSpec((1, SB_DH), lambda b, h: (0, 0))
    rows, cols = side.shape[1] // (batch * SB_HEADS), side.shape[2]
    return pl.pallas_call(
        functools.partial(_sb_kernel, tk=tk, seq=seq),
        grid=(batch, SB_HEADS),
        in_specs=[pl.BlockSpec((seq, SB_DH), lambda b, h: (b, col0 + h)),
                  pl.BlockSpec((seq, SB_DH), lambda b, h: (b, col0 + SB_HEADS + h)),
                  pl.BlockSpec((seq, SB_DH), lambda b, h: (b, col0 + 2 * SB_HEADS + h)),
                  pl.BlockSpec((2 * tk, 2 * tk), lambda b, h: (0, 0)),
                  vec, vec, vec,
                  pl.BlockSpec((None, rows, cols), lambda b, h: (layer, b * SB_HEADS + h, 0))],
        out_specs=[pl.BlockSpec((seq, SB_DH), lambda b, h: (b, h)),
                   pl.BlockSpec((rows, cols), lambda b, h: (b * SB_HEADS + h, 0))],
        out_shape=[jax.ShapeDtypeStruct((m, SB_HEADS * SB_DH), BF16),
                   jax.ShapeDtypeStruct(side.shape[1:], BF16)],
        scratch_shapes=[pltpu.VMEM((seq, SB_DH), BF16), pltpu.VMEM((seq, SB_DH), BF16)],
        compiler_params=_params("parallel", "parallel"),
        name="stickbreak",
    )(proj, proj, proj, wmat, gq, gk, gn, side)


def _layer(x, xb, ss, l, next_gain, w_in_b, w_in, w_gate_up, b_gate, gla_out_norm, sb_q_norm,
           sb_k_norm, sb_out_norm, w_o, mlp_norm, w_up, w_down, *, batch, seq):
    hk = GLA_HEADS * GLA_DK
    hv = GLA_HEADS * GLA_DV
    n_gla = 2 * hk + 2 * hv
    pad = LANES - 3 * GLA_GATE_RANK
    w_lr = jnp.pad(jnp.tile(w_in[l, :, n_gla:n_gla + GLA_GATE_RANK], (1, 3)), ((0, 0), (0, pad)))
    wg = jnp.pad(jnp.tile(w_gate_up[l], (3, 1)), ((0, pad), (0, 0)))
    gq = (sb_q_norm[l] * (math.log2(math.e) / math.sqrt(SB_DH))).reshape(1, SB_DH)

    p_gla = _matmul(_identity, xb, ss, w_in_b, n=n_gla, layer=l, out_dtype=BF16,
                    name="inproj_gla")
    p_sb = _matmul(_identity, xb, ss, w_in_b[l, :, n_gla + GLA_GATE_RANK:], out_dtype=BF16,
                   name="inproj_sb")
    o_gla = _gla(p_gla, xb, ss, w_lr, wg, b_gate[l].reshape(1, hk),
                 gla_out_norm[l].reshape(1, GLA_DV), batch=batch, seq=seq)
    o_sb, w_o_b = _sb(p_sb, gq, sb_k_norm[l].reshape(1, SB_DH), sb_out_norm[l].reshape(1, SB_DH),
                      w_o, layer=l, col0=0, batch=batch, seq=seq)
    x, xb, ss = _oproj(o_gla, o_sb, w_o_b, x, mlp_norm[l])
    up, w_down_b = _matmul(_relu2, xb, ss, w_up, layer=l, side=w_down, out_dtype=BF16,
                           name="mlp_up")
    if next_gain is None:
        return _down(up, w_down_b, x, None, layer=l), None, None
    return _down(up, w_down_b, x, next_gain, layer=l)


def kernel(x, attn_norm, w_in, w_gate_up, b_gate, gla_out_norm, sb_q_norm, sb_k_norm,
           sb_out_norm, w_o, mlp_norm, w_up, w_down):
    batch, seq, d = x.shape
    depth = w_in.shape[0]
    y = x.reshape(batch * seq, d)
    w_in_b = w_in.astype(BF16)
    yb, ss = y, attn_norm[0].reshape(1, d)
    for l in range(depth):
        next_gain = attn_norm[l + 1] if l + 1 < depth else None
        y, yb, ss = _layer(y, yb, ss, l, next_gain, w_in_b, w_in, w_gate_up, b_gate, gla_out_norm,
                           sb_q_norm, sb_k_norm, sb_out_norm, w_o, mlp_norm, w_up, w_down,
                           batch=batch, seq=seq)
    return y.reshape(batch, seq, d)
```

```python
import functools
import math

import jax
import jax.numpy as jnp
from jax import lax
from jax.experimental import pallas as pl
from jax.experimental.pallas import tpu as pltpu

EPS = 1e-6
LANES = 128

GLA_HEADS = 8
GLA_DK = 64
GLA_DV = 128
GLA_GATE_RANK = 16
GLA_GATE_TAU = 16.0
GLA_CHUNK = 64
GLA_CHUNKS_PER_TRIP = 8
SB_HEADS = 8
SB_DH = 128

SB_EXIT_LOG2 = 126.0
SB_DEAD = -1e30

V7X_VMEM_BYTES = 64 * 1024 * 1024
VMEM_LIMIT = V7X_VMEM_BYTES * 7 // 8
DOWN_EPILOGUE_CHUNKS = 4

F32 = jnp.float32
BF16 = jnp.bfloat16
NT_DIMS = (((1,), (1,)), ((), ()))
TN_DIMS = (((0,), (0,)), ((), ()))


def _params(*sem):
    return pltpu.CompilerParams(dimension_semantics=sem, vmem_limit_bytes=VMEM_LIMIT)


def _log_sigmoid(z):
    return jnp.minimum(z, 0.0) - jnp.log(1.0 + jnp.exp(-jnp.abs(z)))


def _rms_scale(y):
    return lax.rsqrt(jnp.mean(y * y, axis=-1, keepdims=True) + EPS)


def _sumsq(y):
    return jnp.broadcast_to(jnp.sum(y * y, axis=-1, keepdims=True), (y.shape[0], LANES))


def _normed_operand(a_ref, aux_ref):
    d = a_ref.shape[1]
    if a_ref.dtype == BF16:
        return a_ref[...], lax.rsqrt(jnp.sum(aux_ref[...], axis=0) * (1.0 / d) + EPS)
    x = a_ref[...]
    return (x * aux_ref[...]).astype(BF16), lax.rsqrt(_sumsq(x) * (1.0 / d) + EPS)


def _normed_specs(a, aux, rows, row_map):
    a_spec = pl.BlockSpec((rows, a.shape[1]), lambda *g: (row_map(*g), 0))
    if a.dtype == BF16:
        return a_spec, pl.BlockSpec((aux.shape[0], rows, LANES), lambda *g: (0, row_map(*g), 0))
    return a_spec, pl.BlockSpec((1, a.shape[1]), lambda *g: (0, 0))


def _emit_normed(y, g_ref, o_ref, xb_ref, ss_ref):
    o_ref[...] = y
    xb_ref[...] = (y * g_ref[...]).astype(BF16)
    ss_ref[...] = _sumsq(y)


def _w_spec(w, rows, cols, index_map, layer):
    if w.ndim == 2:
        return pl.BlockSpec((rows, cols), index_map)
    return pl.BlockSpec((None, rows, cols), lambda *g: (layer,) + tuple(index_map(*g)))


def _mm_kernel(a_ref, aux_ref, w_ref, *rest, post):
    o_ref = rest[-1] if len(rest) == 1 else rest[1]
    xb, r = _normed_operand(a_ref, aux_ref)
    acc = jnp.dot(xb, w_ref[...].astype(BF16), preferred_element_type=F32)
    for c in range(acc.shape[1] // LANES):
        sl = slice(c * LANES, (c + 1) * LANES)
        o_ref[:, sl] = post(acc[:, sl] * r).astype(o_ref.dtype)
    if len(rest) == 3:
        rest[2][...] = rest[0][...].astype(BF16)


def _identity(y):
    return y


def _relu2(y):
    return jnp.square(jnp.maximum(y, 0.0))


def _matmul(post, a, aux, w, *, n=None, layer=0, side=None, out_dtype, tm=1024, tn=1024, name):
    m, k = a.shape
    n = w.shape[-1] if n is None else n
    tm = min(tm, m)
    tn = min(tn, n)
    nj, ni = n // tn, m // tm
    in_specs = [*_normed_specs(a, aux, tm, lambda j, i: i),
                _w_spec(w, k, tn, lambda j, i: (0, j), layer)]
    out_specs = pl.BlockSpec((tm, tn), lambda j, i: (i, j))
    out_shape = jax.ShapeDtypeStruct((m, n), out_dtype)
    args = [a, aux, w]
    if side is not None:
        rows, cols = side.shape[1] // (nj * ni), side.shape[2]
        in_specs.append(pl.BlockSpec((None, rows, cols), lambda j, i: (layer, j * ni + i, 0)))
        out_specs = [out_specs, pl.BlockSpec((rows, cols), lambda j, i: (j * ni + i, 0))]
        out_shape = [out_shape, jax.ShapeDtypeStruct(side.shape[1:], BF16)]
        args.append(side)
    return pl.pallas_call(
        functools.partial(_mm_kernel, post=post),
        grid=(nj, ni),
        in_specs=in_specs,
        out_specs=out_specs,
        out_shape=out_shape,
        compiler_params=_params("parallel", "arbitrary"),
        name=name,
    )(*args)


def _normed_out(m, n, tm, tn, tile_map, slot_map):
    specs = [pl.BlockSpec((tm, tn), tile_map), pl.BlockSpec((tm, tn), tile_map),
             pl.BlockSpec((None, tm, LANES), slot_map)]
    shapes = [jax.ShapeDtypeStruct((m, n), F32), jax.ShapeDtypeStruct((m, n), BF16),
              jax.ShapeDtypeStruct((n // tn, m, LANES), F32)]
    return specs, shapes


def _oproj_kernel(a1_ref, a2_ref, w1_ref, w2_ref, x_ref, g_ref, o_ref, xb_ref, ss_ref):
    acc = jnp.dot(a1_ref[...], w1_ref[...], preferred_element_type=F32)
    acc += jnp.dot(a2_ref[...], w2_ref[...], preferred_element_type=F32)
    _emit_normed(x_ref[...] + acc, g_ref, o_ref, xb_ref, ss_ref)


def _oproj(a1, a2, w, x, gain, *, tm=512):
    m, kh = a1.shape
    n = w.shape[-1]
    tm = min(tm, m)
    out_specs, out_shape = _normed_out(m, n, tm, n, lambda i: (i, 0), lambda i: (0, i, 0))
    return pl.pallas_call(
        _oproj_kernel,
        grid=(m // tm,),
        in_specs=[pl.BlockSpec((tm, kh), lambda i: (i, 0)),
                  pl.BlockSpec((tm, kh), lambda i: (i, 0)),
                  pl.BlockSpec((kh, n), lambda i: (0, 0)),
                  pl.BlockSpec((kh, n), lambda i: (1, 0)),
                  pl.BlockSpec((tm, n), lambda i: (i, 0)),
                  pl.BlockSpec((1, n), lambda i: (0, 0))],
        out_specs=out_specs,
        out_shape=out_shape,
        compiler_params=_params("parallel"),
        name="oproj",
    )(a1, a2, w, w, x, gain.reshape(1, n))


def _down_kernel(a_ref, w_ref, x_ref, *rest, normed):
    acc_ref = rest[-1]
    kk = pl.program_id(2)
    last = pl.num_programs(2) - 1

    @pl.when(kk == 0)
    def _():
        acc_ref[...] = jnp.zeros_like(acc_ref)

    @pl.when(kk < last)
    def _():
        acc_ref[...] += jnp.dot(a_ref[...], w_ref[...].astype(BF16),
                                preferred_element_type=F32)

    @pl.when(kk == last)
    def _():
        wb = w_ref[...].astype(BF16)
        tm = acc_ref.shape[0]
        rb = tm // DOWN_EPILOGUE_CHUNKS
        for c in range(DOWN_EPILOGUE_CHUNKS):
            rows = slice(c * rb, (c + 1) * rb)
            y = x_ref[rows, :] + (acc_ref[rows, :] + jnp.dot(
                a_ref[rows, :], wb, preferred_element_type=F32))
            if normed:
                g_ref, o_ref, xb_ref, ss_ref = rest[:-1]
                o_ref[rows, :] = y
                xb_ref[rows, :] = (y * g_ref[...]).astype(BF16)
                ss_ref[rows, :] = _sumsq(y)
            else:
                rest[0][rows, :] = y


def _down(a, w, x, gain, *, layer, tm=1024, tn=1024, tk=2048):
    m, k = a.shape
    n = w.shape[-1]
    tm = min(tm, m)
    tile_map = lambda i, j, kk: (i, j)
    in_specs = [pl.BlockSpec((tm, tk), lambda i, j, kk: (i, kk)),
                _w_spec(w, tk, tn, lambda i, j, kk: (kk, j), layer),
                pl.BlockSpec((tm, tn), tile_map)]
    args = [a, w, x]
    if gain is None:
        out_specs = pl.BlockSpec((tm, tn), tile_map)
        out_shape = jax.ShapeDtypeStruct((m, n), F32)
    else:
        in_specs.append(pl.BlockSpec((1, tn), lambda i, j, kk: (0, j)))
        args.append(gain.reshape(1, n))
        out_specs, out_shape = _normed_out(m, n, tm, tn, tile_map, lambda i, j, kk: (j, i, 0))
    return pl.pallas_call(
        functools.partial(_down_kernel, normed=gain is not None),
        grid=(m // tm, n // tn, k // tk),
        in_specs=in_specs,
        out_specs=out_specs,
        out_shape=out_shape,
        scratch_shapes=[pltpu.VMEM((tm, tn), F32)],
        compiler_params=_params("parallel", "arbitrary", "arbitrary"),
        name="down",
    )(*args)


def _gla_kernel(q_ref, k_ref, v_ref, gate_ref, a_ref, aux_ref, wlr_ref, wg_ref, bg_ref, gn_ref,
                o_ref, state_ref, qe_ref, ke_ref, kdec_ref, decay_ref, *, tb):
    @pl.when(pl.program_id(1) == 0)
    def _():
        state_ref[...] = jnp.zeros_like(state_ref)

    c = GLA_CHUNK
    row = lax.broadcasted_iota(jnp.int32, (c, c), 0)
    col = lax.broadcasted_iota(jnp.int32, (c, c), 1)
    causal = col <= row
    tri = jnp.where(causal, 1.0, 0.0).astype(BF16)
    tri2 = jnp.concatenate([tri, tri], axis=1)
    lane = lax.broadcasted_iota(jnp.int32, (c, LANES), 1)
    head_mask = (lane < GLA_DK, lane >= GLA_DK)

    r = GLA_GATE_RANK
    xb, x_scale = _normed_operand(a_ref, aux_ref)
    lr = x_scale * jnp.dot(xb, wlr_ref[...].astype(BF16), preferred_element_type=F32)
    lr_hi = lr.astype(BF16)
    lr_lo = (lr - lr_hi.astype(F32)).astype(BF16)
    lr_lane = lax.broadcasted_iota(jnp.int32, lr.shape, 1)
    lr_mix = jnp.where(jnp.logical_and(lr_lane >= r, lr_lane < 2 * r), lr_lo, lr_hi)
    wg = wg_ref[...]
    wg_hi = wg.astype(BF16)
    wg_lo = (wg - wg_hi.astype(F32)).astype(BF16)
    wg_mix = jnp.where(lax.broadcasted_iota(jnp.int32, wg.shape, 0) >= 2 * r, wg_lo, wg_hi)
    logits = jnp.dot(lr_mix, wg_mix, preferred_element_type=F32) + bg_ref[...]
    log_a = _log_sigmoid(logits) * (1.0 / GLA_GATE_TAU)

    hi = log_a.astype(BF16)
    lo = (log_a - hi.astype(F32)).astype(BF16)
    n_chunks = tb // c
    bcum_c = [jnp.dot(tri2, jnp.concatenate([hi[t * c:(t + 1) * c], lo[t * c:(t + 1) * c]],
                                            axis=0), preferred_element_type=F32)
              for t in range(n_chunks)]
    bcum = jnp.concatenate(bcum_c, axis=0)
    b_last = jnp.concatenate(
        [jnp.broadcast_to(b[c - 1:c, :], b.shape) for b in bcum_c], axis=0)
    kk = k_ref[...].astype(F32)
    qe_ref[...] = (q_ref[...].astype(F32) * (GLA_DK ** -0.5) * jnp.exp(bcum)).astype(BF16)
    ke_ref[...] = (kk * jnp.exp(-bcum)).astype(BF16)
    kdec_ref[...] = (kk * jnp.exp(b_last - bcum)).astype(BF16)
    decay_ref[...] = jnp.concatenate([jnp.exp(b[c - 1:c, :]) for b in bcum_c], axis=0)

    def chunk(ci, carry):
        r0 = pl.multiple_of(ci * c, c)
        rows = pl.ds(r0, c)
        q_e = qe_ref[rows, :]
        k_e = ke_ref[rows, :]
        k_dec = kdec_ref[rows, :]
        decay = decay_ref[pl.ds(ci, 1), :]
        heads = range(GLA_HEADS)
        pair = lambda x, h: x[:, (h // 2) * LANES:(h // 2 + 1) * LANES]
        hsl = lambda h: slice(h * GLA_DV, (h + 1) * GLA_DV)
        zero = jnp.zeros((), BF16)
        qm = [jnp.where(head_mask[h % 2], pair(q_e, h), zero) for h in heads]
        vb = [v_ref[rows, hsl(h)] for h in heads]
        s = [lax.dot_general(qm[h], pair(k_e, h), NT_DIMS, preferred_element_type=F32)
             for h in heads]
        st = [state_ref[h] for h in heads]
        o_inter = [lax.dot_general(qm[h], st[h].astype(BF16), NT_DIMS,
                                   preferred_element_type=F32) for h in heads]
        u_t = [lax.dot_general(vb[h], pair(k_dec, h), TN_DIMS, preferred_element_type=F32)
               for h in heads]
        for h in heads:
            state_ref[h] = pair(decay, h) * st[h] + u_t[h]
        o_intra = [jnp.dot(jnp.where(causal, s[h], 0.0).astype(BF16), vb[h],
                           preferred_element_type=F32) for h in heads]
        for h in heads:
            o = o_intra[h] + o_inter[h]
            y = o * _rms_scale(o) * gn_ref[...]
            g = gate_ref[rows, hsl(h)].astype(F32)
            y = y * (g * (1.0 / (1.0 + jnp.exp(-g))))
            o_ref[rows, hsl(h)] = y.astype(o_ref.dtype)
        return carry

    lax.fori_loop(0, tb // c, chunk, 0, unroll=GLA_CHUNKS_PER_TRIP)


def _gla(proj, a, aux, w_lr, wg, bg, gn, *, batch, seq, tb=512):
    m = proj.shape[0]
    d = a.shape[1]
    tb = min(tb, seq)
    nt = seq // tb
    hk = GLA_HEADS * GLA_DK
    hv = GLA_HEADS * GLA_DV
    row = lambda b, t: b * nt + t
    return pl.pallas_call(
        functools.partial(_gla_kernel, tb=tb),
        grid=(batch, nt),
        in_specs=[pl.BlockSpec((tb, hk), lambda b, t: (row(b, t), 0)),
                  pl.BlockSpec((tb, hk), lambda b, t: (row(b, t), 1)),
                  pl.BlockSpec((tb, hv), lambda b, t: (row(b, t), 1)),
                  pl.BlockSpec((tb, hv), lambda b, t: (row(b, t), 2)),
                  *_normed_specs(a, aux, tb, row),
                  pl.BlockSpec((d, LANES), lambda b, t: (0, 0)),
                  pl.BlockSpec((LANES, hk), lambda b, t: (0, 0)),
                  pl.BlockSpec((1, hk), lambda b, t: (0, 0)),
                  pl.BlockSpec((1, GLA_DV), lambda b, t: (0, 0))],
        out_specs=pl.BlockSpec((tb, hv), lambda b, t: (row(b, t), 0)),
        out_shape=jax.ShapeDtypeStruct((m, hv), BF16),
        scratch_shapes=[pltpu.VMEM((GLA_HEADS, GLA_DV, LANES), F32),
                        pltpu.VMEM((tb, hk), BF16), pltpu.VMEM((tb, hk), BF16),
                        pltpu.VMEM((tb, hk), BF16), pltpu.VMEM((tb // GLA_CHUNK, hk), F32)],
        compiler_params=_params("parallel", "arbitrary"),
        name="gla",
    )(proj, proj, proj, proj, a, aux, w_lr, wg, bg, gn)


def _sb_kernel(q_ref, k_ref, v_ref, w_ref, gq_ref, gk_ref, gn_ref, side_ref, o_ref, side_b_ref,
               qn_ref, kn_ref, carry_ref, acc_ref, *, tk, seq):
    wmat2 = w_ref[...]
    wmat = wmat2[:tk, :tk]
    lane_mean = jnp.full((SB_DH, SB_DH), 1.0 / SB_DH, BF16)
    nd = seq // tk
    rowi = lax.broadcasted_iota(jnp.int32, (tk, tk), 0)
    coli = lax.broadcasted_iota(jnp.int32, (tk, tk), 1)
    below_diag = coli < rowi

    def normed(y, g_ref):
        ms = jnp.dot((y * y).astype(BF16), lane_mean, preferred_element_type=F32)
        return (y * (lax.rsqrt(ms + EPS) * g_ref[...])).astype(BF16)

    qn_ref[...] = normed(q_ref[...].astype(F32), gq_ref)
    kn_ref[...] = normed(k_ref[...].astype(F32), gk_ref)

    def log_betas(z):
        log_beta = jnp.minimum(z, 0.0) - jnp.log2(1.0 + jnp.exp2(-jnp.abs(z)))
        return log_beta, log_beta - z

    def last_column(prefix, width):
        return jnp.broadcast_to(prefix[:, width - 1:width], (prefix.shape[0], tk))

    q = qn_ref[...]
    log_beta, log_1m = log_betas(lax.dot_general(q[:tk], kn_ref[:tk, :], NT_DIMS,
                                                 preferred_element_type=F32))
    prefix = jnp.dot(jnp.where(below_diag, log_1m, 0.0).astype(BF16), wmat,
                     preferred_element_type=F32)
    a = jnp.where(below_diag, jnp.exp2(log_beta - prefix + last_column(prefix, tk)), 0.0)
    acc0 = jnp.dot(a.astype(BF16), v_ref[:tk, :], preferred_element_type=F32)
    window = lambda ref, g: ref[(g - 1) * tk:(g + 1) * tk, :]
    z = jnp.concatenate(
        [lax.dot_general(q[g * tk:(g + 1) * tk], window(kn_ref, g), NT_DIMS,
                         preferred_element_type=F32) for g in range(1, nd)], axis=0)
    visible = jnp.concatenate(
        [jnp.concatenate([jnp.ones((tk, tk), jnp.bool_), below_diag], axis=1)] * (nd - 1), axis=0)
    log_beta, log_1m = log_betas(z)
    prefix = jnp.dot(jnp.where(visible, log_1m, 0.0).astype(BF16), wmat2,
                     preferred_element_type=F32)
    carry_diag = last_column(prefix, 2 * tk)
    carry = carry_diag + last_column(prefix, tk)
    a = jnp.exp2(log_beta - prefix + jnp.concatenate([carry, carry_diag], axis=1))
    a = jnp.where(visible, a, 0.0).astype(BF16)
    acc = jnp.concatenate(
        [acc0] + [jnp.dot(a[(g - 1) * tk:g * tk], window(v_ref, g), preferred_element_type=F32)
                  for g in range(1, nd)], axis=0)
    carry = jnp.concatenate([jnp.full((tk, tk), SB_DEAD, F32), carry], axis=0)

    group = lambda x, g: x[g * tk:(g + 1) * tk]

    def sweep(j, carry, acc):
        first = [(g - j) * tk for g in range(nd)]
        start = [pl.multiple_of(jnp.maximum(f, 0), tk) for f in first]
        z = jnp.concatenate(
            [lax.dot_general(group(q, g), kn_ref[pl.ds(start[g], tk), :], NT_DIMS,
                             preferred_element_type=F32) for g in range(nd)], axis=0)
        log_beta, log_1m = log_betas(z)
        prefix = jnp.dot(log_1m.astype(BF16), wmat, preferred_element_type=F32)
        carry = jnp.concatenate(
            [jnp.where(first[g] >= 0, group(carry, g), SB_DEAD) for g in range(nd)], axis=0)
        carry = carry + last_column(prefix, tk)
        a = jnp.exp2(log_beta - prefix + carry).astype(BF16)
        pv = jnp.concatenate(
            [jnp.dot(group(a, g), v_ref[pl.ds(start[g], tk), :], preferred_element_type=F32)
             for g in range(nd)], axis=0)
        return carry, acc + pv

    def live(c):
        return (jnp.max(c) > -SB_EXIT_LOG2).astype(jnp.int32)

    carry_ref[...] = carry
    acc_ref[...] = acc

    assert nd > 2

    @pl.when(live(carry) > 0)
    def _():
        rows = slice(2 * tk, seq)
        z = jnp.concatenate(
            [lax.dot_general(group(q, g), kn_ref[(g - 2) * tk:(g - 1) * tk, :], NT_DIMS,
                             preferred_element_type=F32) for g in range(2, nd)], axis=0)
        log_beta, log_1m = log_betas(z)
        prefix = jnp.dot(log_1m.astype(BF16), wmat, preferred_element_type=F32)
        new_carry = carry_ref[rows, :] + last_column(prefix, tk)
        a = jnp.exp2(log_beta - prefix + new_carry).astype(BF16)
        pv = jnp.concatenate(
            [jnp.dot(a[(g - 2) * tk:(g - 1) * tk], v_ref[(g - 2) * tk:(g - 1) * tk, :],
                     preferred_element_type=F32) for g in range(2, nd)], axis=0)
        carry_ref[tk:2 * tk, :] = jnp.full((tk, tk), SB_DEAD, F32)
        carry_ref[rows, :] = new_carry
        acc_ref[rows, :] += pv

    def cond(st):
        j, go = st
        return jnp.logical_and(go > 0, j < nd)

    def body(st):
        j, _ = st
        carry, acc = sweep(j, carry_ref[...], acc_ref[...])
        carry_ref[...] = carry
        acc_ref[...] = acc
        return j + 1, live(carry)

    lax.while_loop(cond, body, (3, live(carry_ref[...])))
    acc = acc_ref[...]
    o_ref[...] = (acc * _rms_scale(acc) * gn_ref[...]).astype(o_ref.dtype)
    side_b_ref[...] = side_ref[...].astype(BF16)


def _sb(proj, gq, gk, gn, side, *, layer, col0, batch, seq, tk=128):
    m = proj.shape[0]
    assert seq % tk == 0 and seq >= 2 * tk
    jj = lax.broadcasted_iota(jnp.int32, (2 * tk, 2 * tk), 0)
    ss = lax.broadcasted_iota(jnp.int32, (2 * tk, 2 * tk), 1)
    wmat = jnp.logical_and(jj <= ss, jj // tk == ss // tk).astype(BF16)
    vec = pl.BlockSpec((1, SB_DH), lambda b, h: (0, 0))
    rows, cols = side.shape[1] // (batch * SB_HEADS), side.shape[2]
    return pl.pallas_call(
        functools.partial(_sb_kernel, tk=tk, seq=seq),
        grid=(batch, SB_HEADS),
        in_specs=[pl.BlockSpec((seq, SB_DH), lambda b, h: (b, col0 + h)),
                  pl.BlockSpec((seq, SB_DH), lambda b, h: (b, col0 + SB_HEADS + h)),
                  pl.BlockSpec((seq, SB_DH), lambda b, h: (b, col0 + 2 * SB_HEADS + h)),
                  pl.BlockSpec((2 * tk, 2 * tk), lambda b, h: (0, 0)),
                  vec, vec, vec,
                  pl.BlockSpec((None, rows, cols), lambda b, h: (layer, b * SB_HEADS + h, 0))],
        out_specs=[pl.BlockSpec((seq, SB_DH), lambda b, h: (b, h)),
                   pl.BlockSpec((rows, cols), lambda b, h: (b * SB_HEADS + h, 0))],
        out_shape=[jax.ShapeDtypeStruct((m, SB_HEADS * SB_DH), BF16),
                   jax.ShapeDtypeStruct(side.shape[1:], BF16)],
        scratch_shapes=[pltpu.VMEM((seq, SB_DH), BF16), pltpu.VMEM((seq, SB_DH), BF16),
                        pltpu.VMEM((seq, tk), F32), pltpu.VMEM((seq, SB_DH), F32)],
        compiler_params=_params("parallel", "parallel"),
        name="stickbreak",
    )(proj, proj, proj, wmat, gq, gk, gn, side)


def _layer(x, xb, ss, l, next_gain, w_in_b, w_in, w_gate_up, b_gate, gla_out_norm, sb_q_norm,
           sb_k_norm, sb_out_norm, w_o, mlp_norm, w_up, w_down, *, batch, seq):
    hk = GLA_HEADS * GLA_DK
    hv = GLA_HEADS * GLA_DV
    n_gla = 2 * hk + 2 * hv
    pad = LANES - 3 * GLA_GATE_RANK
    w_lr = jnp.pad(jnp.tile(w_in[l, :, n_gla:n_gla + GLA_GATE_RANK], (1, 3)), ((0, 0), (0, pad)))
    wg = jnp.pad(jnp.tile(w_gate_up[l], (3, 1)), ((0, pad), (0, 0)))
    gq = (sb_q_norm[l] * (math.log2(math.e) / math.sqrt(SB_DH))).reshape(1, SB_DH)

    p_gla = _matmul(_identity, xb, ss, w_in_b, n=n_gla, layer=l, out_dtype=BF16,
                    name="inproj_gla")
    p_sb = _matmul(_identity, xb, ss, w_in_b[l, :, n_gla + GLA_GATE_RANK:], out_dtype=BF16,
                   name="inproj_sb")
    o_gla = _gla(p_gla, xb, ss, w_lr, wg, b_gate[l].reshape(1, hk),
                 gla_out_norm[l].reshape(1, GLA_DV), batch=batch, seq=seq)
    o_sb, w_o_b = _sb(p_sb, gq, sb_k_norm[l].reshape(1, SB_DH), sb_out_norm[l].reshape(1, SB_DH),
                      w_o, layer=l, col0=0, batch=batch, seq=seq)
    x, xb, ss = _oproj(o_gla, o_sb, w_o_b, x, mlp_norm[l])
    up, w_down_b = _matmul(_relu2, xb, ss, w_up, layer=l, side=w_down, out_dtype=BF16,
                           name="mlp_up")
    if next_gain is None:
        return _down(up, w_down_b, x, None, layer=l), None, None
    return _down(up, w_down_b, x, next_gain, layer=l)


def kernel(x, attn_norm, w_in, w_gate_up, b_gate, gla_out_norm, sb_q_norm, sb_k_norm,
           sb_out_norm, w_o, mlp_norm, w_up, w_down):
    batch, seq, d = x.shape
    depth = w_in.shape[0]
    y = x.reshape(batch * seq, d)
    w_in_b = w_in.astype(BF16)
    yb, ss = y, attn_norm[0].reshape(1, d)
    for l in range(depth):
        next_gain = attn_norm[l + 1] if l + 1 < depth else None
        y, yb, ss = _layer(y, yb, ss, l, next_gain, w_in_b, w_in, w_gate_up, b_gate, gla_out_norm,
                           sb_q_norm, sb_k_norm, sb_out_norm, w_o, mlp_norm, w_up, w_down,
                           batch=batch, seq=seq)
    return y.reshape(batch, seq, d)
```

```python
import functools
import math

import jax
import jax.numpy as jnp
from jax import lax
from jax.experimental import pallas as pl
from jax.experimental.pallas import tpu as pltpu

EPS = 1e-6
LANES = 128

GLA_HEADS = 8
GLA_DK = 64
GLA_DV = 128
GLA_GATE_RANK = 16
GLA_GATE_TAU = 16.0
GLA_CHUNK = 64
GLA_CHUNKS_PER_TRIP = 8
SB_HEADS = 8
SB_DH = 128

SB_EXIT_LOG2 = 126.0
SB_DEAD = -1e30

V7X_VMEM_BYTES = 64 * 1024 * 1024
VMEM_LIMIT = V7X_VMEM_BYTES * 7 // 8
DOWN_EPILOGUE_CHUNKS = 4

F32 = jnp.float32
BF16 = jnp.bfloat16
NT_DIMS = (((1,), (1,)), ((), ()))
TN_DIMS = (((0,), (0,)), ((), ()))


def _params(*sem):
    return pltpu.CompilerParams(dimension_semantics=sem, vmem_limit_bytes=VMEM_LIMIT)


def _log_sigmoid(z):
    return jnp.minimum(z, 0.0) - jnp.log(1.0 + jnp.exp(-jnp.abs(z)))


def _rms_scale(y):
    return lax.rsqrt(jnp.mean(y * y, axis=-1, keepdims=True) + EPS)


def _sumsq(y):
    return jnp.broadcast_to(jnp.sum(y * y, axis=-1, keepdims=True), (y.shape[0], LANES))


def _normed_operand(a_ref, aux_ref):
    d = a_ref.shape[1]
    if a_ref.dtype == BF16:
        return a_ref[...], lax.rsqrt(jnp.sum(aux_ref[...], axis=0) * (1.0 / d) + EPS)
    x = a_ref[...]
    return (x * aux_ref[...]).astype(BF16), lax.rsqrt(_sumsq(x) * (1.0 / d) + EPS)


def _normed_specs(a, aux, rows, row_map):
    a_spec = pl.BlockSpec((rows, a.shape[1]), lambda *g: (row_map(*g), 0))
    if a.dtype == BF16:
        return a_spec, pl.BlockSpec((aux.shape[0], rows, LANES), lambda *g: (0, row_map(*g), 0))
    return a_spec, pl.BlockSpec((1, a.shape[1]), lambda *g: (0, 0))


def _emit_normed(y, g_ref, o_ref, xb_ref, ss_ref):
    o_ref[...] = y
    xb_ref[...] = (y * g_ref[...]).astype(BF16)
    ss_ref[...] = _sumsq(y)


def _w_spec(w, rows, cols, index_map, layer):
    if w.ndim == 2:
        return pl.BlockSpec((rows, cols), index_map)
    return pl.BlockSpec((None, rows, cols), lambda *g: (layer,) + tuple(index_map(*g)))


def _mm_kernel(a_ref, aux_ref, w_ref, *rest, post):
    o_ref = rest[-1] if len(rest) == 1 else rest[1]
    xb, r = _normed_operand(a_ref, aux_ref)
    acc = jnp.dot(xb, w_ref[...].astype(BF16), preferred_element_type=F32)
    for c in range(acc.shape[1] // LANES):
        sl = slice(c * LANES, (c + 1) * LANES)
        o_ref[:, sl] = post(acc[:, sl] * r).astype(o_ref.dtype)
    if len(rest) == 3:
        rest[2][...] = rest[0][...].astype(BF16)


def _identity(y):
    return y


def _relu2(y):
    return jnp.square(jnp.maximum(y, 0.0))


def _matmul(post, a, aux, w, *, n=None, layer=0, side=None, out_dtype, tm=1024, tn=1024, name):
    m, k = a.shape
    n = w.shape[-1] if n is None else n
    tm = min(tm, m)
    tn = min(tn, n)
    nj, ni = n // tn, m // tm
    in_specs = [*_normed_specs(a, aux, tm, lambda j, i: i),
                _w_spec(w, k, tn, lambda j, i: (0, j), layer)]
    out_specs = pl.BlockSpec((tm, tn), lambda j, i: (i, j))
    out_shape = jax.ShapeDtypeStruct((m, n), out_dtype)
    args = [a, aux, w]
    if side is not None:
        rows, cols = side.shape[1] // (nj * ni), side.shape[2]
        in_specs.append(pl.BlockSpec((None, rows, cols), lambda j, i: (layer, j * ni + i, 0)))
        out_specs = [out_specs, pl.BlockSpec((rows, cols), lambda j, i: (j * ni + i, 0))]
        out_shape = [out_shape, jax.ShapeDtypeStruct(side.shape[1:], BF16)]
        args.append(side)
    return pl.pallas_call(
        functools.partial(_mm_kernel, post=post),
        grid=(nj, ni),
        in_specs=in_specs,
        out_specs=out_specs,
        out_shape=out_shape,
        compiler_params=_params("parallel", "arbitrary"),
        name=name,
    )(*args)


def _normed_out(m, n, tm, tn, tile_map, slot_map):
    specs = [pl.BlockSpec((tm, tn), tile_map), pl.BlockSpec((tm, tn), tile_map),
             pl.BlockSpec((None, tm, LANES), slot_map)]
    shapes = [jax.ShapeDtypeStruct((m, n), F32), jax.ShapeDtypeStruct((m, n), BF16),
              jax.ShapeDtypeStruct((n // tn, m, LANES), F32)]
    return specs, shapes


def _oproj_kernel(a1_ref, a2_ref, w1_ref, w2_ref, x_ref, g_ref, o_ref, xb_ref, ss_ref):
    acc = jnp.dot(a1_ref[...], w1_ref[...], preferred_element_type=F32)
    acc += jnp.dot(a2_ref[...], w2_ref[...], preferred_element_type=F32)
    _emit_normed(x_ref[...] + acc, g_ref, o_ref, xb_ref, ss_ref)


def _oproj(a1, a2, w, x, gain, *, tm=512):
    m, kh = a1.shape
    n = w.shape[-1]
    tm = min(tm, m)
    out_specs, out_shape = _normed_out(m, n, tm, n, lambda i: (i, 0), lambda i: (0, i, 0))
    return pl.pallas_call(
        _oproj_kernel,
        grid=(m // tm,),
        in_specs=[pl.BlockSpec((tm, kh), lambda i: (i, 0)),
                  pl.BlockSpec((tm, kh), lambda i: (i, 0)),
                  pl.BlockSpec((kh, n), lambda i: (0, 0)),
                  pl.BlockSpec((kh, n), lambda i: (1, 0)),
                  pl.BlockSpec((tm, n), lambda i: (i, 0)),
                  pl.BlockSpec((1, n), lambda i: (0, 0))],
        out_specs=out_specs,
        out_shape=out_shape,
        compiler_params=_params("parallel"),
        name="oproj",
    )(a1, a2, w, w, x, gain.reshape(1, n))


def _down_kernel(a_ref, w_ref, x_ref, *rest, normed):
    acc_ref = rest[-1]
    kk = pl.program_id(2)
    last = pl.num_programs(2) - 1

    @pl.when(kk == 0)
    def _():
        acc_ref[...] = jnp.zeros_like(acc_ref)

    @pl.when(kk < last)
    def _():
        acc_ref[...] += jnp.dot(a_ref[...], w_ref[...].astype(BF16),
                                preferred_element_type=F32)

    @pl.when(kk == last)
    def _():
        wb = w_ref[...].astype(BF16)
        tm = acc_ref.shape[0]
        rb = tm // DOWN_EPILOGUE_CHUNKS
        for c in range(DOWN_EPILOGUE_CHUNKS):
            rows = slice(c * rb, (c + 1) * rb)
            y = x_ref[rows, :] + (acc_ref[rows, :] + jnp.dot(
                a_ref[rows, :], wb, preferred_element_type=F32))
            if normed:
                g_ref, o_ref, xb_ref, ss_ref = rest[:-1]
                o_ref[rows, :] = y
                xb_ref[rows, :] = (y * g_ref[...]).astype(BF16)
                ss_ref[rows, :] = _sumsq(y)
            else:
                rest[0][rows, :] = y


def _down(a, w, x, gain, *, layer, tm=1024, tn=1024, tk=2048):
    m, k = a.shape
    n = w.shape[-1]
    tm = min(tm, m)
    tile_map = lambda i, j, kk: (i, j)
    in_specs = [pl.BlockSpec((tm, tk), lambda i, j, kk: (i, kk)),
                _w_spec(w, tk, tn, lambda i, j, kk: (kk, j), layer),
                pl.BlockSpec((tm, tn), tile_map)]
    args = [a, w, x]
    if gain is None:
        out_specs = pl.BlockSpec((tm, tn), tile_map)
        out_shape = jax.ShapeDtypeStruct((m, n), F32)
    else:
        in_specs.append(pl.BlockSpec((1, tn), lambda i, j, kk: (0, j)))
        args.append(gain.reshape(1, n))
        out_specs, out_shape = _normed_out(m, n, tm, tn, tile_map, lambda i, j, kk: (j, i, 0))
    return pl.pallas_call(
        functools.partial(_down_kernel, normed=gain is not None),
        grid=(m // tm, n // tn, k // tk),
        in_specs=in_specs,
        out_specs=out_specs,
        out_shape=out_shape,
        scratch_shapes=[pltpu.VMEM((tm, tn), F32)],
        compiler_params=_params("parallel", "arbitrary", "arbitrary"),
        name="down",
    )(*args)


def _gla_kernel(q_ref, k_ref, v_ref, gate_ref, a_ref, aux_ref, wlr_ref, wg_ref, bg_ref, gn_ref,
                o_ref, state_ref, qe_ref, ke_ref, kdec_ref, decay_ref, *, tb):
    @pl.when(pl.program_id(1) == 0)
    def _():
        state_ref[...] = jnp.zeros_like(state_ref)

    c = GLA_CHUNK
    row = lax.broadcasted_iota(jnp.int32, (c, c), 0)
    col = lax.broadcasted_iota(jnp.int32, (c, c), 1)
    causal = col <= row
    tri = jnp.where(causal, 1.0, 0.0).astype(BF16)
    tri2 = jnp.concatenate([tri, tri], axis=1)
    lane = lax.broadcasted_iota(jnp.int32, (c, LANES), 1)
    head_mask = (lane < GLA_DK, lane >= GLA_DK)

    r = GLA_GATE_RANK
    xb, x_scale = _normed_operand(a_ref, aux_ref)
    lr = x_scale * jnp.dot(xb, wlr_ref[...].astype(BF16), preferred_element_type=F32)
    lr_hi = lr.astype(BF16)
    lr_lo = (lr - lr_hi.astype(F32)).astype(BF16)
    lr_lane = lax.broadcasted_iota(jnp.int32, lr.shape, 1)
    lr_mix = jnp.where(jnp.logical_and(lr_lane >= r, lr_lane < 2 * r), lr_lo, lr_hi)
    wg = wg_ref[...]
    wg_hi = wg.astype(BF16)
    wg_lo = (wg - wg_hi.astype(F32)).astype(BF16)
    wg_mix = jnp.where(lax.broadcasted_iota(jnp.int32, wg.shape, 0) >= 2 * r, wg_lo, wg_hi)
    logits = jnp.dot(lr_mix, wg_mix, preferred_element_type=F32) + bg_ref[...]
    log_a = _log_sigmoid(logits) * (1.0 / GLA_GATE_TAU)

    hi = log_a.astype(BF16)
    lo = (log_a - hi.astype(F32)).astype(BF16)
    n_chunks = tb // c
    bcum_c = [jnp.dot(tri2, jnp.concatenate([hi[t * c:(t + 1) * c], lo[t * c:(t + 1) * c]],
                                            axis=0), preferred_element_type=F32)
              for t in range(n_chunks)]
    bcum = jnp.concatenate(bcum_c, axis=0)
    b_last = jnp.concatenate(
        [jnp.broadcast_to(b[c - 1:c, :], b.shape) for b in bcum_c], axis=0)
    kk = k_ref[...].astype(F32)
    qe_ref[...] = (q_ref[...].astype(F32) * (GLA_DK ** -0.5) * jnp.exp(bcum)).astype(BF16)
    ke_ref[...] = (kk * jnp.exp(-bcum)).astype(BF16)
    kdec_ref[...] = (kk * jnp.exp(b_last - bcum)).astype(BF16)
    decay_ref[...] = jnp.concatenate([jnp.exp(b[c - 1:c, :]) for b in bcum_c], axis=0)

    def chunk(ci, carry):
        r0 = pl.multiple_of(ci * c, c)
        rows = pl.ds(r0, c)
        q_e = qe_ref[rows, :]
        k_e = ke_ref[rows, :]
        k_dec = kdec_ref[rows, :]
        decay = decay_ref[pl.ds(ci, 1), :]
        heads = range(GLA_HEADS)
        pair = lambda x, h: x[:, (h // 2) * LANES:(h // 2 + 1) * LANES]
        hsl = lambda h: slice(h * GLA_DV, (h + 1) * GLA_DV)
        zero = jnp.zeros((), BF16)
        qm = [jnp.where(head_mask[h % 2], pair(q_e, h), zero) for h in heads]
        vb = [v_ref[rows, hsl(h)] for h in heads]
        s = [lax.dot_general(qm[h], pair(k_e, h), NT_DIMS, preferred_element_type=F32)
             for h in heads]
        st = [state_ref[h] for h in heads]
        o_inter = [lax.dot_general(qm[h], st[h].astype(BF16), NT_DIMS,
                                   preferred_element_type=F32) for h in heads]
        u_t = [lax.dot_general(vb[h], pair(k_dec, h), TN_DIMS, preferred_element_type=F32)
               for h in heads]
        for h in heads:
            state_ref[h] = pair(decay, h) * st[h] + u_t[h]
        o_intra = [jnp.dot(jnp.where(causal, s[h], 0.0).astype(BF16), vb[h],
                           preferred_element_type=F32) for h in heads]
        for h in heads:
            o = o_intra[h] + o_inter[h]
            y = o * _rms_scale(o) * gn_ref[...]
            g = gate_ref[rows, hsl(h)].astype(F32)
            y = y * (g * (1.0 / (1.0 + jnp.exp(-g))))
            o_ref[rows, hsl(h)] = y.astype(o_ref.dtype)
        return carry

    lax.fori_loop(0, tb // c, chunk, 0, unroll=GLA_CHUNKS_PER_TRIP)


def _gla(proj, a, aux, w_lr, wg, bg, gn, *, batch, seq, tb=512):
    m = proj.shape[0]
    d = a.shape[1]
    tb = min(tb, seq)
    nt = seq // tb
    hk = GLA_HEADS * GLA_DK
    hv = GLA_HEADS * GLA_DV
    row = lambda b, t: b * nt + t
    return pl.pallas_call(
        functools.partial(_gla_kernel, tb=tb),
        grid=(batch, nt),
        in_specs=[pl.BlockSpec((tb, hk), lambda b, t: (row(b, t), 0)),
                  pl.BlockSpec((tb, hk), lambda b, t: (row(b, t), 1)),
                  pl.BlockSpec((tb, hv), lambda b, t: (row(b, t), 1)),
                  pl.BlockSpec((tb, hv), lambda b, t: (row(b, t), 2)),
                  *_normed_specs(a, aux, tb, row),
                  pl.BlockSpec((d, LANES), lambda b, t: (0, 0)),
                  pl.BlockSpec((LANES, hk), lambda b, t: (0, 0)),
                  pl.BlockSpec((1, hk), lambda b, t: (0, 0)),
                  pl.BlockSpec((1, GLA_DV), lambda b, t: (0, 0))],
        out_specs=pl.BlockSpec((tb, hv), lambda b, t: (row(b, t), 0)),
        out_shape=jax.ShapeDtypeStruct((m, hv), BF16),
        scratch_shapes=[pltpu.VMEM((GLA_HEADS, GLA_DV, LANES), F32),
                        pltpu.VMEM((tb, hk), BF16), pltpu.VMEM((tb, hk), BF16),
                        pltpu.VMEM((tb, hk), BF16), pltpu.VMEM((tb // GLA_CHUNK, hk), F32)],
        compiler_params=_params("parallel", "arbitrary"),
        name="gla",
    )(proj, proj, proj, proj, a, aux, w_lr, wg, bg, gn)


def _sb_kernel(q_ref, k_ref, v_ref, w_ref, gq_ref, gk_ref, gn_ref, side_ref, o_ref, side_b_ref,
               qn_ref, kn_ref, carry_ref, acc_ref, *, tk, seq):
    wmat2 = w_ref[...]
    wmat = wmat2[:tk, :tk]
    lane_mean = jnp.full((SB_DH, SB_DH), 1.0 / SB_DH, BF16)
    nd = seq // tk
    rowi = lax.broadcasted_iota(jnp.int32, (tk, tk), 0)
    coli = lax.broadcasted_iota(jnp.int32, (tk, tk), 1)
    below_diag = coli < rowi

    def normed(y, g_ref):
        ms = jnp.dot((y * y).astype(BF16), lane_mean, preferred_element_type=F32)
        return (y * (lax.rsqrt(ms + EPS) * g_ref[...])).astype(BF16)

    qn_ref[...] = normed(q_ref[...].astype(F32), gq_ref)
    kn_ref[...] = normed(k_ref[...].astype(F32), gk_ref)

    def log_betas(z):
        log_beta = jnp.minimum(z, 0.0) - jnp.log2(1.0 + jnp.exp2(-jnp.abs(z)))
        return log_beta, log_beta - z

    def last_column(prefix, width):
        return jnp.broadcast_to(prefix[:, width - 1:width], (prefix.shape[0], tk))

    q = qn_ref[...]
    log_beta, log_1m = log_betas(lax.dot_general(q[:tk], kn_ref[:tk, :], NT_DIMS,
                                                 preferred_element_type=F32))
    prefix = jnp.dot(jnp.where(below_diag, log_1m, 0.0).astype(BF16), wmat,
                     preferred_element_type=F32)
    a = jnp.where(below_diag, jnp.exp2(log_beta - prefix + last_column(prefix, tk)), 0.0)
    acc0 = jnp.dot(a.astype(BF16), v_ref[:tk, :], preferred_element_type=F32)
    window = lambda ref, g: ref[(g - 1) * tk:(g + 1) * tk, :]
    z = jnp.concatenate(
        [lax.dot_general(q[g * tk:(g + 1) * tk], window(kn_ref, g), NT_DIMS,
                         preferred_element_type=F32) for g in range(1, nd)], axis=0)
    visible = jnp.concatenate(
        [jnp.concatenate([jnp.ones((tk, tk), jnp.bool_), below_diag], axis=1)] * (nd - 1), axis=0)
    log_beta, log_1m = log_betas(z)
    prefix = jnp.dot(jnp.where(visible, log_1m, 0.0).astype(BF16), wmat2,
                     preferred_element_type=F32)
    carry_diag = last_column(prefix, 2 * tk)
    carry = carry_diag + last_column(prefix, tk)
    a = jnp.exp2(log_beta - prefix + jnp.concatenate([carry, carry_diag], axis=1))
    a = jnp.where(visible, a, 0.0).astype(BF16)
    acc = jnp.concatenate(
        [acc0] + [jnp.dot(a[(g - 1) * tk:g * tk], window(v_ref, g), preferred_element_type=F32)
                  for g in range(1, nd)], axis=0)
    carry = jnp.concatenate([jnp.full((tk, tk), SB_DEAD, F32), carry], axis=0)

    group = lambda x, g: x[g * tk:(g + 1) * tk]

    def sweep(j, carry, acc):
        first = [(g - j) * tk for g in range(nd)]
        start = [pl.multiple_of(jnp.maximum(f, 0), tk) for f in first]
        z = jnp.concatenate(
            [lax.dot_general(group(q, g), kn_ref[pl.ds(start[g], tk), :], NT_DIMS,
                             preferred_element_type=F32) for g in range(nd)], axis=0)
        log_beta, log_1m = log_betas(z)
        prefix = jnp.dot(log_1m.astype(BF16), wmat, preferred_element_type=F32)
        carry = jnp.concatenate(
            [jnp.where(first[g] >= 0, group(carry, g), SB_DEAD) for g in range(nd)], axis=0)
        carry = carry + last_column(prefix, tk)
        a = jnp.exp2(log_beta - prefix + carry).astype(BF16)
        pv = jnp.concatenate(
            [jnp.dot(group(a, g), v_ref[pl.ds(start[g], tk), :], preferred_element_type=F32)
             for g in range(nd)], axis=0)
        return carry, acc + pv

    def live(c):
        return (jnp.max(c) > -SB_EXIT_LOG2).astype(jnp.int32)

    carry_ref[...] = carry
    acc_ref[...] = acc

    assert nd > 2

    @pl.when(live(carry) > 0)
    def _():
        rows = slice(2 * tk, seq)
        z = jnp.concatenate(
            [lax.dot_general(group(q, g), kn_ref[(g - 2) * tk:(g - 1) * tk, :], NT_DIMS,
                             preferred_element_type=F32) for g in range(2, nd)], axis=0)
        log_beta, log_1m = log_betas(z)
        prefix = jnp.dot(log_1m.astype(BF16), wmat, preferred_element_type=F32)
        new_carry = carry_ref[rows, :] + last_column(prefix, tk)
        a = jnp.exp2(log_beta - prefix + new_carry).astype(BF16)
        pv = jnp.concatenate(
            [jnp.dot(a[(g - 2) * tk:(g - 1) * tk], v_ref[(g - 2) * tk:(g - 1) * tk, :],
                     preferred_element_type=F32) for g in range(2, nd)], axis=0)
        carry_ref[tk:2 * tk, :] = jnp.full((tk, tk), SB_DEAD, F32)
        carry_ref[rows, :] = new_carry
        acc_ref[rows, :] += pv

    def cond(st):
        j, go = st
        return jnp.logical_and(go > 0, j < nd)

    def body(st):
        j, _ = st
        carry, acc = sweep(j, carry_ref[...], acc_ref[...])
        carry_ref[...] = carry
        acc_ref[...] = acc
        return j + 1, live(carry)

    lax.while_loop(cond, body, (3, live(carry_ref[...])))
    acc = acc_ref[...]
    o_ref[...] = (acc * _rms_scale(acc) * gn_ref[...]).astype(o_ref.dtype)
    side_b_ref[...] = side_ref[...].astype(BF16)


def _sb(proj, gq, gk, gn, side, *, layer, col0, batch, seq, tk=128):
    m = proj.shape[0]
    assert seq % tk == 0 and seq >= 2 * tk
    jj = lax.broadcasted_iota(jnp.int32, (2 * tk, 2 * tk), 0)
    ss = lax.broadcasted_iota(jnp.int32, (2 * tk, 2 * tk), 1)
    wmat = jnp.logical_and(jj <= ss, jj // tk == ss // tk).astype(BF16)
    vec = pl.BlockSpec((1, SB_DH), lambda b, h: (0, 0))
    rows, cols = side.shape[1] // (batch * SB_HEADS), side.shape[2]
    return pl.pallas_call(
        functools.partial(_sb_kernel, tk=tk, seq=seq),
        grid=(batch, SB_HEADS),
        in_specs=[pl.BlockSpec((seq, SB_DH), lambda b, h: (b, col0 + h)),
                  pl.BlockSpec((seq, SB_DH), lambda b, h: (b, col0 + SB_HEADS + h)),
                  pl.BlockSpec((seq, SB_DH), lambda b, h: (b, col0 + 2 * SB_HEADS + h)),
                  pl.BlockSpec((2 * tk, 2 * tk), lambda b, h: (0, 0)),
                  vec, vec, vec,
                  pl.BlockSpec((None, rows, cols), lambda b, h: (layer, b * SB_HEADS + h, 0))],
        out_specs=[pl.BlockSpec((seq, SB_DH), lambda b, h: (b, h)),
                   pl.BlockSpec((rows, cols), lambda b, h: (b * SB_HEADS + h, 0))],
        out_shape=[jax.ShapeDtypeStruct((m, SB_HEADS * SB_DH), BF16),
                   jax.ShapeDtypeStruct(side.shape[1:], BF16)],
        scratch_shapes=[pltpu.VMEM((seq, SB_DH), BF16), pltpu.VMEM((seq, SB_DH), BF16),
                        pltpu.VMEM((seq, tk), F32), pltpu.VMEM((seq, SB_DH), F32)],
        compiler_params=_params("parallel", "parallel"),
        name="stickbreak",
    )(proj, proj, proj, wmat, gq, gk, gn, side)


def _layer(x, xb, ss, l, next_gain, w_in_b, w_in, w_gate_up, b_gate, gla_out_norm, sb_q_norm,
           sb_k_norm, sb_out_norm, w_o, mlp_norm, w_up, w_down, *, batch, seq):
    hk = GLA_HEADS * GLA_DK
    hv = GLA_HEADS * GLA_DV
    n_gla = 2 * hk + 2 * hv
    pad = LANES - 3 * GLA_GATE_RANK
    w_lr = jnp.pad(jnp.tile(w_in[l, :, n_gla:n_gla + GLA_GATE_RANK], (1, 3)), ((0, 0), (0, pad)))
    wg = jnp.pad(jnp.tile(w_gate_up[l], (3, 1)), ((0, pad), (0, 0)))
    gq = (sb_q_norm[l] * (math.log2(math.e) / math.sqrt(SB_DH))).reshape(1, SB_DH)

    p_gla = _matmul(_identity, xb, ss, w_in_b, n=n_gla, layer=l, out_dtype=BF16, tn=n_gla // 2,
                    name="inproj_gla")
    p_sb = _matmul(_identity, xb, ss, w_in_b[l, :, n_gla + GLA_GATE_RANK:], out_dtype=BF16,
                   tn=n_gla // 2, name="inproj_sb")
    o_gla = _gla(p_gla, xb, ss, w_lr, wg, b_gate[l].reshape(1, hk),
                 gla_out_norm[l].reshape(1, GLA_DV), batch=batch, seq=seq)
    o_sb, w_o_b = _sb(p_sb, gq, sb_k_norm[l].reshape(1, SB_DH), sb_out_norm[l].reshape(1, SB_DH),
                      w_o, layer=l, col0=0, batch=batch, seq=seq)
    x, xb, ss = _oproj(o_gla, o_sb, w_o_b, x, mlp_norm[l])
    up, w_down_b = _matmul(_relu2, xb, ss, w_up, layer=l, side=w_down, out_dtype=BF16,
                           name="mlp_up")
    if next_gain is None:
        return _down(up, w_down_b, x, None, layer=l), None, None
    return _down(up, w_down_b, x, next_gain, layer=l)


def kernel(x, attn_norm, w_in, w_gate_up, b_gate, gla_out_norm, sb_q_norm, sb_k_norm,
           sb_out_norm, w_o, mlp_norm, w_up, w_down):
    batch, seq, d = x.shape
    depth = w_in.shape[0]
    y = x.reshape(batch * seq, d)
    w_in_b = w_in.astype(BF16)
    yb, ss = y, attn_norm[0].reshape(1, d)
    for l in range(depth):
        next_gain = attn_norm[l + 1] if l + 1 < depth else None
        y, yb, ss = _layer(y, yb, ss, l, next_gain, w_in_b, w_in, w_gate_up, b_gate, gla_out_norm,
                           sb_q_norm, sb_k_norm, sb_out_norm, w_o, mlp_norm, w_up, w_down,
                           batch=batch, seq=seq)
    return y.reshape(batch, seq, d)
```
